```python
import math
import jax, jax.numpy as jnp
from jax import lax
import numpy as np

D_MODEL = 1024
BATCH = 4
SEQ = 4096
DEPTH = 2
DEC_BATCH = 128
DEC_SEQ = 1
PAST_LEN = 2048
PAGE_SIZE = 128

H_A = 4
DK_A = 128
DV_A = 128
RET_CHUNK = 128
ROPE_BASE = 10000.0
H_B = 4
DH_B = 64
DV_B = 2 * DH_B
Q_BLOCK = 128
N_BUCKETS = 32
MAX_DISTANCE = 128
H_C = 8
DH_C = 64
R_DECAY = 64
R_A = 64
R_V = 32
R_G = 160
LNX_EPS = 64e-5
W_AQ = H_A * DK_A
W_A = H_A * DV_A
W_BQK = H_B * 2 * DH_B
W_B = H_B * DV_B
W_C = H_C * DH_C
D_FF = 4 * D_MODEL
RET_COLS = 2 * W_AQ + 2 * W_A
DIFF_COLS = 2 * W_BQK + W_B
RWKV_COLS = 3 * W_C + R_DECAY + R_A + R_G
GATE_COLS = 3 * D_MODEL
IN_COLS = RET_COLS + DIFF_COLS + RWKV_COLS + GATE_COLS
EPS = 1e-6

kernel_name = 'hybrid_retention_diffattn_rwkv7_decoder_step'


def rms_norm(x, gain=None):
    xf = x.astype(jnp.float32)
    y = xf * lax.rsqrt(jnp.mean(xf * xf, axis=-1, keepdims=True) + EPS)
    if gain is not None:
        y = y * gain.astype(jnp.float32)
    return y.astype(x.dtype)


def rope(x, pos):
    half = x.shape[-1] // 2
    inv = ROPE_BASE ** (-jnp.arange(half, dtype=jnp.float32) / half)
    ang = pos.astype(jnp.float32)[:, None] * inv[None, :]
    cos = jnp.cos(ang)[None, :, None, :]
    sin = jnp.sin(ang)[None, :, None, :]
    x1, x2 = x[..., :half], x[..., half:]
    return jnp.concatenate([x1 * cos - x2 * sin, x1 * sin + x2 * cos], axis=-1)


def retention_chunked(q, k, v, s0):
    B, L = q.shape[0], q.shape[1]
    C = math.gcd(L, RET_CHUNK)
    n = L // C
    log_g = jnp.log1p(-jnp.exp2(-5.0 - jnp.arange(H_A, dtype=jnp.float32)))
    i = jnp.arange(C, dtype=jnp.float32)
    dist = i[:, None] - i[None, :]
    causal = dist >= 0
    inner = jnp.where(causal[None], jnp.exp(jnp.where(causal, dist, 0.0)[None] * log_g[:, None, None]), 0.0)
    cross = jnp.exp((i[:, None] + 1.0) * log_g[None, :])
    tail = jnp.exp((C - 1.0 - i)[:, None] * log_g[None, :])
    chunk = jnp.exp(C * log_g)

    def split(t):
        return t.reshape(B, n, C, *t.shape[2:]).swapaxes(0, 1)

    def step(S, blk):
        qc, kc, vc = blk
        sc = jnp.einsum('bihd,bjhd->bhij', qc, kc) * inner
        o = (jnp.einsum('bhij,bjhe->bihe', sc, vc)
             + jnp.einsum('bihd,bhde->bihe', qc, S) * cross[None, :, :, None])
        S = S * chunk[None, :, None, None] + jnp.einsum('bjhd,bjhe->bhde', kc * tail[None, :, :, None], vc)
        return S, o

    S, o = lax.scan(step, s0, (split(q), split(k), split(v)))
    return o.swapaxes(0, 1).reshape(B, L, H_A, DV_A), S


def t5_bias(table, q_pos, k_pos):
    n = jnp.maximum(q_pos[:, None] - k_pos[None, :], 0)
    max_exact = N_BUCKETS // 2
    nf = jnp.maximum(n, 1).astype(jnp.float32)
    large = max_exact + (jnp.log(nf / max_exact) / math.log(MAX_DISTANCE / max_exact)
                         * (N_BUCKETS - max_exact)).astype(jnp.int32)
    large = jnp.minimum(large, N_BUCKETS - 1)
    bucket = jnp.where(n < max_exact, n, large)
    return jnp.moveaxis(table[bucket], -1, 0)


def diff_attend(q, k, v, q_pos, k_pos, bias_table, lam):
    s = jnp.einsum('bqhcd,bkhcd->bhcqk', q, k).astype(jnp.float32)
    s = s + t5_bias(bias_table, q_pos, k_pos).astype(jnp.float32)[None, :, None]
    s = jnp.where((k_pos[None, :] <= q_pos[:, None])[None, None, None], s, -jnp.inf)
    p = jax.nn.softmax(s, axis=-1)
    a = p[:, :, 0] - lam * p[:, :, 1]
    return jnp.einsum('bhqk,bkhe->bqhe', a.astype(v.dtype), v)


def diff_attend_blocked(q, k, v, pos, bias_table, lam):
    B, L = q.shape[0], q.shape[1]
    bq = math.gcd(L, Q_BLOCK)
    nb = L // bq
    qb = q.reshape(B, nb, bq, H_B, 2, DH_B).swapaxes(0, 1)
    pb = pos.reshape(nb, bq)
    out = lax.map(lambda a: diff_attend(a[0], k, v, a[1], pos, bias_table, lam), (qb, pb))
    return out.swapaxes(0, 1).reshape(B, L, H_B, DV_B)


def rwkv7_scan(r, w, k, v, a_vec, b_vec, s0):
    def step(S, inp):
        rt, wt, kt, vt, at, bt = inp
        sa = jnp.einsum('bhij,bhj->bhi', S, at)
        S = S * wt[:, :, None, :] + sa[..., None] * bt[:, :, None, :] + vt[..., None] * kt[:, :, None, :]
        return S, jnp.einsum('bhij,bhj->bhi', S, rt)
    xs = tuple(t.swapaxes(0, 1) for t in (r, w, k, v, a_vec, b_vec))
    S, y = lax.scan(step, s0, xs)
    return y.swapaxes(0, 1), S


def trunk(x, c, pos, ret_s0, rwkv_s0, shift0, paged, p):
    B, L, _ = x.shape
    dt = x.dtype
    f32 = jnp.float32
    k_rows, v_rows, ret_out, rwkv_out, shift_out = [], [], [], [], []
    v_first = None
    for l in range(DEPTH):
        mod = jnp.einsum('bd,de->be', jax.nn.silu(c), p['w_ada'][l]) + p['b_ada'][l]
        sh1, sc1, gt1, sh2, sc2, gt2 = jnp.split(mod[:, None, :], 6, axis=-1)
        h = rms_norm(x, p['norm1'][l]) * (1 + sc1) + sh1
        z = jnp.einsum('bld,de->ble', h, p['w_in'][l])
        z_ret, z_diff, z_rwkv, z_gate = jnp.split(
            z, [RET_COLS, RET_COLS + DIFF_COLS, RET_COLS + DIFF_COLS + RWKV_COLS], axis=-1)

        qa, ka, va, ga = jnp.split(z_ret, [W_AQ, 2 * W_AQ, 2 * W_AQ + W_A], axis=-1)
        qa = rope(qa.reshape(B, L, H_A, DK_A).astype(f32), pos)
        ka = rope(ka.reshape(B, L, H_A, DK_A).astype(f32), pos) * (DK_A ** -0.5)
        oa, s_ret = retention_chunked(qa, ka, va.reshape(B, L, H_A, DV_A).astype(f32), ret_s0[l].astype(f32))
        ya = rms_norm(oa).reshape(B, L, W_A).astype(dt) * jax.nn.silu(ga)

        qb, kb, vb = jnp.split(z_diff, [W_BQK, 2 * W_BQK], axis=-1)
        qb = rms_norm(qb.reshape(B, L, H_B, 2, DH_B), p['qk_norm_q'][l]) * (DH_B ** -0.5)
        kb = rms_norm(kb.reshape(B, L, H_B, 2, DH_B), p['qk_norm_k'][l])
        vb = vb.reshape(B, L, H_B, DV_B)
        lam_init = 0.8 - 0.6 * math.exp(-0.3 * l)
        lam = (jnp.exp(jnp.sum(p['lambda_q1'][l].astype(f32) * p['lambda_k1'][l].astype(f32)))
               - jnp.exp(jnp.sum(p['lambda_q2'][l].astype(f32) * p['lambda_k2'][l].astype(f32))) + lam_init)
        if paged is None:
            ob = diff_attend_blocked(qb, kb, vb, pos, p['rel_bias'], lam)
        else:
            ck, cv, pt = paged
            nbd, npg = pt.shape
            k_past = ck[l, pt].reshape(nbd, npg * PAGE_SIZE, H_B, 2, DH_B).astype(dt)
            v_past = cv[l, pt].reshape(nbd, npg * PAGE_SIZE, H_B, DV_B).astype(dt)
            k_all = jnp.concatenate([k_past, kb], axis=1)
            v_all = jnp.concatenate([v_past, vb], axis=1)
            k_pos = jnp.arange(npg * PAGE_SIZE + L)
            ob = diff_attend(qb, k_all, v_all, pos, k_pos, p['rel_bias'], lam)
        yb = (rms_norm(ob, p['subln_diff'][l]) * (1.0 - lam_init)).reshape(B, L, W_B)
        k_rows.append(kb.reshape(B, L, H_B, 2 * DH_B))
        v_rows.append(vb)

        zprev = jnp.concatenate([shift0[l].astype(dt), z_rwkv[:, :-1]], axis=1)
        zs = z_rwkv + (zprev - z_rwkv) * p['mu_shift'][l]
        shift_out.append(z_rwkv[:, -1:])
        rc, kc, vc, wd, ad, gd = jnp.split(
            zs, [W_C, 2 * W_C, 3 * W_C, 3 * W_C + R_DECAY, 3 * W_C + R_DECAY + R_A], axis=-1)
        w = -jax.nn.softplus(-(p['w0'][l] + jnp.tanh(wd) @ p['w_decay_up'][l]).astype(f32)) - 0.5
        decay = jnp.exp(-jnp.exp(w))
        if l == 0:
            v_first = vc
        else:
            vgate = jax.nn.sigmoid(p['v0'][l - 1] + (vc @ p['w_v_down'][l - 1]) @ p['w_v_up'][l - 1])
            vc = vc + (v_first - vc) * vgate
        a = jax.nn.sigmoid((p['a0'][l] + ad @ p['w_a_up'][l]).astype(f32))
        gg = jax.nn.sigmoid(gd) @ p['w_g_up'][l]
        kk = (kc * p['k_k'][l]).astype(f32).reshape(B, L, H_C, DH_C)
        kk = kk / jnp.maximum(jnp.sqrt(jnp.sum(kk * kk, axis=-1, keepdims=True)), 1e-12)
        kc = kc.astype(f32) * (1.0 + (a - 1.0) * p['k_a'][l].astype(f32))
        hs = (B, L, H_C, DH_C)
        r_h = rc.astype(f32).reshape(hs)
        k_h = kc.reshape(hs)
        v_h = vc.astype(f32).reshape(hs)
        a_h = a.reshape(hs)
        yc, s_rwkv = rwkv7_scan(r_h, decay.reshape(hs), k_h, v_h, -kk, kk * a_h, rwkv_s0[l].astype(f32))
        mu_c = jnp.mean(yc, axis=-1, keepdims=True)
        var_c = jnp.mean(jnp.square(yc - mu_c), axis=-1, keepdims=True)
        yc = ((yc - mu_c) * lax.rsqrt(var_c + LNX_EPS)).reshape(B, L, W_C) * p['lnx_g'][l] + p['lnx_b'][l]
        yc = yc + (jnp.sum(r_h * k_h * p['r_k'][l].astype(f32), axis=-1, keepdims=True) * v_h).reshape(B, L, W_C)
        yc = yc.astype(dt) * gg

        gate_a, gate_b, gate_c = jnp.split(jax.nn.sigmoid(z_gate), 3, axis=-1)
        merged = (gate_a * (ya @ p['w_up_a'][l]) + gate_b * (yb @ p['w_up_b'][l])
                  + gate_c * (yc @ p['w_up_c'][l]))
        x = x + gt1 * (merged @ p['w_out'][l])

        h2 = rms_norm(x, p['norm2'][l]) * (1 + sc2) + sh2
        x = x + gt2 * (jnp.square(jax.nn.relu(h2 @ p['w_ff1'][l])) @ p['w_ff2'][l])

        ret_out.append(s_ret.astype(dt))
        rwkv_out.append(s_rwkv.astype(dt))
    return (x, jnp.stack(k_rows), jnp.stack(v_rows), jnp.stack(ret_out),
            jnp.stack(rwkv_out), jnp.stack(shift_out))


def setup_inputs(seed: int = 0) -> dict:
    key = jax.random.key(seed)
    keys = jax.random.split(key, 64)
    ctr = [0]
    f32 = jnp.float32

    def nk():
        ctr[0] += 1
        return keys[ctr[0] - 1]

    def nrm(shape, scale=1.0):
        return jax.random.normal(nk(), shape, f32) * scale

    def gain(shape):
        return 1.0 + 0.02 * jax.random.normal(nk(), shape, f32)

    D = D_MODEL
    n_pages = PAST_LEN // PAGE_SIZE
    n_pool = (DEC_BATCH * n_pages * 5) // 4
    x_prompt = nrm((BATCH, SEQ, D))
    x_sample = nrm((DEC_BATCH, DEC_SEQ, D))
    c_prompt = nrm((BATCH, D))
    c_sample = nrm((DEC_BATCH, D))
    cache_k_diff = nrm((DEPTH, n_pool, PAGE_SIZE, H_B, 2 * DH_B))
    cache_v_diff = nrm((DEPTH, n_pool, PAGE_SIZE, H_B, DV_B))
    page_table = jax.random.permutation(nk(), n_pool)[:DEC_BATCH * n_pages].reshape(
        DEC_BATCH, n_pages).astype(jnp.int32)
    state_ret = nrm((DEPTH, DEC_BATCH, H_A, DK_A, DV_A), 0.3)
    state_rwkv = nrm((DEPTH, DEC_BATCH, H_C, DH_C, DH_C), 0.1)
    state_shift = nrm((DEPTH, DEC_BATCH, 1, RWKV_COLS))
    return {
        'x_prompt': x_prompt, 'x_sample': x_sample, 'c_prompt': c_prompt, 'c_sample': c_sample,
        'cache_k_diff': cache_k_diff, 'cache_v_diff': cache_v_diff, 'page_table': page_table,
        'state_ret': state_ret, 'state_rwkv': state_rwkv, 'state_shift': state_shift,
        'rel_bias': nrm((N_BUCKETS, H_B), 0.5),
        'w_ada': nrm((DEPTH, D, 6 * D), 0.5 * D ** -0.5),
        'b_ada': nrm((DEPTH, 6 * D), 0.02),
        'norm1': gain((DEPTH, D)),
        'norm2': gain((DEPTH, D)),
        'w_in': nrm((DEPTH, D, IN_COLS), D ** -0.5),
        'qk_norm_q': gain((DEPTH, DH_B)),
        'qk_norm_k': gain((DEPTH, DH_B)),
        'lambda_q1': nrm((DEPTH, DH_B), 0.1),
        'lambda_k1': nrm((DEPTH, DH_B), 0.1),
        'lambda_q2': nrm((DEPTH, DH_B), 0.1),
        'lambda_k2': nrm((DEPTH, DH_B), 0.1),
        'subln_diff': gain((DEPTH, DV_B)),
        'mu_shift': jax.random.uniform(nk(), (DEPTH, RWKV_COLS), f32, 0.0, 1.0),
        'w0': jax.random.uniform(nk(), (DEPTH, W_C), f32, -6.0, -1.0),
        'w_decay_up': nrm((DEPTH, R_DECAY, W_C), 0.1),
        'a0': nrm((DEPTH, W_C), 0.1),
        'w_a_up': nrm((DEPTH, R_A, W_C), 0.1),
        'w_g_up': nrm((DEPTH, R_G, W_C), R_G ** -0.5),
        'v0': nrm((DEPTH - 1, W_C), 0.1),
        'w_v_down': nrm((DEPTH - 1, W_C, R_V), W_C ** -0.5),
        'w_v_up': nrm((DEPTH - 1, R_V, W_C), 0.1),
        'k_k': 0.85 + 0.02 * jax.random.normal(nk(), (DEPTH, W_C), f32),
        'k_a': gain((DEPTH, W_C)),
        'r_k': nrm((DEPTH, H_C, DH_C), 0.1),
        'lnx_g': gain((DEPTH, W_C)),
        'lnx_b': nrm((DEPTH, W_C), 0.02),
        'w_up_a': nrm((DEPTH, W_A, D), W_A ** -0.5),
        'w_up_b': nrm((DEPTH, W_B, D), W_B ** -0.5),
        'w_up_c': nrm((DEPTH, W_C, D), W_C ** -0.5),
        'w_out': nrm((DEPTH, D, D), D ** -0.5),
        'w_ff1': nrm((DEPTH, D, D_FF), D ** -0.5),
        'w_ff2': nrm((DEPTH, D_FF, D), D_FF ** -0.5),
    }


def reference(x_prompt, x_sample, c_prompt, c_sample, cache_k_diff, cache_v_diff, page_table,
              state_ret, state_rwkv, state_shift, rel_bias, w_ada, b_ada, norm1, norm2, w_in,
              qk_norm_q, qk_norm_k, lambda_q1, lambda_k1, lambda_q2, lambda_k2, subln_diff,
              mu_shift, w0, w_decay_up, a0, w_a_up, w_g_up, v0, w_v_down, w_v_up, k_k, k_a, r_k,
              lnx_g, lnx_b, w_up_a, w_up_b, w_up_c, w_out, w_ff1, w_ff2):
    p = dict(rel_bias=rel_bias, w_ada=w_ada, b_ada=b_ada, norm1=norm1, norm2=norm2, w_in=w_in,
             qk_norm_q=qk_norm_q, qk_norm_k=qk_norm_k, lambda_q1=lambda_q1, lambda_k1=lambda_k1,
             lambda_q2=lambda_q2, lambda_k2=lambda_k2, subln_diff=subln_diff, mu_shift=mu_shift,
             w0=w0, w_decay_up=w_decay_up, a0=a0, w_a_up=w_a_up, w_g_up=w_g_up, v0=v0,
             w_v_down=w_v_down, w_v_up=w_v_up, k_k=k_k, k_a=k_a, r_k=r_k, lnx_g=lnx_g, lnx_b=lnx_b,
             w_up_a=w_up_a, w_up_b=w_up_b, w_up_c=w_up_c, w_out=w_out, w_ff1=w_ff1, w_ff2=w_ff2)
    bp, lp = x_prompt.shape[0], x_prompt.shape[1]
    dtp = x_prompt.dtype
    pos_p = jnp.arange(lp)
    past = page_table.shape[1] * PAGE_SIZE
    pos_s = past + jnp.arange(x_sample.shape[1])
    ret0 = jnp.zeros((DEPTH, bp, H_A, DK_A, DV_A), dtp)
    rwkv0 = jnp.zeros((DEPTH, bp, H_C, DH_C, DH_C), dtp)
    shift_init = jnp.zeros((DEPTH, bp, 1, RWKV_COLS), dtp)
    y_prompt, k_p, v_p, ret_p, rwkv_p, shift_p = trunk(
        x_prompt, c_prompt, pos_p, ret0, rwkv0, shift_init, None, p)
    y_sample, k_s, v_s, ret_s, rwkv_s, shift_s = trunk(
        x_sample, c_sample, pos_s, state_ret, state_rwkv, state_shift,
        (cache_k_diff, cache_v_diff, page_table), p)
    return (y_prompt, y_sample, k_p, v_p, k_s, v_s, ret_p, ret_s, rwkv_p, rwkv_s, shift_p, shift_s)
```

```python
import functools
import math

import numpy as np
import jax
import jax.numpy as jnp
from jax import lax
from jax.experimental import pallas as pl
from jax.experimental.pallas import tpu as pltpu

F32 = jnp.float32
BF16 = jnp.bfloat16
HIGHEST = lax.Precision.HIGHEST

H_A, DK_A, DV_A = 4, 128, 128
RET_CHUNK = 128
ROPE_BASE = 10000.0
H_B, DH_B, DV_B = 4, 64, 128
N_BUCKETS, MAX_DISTANCE = 32, 128
H_C, DH_C = 8, 64
R_DECAY, R_A, R_V, R_G = 64, 64, 32, 160
LNX_EPS = 64e-5
EPS = 1e-6
PAGE_SIZE = 128
W_AQ, W_A = H_A * DK_A, H_A * DV_A
W_BQK, W_B = H_B * 2 * DH_B, H_B * DV_B
W_C = H_C * DH_C
RET_COLS = 2 * W_AQ + 2 * W_A
DIFF_COLS = 2 * W_BQK + W_B
RWKV_COLS = 3 * W_C + R_DECAY + R_A + R_G
RWKV_PAD = 2048
LR_COLS = RWKV_PAD - 3 * W_C

Z_RET, Z_RWKV = 0, RET_COLS
Z_GATE = Z_RWKV + RWKV_PAD
NEG = -1e30
RW_CHUNK = 64
VMEM_LIMIT = 56 * 1024 * 1024


def _cp(*sem):
    return pltpu.CompilerParams(dimension_semantics=sem, vmem_limit_bytes=VMEM_LIMIT)


def _silu(x):
    return x * jax.nn.sigmoid(x)


def _split2(x):
    hi = x.astype(BF16)
    lo = (x - hi.astype(F32)).astype(BF16)
    return hi, lo


def _seg_sum(x, ones2):
    hi, lo = _split2(x)
    return jnp.dot(jnp.concatenate([hi, lo], axis=1), ones2, preferred_element_type=F32)


def _seg_sum_wide(x, ones2):
    return jnp.concatenate(
        [_seg_sum(x[:, c * 128:(c + 1) * 128], ones2) for c in range(x.shape[1] // 128)], axis=1)


def _mod_kernel(c_ref, w_ref, b_ref, o_ref):
    s = _silu(c_ref[...])
    o_ref[...] = jnp.dot(s.astype(BF16), w_ref[...], preferred_element_type=F32) + b_ref[...]


def ada_mod(c_pad, w_bf, b_row):
    R, D = c_pad.shape
    N = w_bf.shape[1]
    tn = 1536
    return pl.pallas_call(
        _mod_kernel,
        grid=(N // tn,),
        in_specs=[pl.BlockSpec((R, D), lambda n: (0, 0)),
                  pl.BlockSpec((D, tn), lambda n: (0, n)),
                  pl.BlockSpec((1, tn), lambda n: (0, n))],
        out_specs=pl.BlockSpec((R, tn), lambda n: (0, n)),
        out_shape=jax.ShapeDtypeStruct((R, N), F32),
        compiler_params=_cp("parallel"),
        name="ada_mod",
    )(c_pad, w_bf, b_row)


def _inproj_kernel(x_ref, g_ref, sc_ref, sh_ref, w_ref, z_ref, h_scr):
    @pl.when(pl.program_id(1) == 0)
    def _():
        x = x_ref[...]
        y = x * lax.rsqrt(jnp.mean(x * x, axis=-1, keepdims=True) + EPS) * g_ref[...]
        h_scr[...] = (y * (1.0 + sc_ref[...]) + sh_ref[...]).astype(BF16)

    z_ref[...] = jnp.dot(h_scr[...], w_ref[...], preferred_element_type=F32)


def in_proj(x2, gain_row, sc, sh, w_bf, tm, tn):
    T, D = x2.shape
    N = w_bf.shape[1]
    G, R, _ = sc.shape
    tiles_per_group = (T // tm) // G
    mod_spec = pl.BlockSpec((None, R, D), lambda m, n: (m // tiles_per_group, 0, 0))
    return pl.pallas_call(
        _inproj_kernel,
        grid=(T // tm, N // tn),
        in_specs=[pl.BlockSpec((tm, D), lambda m, n: (m, 0)),
                  pl.BlockSpec((1, D), lambda m, n: (0, 0)),
                  mod_spec, mod_spec,
                  pl.BlockSpec((D, tn), lambda m, n: (0, n))],
        out_specs=pl.BlockSpec((tm, tn), lambda m, n: (m, n)),
        out_shape=jax.ShapeDtypeStruct((T, N), F32),
        scratch_shapes=[pltpu.VMEM((tm, D), BF16)],
        compiler_params=_cp("parallel", "arbitrary"),
        name="in_proj",
    )(x2, gain_row, sc, sh, w_bf)


def _rope(x, cos, sin_signed):
    return x * cos + pltpu.roll(x, DK_A // 2, 1) * sin_signed


def _ret_kernel(q_ref, k_ref, v_ref, g_ref, cos_ref, sin_ref, inner_ref, cross_ref, tail_ref, cd_ref,
                ya_ref, so_ref, s_scr, *, n_sub):
    lt = pl.program_id(1)

    @pl.when(lt == 0)
    def _():
        s_scr[...] = jnp.zeros_like(s_scr)

    C = RET_CHUNK
    for c in range(n_sub):
        rows = slice(c * C, (c + 1) * C)
        cos = cos_ref[rows, :]
        sin = sin_ref[rows, :]
        for h in range(H_A):
            cols = slice(h * DK_A, (h + 1) * DK_A)
            q = _rope(q_ref[rows, cols], cos, sin)
            k = _rope(k_ref[rows, cols], cos, sin) * (DK_A ** -0.5)
            qb, kb, vb = q.astype(BF16), k.astype(BF16), v_ref[rows, cols].astype(BF16)
            S = s_scr[h]
            sc = lax.dot_general(qb, kb, (((1,), (1,)), ((), ())), preferred_element_type=F32) * inner_ref[h]
            o = (jnp.dot(sc.astype(BF16), vb, preferred_element_type=F32)
                 + jnp.dot(qb, S.astype(BF16), preferred_element_type=F32) * cross_ref[h])
            kt = (k * tail_ref[h]).T.astype(BF16)
            s_scr[h] = S * cd_ref[h] + jnp.dot(kt, vb, preferred_element_type=F32)
            on = o * lax.rsqrt(jnp.mean(o * o, axis=-1, keepdims=True) + EPS)
            ya_ref[rows, cols] = on * _silu(g_ref[rows, cols])

    @pl.when(lt == pl.num_programs(1) - 1)
    def _():
        so_ref[...] = s_scr[...]


def _ret_tables(L):
    C = math.gcd(L, RET_CHUNK)
    log_g = np.log1p(-np.exp2(-5.0 - np.arange(H_A, dtype=np.float32))).astype(np.float32)
    i = np.arange(C, dtype=np.float32)
    dist = i[:, None] - i[None, :]
    causal = dist >= 0
    inner = np.where(causal[None], np.exp(np.where(causal, dist, 0.0)[None] * log_g[:, None, None]), 0.0)
    cross = np.exp((i[None, :] + 1.0) * log_g[:, None])
    tail = np.exp((C - 1.0 - i)[None, :] * log_g[:, None])
    chunk = np.exp(C * log_g)
    bc = lambda t: np.broadcast_to(t[:, :, None], (H_A, C, 128)).astype(np.float32)
    cd = np.broadcast_to(chunk[:, None, None], (H_A, 1, 128)).astype(np.float32)
    return inner.astype(np.float32), bc(cross), bc(tail), cd


def _rope_tables(pos):
    half = DK_A // 2
    inv = ROPE_BASE ** (-jnp.arange(half, dtype=F32) / half)
    ang = pos.astype(F32)[:, None] * inv[None, :]
    cos, sin = jnp.cos(ang), jnp.sin(ang)
    return jnp.concatenate([cos, cos], axis=-1), jnp.concatenate([-sin, sin], axis=-1)


def retention_prompt(z3, cos_t, sin_t, tb):
    B, L, ZC = z3.shape
    inner, cross, tail, cd = _ret_tables(L)
    zspec = lambda idx: pl.BlockSpec((None, tb, W_AQ), lambda b, l: (b, l, idx))
    full3 = lambda s: pl.BlockSpec(s, lambda b, l: (0, 0, 0))
    return pl.pallas_call(
        functools.partial(_ret_kernel, n_sub=tb // RET_CHUNK),
        grid=(B, L // tb),
        in_specs=[zspec(0), zspec(1), zspec(2), zspec(3),
                  pl.BlockSpec((tb, 128), lambda b, l: (l, 0)),
                  pl.BlockSpec((tb, 128), lambda b, l: (l, 0)),
                  full3(inner.shape), full3(cross.shape), full3(tail.shape), full3(cd.shape)],
        out_specs=[pl.BlockSpec((None, tb, W_A), lambda b, l: (b, l, 0)),
                   pl.BlockSpec((None, H_A, DK_A, DV_A), lambda b, l: (b, 0, 0, 0))],
        out_shape=[jax.ShapeDtypeStruct((B, L, W_A), F32),
                   jax.ShapeDtypeStruct((B, H_A, DK_A, DV_A), F32)],
        scratch_shapes=[pltpu.VMEM((H_A, DK_A, DV_A), F32)],
        compiler_params=_cp("parallel", "arbitrary"),
        name="retention_prompt",
    )(z3, z3, z3, z3, cos_t, sin_t, inner, cross, tail, cd)


def _ret_step_kernel(q_ref, k_ref, v_ref, g_ref, cos_ref, sin_ref, gam_ref, s_ref, ya_ref, so_ref, *, bb):
    cos, sin = cos_ref[...], sin_ref[...]
    row = lax.broadcasted_iota(jnp.int32, (bb, 128), 0)
    for h in range(H_A):
        cols = slice(h * DK_A, (h + 1) * DK_A)
        gam = gam_ref[h]
        q = _rope(q_ref[:, cols], cos, sin)
        k = _rope(k_ref[:, cols], cos, sin) * (DK_A ** -0.5)
        v = v_ref[:, cols]
        qb = q.astype(BF16)
        qk = jnp.sum(qb.astype(F32) * k.astype(BF16).astype(F32), axis=-1, keepdims=True)
        qs = jnp.zeros((bb, 128), F32)
        for b in range(bb):
            S = s_ref[b, h]
            onehot = row == b
            qs = jnp.where(onehot, jnp.dot(qb, S.astype(BF16), preferred_element_type=F32), qs)
            kb_t = jnp.where(onehot, k, 0.0).T.astype(BF16)
            so_ref[b, h] = S * gam + jnp.dot(kb_t, v.astype(BF16), preferred_element_type=F32)
        o = qk * v.astype(BF16).astype(F32) + qs * gam
        on = o * lax.rsqrt(jnp.mean(o * o, axis=-1, keepdims=True) + EPS)
        ya_ref[:, cols] = on * _silu(g_ref[:, cols])


def retention_step(z2, state, layer, pos):
    B = z2.shape[0]
    bb = 8
    cos_t, sin_t = _rope_tables(jnp.full((1,), pos))
    log_g = np.log1p(-np.exp2(-5.0 - np.arange(H_A, dtype=np.float32))).astype(np.float32)
    gam = np.broadcast_to(np.exp(log_g)[:, None, None], (H_A, 1, 128)).astype(np.float32)
    zspec = lambda idx: pl.BlockSpec((bb, W_AQ), lambda i: (i, idx))
    return pl.pallas_call(
        functools.partial(_ret_step_kernel, bb=bb),
        grid=(B // bb,),
        in_specs=[zspec(0), zspec(1), zspec(2), zspec(3),
                  pl.BlockSpec((1, 128), lambda i: (0, 0)),
                  pl.BlockSpec((1, 128), lambda i: (0, 0)),
                  pl.BlockSpec((H_A, 1, 128), lambda i: (0, 0, 0)),
                  pl.BlockSpec((None, bb, H_A, DK_A, DV_A), lambda i: (layer, i, 0, 0, 0))],
        out_specs=[pl.BlockSpec((bb, W_A), lambda i: (i, 0)),
                   pl.BlockSpec((bb, H_A, DK_A, DV_A), lambda i: (i, 0, 0, 0))],
        out_shape=[jax.ShapeDtypeStruct((B, W_A), F32),
                   jax.ShapeDtypeStruct((B, H_A, DK_A, DV_A), F32)],
        compiler_params=_cp("parallel"),
        name="retention_step",
    )(z2, z2, z2, z2, cos_t, sin_t, gam, state)


def _qknorm_kernel(q_ref, k_ref, v_ref, gq_ref, gk_ref, ones_ref, qb_ref, k32_ref, kb_ref, v32_ref, vb_ref):
    ones2 = ones_ref[...]
    q, k, v = q_ref[...], k_ref[...], v_ref[...]
    qn = q * lax.rsqrt(_seg_sum_wide(q * q, ones2) * (1.0 / DH_B) + EPS) * gq_ref[...] * (DH_B ** -0.5)
    kn = k * lax.rsqrt(_seg_sum_wide(k * k, ones2) * (1.0 / DH_B) + EPS) * gk_ref[...]
    qb_ref[...] = qn.astype(BF16)
    k32_ref[...] = kn
    kb_ref[...] = kn.astype(BF16)
    v32_ref[...] = v
    vb_ref[...] = v.astype(BF16)


def _block_ones2(group):
    i = np.arange(128)
    m = (i[:, None] // group == i[None, :] // group).astype(np.float32)
    return jnp.asarray(np.concatenate([m, m], axis=0), BF16)


def qk_norm(z2, gq_row, gk_row, tm):
    T = z2.shape[0]
    base = (Z_GATE + 3 * 1024) // W_BQK
    zspec = lambda idx: pl.BlockSpec((tm, W_BQK), lambda m: (m, base + idx))
    row = pl.BlockSpec((1, W_BQK), lambda m: (0, 0))
    out = pl.BlockSpec((tm, W_BQK), lambda m: (m, 0))
    return pl.pallas_call(
        _qknorm_kernel,
        grid=(T // tm,),
        in_specs=[zspec(0), zspec(1), zspec(2), row, row, pl.BlockSpec((256, 128), lambda m: (0, 0))],
        out_specs=[out] * 5,
        out_shape=[jax.ShapeDtypeStruct((T, W_BQK), BF16), jax.ShapeDtypeStruct((T, W_BQK), F32),
                   jax.ShapeDtypeStruct((T, W_BQK), BF16), jax.ShapeDtypeStruct((T, W_B), F32),
                   jax.ShapeDtypeStruct((T, W_B), BF16)],
        compiler_params=_cp("parallel"),
        name="qk_norm",
    )(z2, z2, z2, gq_row, gk_row, _block_ones2(DH_B))


def _bucket_np(n):
    max_exact = N_BUCKETS // 2
    nf = np.maximum(n, 1).astype(np.float32)
    large = max_exact + (np.log(nf / np.float32(max_exact)) / np.float32(math.log(MAX_DISTANCE / max_exact))
                         * np.float32(N_BUCKETS - max_exact)).astype(np.int32)
    large = np.minimum(large, N_BUCKETS - 1)
    return np.where(n < max_exact, n, large).astype(np.int32)


def _bias_kernel(tab_ref, bkt_ref, o_ref):
    h = pl.program_id(0)
    bk = bkt_ref[...]
    acc = jnp.full(bk.shape, NEG, F32)
    for b in range(N_BUCKETS):
        acc = jnp.where(bk == b, tab_ref[b, h], acc)
    o_ref[...] = acc


def bias_tiles(rel_bias, buckets):
    R, C = buckets.shape
    tr = min(R, 512)
    return pl.pallas_call(
        _bias_kernel,
        grid=(H_B, R // tr),
        in_specs=[pl.BlockSpec(memory_space=pltpu.SMEM),
                  pl.BlockSpec((tr, C), lambda h, r: (r, 0))],
        out_specs=pl.BlockSpec((None, tr, C), lambda h, r: (h, r, 0)),
        out_shape=jax.ShapeDtypeStruct((H_B, R, C), F32),
        compiler_params=_cp("parallel", "parallel"),
        name="bias_tiles",
    )(rel_bias, jnp.asarray(buckets))


def _prompt_buckets(t):
    r = np.arange(t)[:, None]
    c = np.arange(t)[None, :]
    diag = np.where(c <= r, _bucket_np(np.maximum(r - c, 0)), -1)
    off1 = _bucket_np(t + r - c)
    far = _bucket_np(np.full((t, t), 2 * t))
    assert t >= MAX_DISTANCE
    return np.concatenate([diag, off1, far], axis=0).astype(np.int32)


def _lambda_full(lam_ref, lam_init):
    lv = lam_ref[...]
    s1 = jnp.sum(lv[0:1] * lv[1:2], axis=-1, keepdims=True)
    s2 = jnp.sum(lv[2:3] * lv[3:4], axis=-1, keepdims=True)
    return jnp.exp(s1) - jnp.exp(s2) + lam_init


def _flash_kernel(q_ref, k_ref, v_ref, bias_ref, lam_ref, gain_ref, o_ref, qq_scr, m_scr, l_scr, acc_scr,
                  *, lam_init):
    i, j = pl.program_id(2), pl.program_id(3)

    @pl.when(j == 0)
    def _():
        q = q_ref[...]
        first = lax.broadcasted_iota(jnp.int32, q.shape, 1) < DH_B
        qq_scr[0] = jnp.where(first, q, jnp.zeros_like(q))
        qq_scr[1] = jnp.where(first, jnp.zeros_like(q), q)
        m_scr[...] = jnp.full(m_scr.shape, NEG, F32)
        l_scr[...] = jnp.zeros_like(l_scr)
        acc_scr[...] = jnp.zeros_like(acc_scr)

    @pl.when(j <= i)
    def _():
        k, v, bias = k_ref[...], v_ref[...], bias_ref[...]
        for c in range(2):
            s = lax.dot_general(qq_scr[c], k, (((1,), (1,)), ((), ())), preferred_element_type=F32) + bias
            m_old = m_scr[c]
            m_new = jnp.maximum(m_old, jnp.max(s, axis=-1, keepdims=True))
            alpha = jnp.exp(m_old - m_new)
            p = jnp.exp(s - m_new)
            l_scr[c] = alpha * l_scr[c] + jnp.sum(p, axis=-1, keepdims=True)
            acc_scr[c] = alpha * acc_scr[c] + jnp.dot(p.astype(BF16), v, preferred_element_type=F32)
            m_scr[c] = m_new

    @pl.when(j == i)
    def _():
        lam = _lambda_full(lam_ref, lam_init)
        ob = acc_scr[0] / l_scr[0] - lam * (acc_scr[1] / l_scr[1])
        on = ob * lax.rsqrt(jnp.mean(ob * ob, axis=-1, keepdims=True) + EPS) * gain_ref[...]
        o_ref[...] = on * (1.0 - lam_init)


def diff_attention_prompt(qb, kb, vb, bias, lam_rows, gain_row, lam_init, t):
    B, L, _ = qb.shape
    n = L // t
    kv_spec = pl.BlockSpec((None, t, DV_B), lambda b, h, i, j: (b, jnp.minimum(j, i), h))
    return pl.pallas_call(
        functools.partial(_flash_kernel, lam_init=lam_init),
        grid=(B, H_B, n, n),
        in_specs=[pl.BlockSpec((None, t, DV_B), lambda b, h, i, j: (b, i, h)),
                  kv_spec, kv_spec,
                  pl.BlockSpec((None, t, t), lambda b, h, i, j: (h, jnp.clip(i - j, 0, 2), 0)),
                  pl.BlockSpec((4, DH_B), lambda b, h, i, j: (0, 0)),
                  pl.BlockSpec((1, DV_B), lambda b, h, i, j: (0, 0))],
        out_specs=pl.BlockSpec((None, t, DV_B), lambda b, h, i, j: (b, i, h)),
        out_shape=jax.ShapeDtypeStruct((B, L, W_B), F32),
        scratch_shapes=[pltpu.VMEM((2, t, DV_B), BF16), pltpu.VMEM((2, t, 1), F32),
                        pltpu.VMEM((2, t, 1), F32), pltpu.VMEM((2, t, DV_B), F32)],
        compiler_params=_cp("parallel", "parallel", "parallel", "arbitrary"),
        name="diff_attention_prompt",
    )(qb, kb, vb, bias, lam_rows, gain_row)


def _decode_kernel(pt_ref, q_ref, kn_ref, vn_ref, bias_ref, qmask_ref, w1_ref, w2_ref, lam_ref, gain_ref, *rest,
                   n_pages, lam_init):
    k_refs, v_refs, o_ref = rest[:n_pages], rest[n_pages:2 * n_pages], rest[2 * n_pages]
    qrows = jnp.where(qmask_ref[...] > 0, q_ref[...].astype(F32), 0.0).astype(BF16)
    row0 = lax.broadcasted_iota(jnp.int32, (PAGE_SIZE, W_BQK), 0) == 0
    k_new = jnp.where(row0, kn_ref[...], 0.0).astype(BF16)
    v_new = jnp.where(row0, vn_ref[...], 0.0).astype(BF16)
    ks = [r[...].astype(BF16) for r in k_refs] + [k_new]
    vs = [r[...].astype(BF16) for r in v_refs] + [v_new]
    ss = [lax.dot_general(qrows, kp, (((1,), (1,)), ((), ())), preferred_element_type=F32) + bias_ref[p]
          for p, kp in enumerate(ks)]
    m = functools.reduce(jnp.maximum, [jnp.max(s, axis=-1, keepdims=True) for s in ss])
    ps = [jnp.exp(s - m) for s in ss]
    l = functools.reduce(jnp.add, [jnp.sum(p, axis=-1, keepdims=True) for p in ps])
    acc = functools.reduce(jnp.add, [jnp.dot(p.astype(BF16), vp, preferred_element_type=F32)
                                     for p, vp in zip(ps, vs)])
    outn = acc / l
    lam = _lambda_full(lam_ref, lam_init)
    ob = (jnp.sum(outn * w1_ref[...], axis=0, keepdims=True)
          - lam * jnp.sum(outn * w2_ref[...], axis=0, keepdims=True))
    gain = gain_ref[...]
    for h in range(H_B):
        cols = slice(h * DV_B, (h + 1) * DV_B)
        oh = ob[:, cols]
        on = oh * lax.rsqrt(jnp.mean(oh * oh, axis=-1, keepdims=True) + EPS) * gain
        o_ref[:, cols] = on * (1.0 - lam_init)


def diff_attention_decode(qb, k32, v32, cache_k, cache_v, pt_flat, bias_dec, lam_rows, gain_row, layer,
                          lam_init, n_pages):
    B = qb.shape[0]
    r = np.arange(8)[:, None]
    lane = np.arange(W_BQK)[None, :]
    qmask = ((lane // DH_B) == r).astype(np.float32)
    w1 = ((lane // DV_B) * 2 == r).astype(np.float32)
    w2 = ((lane // DV_B) * 2 + 1 == r).astype(np.float32)
    rowspec = pl.BlockSpec((None, 1, W_BQK), lambda b, pt: (b, 0, 0))
    const = lambda s: pl.BlockSpec(s, lambda b, pt: tuple(0 for _ in s))
    page_spec = lambda p: pl.BlockSpec((None, None, PAGE_SIZE, W_BQK),
                                       lambda b, pt, p=p: (layer, pt[b * n_pages + p], 0, 0))
    grid_spec = pltpu.PrefetchScalarGridSpec(
        num_scalar_prefetch=1,
        grid=(B,),
        in_specs=[rowspec, rowspec, rowspec, const(bias_dec.shape), const((8, W_BQK)), const((8, W_BQK)),
                  const((8, W_BQK)), const((4, DH_B)), const((1, DV_B))]
                 + [page_spec(p) for p in range(n_pages)] * 2,
        out_specs=pl.BlockSpec((None, 1, W_B), lambda b, pt: (b, 0, 0)),
    )
    return pl.pallas_call(
        functools.partial(_decode_kernel, n_pages=n_pages, lam_init=lam_init),
        grid_spec=grid_spec,
        out_shape=jax.ShapeDtypeStruct((B, 1, W_B), F32),
        compiler_params=_cp("arbitrary"),
        name="diff_attention_decode",
    )(pt_flat, qb, k32, v32, bias_dec, qmask, w1, w2, lam_rows, gain_row,
      *([cache_k] * n_pages), *([cache_v] * n_pages))


def _rwkv_prep_kernel(*refs, carry_shift, gate_v):
    it = iter(refs)
    z_ref, prev_ref, mu_ref, w0_ref, a0_ref, kk_ref, ka_ref, wlr_ref, ones_ref = (next(it) for _ in range(9))
    if gate_v:
        v0_ref, wvd_ref, wvu_ref, vf_ref = (next(it) for _ in range(4))
    r_ref, lw_ref, k_ref, v_ref, a_ref, b_ref, gg_ref = (next(it) for _ in range(7))
    z = z_ref[...]
    if carry_shift:
        carry_scr = next(it)

        @pl.when(pl.program_id(1) == 0)
        def _():
            carry_scr[...] = prev_ref[...]

        first = lax.broadcasted_iota(jnp.int32, z.shape, 0) == 0
        zprev = jnp.where(first, carry_scr[...], pltpu.roll(z, 1, 0))
        carry_scr[...] = z_ref[z.shape[0] - 1:z.shape[0], :]
    else:
        zprev = prev_ref[...]
    zs = z + (zprev - z) * mu_ref[...]
    rc, kc, vc = zs[:, :W_C], zs[:, W_C:2 * W_C], zs[:, 2 * W_C:3 * W_C]
    lr = zs[:, 3 * W_C:]
    col = lax.broadcasted_iota(jnp.int32, lr.shape, 1)
    act = jnp.where(col < R_DECAY, jnp.tanh(lr), jnp.where(col < R_DECAY + R_A, lr, jax.nn.sigmoid(lr)))
    up = jnp.dot(act.astype(BF16), wlr_ref[...], preferred_element_type=F32)
    y = -(w0_ref[...] + up[:, :W_C])
    softplus = jnp.maximum(y, 0.0) + jnp.log(1.0 + jnp.exp(-jnp.abs(y)))
    lw_ref[...] = -jnp.exp(-softplus - 0.5)
    if gate_v:
        down = jnp.dot(vc.astype(BF16), wvd_ref[...], preferred_element_type=F32)
        vgate = jax.nn.sigmoid(v0_ref[...] + jnp.dot(down.astype(BF16), wvu_ref[...], preferred_element_type=F32))
        vc = vc + (vf_ref[...] - vc) * vgate
    a = jax.nn.sigmoid(a0_ref[...] + up[:, W_C:2 * W_C])
    gg_ref[...] = up[:, 2 * W_C:]
    kk = kc * kk_ref[...]
    norm = jnp.sqrt(_seg_sum_wide(kk * kk, ones_ref[...]))
    kk = kk / jnp.maximum(norm, 1e-12)
    r_ref[...] = rc
    k_ref[...] = kc * (1.0 + (a - 1.0) * ka_ref[...])
    v_ref[...] = vc
    a_ref[...] = -kk
    b_ref[...] = kk * a


def rwkv_prep(z3, prev, mu_row, w0, a0, k_k, k_a, w_lr, vgate, tm, carry_shift):
    G, Lg, _ = z3.shape
    blk = lambda w, idx: pl.BlockSpec((None, tm, w), lambda g, l: (g, l, idx))
    row = lambda w: pl.BlockSpec((1, w), lambda g, l: (0, 0))
    prev_spec = (pl.BlockSpec((None, 1, RWKV_PAD), lambda g, l: (g, 0, 0)) if carry_shift
                 else blk(RWKV_PAD, 0))
    in_specs = [blk(RWKV_PAD, Z_RWKV // RWKV_PAD), prev_spec, row(RWKV_PAD), row(W_C), row(W_C), row(W_C),
                row(W_C), pl.BlockSpec(w_lr.shape, lambda g, l: (0, 0)),
                pl.BlockSpec((256, 128), lambda g, l: (0, 0))]
    args = [z3, prev, mu_row, w0, a0, k_k, k_a, w_lr, _block_ones2(DH_C)]
    if vgate is not None:
        v0, wvd, wvu, vfirst = vgate
        in_specs += [row(W_C), pl.BlockSpec(wvd.shape, lambda g, l: (0, 0)),
                     pl.BlockSpec(wvu.shape, lambda g, l: (0, 0)), blk(W_C, 0)]
        args += [v0, wvd, wvu, vfirst]
    return pl.pallas_call(
        functools.partial(_rwkv_prep_kernel, carry_shift=carry_shift, gate_v=vgate is not None),
        grid=(G, Lg // tm),
        in_specs=in_specs,
        out_specs=[blk(W_C, 0)] * 7,
        out_shape=[jax.ShapeDtypeStruct((G, Lg, W_C), F32)] * 7,
        scratch_shapes=[pltpu.VMEM((1, RWKV_PAD), F32)] if carry_shift else [],
        compiler_params=_cp("parallel", "arbitrary"),
        name="rwkv_prep",
    )(*args)


def _mm_hi(a, b):
    return jnp.dot(a, b, precision=HIGHEST, preferred_element_type=F32)


def _rwkv_chunk_kernel(r_ref, lw_ref, k_ref, v_ref, a_ref, b_ref, tri_ref, msl_ref, mli_ref, lvl_ref,
                       y_ref, so_ref, s_scr, *, n_chunks):
    C = RW_CHUNK

    @pl.when(pl.program_id(1) == 0)
    def _():
        s_scr[...] = jnp.zeros_like(s_scr)

    head0 = lax.broadcasted_iota(jnp.int32, (C, 128), 1) < DH_C
    tri3, msl, mli = tri_ref[...], msl_ref[...], mli_ref[...]
    ri = lax.broadcasted_iota(jnp.int32, (128, 128), 0)
    ci = lax.broadcasted_iota(jnp.int32, (128, 128), 1)
    eye = (ri == ci).astype(F32)

    def stack(x):
        return jnp.concatenate([jnp.where(head0, x, 0.0), jnp.where(head0, 0.0, x)], axis=0)

    def nt(x, y):
        return lax.dot_general(x, y, (((1,), (1,)), ((), ())), preferred_element_type=F32)

    def chunk(ci_, carry):
        rows = pl.ds(pl.multiple_of(ci_ * C, C), C)
        for p in range(H_C // 2):
            cols = slice(p * 128, (p + 1) * 128)
            r, lw, k = r_ref[rows, cols], lw_ref[rows, cols], k_ref[rows, cols]
            v, a, b = v_ref[rows, cols], a_ref[rows, cols], b_ref[rows, cols]
            hi = lw.astype(BF16)
            mid = (lw - hi.astype(F32)).astype(BF16)
            lo = (lw - hi.astype(F32) - mid.astype(F32)).astype(BF16)
            lam = jnp.dot(tri3, jnp.concatenate([hi, mid, lo], axis=0), preferred_element_type=F32)
            lam_c = lam[C - 1:C, :]
            e_neg = jnp.exp(-lam)
            e_tail = jnp.exp(lam_c - lam)
            AR = jnp.concatenate([stack(a * jnp.exp(lam - lw)), stack(r * jnp.exp(lam))], axis=0).astype(BF16)
            BK = jnp.concatenate([stack(b * e_neg), stack(k * e_neg)], axis=0).astype(BF16)
            S = s_scr[p]
            G = nt(AR, BK)
            I0 = nt(AR, S.astype(BF16))
            Nab, Nak = G[:128, :128] * msl, G[:128, 128:] * msl
            Nrb, Nrk = G[128:, :128] * mli, G[128:, 128:] * mli
            Vs = stack(v)
            rhs = I0[:128] + jnp.dot(Nak.astype(BF16), Vs.astype(BF16), preferred_element_type=F32)
            D = eye
            for lv in range(lvl_ref.shape[0]):
                D = D + _mm_hi(D, _mm_hi(Nab * lvl_ref[lv], D))
            Us = _mm_hi(D, rhs)
            Ys = I0[128:] + jnp.dot(jnp.concatenate([Nrb, Nrk], axis=1).astype(BF16),
                                    jnp.concatenate([Us, Vs], axis=0).astype(BF16),
                                    preferred_element_type=F32)
            y_ref[rows, cols] = Ys[:C] + Ys[C:]
            VU = jnp.concatenate([Vs, Us], axis=0)
            KB = jnp.concatenate([stack(k * e_tail), stack(b * e_tail)], axis=0)
            s_scr[p] = S * jnp.exp(lam_c) + jnp.dot(VU.T.astype(BF16), KB.astype(BF16),
                                                    preferred_element_type=F32)
        return carry

    lax.fori_loop(0, n_chunks, chunk, 0)

    @pl.when(pl.program_id(1) == pl.num_programs(1) - 1)
    def _():
        so_ref[...] = s_scr[...]


def _rwkv_chunk_tables():
    C = RW_CHUNK
    t = np.arange(C)
    tri = (t[:, None] >= t[None, :]).astype(np.float32)
    tri3 = np.concatenate([tri, tri, tri], axis=1)
    i = np.arange(2 * C)
    same = (i[:, None] // C) == (i[None, :] // C)
    msl = (same & (i[:, None] > i[None, :])).astype(np.float32)
    mli = (same & (i[:, None] >= i[None, :])).astype(np.float32)
    levels = []
    n = 1
    while n < C:
        levels.append(same & ((i[:, None] // (2 * n)) == (i[None, :] // (2 * n)))
                      & ((i[:, None] // n) % 2 == 1) & ((i[None, :] // n) % 2 == 0))
        n *= 2
    return jnp.asarray(tri3, BF16), msl, mli, np.stack(levels).astype(np.float32)


def rwkv_chunk_scan(r, lw, k, v, a, b, tb):
    B, L, _ = r.shape
    tri3, msl, mli, lvl = _rwkv_chunk_tables()
    blk = pl.BlockSpec((None, tb, W_C), lambda bi, l: (bi, l, 0))
    c2 = lambda s: pl.BlockSpec(s, lambda bi, l: tuple(0 for _ in s))
    return pl.pallas_call(
        functools.partial(_rwkv_chunk_kernel, n_chunks=tb // RW_CHUNK),
        grid=(B, L // tb),
        in_specs=[blk] * 6 + [c2(tri3.shape), c2(msl.shape), c2(mli.shape), c2(lvl.shape)],
        out_specs=[blk, pl.BlockSpec((None, H_C // 2, 128, 128), lambda bi, l: (bi, 0, 0, 0))],
        out_shape=[jax.ShapeDtypeStruct((B, L, W_C), F32),
                   jax.ShapeDtypeStruct((B, H_C // 2, 128, 128), F32)],
        scratch_shapes=[pltpu.VMEM((H_C // 2, 128, 128), F32)],
        compiler_params=_cp("parallel", "arbitrary"),
        name="rwkv_chunk_scan",
    )(r, lw, k, v, a, b, tri3, msl, mli, lvl)


def _rwkv_step_kernel(s_ref, w_ref, a_ref, b_ref, k_ref, r_ref, v_ref, y_ref, so_ref):
    S = s_ref[...]
    sa = jnp.sum(S * a_ref[...], axis=-1, keepdims=True)
    Sn = S * jnp.exp(w_ref[...]) + sa * b_ref[...] + v_ref[...] * k_ref[...]
    so_ref[...] = Sn
    y_ref[...] = jnp.sum(Sn * r_ref[...], axis=-1, keepdims=True)


def rwkv_step(state, layer, lw, a, b, k, r, v):
    B = lw.shape[0]
    bb = 8
    rowspec = pl.BlockSpec((bb, H_C, 1, DH_C), lambda i: (i, 0, 0, 0))
    colspec = pl.BlockSpec((bb, H_C, DH_C, 1), lambda i: (i, 0, 0, 0))
    sspec = pl.BlockSpec((bb, H_C, DH_C, DH_C), lambda i: (i, 0, 0, 0))
    return pl.pallas_call(
        _rwkv_step_kernel,
        grid=(B // bb,),
        in_specs=[pl.BlockSpec((None, bb, H_C, DH_C, DH_C), lambda i: (layer, i, 0, 0, 0))]
                 + [rowspec] * 5 + [colspec],
        out_specs=[colspec, sspec],
        out_shape=[jax.ShapeDtypeStruct((B, H_C, DH_C, 1), F32),
                   jax.ShapeDtypeStruct((B, H_C, DH_C, DH_C), F32)],
        compiler_params=_cp("parallel"),
        name="rwkv_step",
    )(state, lw, a, b, k, r, v)


def _merge_kernel(x_ref, ya_ref, yb_ref, y_ref, r_ref, k_ref, v_ref, gg_ref, za_ref, zb_ref, zc_ref, gt_ref,
                  lng_ref, lnb_ref, rk_ref, ones_ref, wa_ref, wb_ref, wc_ref, wo_ref, o_ref):
    ones2 = ones_ref[...]
    y = y_ref[...]
    mu = _seg_sum_wide(y, ones2) * (1.0 / DH_C)
    d = y - mu
    var = _seg_sum_wide(d * d, ones2) * (1.0 / DH_C)
    ycn = d * lax.rsqrt(var + LNX_EPS) * lng_ref[...] + lnb_ref[...]
    v = v_ref[...]
    bonus = _seg_sum_wide(r_ref[...] * k_ref[...] * rk_ref[...], ones2) * v
    yc = (ycn + bonus) * gg_ref[...]
    proj = lambda t, w: jnp.dot(t.astype(BF16), w[...], preferred_element_type=F32)
    merged = (jax.nn.sigmoid(za_ref[...]) * proj(ya_ref[...], wa_ref)
              + jax.nn.sigmoid(zb_ref[...]) * proj(yb_ref[...], wb_ref)
              + jax.nn.sigmoid(zc_ref[...]) * proj(yc, wc_ref))
    o_ref[...] = x_ref[...] + gt_ref[...] * proj(merged, wo_ref)


def merge_out(x2, ya, yb, y, r, k, v, gg, z2, gt, lng, lnb, rk, wa, wb, wc, wo, tm):
    T, D = x2.shape
    G, R, _ = gt.shape
    tiles_per_group = (T // tm) // G
    tok = lambda w: pl.BlockSpec((tm, w), lambda m: (m, 0))
    gate = lambda idx: pl.BlockSpec((tm, D), lambda m: (m, Z_GATE // D + idx))
    row = lambda w: pl.BlockSpec((1, w), lambda m: (0, 0))
    full = lambda a: pl.BlockSpec(a.shape, lambda m: (0, 0))
    return pl.pallas_call(
        _merge_kernel,
        grid=(T // tm,),
        in_specs=[tok(D)] + [tok(W_C)] * 7 + [gate(0), gate(1), gate(2),
                  pl.BlockSpec((None, R, D), lambda m: (m // tiles_per_group, 0, 0)),
                  row(W_C), row(W_C), row(W_C), pl.BlockSpec((256, 128), lambda m: (0, 0)),
                  full(wa), full(wb), full(wc), full(wo)],
        out_specs=tok(D),
        out_shape=jax.ShapeDtypeStruct((T, D), F32),
        compiler_params=_cp("parallel"),
        name="merge_out",
    )(x2, ya, yb, y, r, k, v, gg, z2, z2, z2, gt, lng, lnb, rk, _block_ones2(DH_C), wa, wb, wc, wo)


def _ffn_kernel(x_ref, g_ref, sc_ref, sh_ref, gt_ref, w1_ref, w2_ref, o_ref, h_scr, acc_scr):
    f = pl.program_id(1)

    @pl.when(f == 0)
    def _():
        x = x_ref[...]
        y = x * lax.rsqrt(jnp.mean(x * x, axis=-1, keepdims=True) + EPS) * g_ref[...]
        h_scr[...] = (y * (1.0 + sc_ref[...]) + sh_ref[...]).astype(BF16)
        acc_scr[...] = jnp.zeros_like(acc_scr)

    u = jnp.maximum(jnp.dot(h_scr[...], w1_ref[...], preferred_element_type=F32), 0.0)
    acc_scr[...] += jnp.dot((u * u).astype(BF16), w2_ref[...], preferred_element_type=F32)

    @pl.when(f == pl.num_programs(1) - 1)
    def _():
        o_ref[...] = x_ref[...] + gt_ref[...] * acc_scr[...]


def ffn(x2, gain_row, sc, sh, gt, w1, w2, tm, tf):
    T, D = x2.shape
    F = w1.shape[1]
    G, R, _ = sc.shape
    tiles_per_group = (T // tm) // G
    mod_spec = pl.BlockSpec((None, R, D), lambda m, f: (m // tiles_per_group, 0, 0))
    return pl.pallas_call(
        _ffn_kernel,
        grid=(T // tm, F // tf),
        in_specs=[pl.BlockSpec((tm, D), lambda m, f: (m, 0)),
                  pl.BlockSpec((1, D), lambda m, f: (0, 0)),
                  mod_spec, mod_spec, mod_spec,
                  pl.BlockSpec((D, tf), lambda m, f: (0, f)),
                  pl.BlockSpec((tf, D), lambda m, f: (f, 0))],
        out_specs=pl.BlockSpec((tm, D), lambda m, f: (m, 0)),
        out_shape=jax.ShapeDtypeStruct((T, D), F32),
        scratch_shapes=[pltpu.VMEM((tm, D), BF16), pltpu.VMEM((tm, D), F32)],
        compiler_params=_cp("parallel", "arbitrary"),
        name="ffn",
    )(x2, gain_row, sc, sh, gt, w1, w2)


def _prep_layer(p, l):
    D = p['w_in'].shape[1]
    w_in = p['w_in'][l]
    o_diff, o_rwkv, o_gate = RET_COLS, RET_COLS + DIFF_COLS, RET_COLS + DIFF_COLS + RWKV_COLS
    w_in_p = jnp.concatenate([
        w_in[:, :RET_COLS], w_in[:, o_rwkv:o_gate], jnp.zeros((D, RWKV_PAD - RWKV_COLS), F32),
        w_in[:, o_gate:], w_in[:, o_diff:o_rwkv]], axis=1).astype(BF16)
    w_lr = jnp.zeros((LR_COLS, 3 * W_C), F32)
    w_lr = w_lr.at[:R_DECAY, :W_C].set(p['w_decay_up'][l])
    w_lr = w_lr.at[R_DECAY:R_DECAY + R_A, W_C:2 * W_C].set(p['w_a_up'][l])
    w_lr = w_lr.at[R_DECAY + R_A:R_DECAY + R_A + R_G, 2 * W_C:].set(p['w_g_up'][l])
    row = lambda t: t.reshape(1, -1)
    lp = dict(
        w_ada=p['w_ada'][l].astype(BF16), b_ada=row(p['b_ada'][l]),
        norm1=row(p['norm1'][l]), norm2=row(p['norm2'][l]), w_in=w_in_p,
        gq=row(jnp.tile(p['qk_norm_q'][l], 2 * H_B)), gk=row(jnp.tile(p['qk_norm_k'][l], 2 * H_B)),
        lam_rows=jnp.stack([p['lambda_q1'][l], p['lambda_k1'][l], p['lambda_q2'][l], p['lambda_k2'][l]]),
        subln=row(p['subln_diff'][l]),
        mu=row(jnp.pad(p['mu_shift'][l], (0, RWKV_PAD - RWKV_COLS))),
        w0=row(p['w0'][l]), a0=row(p['a0'][l]), k_k=row(p['k_k'][l]), k_a=row(p['k_a'][l]),
        w_lr=w_lr.astype(BF16), r_k=row(p['r_k'][l]), lnx_g=row(p['lnx_g'][l]), lnx_b=row(p['lnx_b'][l]),
        w_up_a=p['w_up_a'][l].astype(BF16), w_up_b=p['w_up_b'][l].astype(BF16),
        w_up_c=p['w_up_c'][l].astype(BF16), w_out=p['w_out'][l].astype(BF16),
        w_ff1=p['w_ff1'][l].astype(BF16), w_ff2=p['w_ff2'][l].astype(BF16),
        lam_init=0.8 - 0.6 * math.exp(-0.3 * l),
    )
    if l > 0:
        lp['v0'] = row(p['v0'][l - 1])
        lp['w_v_down'] = jnp.pad(p['w_v_down'][l - 1], ((0, 0), (0, 128 - R_V))).astype(BF16)
        lp['w_v_up'] = jnp.pad(p['w_v_up'][l - 1], ((0, 128 - R_V), (0, 0))).astype(BF16)
    return lp


def _modulation(c, lp, per_token):
    B, D = c.shape
    rows = -(-B // 16) * 16
    mod = ada_mod(jnp.pad(c, ((0, rows - B), (0, 0))), lp['w_ada'], lp['b_ada'])[:B]
    parts = [mod[:, i * D:(i + 1) * D] for i in range(6)]
    shape = (1, B, D) if per_token else (B, 1, D)
    return [t.reshape(shape) for t in parts]


def _trunk_prompt(x, c, layers, bias_p):
    B, L, D = x.shape
    T = B * L
    tm = min(1024, L)
    t_attn = min(512, L)
    x2 = x.reshape(T, D)
    cos_t, sin_t = _rope_tables(jnp.arange(L))
    k_rows, v_rows, ret_out, rwkv_out, shift_out = [], [], [], [], []
    v_first = None
    for l, lp in enumerate(layers):
        sh1, sc1, gt1, sh2, sc2, gt2 = _modulation(c, lp, per_token=False)
        z2 = in_proj(x2, lp['norm1'], sc1, sh1, lp['w_in'], tm, 512)
        ZC = z2.shape[1]
        z3 = z2.reshape(B, L, ZC)
        ya, s_ret = retention_prompt(z3, cos_t, sin_t, min(256, L))
        qb, k32, kb, v32, vb = qk_norm(z2, lp['gq'], lp['gk'], tm)
        r3 = lambda t: t.reshape(B, L, W_B)
        yb = diff_attention_prompt(r3(qb), r3(kb), r3(vb), bias_p, lp['lam_rows'], lp['subln'],
                                   lp['lam_init'], t_attn)
        k_rows.append(k32.reshape(B, L, H_B, 2 * DH_B))
        v_rows.append(v32.reshape(B, L, H_B, DV_B))
        vgate = None if l == 0 else (lp['v0'], lp['w_v_down'], lp['w_v_up'], v_first)
        shift0 = jnp.zeros((B, 1, RWKV_PAD), F32)
        r, lw, k, v, a, b, gg = rwkv_prep(z3, shift0, lp['mu'], lp['w0'], lp['a0'], lp['k_k'], lp['k_a'],
                                          lp['w_lr'], vgate, min(256, L), carry_shift=True)
        if l == 0:
            v_first = v
        y, s_pair = rwkv_chunk_scan(r, lw, k, v, a, b, min(256, L))
        shift_out.append(z3[:, L - 1:, Z_RWKV:Z_RWKV + RWKV_COLS])
        f2 = lambda t: t.reshape(T, -1)
        x2 = merge_out(x2, f2(ya), f2(yb), f2(y), f2(r), f2(k), f2(v), f2(gg), z2, gt1, lp['lnx_g'],
                       lp['lnx_b'], lp['r_k'], lp['w_up_a'], lp['w_up_b'], lp['w_up_c'], lp['w_out'],
                       min(512, L))
        x2 = ffn(x2, lp['norm2'], sc2, sh2, gt2, lp['w_ff1'], lp['w_ff2'], tm, 1024)
        ret_out.append(s_ret)
        sp = s_pair.reshape(B, H_C // 2, 2, DH_C, 2, DH_C)
        rwkv_out.append(jnp.stack([sp[:, :, 0, :, 0, :], sp[:, :, 1, :, 1, :]], axis=2)
                        .reshape(B, H_C, DH_C, DH_C))
    return (x2.reshape(B, L, D), jnp.stack(k_rows), jnp.stack(v_rows), jnp.stack(ret_out),
            jnp.stack(rwkv_out), jnp.stack(shift_out))


def _trunk_decode(x, c, layers, bias_d, state_ret, state_rwkv, state_shift, cache_k, cache_v, page_table):
    B, _, D = x.shape
    n_pages = page_table.shape[1]
    past = n_pages * PAGE_SIZE
    x2 = x.reshape(B, D)
    pt_flat = page_table.reshape(-1)
    ck = cache_k.reshape(cache_k.shape[0], cache_k.shape[1], PAGE_SIZE, W_BQK)
    cv = cache_v.reshape(cache_v.shape[0], cache_v.shape[1], PAGE_SIZE, W_B)
    k_rows, v_rows, ret_out, rwkv_out, shift_out = [], [], [], [], []
    v_first = None
    for l, lp in enumerate(layers):
        sh1, sc1, gt1, sh2, sc2, gt2 = _modulation(c, lp, per_token=True)
        z2 = in_proj(x2, lp['norm1'], sc1, sh1, lp['w_in'], B, 512)
        ZC = z2.shape[1]
        ya, s_ret = retention_step(z2, state_ret, l, past)
        qb, k32, kb, v32, vb = qk_norm(z2, lp['gq'], lp['gk'], B)
        r3 = lambda t: t.reshape(B, 1, -1)
        yb = diff_attention_decode(r3(qb), r3(k32), r3(v32), ck, cv, pt_flat, bias_d, lp['lam_rows'],
                                   lp['subln'], l, lp['lam_init'], n_pages).reshape(B, W_B)
        k_rows.append(k32.reshape(B, 1, H_B, 2 * DH_B))
        v_rows.append(v32.reshape(B, 1, H_B, DV_B))
        vgate = None if l == 0 else (lp['v0'], lp['w_v_down'], lp['w_v_up'], v_first)
        prev = jnp.pad(state_shift[l].reshape(1, B, RWKV_COLS), ((0, 0), (0, 0), (0, RWKV_PAD - RWKV_COLS)))
        r, lw, k, v, a, b, gg = rwkv_prep(z2.reshape(1, B, ZC), prev, lp['mu'], lp['w0'], lp['a0'], lp['k_k'],
                                          lp['k_a'], lp['w_lr'], vgate, B, carry_shift=False)
        if l == 0:
            v_first = v
        keyrow = lambda t: t.reshape(B, H_C, 1, DH_C)
        y_col, s_rwkv = rwkv_step(state_rwkv, l, keyrow(lw), keyrow(a), keyrow(b), keyrow(k), keyrow(r),
                                  v.reshape(B, H_C, DH_C, 1))
        shift_out.append(z2[:, Z_RWKV:Z_RWKV + RWKV_COLS].reshape(B, 1, RWKV_COLS))
        f2 = lambda t: t.reshape(B, -1)
        x2 = merge_out(x2, ya, yb, f2(y_col), f2(r), f2(k), f2(v), f2(gg), z2, gt1, lp['lnx_g'], lp['lnx_b'],
                       lp['r_k'], lp['w_up_a'], lp['w_up_b'], lp['w_up_c'], lp['w_out'], B)
        x2 = ffn(x2, lp['norm2'], sc2, sh2, gt2, lp['w_ff1'], lp['w_ff2'], B, 1024)
        ret_out.append(s_ret)
        rwkv_out.append(s_rwkv)
    return (x2.reshape(B, 1, D), jnp.stack(k_rows), jnp.stack(v_rows), jnp.stack(ret_out),
            jnp.stack(rwkv_out), jnp.stack(shift_out))


def _decode_buckets(n_pages):
    past = n_pages * PAGE_SIZE
    key = np.arange((n_pages + 1) * PAGE_SIZE)
    bk = np.where(key <= past, _bucket_np(np.maximum(past - key, 0)), -1)
    return np.broadcast_to(bk.reshape(n_pages + 1, 1, PAGE_SIZE), (n_pages + 1, 8, PAGE_SIZE)).astype(np.int32)


def kernel(x_prompt, x_sample, c_prompt, c_sample, cache_k_diff, cache_v_diff, page_table, state_ret, state_rwkv, state_shift, rel_bias, w_ada, b_ada, norm1, norm2, w_in, qk_norm_q, qk_norm_k, lambda_q1, lambda_k1, lambda_q2, lambda_k2, subln_diff, mu_shift, w0, w_decay_up, a0, w_a_up, w_g_up, v0, w_v_down, w_v_up, k_k, k_a, r_k, lnx_g, lnx_b, w_up_a, w_up_b, w_up_c, w_out, w_ff1, w_ff2):
    p = dict(w_ada=w_ada, b_ada=b_ada, norm1=norm1, norm2=norm2, w_in=w_in,
             qk_norm_q=qk_norm_q, qk_norm_k=qk_norm_k, lambda_q1=lambda_q1, lambda_k1=lambda_k1,
             lambda_q2=lambda_q2, lambda_k2=lambda_k2, subln_diff=subln_diff, mu_shift=mu_shift,
             w0=w0, w_decay_up=w_decay_up, a0=a0, w_a_up=w_a_up, w_g_up=w_g_up, v0=v0,
             w_v_down=w_v_down, w_v_up=w_v_up, k_k=k_k, k_a=k_a, r_k=r_k, lnx_g=lnx_g, lnx_b=lnx_b,
             w_up_a=w_up_a, w_up_b=w_up_b, w_up_c=w_up_c, w_out=w_out, w_ff1=w_ff1, w_ff2=w_ff2)
    depth = w_in.shape[0]
    layers = [_prep_layer(p, l) for l in range(depth)]
    L = x_prompt.shape[1]
    n_pages = page_table.shape[1]
    t_attn = min(512, L)
    bias_p = bias_tiles(rel_bias, _prompt_buckets(t_attn))
    bd = bias_tiles(rel_bias, _decode_buckets(n_pages).reshape((n_pages + 1) * 8, PAGE_SIZE))
    bd = bd.reshape(H_B, n_pages + 1, 8, PAGE_SIZE)
    bias_d = jnp.stack([bd[r // 2, :, r, :] for r in range(8)], axis=1)

    y_p, k_p, v_p, ret_p, rwkv_p, shift_p = _trunk_prompt(x_prompt, c_prompt, layers, bias_p)
    y_s, k_s, v_s, ret_s, rwkv_s, shift_s = _trunk_decode(
        x_sample, c_sample, layers, bias_d, state_ret, state_rwkv, state_shift,
        cache_k_diff, cache_v_diff, page_table)
    return (y_p, y_s, k_p, v_p, k_s, v_s, ret_p, ret_s, rwkv_p, rwkv_s, shift_p, shift_s)
```

```python
import functools
import math

import numpy as np
import jax
import jax.numpy as jnp
from jax import lax
from jax.experimental import pallas as pl
from jax.experimental.pallas import tpu as pltpu

F32 = jnp.float32
BF16 = jnp.bfloat16
HIGHEST = lax.Precision.HIGHEST

H_A, DK_A, DV_A = 4, 128, 128
RET_CHUNK = 128
ROPE_BASE = 10000.0
H_B, DH_B, DV_B = 4, 64, 128
N_BUCKETS, MAX_DISTANCE = 32, 128
H_C, DH_C = 8, 64
R_DECAY, R_A, R_V, R_G = 64, 64, 32, 160
LNX_EPS = 64e-5
EPS = 1e-6
PAGE_SIZE = 128
W_AQ, W_A = H_A * DK_A, H_A * DV_A
W_BQK, W_B = H_B * 2 * DH_B, H_B * DV_B
W_C = H_C * DH_C
RET_COLS = 2 * W_AQ + 2 * W_A
DIFF_COLS = 2 * W_BQK + W_B
RWKV_COLS = 3 * W_C + R_DECAY + R_A + R_G
RWKV_PAD = 2048
LR_COLS = RWKV_PAD - 3 * W_C

Z_RET, Z_RWKV = 0, RET_COLS
Z_GATE = Z_RWKV + RWKV_PAD
NEG = -1e30
RW_CHUNK = 64
VMEM_LIMIT = 56 * 1024 * 1024


def _cp(*sem):
    return pltpu.CompilerParams(dimension_semantics=sem, vmem_limit_bytes=VMEM_LIMIT)


def _silu(x):
    return x * jax.nn.sigmoid(x)


def _split2(x):
    hi = x.astype(BF16)
    lo = (x - hi.astype(F32)).astype(BF16)
    return hi, lo


def _seg_sum(x, ones2):
    hi, lo = _split2(x)
    return jnp.dot(jnp.concatenate([hi, lo], axis=1), ones2, preferred_element_type=F32)


def _seg_sum_wide(x, ones2):
    return jnp.concatenate(
        [_seg_sum(x[:, c * 128:(c + 1) * 128], ones2) for c in range(x.shape[1] // 128)], axis=1)


def _mod_kernel(c_ref, w_ref, b_ref, o_ref):
    s = _silu(c_ref[...])
    o_ref[...] = jnp.dot(s.astype(BF16), w_ref[...], preferred_element_type=F32) + b_ref[...]


def ada_mod(c_pad, w_bf, b_row):
    R, D = c_pad.shape
    N = w_bf.shape[1]
    tn = 1536
    return pl.pallas_call(
        _mod_kernel,
        grid=(N // tn,),
        in_specs=[pl.BlockSpec((R, D), lambda n: (0, 0)),
                  pl.BlockSpec((D, tn), lambda n: (0, n)),
                  pl.BlockSpec((1, tn), lambda n: (0, n))],
        out_specs=pl.BlockSpec((R, tn), lambda n: (0, n)),
        out_shape=jax.ShapeDtypeStruct((R, N), F32),
        compiler_params=_cp("parallel"),
        name="ada_mod",
    )(c_pad, w_bf, b_row)


def _inproj_kernel(x_ref, g_ref, sc_ref, sh_ref, w_ref, z_ref, h_scr):
    @pl.when(pl.program_id(1) == 0)
    def _():
        x = x_ref[...]
        y = x * lax.rsqrt(jnp.mean(x * x, axis=-1, keepdims=True) + EPS) * g_ref[...]
        h_scr[...] = (y * (1.0 + sc_ref[...]) + sh_ref[...]).astype(BF16)

    z_ref[...] = jnp.dot(h_scr[...], w_ref[...], preferred_element_type=F32)


def in_proj(x2, gain_row, sc, sh, w_bf, tm, tn):
    T, D = x2.shape
    N = w_bf.shape[1]
    G, R, _ = sc.shape
    tiles_per_group = (T // tm) // G
    mod_spec = pl.BlockSpec((None, R, D), lambda m, n: (m // tiles_per_group, 0, 0))
    return pl.pallas_call(
        _inproj_kernel,
        grid=(T // tm, N // tn),
        in_specs=[pl.BlockSpec((tm, D), lambda m, n: (m, 0)),
                  pl.BlockSpec((1, D), lambda m, n: (0, 0)),
                  mod_spec, mod_spec,
                  pl.BlockSpec((D, tn), lambda m, n: (0, n))],
        out_specs=pl.BlockSpec((tm, tn), lambda m, n: (m, n)),
        out_shape=jax.ShapeDtypeStruct((T, N), F32),
        scratch_shapes=[pltpu.VMEM((tm, D), BF16)],
        compiler_params=_cp("parallel", "arbitrary"),
        name="in_proj",
    )(x2, gain_row, sc, sh, w_bf)


def _rope(x, cos, sin_signed):
    return x * cos + pltpu.roll(x, DK_A // 2, 1) * sin_signed


def _ret_kernel(q_ref, k_ref, v_ref, g_ref, cos_ref, sin_ref, inner_ref, cross_ref, tail_ref, cd_ref,
                ya_ref, so_ref, s_scr, *, n_sub):
    lt = pl.program_id(1)

    @pl.when(lt == 0)
    def _():
        s_scr[...] = jnp.zeros_like(s_scr)

    C = RET_CHUNK
    for c in range(n_sub):
        rows = slice(c * C, (c + 1) * C)
        cos = cos_ref[rows, :]
        sin = sin_ref[rows, :]
        for h in range(H_A):
            cols = slice(h * DK_A, (h + 1) * DK_A)
            q = _rope(q_ref[rows, cols], cos, sin)
            k = _rope(k_ref[rows, cols], cos, sin) * (DK_A ** -0.5)
            qb, kb, vb = q.astype(BF16), k.astype(BF16), v_ref[rows, cols].astype(BF16)
            S = s_scr[h]
            sc = lax.dot_general(qb, kb, (((1,), (1,)), ((), ())), preferred_element_type=F32) * inner_ref[h]
            o = (jnp.dot(sc.astype(BF16), vb, preferred_element_type=F32)
                 + jnp.dot(qb, S.astype(BF16), preferred_element_type=F32) * cross_ref[h])
            kt = (k * tail_ref[h]).T.astype(BF16)
            s_scr[h] = S * cd_ref[h] + jnp.dot(kt, vb, preferred_element_type=F32)
            on = o * lax.rsqrt(jnp.mean(o * o, axis=-1, keepdims=True) + EPS)
            ya_ref[rows, cols] = on * _silu(g_ref[rows, cols])

    @pl.when(lt == pl.num_programs(1) - 1)
    def _():
        so_ref[...] = s_scr[...]


def _ret_tables(L):
    C = math.gcd(L, RET_CHUNK)
    log_g = np.log1p(-np.exp2(-5.0 - np.arange(H_A, dtype=np.float32))).astype(np.float32)
    i = np.arange(C, dtype=np.float32)
    dist = i[:, None] - i[None, :]
    causal = dist >= 0
    inner = np.where(causal[None], np.exp(np.where(causal, dist, 0.0)[None] * log_g[:, None, None]), 0.0)
    cross = np.exp((i[None, :] + 1.0) * log_g[:, None])
    tail = np.exp((C - 1.0 - i)[None, :] * log_g[:, None])
    chunk = np.exp(C * log_g)
    bc = lambda t: np.broadcast_to(t[:, :, None], (H_A, C, 128)).astype(np.float32)
    cd = np.broadcast_to(chunk[:, None, None], (H_A, 1, 128)).astype(np.float32)
    return inner.astype(np.float32), bc(cross), bc(tail), cd


def _rope_tables(pos):
    half = DK_A // 2
    inv = ROPE_BASE ** (-jnp.arange(half, dtype=F32) / half)
    ang = pos.astype(F32)[:, None] * inv[None, :]
    cos, sin = jnp.cos(ang), jnp.sin(ang)
    return jnp.concatenate([cos, cos], axis=-1), jnp.concatenate([-sin, sin], axis=-1)


def retention_prompt(z3, cos_t, sin_t, tb):
    B, L, ZC = z3.shape
    inner, cross, tail, cd = _ret_tables(L)
    zspec = lambda idx: pl.BlockSpec((None, tb, W_AQ), lambda b, l: (b, l, idx))
    full3 = lambda s: pl.BlockSpec(s, lambda b, l: (0, 0, 0))
    return pl.pallas_call(
        functools.partial(_ret_kernel, n_sub=tb // RET_CHUNK),
        grid=(B, L // tb),
        in_specs=[zspec(0), zspec(1), zspec(2), zspec(3),
                  pl.BlockSpec((tb, 128), lambda b, l: (l, 0)),
                  pl.BlockSpec((tb, 128), lambda b, l: (l, 0)),
                  full3(inner.shape), full3(cross.shape), full3(tail.shape), full3(cd.shape)],
        out_specs=[pl.BlockSpec((None, tb, W_A), lambda b, l: (b, l, 0)),
                   pl.BlockSpec((None, H_A, DK_A, DV_A), lambda b, l: (b, 0, 0, 0))],
        out_shape=[jax.ShapeDtypeStruct((B, L, W_A), F32),
                   jax.ShapeDtypeStruct((B, H_A, DK_A, DV_A), F32)],
        scratch_shapes=[pltpu.VMEM((H_A, DK_A, DV_A), F32)],
        compiler_params=_cp("parallel", "arbitrary"),
        name="retention_prompt",
    )(z3, z3, z3, z3, cos_t, sin_t, inner, cross, tail, cd)


def _ret_step_kernel(q_ref, k_ref, v_ref, g_ref, cos_ref, sin_ref, gam_ref, s_ref, ya_ref, so_ref, *, bb):
    cos, sin = cos_ref[...], sin_ref[...]
    row = lax.broadcasted_iota(jnp.int32, (bb, 128), 0)
    for h in range(H_A):
        cols = slice(h * DK_A, (h + 1) * DK_A)
        gam = gam_ref[h]
        q = _rope(q_ref[:, cols], cos, sin)
        k = _rope(k_ref[:, cols], cos, sin) * (DK_A ** -0.5)
        v = v_ref[:, cols]
        qb = q.astype(BF16)
        qk = jnp.sum(qb.astype(F32) * k.astype(BF16).astype(F32), axis=-1, keepdims=True)
        qs = jnp.zeros((bb, 128), F32)
        for b in range(bb):
            S = s_ref[b, h]
            onehot = row == b
            qs = jnp.where(onehot, jnp.dot(qb, S.astype(BF16), preferred_element_type=F32), qs)
            kb_t = jnp.where(onehot, k, 0.0).T.astype(BF16)
            so_ref[b, h] = S * gam + jnp.dot(kb_t, v.astype(BF16), preferred_element_type=F32)
        o = qk * v.astype(BF16).astype(F32) + qs * gam
        on = o * lax.rsqrt(jnp.mean(o * o, axis=-1, keepdims=True) + EPS)
        ya_ref[:, cols] = on * _silu(g_ref[:, cols])


def retention_step(z2, state, layer, pos):
    B = z2.shape[0]
    bb = 8
    cos_t, sin_t = _rope_tables(jnp.full((1,), pos))
    log_g = np.log1p(-np.exp2(-5.0 - np.arange(H_A, dtype=np.float32))).astype(np.float32)
    gam = np.broadcast_to(np.exp(log_g)[:, None, None], (H_A, 1, 128)).astype(np.float32)
    zspec = lambda idx: pl.BlockSpec((bb, W_AQ), lambda i: (i, idx))
    return pl.pallas_call(
        functools.partial(_ret_step_kernel, bb=bb),
        grid=(B // bb,),
        in_specs=[zspec(0), zspec(1), zspec(2), zspec(3),
                  pl.BlockSpec((1, 128), lambda i: (0, 0)),
                  pl.BlockSpec((1, 128), lambda i: (0, 0)),
                  pl.BlockSpec((H_A, 1, 128), lambda i: (0, 0, 0)),
                  pl.BlockSpec((None, bb, H_A, DK_A, DV_A), lambda i: (layer, i, 0, 0, 0))],
        out_specs=[pl.BlockSpec((bb, W_A), lambda i: (i, 0)),
                   pl.BlockSpec((bb, H_A, DK_A, DV_A), lambda i: (i, 0, 0, 0))],
        out_shape=[jax.ShapeDtypeStruct((B, W_A), F32),
                   jax.ShapeDtypeStruct((B, H_A, DK_A, DV_A), F32)],
        compiler_params=_cp("parallel"),
        name="retention_step",
    )(z2, z2, z2, z2, cos_t, sin_t, gam, state)


def _qknorm_kernel(q_ref, k_ref, v_ref, gq_ref, gk_ref, ones_ref, qb_ref, k32_ref, kb_ref, v32_ref, vb_ref):
    ones2 = ones_ref[...]
    q, k, v = q_ref[...], k_ref[...], v_ref[...]
    qn = q * lax.rsqrt(_seg_sum_wide(q * q, ones2) * (1.0 / DH_B) + EPS) * gq_ref[...] * (DH_B ** -0.5)
    kn = k * lax.rsqrt(_seg_sum_wide(k * k, ones2) * (1.0 / DH_B) + EPS) * gk_ref[...]
    qb_ref[...] = qn.astype(BF16)
    k32_ref[...] = kn
    kb_ref[...] = kn.astype(BF16)
    v32_ref[...] = v
    vb_ref[...] = v.astype(BF16)


def _block_ones2(group):
    i = np.arange(128)
    m = (i[:, None] // group == i[None, :] // group).astype(np.float32)
    return jnp.asarray(np.concatenate([m, m], axis=0), BF16)


def qk_norm(z2, gq_row, gk_row, tm):
    T = z2.shape[0]
    base = (Z_GATE + 3 * 1024) // W_BQK
    zspec = lambda idx: pl.BlockSpec((tm, W_BQK), lambda m: (m, base + idx))
    row = pl.BlockSpec((1, W_BQK), lambda m: (0, 0))
    out = pl.BlockSpec((tm, W_BQK), lambda m: (m, 0))
    return pl.pallas_call(
        _qknorm_kernel,
        grid=(T // tm,),
        in_specs=[zspec(0), zspec(1), zspec(2), row, row, pl.BlockSpec((256, 128), lambda m: (0, 0))],
        out_specs=[out] * 5,
        out_shape=[jax.ShapeDtypeStruct((T, W_BQK), BF16), jax.ShapeDtypeStruct((T, W_BQK), F32),
                   jax.ShapeDtypeStruct((T, W_BQK), BF16), jax.ShapeDtypeStruct((T, W_B), F32),
                   jax.ShapeDtypeStruct((T, W_B), BF16)],
        compiler_params=_cp("parallel"),
        name="qk_norm",
    )(z2, z2, z2, gq_row, gk_row, _block_ones2(DH_B))


def _bucket_np(n):
    max_exact = N_BUCKETS // 2
    nf = np.maximum(n, 1).astype(np.float32)
    large = max_exact + (np.log(nf / np.float32(max_exact)) / np.float32(math.log(MAX_DISTANCE / max_exact))
                         * np.float32(N_BUCKETS - max_exact)).astype(np.int32)
    large = np.minimum(large, N_BUCKETS - 1)
    return np.where(n < max_exact, n, large).astype(np.int32)


def _bias_kernel(tab_ref, bkt_ref, o_ref):
    h = pl.program_id(0)
    bk = bkt_ref[...]
    acc = jnp.full(bk.shape, NEG, F32)
    for b in range(N_BUCKETS):
        acc = jnp.where(bk == b, tab_ref[b, h], acc)
    o_ref[...] = acc


def bias_tiles(rel_bias, buckets):
    R, C = buckets.shape
    tr = min(R, 512)
    return pl.pallas_call(
        _bias_kernel,
        grid=(H_B, R // tr),
        in_specs=[pl.BlockSpec(memory_space=pltpu.SMEM),
                  pl.BlockSpec((tr, C), lambda h, r: (r, 0))],
        out_specs=pl.BlockSpec((None, tr, C), lambda h, r: (h, r, 0)),
        out_shape=jax.ShapeDtypeStruct((H_B, R, C), F32),
        compiler_params=_cp("parallel", "parallel"),
        name="bias_tiles",
    )(rel_bias, jnp.asarray(buckets))


def _prompt_buckets(t):
    r = np.arange(t)[:, None]
    c = np.arange(t)[None, :]
    diag = np.where(c <= r, _bucket_np(np.maximum(r - c, 0)), -1)
    off1 = _bucket_np(t + r - c)
    far = _bucket_np(np.full((t, t), 2 * t))
    assert t >= MAX_DISTANCE
    return np.concatenate([diag, off1, far], axis=0).astype(np.int32)


def _lambda_full(lam_ref, lam_init):
    lv = lam_ref[...]
    s1 = jnp.sum(lv[0:1] * lv[1:2], axis=-1, keepdims=True)
    s2 = jnp.sum(lv[2:3] * lv[3:4], axis=-1, keepdims=True)
    return jnp.exp(s1) - jnp.exp(s2) + lam_init


def _flash_kernel(q_ref, k_ref, v_ref, bias_ref, lam_ref, gain_ref, o_ref, qq_scr, m_scr, acc_scr,
                  *, lam_init, sub):
    i, j = pl.program_id(2), pl.program_id(3)

    t = q_ref.shape[0]
    tk = k_ref.shape[0]

    @pl.when(j == 0)
    def _():
        q = q_ref[...]
        first = lax.broadcasted_iota(jnp.int32, q.shape, 1) < DH_B
        qq_scr[:t] = jnp.where(first, q, jnp.zeros_like(q))
        qq_scr[t:] = jnp.where(first, jnp.zeros_like(q), q)
        m_scr[...] = jnp.full(m_scr.shape, NEG, F32)
        acc_scr[...] = jnp.zeros_like(acc_scr)

    @pl.when(j <= i)
    def _():
        k = k_ref[...]
        v1 = jnp.concatenate([v_ref[...], jnp.ones((tk, DV_B), BF16)], axis=1)

        def scores(u):
            rows = slice(u * sub, (u + 1) * sub)
            brow = (u * sub) % t
            s = lax.dot_general(qq_scr[rows], k, (((1,), (1,)), ((), ())), preferred_element_type=F32)
            return s + bias_ref[brow:brow + sub, :]

        n_sub = 2 * t // sub
        s_next = scores(0)
        for u in range(n_sub):
            s = s_next
            if u + 1 < n_sub:
                s_next = scores(u + 1)
            rows = slice(u * sub, (u + 1) * sub)
            m_old = m_scr[rows]
            m_new = jnp.maximum(m_old, jnp.max(s, axis=-1, keepdims=True))
            alpha = jnp.exp(m_old - m_new)
            p = jnp.concatenate([jnp.exp(s[:, c * 128:(c + 1) * 128] - m_new) for c in range(tk // 128)],
                                axis=1).astype(BF16)
            acc_scr[rows] = (jnp.concatenate([alpha, alpha], axis=1) * acc_scr[rows]
                             + jnp.dot(p, v1, preferred_element_type=F32))
            m_scr[rows] = m_new

    @pl.when(j == i)
    def _():
        lam = _lambda_full(lam_ref, lam_init)
        acc = acc_scr[...]
        ob = acc[:t, :DV_B] / acc[:t, DV_B:] - lam * (acc[t:, :DV_B] / acc[t:, DV_B:])
        on = ob * lax.rsqrt(jnp.mean(ob * ob, axis=-1, keepdims=True) + EPS) * gain_ref[...]
        o_ref[...] = on * (1.0 - lam_init)


def diff_attention_prompt(qb, kb, vb, bias, lam_rows, gain_row, lam_init, t):
    B, L, _ = qb.shape
    n = L // t
    kv_spec = pl.BlockSpec((None, t, DV_B), lambda b, h, i, j: (b, jnp.minimum(j, i), h))
    return pl.pallas_call(
        functools.partial(_flash_kernel, lam_init=lam_init, sub=min(256, t)),
        grid=(B, H_B, n, n),
        in_specs=[pl.BlockSpec((None, t, DV_B), lambda b, h, i, j: (b, i, h)),
                  kv_spec, kv_spec,
                  pl.BlockSpec((None, t, t), lambda b, h, i, j: (h, jnp.clip(i - j, 0, 2), 0)),
                  pl.BlockSpec((4, DH_B), lambda b, h, i, j: (0, 0)),
                  pl.BlockSpec((1, DV_B), lambda b, h, i, j: (0, 0))],
        out_specs=pl.BlockSpec((None, t, DV_B), lambda b, h, i, j: (b, i, h)),
        out_shape=jax.ShapeDtypeStruct((B, L, W_B), F32),
        scratch_shapes=[pltpu.VMEM((2 * t, DV_B), BF16), pltpu.VMEM((2 * t, 128), F32),
                        pltpu.VMEM((2 * t, 2 * DV_B), F32)],
        compiler_params=_cp("parallel", "parallel", "parallel", "arbitrary"),
        name="diff_attention_prompt",
    )(qb, kb, vb, bias, lam_rows, gain_row)


def _decode_kernel(pt_ref, q_ref, kn_ref, vn_ref, bias_ref, lam_ref, gain_ref, *rest, n_pages, lam_init):
    k_refs, v_refs, o_ref = rest[:n_pages], rest[n_pages:2 * n_pages], rest[2 * n_pages]
    lam = _lambda_full(lam_ref, lam_init)
    gain = gain_ref[...]
    r8 = lax.broadcasted_iota(jnp.int32, (8, 2 * DH_B), 0)
    l8 = lax.broadcasted_iota(jnp.int32, (8, 2 * DH_B), 1)
    comp_rows = (l8 // DH_B) == r8
    row0 = lax.broadcasted_iota(jnp.int32, (PAGE_SIZE, DV_B), 0) == 0
    nt = lambda x, y: lax.dot_general(x, y, (((1,), (1,)), ((), ())), preferred_element_type=F32)
    for h in range(H_B):
        cols = slice(h * DV_B, (h + 1) * DV_B)
        qrows = jnp.where(comp_rows, q_ref[:, cols].astype(F32), 0.0).astype(BF16)
        head_rows = pl.ds(h, PAGE_SIZE, stride=H_B)
        ks = [r[head_rows, :].astype(BF16) for r in k_refs]
        vs = [r[head_rows, :].astype(BF16) for r in v_refs]
        ks.append(jnp.where(row0, kn_ref[:, cols], 0.0).astype(BF16))
        vs.append(jnp.where(row0, vn_ref[:, cols], 0.0).astype(BF16))
        ss = [nt(qrows, kp) + bias_ref[h, p] for p, kp in enumerate(ks)]
        m = functools.reduce(jnp.maximum, [jnp.max(s, axis=-1, keepdims=True) for s in ss])
        ps = [jnp.exp(s - m) for s in ss]
        l = functools.reduce(jnp.add, [jnp.sum(p, axis=-1, keepdims=True) for p in ps])
        acc = functools.reduce(jnp.add, [jnp.dot(p.astype(BF16), vp, preferred_element_type=F32)
                                         for p, vp in zip(ps, vs)])
        outn = acc / l
        oh = outn[0:1] - lam * outn[1:2]
        on = oh * lax.rsqrt(jnp.mean(oh * oh, axis=-1, keepdims=True) + EPS) * gain
        o_ref[:, cols] = on * (1.0 - lam_init)


def diff_attention_decode(qb, k32, v32, cache_k, cache_v, pt_flat, bias_dec, lam_rows, gain_row, layer,
                          lam_init, n_pages):
    B = qb.shape[0]
    rowspec = pl.BlockSpec((None, 1, W_BQK), lambda b, pt: (b, 0, 0))
    const = lambda s: pl.BlockSpec(s, lambda b, pt: tuple(0 for _ in s))
    page_spec = lambda p: pl.BlockSpec((None, None, PAGE_SIZE * H_B, DV_B),
                                       lambda b, pt, p=p: (layer, pt[b * n_pages + p], 0, 0))
    grid_spec = pltpu.PrefetchScalarGridSpec(
        num_scalar_prefetch=1,
        grid=(B,),
        in_specs=[rowspec, rowspec, rowspec, const(bias_dec.shape), const((4, DH_B)), const((1, DV_B))]
                 + [page_spec(p) for p in range(n_pages)] * 2,
        out_specs=pl.BlockSpec((None, 1, W_B), lambda b, pt: (b, 0, 0)),
    )
    return pl.pallas_call(
        functools.partial(_decode_kernel, n_pages=n_pages, lam_init=lam_init),
        grid_spec=grid_spec,
        out_shape=jax.ShapeDtypeStruct((B, 1, W_B), F32),
        compiler_params=_cp("arbitrary"),
        name="diff_attention_decode",
    )(pt_flat, qb, k32, v32, bias_dec, lam_rows, gain_row,
      *([cache_k] * n_pages), *([cache_v] * n_pages))


def _rwkv_prep_kernel(*refs, carry_shift, gate_v):
    it = iter(refs)
    z_ref, prev_ref, mu_ref, w0_ref, a0_ref, kk_ref, ka_ref, wlr_ref, ones_ref = (next(it) for _ in range(9))
    if gate_v:
        v0_ref, wvd_ref, wvu_ref, vf_ref = (next(it) for _ in range(4))
    r_ref, lw_ref, k_ref, v_ref, a_ref, b_ref, gg_ref = (next(it) for _ in range(7))
    z = z_ref[...]
    if carry_shift:
        carry_scr = next(it)

        @pl.when(pl.program_id(1) == 0)
        def _():
            carry_scr[...] = prev_ref[...]

        first = lax.broadcasted_iota(jnp.int32, z.shape, 0) == 0
        zprev = jnp.where(first, carry_scr[...], pltpu.roll(z, 1, 0))
        carry_scr[...] = z_ref[z.shape[0] - 1:z.shape[0], :]
    else:
        zprev = prev_ref[...]
    zs = z + (zprev - z) * mu_ref[...]
    rc, kc, vc = zs[:, :W_C], zs[:, W_C:2 * W_C], zs[:, 2 * W_C:3 * W_C]
    lr = zs[:, 3 * W_C:]
    col = lax.broadcasted_iota(jnp.int32, lr.shape, 1)
    act = jnp.where(col < R_DECAY, jnp.tanh(lr), jnp.where(col < R_DECAY + R_A, lr, jax.nn.sigmoid(lr)))
    up = jnp.dot(act.astype(BF16), wlr_ref[...], preferred_element_type=F32)
    y = -(w0_ref[...] + up[:, :W_C])
    softplus = jnp.maximum(y, 0.0) + jnp.log(1.0 + jnp.exp(-jnp.abs(y)))
    lw_ref[...] = -jnp.exp(-softplus - 0.5)
    if gate_v:
        down = jnp.dot(vc.astype(BF16), wvd_ref[...], preferred_element_type=F32)
        vgate = jax.nn.sigmoid(v0_ref[...] + jnp.dot(down.astype(BF16), wvu_ref[...], preferred_element_type=F32))
        vc = vc + (vf_ref[...] - vc) * vgate
    a = jax.nn.sigmoid(a0_ref[...] + up[:, W_C:2 * W_C])
    gg_ref[...] = up[:, 2 * W_C:]
    kk = kc * kk_ref[...]
    norm = jnp.sqrt(_seg_sum_wide(kk * kk, ones_ref[...]))
    kk = kk / jnp.maximum(norm, 1e-12)
    r_ref[...] = rc
    k_ref[...] = kc * (1.0 + (a - 1.0) * ka_ref[...])
    v_ref[...] = vc
    a_ref[...] = -kk
    b_ref[...] = kk * a


def rwkv_prep(z3, prev, mu_row, w0, a0, k_k, k_a, w_lr, vgate, tm, carry_shift):
    G, Lg, _ = z3.shape
    blk = lambda w, idx: pl.BlockSpec((None, tm, w), lambda g, l: (g, l, idx))
    row = lambda w: pl.BlockSpec((1, w), lambda g, l: (0, 0))
    prev_spec = (pl.BlockSpec((None, 1, RWKV_PAD), lambda g, l: (g, 0, 0)) if carry_shift
                 else blk(RWKV_PAD, 0))
    in_specs = [blk(RWKV_PAD, Z_RWKV // RWKV_PAD), prev_spec, row(RWKV_PAD), row(W_C), row(W_C), row(W_C),
                row(W_C), pl.BlockSpec(w_lr.shape, lambda g, l: (0, 0)),
                pl.BlockSpec((256, 128), lambda g, l: (0, 0))]
    args = [z3, prev, mu_row, w0, a0, k_k, k_a, w_lr, _block_ones2(DH_C)]
    if vgate is not None:
        v0, wvd, wvu, vfirst = vgate
        in_specs += [row(W_C), pl.BlockSpec(wvd.shape, lambda g, l: (0, 0)),
                     pl.BlockSpec(wvu.shape, lambda g, l: (0, 0)), blk(W_C, 0)]
        args += [v0, wvd, wvu, vfirst]
    return pl.pallas_call(
        functools.partial(_rwkv_prep_kernel, carry_shift=carry_shift, gate_v=vgate is not None),
        grid=(G, Lg // tm),
        in_specs=in_specs,
        out_specs=[blk(W_C, 0)] * 7,
        out_shape=[jax.ShapeDtypeStruct((G, Lg, W_C), F32)] * 7,
        scratch_shapes=[pltpu.VMEM((1, RWKV_PAD), F32)] if carry_shift else [],
        compiler_params=_cp("parallel", "arbitrary"),
        name="rwkv_prep",
    )(*args)


def _rwkv_chunk_kernel(r_ref, lw_ref, k_ref, v_ref, a_ref, b_ref, tri_ref, msl_ref, mli_ref, lvl_ref,
                       y_ref, so_ref, s_scr, *, n_chunks, nb):
    C = RW_CHUNK

    @pl.when(pl.program_id(1) == 0)
    def _():
        s_scr[...] = jnp.zeros_like(s_scr)

    head0 = lax.broadcasted_iota(jnp.int32, (C, 128), 1) < DH_C
    tri3, msl, mli = tri_ref[...], msl_ref[...], mli_ref[...]
    ri = lax.broadcasted_iota(jnp.int32, (128, 128), 0)
    ci = lax.broadcasted_iota(jnp.int32, (128, 128), 1)
    eye = (ri == ci).astype(F32)
    n_levels = lvl_ref.shape[0]
    chains = [(bi, p) for bi in range(nb) for p in range(H_C // 2)]

    def stack(x):
        return jnp.concatenate([jnp.where(head0, x, 0.0), jnp.where(head0, 0.0, x)], axis=0)

    def nt(x, y):
        return lax.dot_general(x, y, (((1,), (1,)), ((), ())), preferred_element_type=F32)

    def dot(x, y):
        return jnp.dot(x, y, preferred_element_type=F32)

    def lhs3(h, l):
        return jnp.concatenate([h, l, h], axis=1)

    def rhs3(x):
        h, l = _split2(x)
        return jnp.concatenate([h, h, l], axis=0)

    def chunk(ci_, carry):
        rows = pl.ds(pl.multiple_of(ci_ * C, C), C)
        ld = lambda ref, bi, p: ref[bi, rows, p * 128:(p + 1) * 128]
        lams = []
        for bi, p in chains:
            lw = ld(lw_ref, bi, p)
            hi = lw.astype(BF16)
            mid = (lw - hi.astype(F32)).astype(BF16)
            lo = (lw - hi.astype(F32) - mid.astype(F32)).astype(BF16)
            lams.append(dot(tri3, jnp.concatenate([hi, mid, lo], axis=0)))
        AR, BK, Vs, KB, dec = [], [], [], [], []
        for (bi, p), lam in zip(chains, lams):
            r, lw, k = ld(r_ref, bi, p), ld(lw_ref, bi, p), ld(k_ref, bi, p)
            v, a, b = ld(v_ref, bi, p), ld(a_ref, bi, p), ld(b_ref, bi, p)
            lam_c = lam[C - 1:C, :]
            e_neg = jnp.exp(-lam)
            e_tail = jnp.exp(lam_c - lam)
            AR.append(jnp.concatenate([stack(a * jnp.exp(lam - lw)), stack(r * jnp.exp(lam))],
                                      axis=0).astype(BF16))
            BK.append(jnp.concatenate([stack(b * e_neg), stack(k * e_neg)], axis=0).astype(BF16))
            KB.append(jnp.concatenate([stack(k * e_tail), stack(b * e_tail)], axis=0).astype(BF16))
            Vs.append(stack(v))
            dec.append(jnp.exp(lam_c))
        G = [nt(x, y) for x, y in zip(AR, BK)]
        I0 = [nt(x, s_scr[bi, p].astype(BF16)) for x, (bi, p) in zip(AR, chains)]
        rhs = [i0[:128] + dot((g[:128, 128:] * msl).astype(BF16), v.astype(BF16))
               for g, i0, v in zip(G, I0, Vs)]
        Nh, Nl, D = [], [], []
        for g in G:
            h, l = _split2(g[:128, :128] * msl)
            Nh.append(h)
            Nl.append(l)
            D.append(eye + (h * lvl_ref[0]).astype(F32) + (l * lvl_ref[0]).astype(F32))
        for lv in range(1, n_levels):
            m = lvl_ref[lv]
            Ds = [_split2(d) for d in D]
            X = [dot(lhs3(h * m, l * m), jnp.concatenate([dh, dh, dl], axis=0))
                 for h, l, (dh, dl) in zip(Nh, Nl, Ds)]
            D = [d + dot(lhs3(dh, dl), rhs3(x)) for d, (dh, dl), x in zip(D, Ds, X)]
        Ds = [_split2(d) for d in D]
        Us = [dot(lhs3(dh, dl), rhs3(x)) for (dh, dl), x in zip(Ds, rhs)]
        for (bi, p), g, i0, u, v in zip(chains, G, I0, Us, Vs):
            ys = i0[128:] + dot(jnp.concatenate([g[128:, :128] * mli, g[128:, 128:] * mli], axis=1).astype(BF16),
                                jnp.concatenate([u, v], axis=0).astype(BF16))
            y_ref[bi, rows, p * 128:(p + 1) * 128] = ys[:C] + ys[C:]
        for (bi, p), u, v, kb, d in zip(chains, Us, Vs, KB, dec):
            vu = jnp.concatenate([v, u], axis=0)
            s_scr[bi, p] = s_scr[bi, p] * d + dot(vu.T.astype(BF16), kb)
        return carry

    lax.fori_loop(0, n_chunks, chunk, 0)

    @pl.when(pl.program_id(1) == pl.num_programs(1) - 1)
    def _():
        so_ref[...] = s_scr[...]


def _rwkv_chunk_tables():
    C = RW_CHUNK
    t = np.arange(C)
    tri = (t[:, None] >= t[None, :]).astype(np.float32)
    tri3 = np.concatenate([tri, tri, tri], axis=1)
    i = np.arange(2 * C)
    same = (i[:, None] // C) == (i[None, :] // C)
    msl = (same & (i[:, None] > i[None, :])).astype(np.float32)
    mli = (same & (i[:, None] >= i[None, :])).astype(np.float32)
    levels = []
    n = 1
    while n < C:
        levels.append(same & ((i[:, None] // (2 * n)) == (i[None, :] // (2 * n)))
                      & ((i[:, None] // n) % 2 == 1) & ((i[None, :] // n) % 2 == 0))
        n *= 2
    return jnp.asarray(tri3, BF16), msl, mli, jnp.asarray(np.stack(levels).astype(np.float32), BF16)


def rwkv_chunk_scan(r, lw, k, v, a, b, tb, nb):
    B, L, _ = r.shape
    tri3, msl, mli, lvl = _rwkv_chunk_tables()
    blk = pl.BlockSpec((nb, tb, W_C), lambda bi, l: (bi, l, 0))
    c2 = lambda s: pl.BlockSpec(s, lambda bi, l: tuple(0 for _ in s))
    return pl.pallas_call(
        functools.partial(_rwkv_chunk_kernel, n_chunks=tb // RW_CHUNK, nb=nb),
        grid=(B // nb, L // tb),
        in_specs=[blk] * 6 + [c2(tri3.shape), c2(msl.shape), c2(mli.shape), c2(lvl.shape)],
        out_specs=[blk, pl.BlockSpec((nb, H_C // 2, 128, 128), lambda bi, l: (bi, 0, 0, 0))],
        out_shape=[jax.ShapeDtypeStruct((B, L, W_C), F32),
                   jax.ShapeDtypeStruct((B, H_C // 2, 128, 128), F32)],
        scratch_shapes=[pltpu.VMEM((nb, H_C // 2, 128, 128), F32)],
        compiler_params=_cp("parallel", "arbitrary"),
        name="rwkv_chunk_scan",
    )(r, lw, k, v, a, b, tri3, msl, mli, lvl)


def _rwkv_step_kernel(s_ref, w_ref, a_ref, b_ref, k_ref, r_ref, v_ref, y_ref, so_ref):
    S = s_ref[...]
    sa = jnp.sum(S * a_ref[...], axis=-1, keepdims=True)
    Sn = S * jnp.exp(w_ref[...]) + sa * b_ref[...] + v_ref[...] * k_ref[...]
    so_ref[...] = Sn
    y_ref[...] = jnp.sum(Sn * r_ref[...], axis=-1, keepdims=True)


def rwkv_step(state, layer, lw, a, b, k, r, v):
    B = lw.shape[0]
    bb = 8
    rowspec = pl.BlockSpec((bb, H_C, 1, DH_C), lambda i: (i, 0, 0, 0))
    colspec = pl.BlockSpec((bb, H_C, DH_C, 1), lambda i: (i, 0, 0, 0))
    sspec = pl.BlockSpec((bb, H_C, DH_C, DH_C), lambda i: (i, 0, 0, 0))
    return pl.pallas_call(
        _rwkv_step_kernel,
        grid=(B // bb,),
        in_specs=[pl.BlockSpec((None, bb, H_C, DH_C, DH_C), lambda i: (layer, i, 0, 0, 0))]
                 + [rowspec] * 5 + [colspec],
        out_specs=[colspec, sspec],
        out_shape=[jax.ShapeDtypeStruct((B, H_C, DH_C, 1), F32),
                   jax.ShapeDtypeStruct((B, H_C, DH_C, DH_C), F32)],
        compiler_params=_cp("parallel"),
        name="rwkv_step",
    )(state, lw, a, b, k, r, v)


def _merge_kernel(x_ref, ya_ref, yb_ref, y_ref, r_ref, k_ref, v_ref, gg_ref, za_ref, zb_ref, zc_ref, gt_ref,
                  lng_ref, lnb_ref, rk_ref, ones_ref, wa_ref, wb_ref, wc_ref, wo_ref, o_ref):
    ones2 = ones_ref[...]
    y = y_ref[...]
    mu = _seg_sum_wide(y, ones2) * (1.0 / DH_C)
    d = y - mu
    var = _seg_sum_wide(d * d, ones2) * (1.0 / DH_C)
    ycn = d * lax.rsqrt(var + LNX_EPS) * lng_ref[...] + lnb_ref[...]
    v = v_ref[...]
    bonus = _seg_sum_wide(r_ref[...] * k_ref[...] * rk_ref[...], ones2) * v
    yc = (ycn + bonus) * gg_ref[...]
    proj = lambda t, w: jnp.dot(t.astype(BF16), w[...], preferred_element_type=F32)
    merged = (jax.nn.sigmoid(za_ref[...]) * proj(ya_ref[...], wa_ref)
              + jax.nn.sigmoid(zb_ref[...]) * proj(yb_ref[...], wb_ref)
              + jax.nn.sigmoid(zc_ref[...]) * proj(yc, wc_ref))
    o_ref[...] = x_ref[...] + gt_ref[...] * proj(merged, wo_ref)


def merge_out(x2, ya, yb, y, r, k, v, gg, z2, gt, lng, lnb, rk, wa, wb, wc, wo, tm):
    T, D = x2.shape
    G, R, _ = gt.shape
    tiles_per_group = (T // tm) // G
    tok = lambda w: pl.BlockSpec((tm, w), lambda m: (m, 0))
    gate = lambda idx: pl.BlockSpec((tm, D), lambda m: (m, Z_GATE // D + idx))
    row = lambda w: pl.BlockSpec((1, w), lambda m: (0, 0))
    full = lambda a: pl.BlockSpec(a.shape, lambda m: (0, 0))
    return pl.pallas_call(
        _merge_kernel,
        grid=(T // tm,),
        in_specs=[tok(D)] + [tok(W_C)] * 7 + [gate(0), gate(1), gate(2),
                  pl.BlockSpec((None, R, D), lambda m: (m // tiles_per_group, 0, 0)),
                  row(W_C), row(W_C), row(W_C), pl.BlockSpec((256, 128), lambda m: (0, 0)),
                  full(wa), full(wb), full(wc), full(wo)],
        out_specs=tok(D),
        out_shape=jax.ShapeDtypeStruct((T, D), F32),
        compiler_params=_cp("parallel"),
        name="merge_out",
    )(x2, ya, yb, y, r, k, v, gg, z2, z2, z2, gt, lng, lnb, rk, _block_ones2(DH_C), wa, wb, wc, wo)


def _ffn_kernel(x_ref, g_ref, sc_ref, sh_ref, gt_ref, w1_ref, w2_ref, o_ref, h_scr, acc_scr):
    f = pl.program_id(1)

    @pl.when(f == 0)
    def _():
        x = x_ref[...]
        y = x * lax.rsqrt(jnp.mean(x * x, axis=-1, keepdims=True) + EPS) * g_ref[...]
        h_scr[...] = (y * (1.0 + sc_ref[...]) + sh_ref[...]).astype(BF16)
        acc_scr[...] = jnp.zeros_like(acc_scr)

    u = jnp.maximum(jnp.dot(h_scr[...], w1_ref[...], preferred_element_type=F32), 0.0)
    acc_scr[...] += jnp.dot((u * u).astype(BF16), w2_ref[...], preferred_element_type=F32)

    @pl.when(f == pl.num_programs(1) - 1)
    def _():
        o_ref[...] = x_ref[...] + gt_ref[...] * acc_scr[...]


def ffn(x2, gain_row, sc, sh, gt, w1, w2, tm, tf):
    T, D = x2.shape
    F = w1.shape[1]
    G, R, _ = sc.shape
    tiles_per_group = (T // tm) // G
    mod_spec = pl.BlockSpec((None, R, D), lambda m, f: (m // tiles_per_group, 0, 0))
    return pl.pallas_call(
        _ffn_kernel,
        grid=(T // tm, F // tf),
        in_specs=[pl.BlockSpec((tm, D), lambda m, f: (m, 0)),
                  pl.BlockSpec((1, D), lambda m, f: (0, 0)),
                  mod_spec, mod_spec, mod_spec,
                  pl.BlockSpec((D, tf), lambda m, f: (0, f)),
                  pl.BlockSpec((tf, D), lambda m, f: (f, 0))],
        out_specs=pl.BlockSpec((tm, D), lambda m, f: (m, 0)),
        out_shape=jax.ShapeDtypeStruct((T, D), F32),
        scratch_shapes=[pltpu.VMEM((tm, D), BF16), pltpu.VMEM((tm, D), F32)],
        compiler_params=_cp("parallel", "arbitrary"),
        name="ffn",
    )(x2, gain_row, sc, sh, gt, w1, w2)


def _prep_layer(p, l):
    D = p['w_in'].shape[1]
    w_in = p['w_in'][l]
    o_diff, o_rwkv, o_gate = RET_COLS, RET_COLS + DIFF_COLS, RET_COLS + DIFF_COLS + RWKV_COLS
    w_in_p = jnp.concatenate([
        w_in[:, :RET_COLS], w_in[:, o_rwkv:o_gate], jnp.zeros((D, RWKV_PAD - RWKV_COLS), F32),
        w_in[:, o_gate:], w_in[:, o_diff:o_rwkv]], axis=1).astype(BF16)
    w_lr = jnp.zeros((LR_COLS, 3 * W_C), F32)
    w_lr = w_lr.at[:R_DECAY, :W_C].set(p['w_decay_up'][l])
    w_lr = w_lr.at[R_DECAY:R_DECAY + R_A, W_C:2 * W_C].set(p['w_a_up'][l])
    w_lr = w_lr.at[R_DECAY + R_A:R_DECAY + R_A + R_G, 2 * W_C:].set(p['w_g_up'][l])
    row = lambda t: t.reshape(1, -1)
    lp = dict(
        w_ada=p['w_ada'][l].astype(BF16), b_ada=row(p['b_ada'][l]),
        norm1=row(p['norm1'][l]), norm2=row(p['norm2'][l]), w_in=w_in_p,
        gq=row(jnp.tile(p['qk_norm_q'][l], 2 * H_B)), gk=row(jnp.tile(p['qk_norm_k'][l], 2 * H_B)),
        lam_rows=jnp.stack([p['lambda_q1'][l], p['lambda_k1'][l], p['lambda_q2'][l], p['lambda_k2'][l]]),
        subln=row(p['subln_diff'][l]),
        mu=row(jnp.pad(p['mu_shift'][l], (0, RWKV_PAD - RWKV_COLS))),
        w0=row(p['w0'][l]), a0=row(p['a0'][l]), k_k=row(p['k_k'][l]), k_a=row(p['k_a'][l]),
        w_lr=w_lr.astype(BF16), r_k=row(p['r_k'][l]), lnx_g=row(p['lnx_g'][l]), lnx_b=row(p['lnx_b'][l]),
        w_up_a=p['w_up_a'][l].astype(BF16), w_up_b=p['w_up_b'][l].astype(BF16),
        w_up_c=p['w_up_c'][l].astype(BF16), w_out=p['w_out'][l].astype(BF16),
        w_ff1=p['w_ff1'][l].astype(BF16), w_ff2=p['w_ff2'][l].astype(BF16),
        lam_init=0.8 - 0.6 * math.exp(-0.3 * l),
    )
    if l > 0:
        lp['v0'] = row(p['v0'][l - 1])
        lp['w_v_down'] = jnp.pad(p['w_v_down'][l - 1], ((0, 0), (0, 128 - R_V))).astype(BF16)
        lp['w_v_up'] = jnp.pad(p['w_v_up'][l - 1], ((0, 128 - R_V), (0, 0))).astype(BF16)
    return lp


def _modulation(c, lp, per_token):
    B, D = c.shape
    rows = -(-B // 16) * 16
    mod = ada_mod(jnp.pad(c, ((0, rows - B), (0, 0))), lp['w_ada'], lp['b_ada'])[:B]
    parts = [mod[:, i * D:(i + 1) * D] for i in range(6)]
    shape = (1, B, D) if per_token else (B, 1, D)
    return [t.reshape(shape) for t in parts]


def _trunk_prompt(x, c, layers, bias_p):
    B, L, D = x.shape
    T = B * L
    tm = min(1024, L)
    t_attn = min(512, L)
    x2 = x.reshape(T, D)
    cos_t, sin_t = _rope_tables(jnp.arange(L))
    k_rows, v_rows, ret_out, rwkv_out, shift_out = [], [], [], [], []
    v_first = None
    for l, lp in enumerate(layers):
        sh1, sc1, gt1, sh2, sc2, gt2 = _modulation(c, lp, per_token=False)
        z2 = in_proj(x2, lp['norm1'], sc1, sh1, lp['w_in'], tm, 512)
        ZC = z2.shape[1]
        z3 = z2.reshape(B, L, ZC)
        ya, s_ret = retention_prompt(z3, cos_t, sin_t, min(256, L))
        qb, k32, kb, v32, vb = qk_norm(z2, lp['gq'], lp['gk'], tm)
        r3 = lambda t: t.reshape(B, L, W_B)
        yb = diff_attention_prompt(r3(qb), r3(kb), r3(vb), bias_p, lp['lam_rows'], lp['subln'],
                                   lp['lam_init'], t_attn)
        k_rows.append(k32.reshape(B, L, H_B, 2 * DH_B))
        v_rows.append(v32.reshape(B, L, H_B, DV_B))
        vgate = None if l == 0 else (lp['v0'], lp['w_v_down'], lp['w_v_up'], v_first)
        shift0 = jnp.zeros((B, 1, RWKV_PAD), F32)
        r, lw, k, v, a, b, gg = rwkv_prep(z3, shift0, lp['mu'], lp['w0'], lp['a0'], lp['k_k'], lp['k_a'],
                                          lp['w_lr'], vgate, min(256, L), carry_shift=True)
        if l == 0:
            v_first = v
        y, s_pair = rwkv_chunk_scan(r, lw, k, v, a, b, min(256, L), B)
        shift_out.append(z3[:, L - 1:, Z_RWKV:Z_RWKV + RWKV_COLS])
        f2 = lambda t: t.reshape(T, -1)
        x2 = merge_out(x2, f2(ya), f2(yb), f2(y), f2(r), f2(k), f2(v), f2(gg), z2, gt1, lp['lnx_g'],
                       lp['lnx_b'], lp['r_k'], lp['w_up_a'], lp['w_up_b'], lp['w_up_c'], lp['w_out'],
                       min(512, L))
        x2 = ffn(x2, lp['norm2'], sc2, sh2, gt2, lp['w_ff1'], lp['w_ff2'], tm, 1024)
        ret_out.append(s_ret)
        sp = s_pair.reshape(B, H_C // 2, 2, DH_C, 2, DH_C)
        rwkv_out.append(jnp.stack([sp[:, :, 0, :, 0, :], sp[:, :, 1, :, 1, :]], axis=2)
                        .reshape(B, H_C, DH_C, DH_C))
    return (x2.reshape(B, L, D), jnp.stack(k_rows), jnp.stack(v_rows), jnp.stack(ret_out),
            jnp.stack(rwkv_out), jnp.stack(shift_out))


def _trunk_decode(x, c, layers, bias_d, state_ret, state_rwkv, state_shift, cache_k, cache_v, page_table):
    B, _, D = x.shape
    n_pages = page_table.shape[1]
    past = n_pages * PAGE_SIZE
    x2 = x.reshape(B, D)
    pt_flat = page_table.reshape(-1)
    ck = cache_k.reshape(cache_k.shape[0], cache_k.shape[1], PAGE_SIZE * H_B, 2 * DH_B)
    cv = cache_v.reshape(cache_v.shape[0], cache_v.shape[1], PAGE_SIZE * H_B, DV_B)
    k_rows, v_rows, ret_out, rwkv_out, shift_out = [], [], [], [], []
    v_first = None
    for l, lp in enumerate(layers):
        sh1, sc1, gt1, sh2, sc2, gt2 = _modulation(c, lp, per_token=True)
        z2 = in_proj(x2, lp['norm1'], sc1, sh1, lp['w_in'], B, 512)
        ZC = z2.shape[1]
        ya, s_ret = retention_step(z2, state_ret, l, past)
        qb, k32, kb, v32, vb = qk_norm(z2, lp['gq'], lp['gk'], B)
        r3 = lambda t: t.reshape(B, 1, -1)
        yb = diff_attention_decode(r3(qb), r3(k32), r3(v32), ck, cv, pt_flat, bias_d, lp['lam_rows'],
                                   lp['subln'], l, lp['lam_init'], n_pages).reshape(B, W_B)
        k_rows.append(k32.reshape(B, 1, H_B, 2 * DH_B))
        v_rows.append(v32.reshape(B, 1, H_B, DV_B))
        vgate = None if l == 0 else (lp['v0'], lp['w_v_down'], lp['w_v_up'], v_first)
        prev = jnp.pad(state_shift[l].reshape(1, B, RWKV_COLS), ((0, 0), (0, 0), (0, RWKV_PAD - RWKV_COLS)))
        r, lw, k, v, a, b, gg = rwkv_prep(z2.reshape(1, B, ZC), prev, lp['mu'], lp['w0'], lp['a0'], lp['k_k'],
                                          lp['k_a'], lp['w_lr'], vgate, B, carry_shift=False)
        if l == 0:
            v_first = v
        keyrow = lambda t: t.reshape(B, H_C, 1, DH_C)
        y_col, s_rwkv = rwkv_step(state_rwkv, l, keyrow(lw), keyrow(a), keyrow(b), keyrow(k), keyrow(r),
                                  v.reshape(B, H_C, DH_C, 1))
        shift_out.append(z2[:, Z_RWKV:Z_RWKV + RWKV_COLS].reshape(B, 1, RWKV_COLS))
        f2 = lambda t: t.reshape(B, -1)
        x2 = merge_out(x2, ya, yb, f2(y_col), f2(r), f2(k), f2(v), f2(gg), z2, gt1, lp['lnx_g'], lp['lnx_b'],
                       lp['r_k'], lp['w_up_a'], lp['w_up_b'], lp['w_up_c'], lp['w_out'], B)
        x2 = ffn(x2, lp['norm2'], sc2, sh2, gt2, lp['w_ff1'], lp['w_ff2'], B, 1024)
        ret_out.append(s_ret)
        rwkv_out.append(s_rwkv)
    return (x2.reshape(B, 1, D), jnp.stack(k_rows), jnp.stack(v_rows), jnp.stack(ret_out),
            jnp.stack(rwkv_out), jnp.stack(shift_out))


def _decode_buckets(n_pages):
    past = n_pages * PAGE_SIZE
    key = np.arange((n_pages + 1) * PAGE_SIZE)
    bk = np.where(key <= past, _bucket_np(np.maximum(past - key, 0)), -1)
    return np.broadcast_to(bk.reshape(n_pages + 1, 1, PAGE_SIZE), (n_pages + 1, 8, PAGE_SIZE)).astype(np.int32)


def kernel(x_prompt, x_sample, c_prompt, c_sample, cache_k_diff, cache_v_diff, page_table, state_ret, state_rwkv, state_shift, rel_bias, w_ada, b_ada, norm1, norm2, w_in, qk_norm_q, qk_norm_k, lambda_q1, lambda_k1, lambda_q2, lambda_k2, subln_diff, mu_shift, w0, w_decay_up, a0, w_a_up, w_g_up, v0, w_v_down, w_v_up, k_k, k_a, r_k, lnx_g, lnx_b, w_up_a, w_up_b, w_up_c, w_out, w_ff1, w_ff2):
    p = dict(w_ada=w_ada, b_ada=b_ada, norm1=norm1, norm2=norm2, w_in=w_in,
             qk_norm_q=qk_norm_q, qk_norm_k=qk_norm_k, lambda_q1=lambda_q1, lambda_k1=lambda_k1,
             lambda_q2=lambda_q2, lambda_k2=lambda_k2, subln_diff=subln_diff, mu_shift=mu_shift,
             w0=w0, w_decay_up=w_decay_up, a0=a0, w_a_up=w_a_up, w_g_up=w_g_up, v0=v0,
             w_v_down=w_v_down, w_v_up=w_v_up, k_k=k_k, k_a=k_a, r_k=r_k, lnx_g=lnx_g, lnx_b=lnx_b,
             w_up_a=w_up_a, w_up_b=w_up_b, w_up_c=w_up_c, w_out=w_out, w_ff1=w_ff1, w_ff2=w_ff2)
    depth = w_in.shape[0]
    layers = [_prep_layer(p, l) for l in range(depth)]
    L = x_prompt.shape[1]
    n_pages = page_table.shape[1]
    t_attn = min(512, L)
    bias_p = bias_tiles(rel_bias, _prompt_buckets(t_attn))
    bd = bias_tiles(rel_bias, _decode_buckets(n_pages).reshape((n_pages + 1) * 8, PAGE_SIZE))
    bias_d = bd.reshape(H_B, n_pages + 1, 8, PAGE_SIZE)

    y_p, k_p, v_p, ret_p, rwkv_p, shift_p = _trunk_prompt(x_prompt, c_prompt, layers, bias_p)
    y_s, k_s, v_s, ret_s, rwkv_s, shift_s = _trunk_decode(
        x_sample, c_sample, layers, bias_d, state_ret, state_rwkv, state_shift,
        cache_k_diff, cache_v_diff, page_table)
    return (y_p, y_s, k_p, v_p, k_s, v_s, ret_p, ret_s, rwkv_p, rwkv_s, shift_p, shift_s)
```

```python
import functools
import math

import numpy as np
import jax
import jax.numpy as jnp
from jax import lax
from jax.experimental import pallas as pl
from jax.experimental.pallas import tpu as pltpu

F32 = jnp.float32
BF16 = jnp.bfloat16
HIGHEST = lax.Precision.HIGHEST

H_A, DK_A, DV_A = 4, 128, 128
RET_CHUNK = 128
ROPE_BASE = 10000.0
H_B, DH_B, DV_B = 4, 64, 128
N_BUCKETS, MAX_DISTANCE = 32, 128
H_C, DH_C = 8, 64
R_DECAY, R_A, R_V, R_G = 64, 64, 32, 160
LNX_EPS = 64e-5
EPS = 1e-6
PAGE_SIZE = 128
W_AQ, W_A = H_A * DK_A, H_A * DV_A
W_BQK, W_B = H_B * 2 * DH_B, H_B * DV_B
W_C = H_C * DH_C
RET_COLS = 2 * W_AQ + 2 * W_A
DIFF_COLS = 2 * W_BQK + W_B
RWKV_COLS = 3 * W_C + R_DECAY + R_A + R_G
RWKV_PAD = 2048
LR_COLS = RWKV_PAD - 3 * W_C

Z_RET, Z_RWKV = 0, RET_COLS
Z_GATE = Z_RWKV + RWKV_PAD
NEG = -1e30
RW_CHUNK = 64
VMEM_LIMIT = 56 * 1024 * 1024


def _cp(*sem):
    return pltpu.CompilerParams(dimension_semantics=sem, vmem_limit_bytes=VMEM_LIMIT)


def _silu(x):
    return x * jax.nn.sigmoid(x)


def _split2(x):
    hi = x.astype(BF16)
    lo = (x - hi.astype(F32)).astype(BF16)
    return hi, lo


def _seg_sum(x, ones2):
    hi, lo = _split2(x)
    return jnp.dot(jnp.concatenate([hi, lo], axis=1), ones2, preferred_element_type=F32)


def _seg_sum_wide(x, ones2):
    return jnp.concatenate(
        [_seg_sum(x[:, c * 128:(c + 1) * 128], ones2) for c in range(x.shape[1] // 128)], axis=1)


def _mod_kernel(c_ref, w_ref, b_ref, o_ref):
    s = _silu(c_ref[...])
    o_ref[...] = jnp.dot(s.astype(BF16), w_ref[...], preferred_element_type=F32) + b_ref[...]


def ada_mod(c_pad, w_bf, b_row):
    R, D = c_pad.shape
    N = w_bf.shape[1]
    tn = 1536
    return pl.pallas_call(
        _mod_kernel,
        grid=(N // tn,),
        in_specs=[pl.BlockSpec((R, D), lambda n: (0, 0)),
                  pl.BlockSpec((D, tn), lambda n: (0, n)),
                  pl.BlockSpec((1, tn), lambda n: (0, n))],
        out_specs=pl.BlockSpec((R, tn), lambda n: (0, n)),
        out_shape=jax.ShapeDtypeStruct((R, N), F32),
        compiler_params=_cp("parallel"),
        name="ada_mod",
    )(c_pad, w_bf, b_row)


def _inproj_kernel(x_ref, g_ref, sc_ref, sh_ref, w_ref, z_ref, h_scr):
    @pl.when(pl.program_id(1) == 0)
    def _():
        x = x_ref[...]
        y = x * lax.rsqrt(jnp.mean(x * x, axis=-1, keepdims=True) + EPS) * g_ref[...]
        h_scr[...] = (y * (1.0 + sc_ref[...]) + sh_ref[...]).astype(BF16)

    z_ref[...] = jnp.dot(h_scr[...], w_ref[...], preferred_element_type=F32)


def in_proj(x2, gain_row, sc, sh, w_bf, tm, tn):
    T, D = x2.shape
    N = w_bf.shape[1]
    G, R, _ = sc.shape
    tiles_per_group = (T // tm) // G
    mod_spec = pl.BlockSpec((None, R, D), lambda m, n: (m // tiles_per_group, 0, 0))
    return pl.pallas_call(
        _inproj_kernel,
        grid=(T // tm, N // tn),
        in_specs=[pl.BlockSpec((tm, D), lambda m, n: (m, 0)),
                  pl.BlockSpec((1, D), lambda m, n: (0, 0)),
                  mod_spec, mod_spec,
                  pl.BlockSpec((D, tn), lambda m, n: (0, n))],
        out_specs=pl.BlockSpec((tm, tn), lambda m, n: (m, n)),
        out_shape=jax.ShapeDtypeStruct((T, N), F32),
        scratch_shapes=[pltpu.VMEM((tm, D), BF16)],
        compiler_params=_cp("parallel", "arbitrary"),
        name="in_proj",
    )(x2, gain_row, sc, sh, w_bf)


def _rope(x, cos, sin_signed):
    return x * cos + pltpu.roll(x, DK_A // 2, 1) * sin_signed


def _ret_kernel(q_ref, k_ref, v_ref, g_ref, cos_ref, sin_ref, inner_ref, cross_ref, tail_ref, cd_ref,
                ya_ref, so_ref, s_scr, *, n_sub):
    lt = pl.program_id(1)

    @pl.when(lt == 0)
    def _():
        s_scr[...] = jnp.zeros_like(s_scr)

    C = RET_CHUNK
    for c in range(n_sub):
        rows = slice(c * C, (c + 1) * C)
        cos = cos_ref[rows, :]
        sin = sin_ref[rows, :]
        for h in range(H_A):
            cols = slice(h * DK_A, (h + 1) * DK_A)
            q = _rope(q_ref[rows, cols], cos, sin)
            k = _rope(k_ref[rows, cols], cos, sin) * (DK_A ** -0.5)
            qb, kb, vb = q.astype(BF16), k.astype(BF16), v_ref[rows, cols].astype(BF16)
            S = s_scr[h]
            sc = lax.dot_general(qb, kb, (((1,), (1,)), ((), ())), preferred_element_type=F32) * inner_ref[h]
            o = (jnp.dot(sc.astype(BF16), vb, preferred_element_type=F32)
                 + jnp.dot(qb, S.astype(BF16), preferred_element_type=F32) * cross_ref[h])
            kt = (k * tail_ref[h]).T.astype(BF16)
            s_scr[h] = S * cd_ref[h] + jnp.dot(kt, vb, preferred_element_type=F32)
            on = o * lax.rsqrt(jnp.mean(o * o, axis=-1, keepdims=True) + EPS)
            ya_ref[rows, cols] = on * _silu(g_ref[rows, cols])

    @pl.when(lt == pl.num_programs(1) - 1)
    def _():
        so_ref[...] = s_scr[...]


def _ret_tables(L):
    C = math.gcd(L, RET_CHUNK)
    log_g = np.log1p(-np.exp2(-5.0 - np.arange(H_A, dtype=np.float32))).astype(np.float32)
    i = np.arange(C, dtype=np.float32)
    dist = i[:, None] - i[None, :]
    causal = dist >= 0
    inner = np.where(causal[None], np.exp(np.where(causal, dist, 0.0)[None] * log_g[:, None, None]), 0.0)
    cross = np.exp((i[None, :] + 1.0) * log_g[:, None])
    tail = np.exp((C - 1.0 - i)[None, :] * log_g[:, None])
    chunk = np.exp(C * log_g)
    bc = lambda t: np.broadcast_to(t[:, :, None], (H_A, C, 128)).astype(np.float32)
    cd = np.broadcast_to(chunk[:, None, None], (H_A, 1, 128)).astype(np.float32)
    return inner.astype(np.float32), bc(cross), bc(tail), cd


def _rope_tables(pos):
    half = DK_A // 2
    inv = ROPE_BASE ** (-jnp.arange(half, dtype=F32) / half)
    ang = pos.astype(F32)[:, None] * inv[None, :]
    cos, sin = jnp.cos(ang), jnp.sin(ang)
    return jnp.concatenate([cos, cos], axis=-1), jnp.concatenate([-sin, sin], axis=-1)


def retention_prompt(z3, cos_t, sin_t, tb):
    B, L, ZC = z3.shape
    inner, cross, tail, cd = _ret_tables(L)
    zspec = lambda idx: pl.BlockSpec((None, tb, W_AQ), lambda b, l: (b, l, idx))
    full3 = lambda s: pl.BlockSpec(s, lambda b, l: (0, 0, 0))
    return pl.pallas_call(
        functools.partial(_ret_kernel, n_sub=tb // RET_CHUNK),
        grid=(B, L // tb),
        in_specs=[zspec(0), zspec(1), zspec(2), zspec(3),
                  pl.BlockSpec((tb, 128), lambda b, l: (l, 0)),
                  pl.BlockSpec((tb, 128), lambda b, l: (l, 0)),
                  full3(inner.shape), full3(cross.shape), full3(tail.shape), full3(cd.shape)],
        out_specs=[pl.BlockSpec((None, tb, W_A), lambda b, l: (b, l, 0)),
                   pl.BlockSpec((None, H_A, DK_A, DV_A), lambda b, l: (b, 0, 0, 0))],
        out_shape=[jax.ShapeDtypeStruct((B, L, W_A), F32),
                   jax.ShapeDtypeStruct((B, H_A, DK_A, DV_A), F32)],
        scratch_shapes=[pltpu.VMEM((H_A, DK_A, DV_A), F32)],
        compiler_params=_cp("parallel", "arbitrary"),
        name="retention_prompt",
    )(z3, z3, z3, z3, cos_t, sin_t, inner, cross, tail, cd)


def _ret_step_kernel(q_ref, k_ref, v_ref, g_ref, cos_ref, sin_ref, gam_ref, s_ref, ya_ref, so_ref, *, bb):
    cos, sin = cos_ref[...], sin_ref[...]
    row = lax.broadcasted_iota(jnp.int32, (bb, 128), 0)
    for h in range(H_A):
        cols = slice(h * DK_A, (h + 1) * DK_A)
        gam = gam_ref[h]
        q = _rope(q_ref[:, cols], cos, sin)
        k = _rope(k_ref[:, cols], cos, sin) * (DK_A ** -0.5)
        v = v_ref[:, cols]
        qb = q.astype(BF16)
        qk = jnp.sum(qb.astype(F32) * k.astype(BF16).astype(F32), axis=-1, keepdims=True)
        qs = jnp.zeros((bb, 128), F32)
        for b in range(bb):
            S = s_ref[b, h]
            onehot = row == b
            qs = jnp.where(onehot, jnp.dot(qb, S.astype(BF16), preferred_element_type=F32), qs)
            kb_t = jnp.where(onehot, k, 0.0).T.astype(BF16)
            so_ref[b, h] = S * gam + jnp.dot(kb_t, v.astype(BF16), preferred_element_type=F32)
        o = qk * v.astype(BF16).astype(F32) + qs * gam
        on = o * lax.rsqrt(jnp.mean(o * o, axis=-1, keepdims=True) + EPS)
        ya_ref[:, cols] = on * _silu(g_ref[:, cols])


def retention_step(z2, state, layer, pos):
    B = z2.shape[0]
    bb = 8
    cos_t, sin_t = _rope_tables(jnp.full((1,), pos))
    log_g = np.log1p(-np.exp2(-5.0 - np.arange(H_A, dtype=np.float32))).astype(np.float32)
    gam = np.broadcast_to(np.exp(log_g)[:, None, None], (H_A, 1, 128)).astype(np.float32)
    zspec = lambda idx: pl.BlockSpec((bb, W_AQ), lambda i: (i, idx))
    return pl.pallas_call(
        functools.partial(_ret_step_kernel, bb=bb),
        grid=(B // bb,),
        in_specs=[zspec(0), zspec(1), zspec(2), zspec(3),
                  pl.BlockSpec((1, 128), lambda i: (0, 0)),
                  pl.BlockSpec((1, 128), lambda i: (0, 0)),
                  pl.BlockSpec((H_A, 1, 128), lambda i: (0, 0, 0)),
                  pl.BlockSpec((None, bb, H_A, DK_A, DV_A), lambda i: (layer, i, 0, 0, 0))],
        out_specs=[pl.BlockSpec((bb, W_A), lambda i: (i, 0)),
                   pl.BlockSpec((bb, H_A, DK_A, DV_A), lambda i: (i, 0, 0, 0))],
        out_shape=[jax.ShapeDtypeStruct((B, W_A), F32),
                   jax.ShapeDtypeStruct((B, H_A, DK_A, DV_A), F32)],
        compiler_params=_cp("parallel"),
        name="retention_step",
    )(z2, z2, z2, z2, cos_t, sin_t, gam, state)


def _qknorm_kernel(q_ref, k_ref, v_ref, gq_ref, gk_ref, ones_ref, qb_ref, k32_ref, kb_ref, v32_ref, vb_ref):
    ones2 = ones_ref[...]
    q, k, v = q_ref[...], k_ref[...], v_ref[...]
    qn = q * lax.rsqrt(_seg_sum_wide(q * q, ones2) * (1.0 / DH_B) + EPS) * gq_ref[...] * (DH_B ** -0.5)
    kn = k * lax.rsqrt(_seg_sum_wide(k * k, ones2) * (1.0 / DH_B) + EPS) * gk_ref[...]
    qb_ref[...] = qn.astype(BF16)
    k32_ref[...] = kn
    kb_ref[...] = kn.astype(BF16)
    v32_ref[...] = v
    vb_ref[...] = v.astype(BF16)


def _block_ones2(group):
    i = np.arange(128)
    m = (i[:, None] // group == i[None, :] // group).astype(np.float32)
    return jnp.asarray(np.concatenate([m, m], axis=0), BF16)


def qk_norm(z2, gq_row, gk_row, tm):
    T = z2.shape[0]
    base = (Z_GATE + 3 * 1024) // W_BQK
    zspec = lambda idx: pl.BlockSpec((tm, W_BQK), lambda m: (m, base + idx))
    row = pl.BlockSpec((1, W_BQK), lambda m: (0, 0))
    out = pl.BlockSpec((tm, W_BQK), lambda m: (m, 0))
    return pl.pallas_call(
        _qknorm_kernel,
        grid=(T // tm,),
        in_specs=[zspec(0), zspec(1), zspec(2), row, row, pl.BlockSpec((256, 128), lambda m: (0, 0))],
        out_specs=[out] * 5,
        out_shape=[jax.ShapeDtypeStruct((T, W_BQK), BF16), jax.ShapeDtypeStruct((T, W_BQK), F32),
                   jax.ShapeDtypeStruct((T, W_BQK), BF16), jax.ShapeDtypeStruct((T, W_B), F32),
                   jax.ShapeDtypeStruct((T, W_B), BF16)],
        compiler_params=_cp("parallel"),
        name="qk_norm",
    )(z2, z2, z2, gq_row, gk_row, _block_ones2(DH_B))


def _bucket_np(n):
    max_exact = N_BUCKETS // 2
    nf = np.maximum(n, 1).astype(np.float32)
    large = max_exact + (np.log(nf / np.float32(max_exact)) / np.float32(math.log(MAX_DISTANCE / max_exact))
                         * np.float32(N_BUCKETS - max_exact)).astype(np.int32)
    large = np.minimum(large, N_BUCKETS - 1)
    return np.where(n < max_exact, n, large).astype(np.int32)


def _bias_kernel(tab_ref, bkt_ref, o_ref):
    h = pl.program_id(0)
    bk = bkt_ref[...]
    acc = jnp.full(bk.shape, NEG, F32)
    for b in range(N_BUCKETS):
        acc = jnp.where(bk == b, tab_ref[b, h], acc)
    o_ref[...] = acc


def bias_tiles(rel_bias, buckets):
    R, C = buckets.shape
    tr = min(R, 512)
    return pl.pallas_call(
        _bias_kernel,
        grid=(H_B, R // tr),
        in_specs=[pl.BlockSpec(memory_space=pltpu.SMEM),
                  pl.BlockSpec((tr, C), lambda h, r: (r, 0))],
        out_specs=pl.BlockSpec((None, tr, C), lambda h, r: (h, r, 0)),
        out_shape=jax.ShapeDtypeStruct((H_B, R, C), F32),
        compiler_params=_cp("parallel", "parallel"),
        name="bias_tiles",
    )(rel_bias, jnp.asarray(buckets))


def _prompt_buckets(t):
    r = np.arange(t)[:, None]
    c = np.arange(t)[None, :]
    diag = np.where(c <= r, _bucket_np(np.maximum(r - c, 0)), -1)
    off1 = _bucket_np(t + r - c)
    far = _bucket_np(np.full((t, t), 2 * t))
    assert t >= MAX_DISTANCE
    return np.concatenate([diag, off1, far], axis=0).astype(np.int32)


def _lambda_full(lam_ref, lam_init):
    lv = lam_ref[...]
    s1 = jnp.sum(lv[0:1] * lv[1:2], axis=-1, keepdims=True)
    s2 = jnp.sum(lv[2:3] * lv[3:4], axis=-1, keepdims=True)
    return jnp.exp(s1) - jnp.exp(s2) + lam_init


def _flash_kernel(q_ref, k_ref, v_ref, bias_ref, lam_ref, gain_ref, o_ref, qq_scr, m_scr, acc_scr,
                  *, lam_init, sub):
    i = pl.program_id(2)
    t = q_ref.shape[0]

    q = q_ref[...]
    first = lax.broadcasted_iota(jnp.int32, q.shape, 1) < DH_B
    qq_scr[:t] = jnp.where(first, q, jnp.zeros_like(q))
    qq_scr[t:] = jnp.where(first, jnp.zeros_like(q), q)
    m_scr[...] = jnp.full(m_scr.shape, NEG, F32)
    acc_scr[...] = jnp.zeros_like(acc_scr)

    def kv_block(j, carry):
        krows = pl.ds(pl.multiple_of(j * t, t), t)
        k = k_ref[krows, :]
        v1 = jnp.concatenate([v_ref[krows, :], jnp.ones((t, DV_B), BF16)], axis=1)
        tile = jnp.minimum(i - j, 2)

        def scores(u):
            rows = slice(u * sub, (u + 1) * sub)
            brow = pl.multiple_of(tile * t + (u * sub) % t, sub)
            s = lax.dot_general(qq_scr[rows], k, (((1,), (1,)), ((), ())), preferred_element_type=F32)
            return s + bias_ref[pl.ds(brow, sub), :]

        n_sub = 2 * t // sub
        s_next = scores(0)
        for u in range(n_sub):
            s = s_next
            if u + 1 < n_sub:
                s_next = scores(u + 1)
            rows = slice(u * sub, (u + 1) * sub)
            m_old = m_scr[rows]
            m_new = jnp.maximum(m_old, jnp.max(s, axis=-1, keepdims=True))
            alpha = jnp.exp(m_old - m_new)
            p = jnp.concatenate([jnp.exp(s[:, c * 128:(c + 1) * 128] - m_new) for c in range(t // 128)],
                                axis=1).astype(BF16)
            acc_scr[rows] = (jnp.concatenate([alpha, alpha], axis=1) * acc_scr[rows]
                             + jnp.dot(p, v1, preferred_element_type=F32))
            m_scr[rows] = m_new
        return carry

    lax.fori_loop(0, i + 1, kv_block, 0)

    lam = _lambda_full(lam_ref, lam_init)
    acc = acc_scr[...]
    ob = acc[:t, :DV_B] / acc[:t, DV_B:] - lam * (acc[t:, :DV_B] / acc[t:, DV_B:])
    on = ob * lax.rsqrt(jnp.mean(ob * ob, axis=-1, keepdims=True) + EPS) * gain_ref[...]
    o_ref[...] = on * (1.0 - lam_init)


def diff_attention_prompt(qb, kb, vb, bias, lam_rows, gain_row, lam_init, t):
    B, L, _ = qb.shape
    n = L // t
    kv_spec = pl.BlockSpec((None, L, DV_B), lambda b, h, i: (b, 0, h))
    return pl.pallas_call(
        functools.partial(_flash_kernel, lam_init=lam_init, sub=min(256, t)),
        grid=(B, H_B, n),
        in_specs=[pl.BlockSpec((None, t, DV_B), lambda b, h, i: (b, i, h)),
                  kv_spec, kv_spec,
                  pl.BlockSpec((None, 3 * t, t), lambda b, h, i: (h, 0, 0)),
                  pl.BlockSpec((4, DH_B), lambda b, h, i: (0, 0)),
                  pl.BlockSpec((1, DV_B), lambda b, h, i: (0, 0))],
        out_specs=pl.BlockSpec((None, t, DV_B), lambda b, h, i: (b, i, h)),
        out_shape=jax.ShapeDtypeStruct((B, L, W_B), F32),
        scratch_shapes=[pltpu.VMEM((2 * t, DV_B), BF16), pltpu.VMEM((2 * t, 128), F32),
                        pltpu.VMEM((2 * t, 2 * DV_B), F32)],
        compiler_params=_cp("parallel", "parallel", "arbitrary"),
        name="diff_attention_prompt",
    )(qb, kb, vb, bias, lam_rows, gain_row)


def _decode_kernel(pt_ref, q_ref, kn_ref, vn_ref, bias_ref, lam_ref, gain_ref, *rest, n_pages, lam_init):
    k_refs, v_refs, o_ref = rest[:n_pages], rest[n_pages:2 * n_pages], rest[2 * n_pages]
    lam = _lambda_full(lam_ref, lam_init)
    gain = gain_ref[...]
    r8 = lax.broadcasted_iota(jnp.int32, (8, 2 * DH_B), 0)
    l8 = lax.broadcasted_iota(jnp.int32, (8, 2 * DH_B), 1)
    comp_rows = (l8 // DH_B) == r8
    row0 = lax.broadcasted_iota(jnp.int32, (PAGE_SIZE, DV_B), 0) == 0
    nt = lambda x, y: lax.dot_general(x, y, (((1,), (1,)), ((), ())), preferred_element_type=F32)
    heads = range(H_B)
    cols = [slice(h * DV_B, (h + 1) * DV_B) for h in heads]
    head_rows = [pl.ds(h, PAGE_SIZE, stride=H_B) for h in heads]
    ss = []
    for h in heads:
        qrows = jnp.where(comp_rows, q_ref[:, cols[h]].astype(F32), 0.0).astype(BF16)
        ks = [r[head_rows[h], :].astype(BF16) for r in k_refs]
        ks.append(jnp.where(row0, kn_ref[:, cols[h]], 0.0).astype(BF16))
        ss.append([nt(qrows, kp) + bias_ref[h, p] for p, kp in enumerate(ks)])
    ps, ls = [], []
    for h in heads:
        m = functools.reduce(jnp.maximum, [jnp.max(s, axis=-1, keepdims=True) for s in ss[h]])
        p = [jnp.exp(s - m) for s in ss[h]]
        ls.append(functools.reduce(jnp.add, [jnp.sum(x, axis=-1, keepdims=True) for x in p]))
        ps.append([x.astype(BF16) for x in p])
    for h in heads:
        vs = [r[head_rows[h], :].astype(BF16) for r in v_refs]
        vs.append(jnp.where(row0, vn_ref[:, cols[h]], 0.0).astype(BF16))
        acc = functools.reduce(jnp.add, [jnp.dot(p, vp, preferred_element_type=F32)
                                         for p, vp in zip(ps[h], vs)])
        outn = acc / ls[h]
        oh = outn[0:1] - lam * outn[1:2]
        on = oh * lax.rsqrt(jnp.mean(oh * oh, axis=-1, keepdims=True) + EPS) * gain
        o_ref[:, cols[h]] = on * (1.0 - lam_init)


def diff_attention_decode(qb, k32, v32, cache_k, cache_v, pt_flat, bias_dec, lam_rows, gain_row, layer,
                          lam_init, n_pages):
    B = qb.shape[0]
    rowspec = pl.BlockSpec((None, 1, W_BQK), lambda b, pt: (b, 0, 0))
    const = lambda s: pl.BlockSpec(s, lambda b, pt: tuple(0 for _ in s))
    page_spec = lambda p: pl.BlockSpec((None, None, PAGE_SIZE * H_B, DV_B),
                                       lambda b, pt, p=p: (layer, pt[b * n_pages + p], 0, 0))
    grid_spec = pltpu.PrefetchScalarGridSpec(
        num_scalar_prefetch=1,
        grid=(B,),
        in_specs=[rowspec, rowspec, rowspec, const(bias_dec.shape), const((4, DH_B)), const((1, DV_B))]
                 + [page_spec(p) for p in range(n_pages)] * 2,
        out_specs=pl.BlockSpec((None, 1, W_B), lambda b, pt: (b, 0, 0)),
    )
    return pl.pallas_call(
        functools.partial(_decode_kernel, n_pages=n_pages, lam_init=lam_init),
        grid_spec=grid_spec,
        out_shape=jax.ShapeDtypeStruct((B, 1, W_B), F32),
        compiler_params=_cp("arbitrary"),
        name="diff_attention_decode",
    )(pt_flat, qb, k32, v32, bias_dec, lam_rows, gain_row,
      *([cache_k] * n_pages), *([cache_v] * n_pages))


def _rwkv_prep_kernel(*refs, carry_shift, gate_v):
    it = iter(refs)
    z_ref, prev_ref, mu_ref, w0_ref, a0_ref, kk_ref, ka_ref, wlr_ref, ones_ref = (next(it) for _ in range(9))
    if gate_v:
        v0_ref, wvd_ref, wvu_ref, vf_ref = (next(it) for _ in range(4))
    r_ref, lw_ref, k_ref, v_ref, a_ref, b_ref, gg_ref = (next(it) for _ in range(7))
    z = z_ref[...]
    if carry_shift:
        carry_scr = next(it)

        @pl.when(pl.program_id(1) == 0)
        def _():
            carry_scr[...] = prev_ref[...]

        first = lax.broadcasted_iota(jnp.int32, z.shape, 0) == 0
        zprev = jnp.where(first, carry_scr[...], pltpu.roll(z, 1, 0))
        carry_scr[...] = z_ref[z.shape[0] - 1:z.shape[0], :]
    else:
        zprev = prev_ref[...]
    zs = z + (zprev - z) * mu_ref[...]
    rc, kc, vc = zs[:, :W_C], zs[:, W_C:2 * W_C], zs[:, 2 * W_C:3 * W_C]
    lr = zs[:, 3 * W_C:]
    col = lax.broadcasted_iota(jnp.int32, lr.shape, 1)
    act = jnp.where(col < R_DECAY, jnp.tanh(lr), jnp.where(col < R_DECAY + R_A, lr, jax.nn.sigmoid(lr)))
    up = jnp.dot(act.astype(BF16), wlr_ref[...], preferred_element_type=F32)
    y = -(w0_ref[...] + up[:, :W_C])
    softplus = jnp.maximum(y, 0.0) + jnp.log(1.0 + jnp.exp(-jnp.abs(y)))
    lw_ref[...] = -jnp.exp(-softplus - 0.5)
    if gate_v:
        down = jnp.dot(vc.astype(BF16), wvd_ref[...], preferred_element_type=F32)
        vgate = jax.nn.sigmoid(v0_ref[...] + jnp.dot(down.astype(BF16), wvu_ref[...], preferred_element_type=F32))
        vc = vc + (vf_ref[...] - vc) * vgate
    a = jax.nn.sigmoid(a0_ref[...] + up[:, W_C:2 * W_C])
    gg_ref[...] = up[:, 2 * W_C:]
    kk = kc * kk_ref[...]
    norm = jnp.sqrt(_seg_sum_wide(kk * kk, ones_ref[...]))
    kk = kk / jnp.maximum(norm, 1e-12)
    r_ref[...] = rc
    k_ref[...] = kc * (1.0 + (a - 1.0) * ka_ref[...])
    v_ref[...] = vc
    a_ref[...] = -kk
    b_ref[...] = kk * a


def rwkv_prep(z3, prev, mu_row, w0, a0, k_k, k_a, w_lr, vgate, tm, carry_shift):
    G, Lg, _ = z3.shape
    blk = lambda w, idx: pl.BlockSpec((None, tm, w), lambda g, l: (g, l, idx))
    row = lambda w: pl.BlockSpec((1, w), lambda g, l: (0, 0))
    prev_spec = (pl.BlockSpec((None, 1, RWKV_PAD), lambda g, l: (g, 0, 0)) if carry_shift
                 else blk(RWKV_PAD, 0))
    in_specs = [blk(RWKV_PAD, Z_RWKV // RWKV_PAD), prev_spec, row(RWKV_PAD), row(W_C), row(W_C), row(W_C),
                row(W_C), pl.BlockSpec(w_lr.shape, lambda g, l: (0, 0)),
                pl.BlockSpec((256, 128), lambda g, l: (0, 0))]
    args = [z3, prev, mu_row, w0, a0, k_k, k_a, w_lr, _block_ones2(DH_C)]
    if vgate is not None:
        v0, wvd, wvu, vfirst = vgate
        in_specs += [row(W_C), pl.BlockSpec(wvd.shape, lambda g, l: (0, 0)),
                     pl.BlockSpec(wvu.shape, lambda g, l: (0, 0)), blk(W_C, 0)]
        args += [v0, wvd, wvu, vfirst]
    return pl.pallas_call(
        functools.partial(_rwkv_prep_kernel, carry_shift=carry_shift, gate_v=vgate is not None),
        grid=(G, Lg // tm),
        in_specs=in_specs,
        out_specs=[blk(W_C, 0)] * 7,
        out_shape=[jax.ShapeDtypeStruct((G, Lg, W_C), F32)] * 7,
        scratch_shapes=[pltpu.VMEM((1, RWKV_PAD), F32)] if carry_shift else [],
        compiler_params=_cp("parallel", "arbitrary"),
        name="rwkv_prep",
    )(*args)


def _rwkv_chunk_kernel(r_ref, lw_ref, k_ref, v_ref, a_ref, b_ref, tri_ref, msl_ref, mli_ref, lvl_ref,
                       y_ref, so_ref, s_scr, *, n_chunks, nb):
    C = RW_CHUNK

    @pl.when(pl.program_id(1) == 0)
    def _():
        s_scr[...] = jnp.zeros_like(s_scr)

    head0 = lax.broadcasted_iota(jnp.int32, (C, 128), 1) < DH_C
    tri3, msl, mli = tri_ref[...], msl_ref[...], mli_ref[...]
    ri = lax.broadcasted_iota(jnp.int32, (128, 128), 0)
    ci = lax.broadcasted_iota(jnp.int32, (128, 128), 1)
    eye = (ri == ci).astype(F32)
    n_levels = lvl_ref.shape[0]
    chains = [(bi, p) for bi in range(nb) for p in range(H_C // 2)]

    def stack(x):
        return jnp.concatenate([jnp.where(head0, x, 0.0), jnp.where(head0, 0.0, x)], axis=0)

    def nt(x, y):
        return lax.dot_general(x, y, (((1,), (1,)), ((), ())), preferred_element_type=F32)

    def dot(x, y):
        return jnp.dot(x, y, preferred_element_type=F32)

    def lhs2(h, l):
        return jnp.concatenate([h, l], axis=1)

    def rhs2(x):
        h = x.astype(BF16)
        return jnp.concatenate([h, h], axis=0)

    def chunk(ci_, carry):
        rows = pl.ds(pl.multiple_of(ci_ * C, C), C)
        ld = lambda ref, bi, p: ref[bi, rows, p * 128:(p + 1) * 128]
        lams = []
        for bi, p in chains:
            lw = ld(lw_ref, bi, p)
            hi = lw.astype(BF16)
            mid = (lw - hi.astype(F32)).astype(BF16)
            lo = (lw - hi.astype(F32) - mid.astype(F32)).astype(BF16)
            lams.append(dot(tri3, jnp.concatenate([hi, mid, lo], axis=0)))
        AR, BK, Vs, KB, dec = [], [], [], [], []
        for (bi, p), lam in zip(chains, lams):
            r, lw, k = ld(r_ref, bi, p), ld(lw_ref, bi, p), ld(k_ref, bi, p)
            v, a, b = ld(v_ref, bi, p), ld(a_ref, bi, p), ld(b_ref, bi, p)
            lam_c = lam[C - 1:C, :]
            e_neg = jnp.exp(-lam)
            e_tail = jnp.exp(lam_c - lam)
            AR.append(jnp.concatenate([stack(a * jnp.exp(lam - lw)), stack(r * jnp.exp(lam))],
                                      axis=0).astype(BF16))
            BK.append(jnp.concatenate([stack(b * e_neg), stack(k * e_neg)], axis=0).astype(BF16))
            KB.append(jnp.concatenate([stack(k * e_tail), stack(b * e_tail)], axis=0).astype(BF16))
            Vs.append(stack(v))
            dec.append(jnp.exp(lam_c))
        G = [nt(x, y) for x, y in zip(AR, BK)]
        I0 = [nt(x, s_scr[bi, p].astype(BF16)) for x, (bi, p) in zip(AR, chains)]
        rhs = [i0[:128] + dot((g[:128, 128:] * msl).astype(BF16), v.astype(BF16))
               for g, i0, v in zip(G, I0, Vs)]
        Nh, Nl, D = [], [], []
        for g in G:
            h, l = _split2(g[:128, :128] * msl)
            Nh.append(h)
            Nl.append(l)
            D.append(eye + (h * lvl_ref[0]).astype(F32) + (l * lvl_ref[0]).astype(F32))
        for lv in range(1, n_levels):
            m = lvl_ref[lv]
            Ds = [_split2(d) for d in D]
            X = [dot(lhs2(h * m, l * m), jnp.concatenate([dh, dh], axis=0))
                 for h, l, (dh, dl) in zip(Nh, Nl, Ds)]
            D = [d + dot(lhs2(dh, dl), rhs2(x)) for d, (dh, dl), x in zip(D, Ds, X)]
        Ds = [_split2(d) for d in D]
        Us = [dot(lhs2(dh, dl), rhs2(x)) for (dh, dl), x in zip(Ds, rhs)]
        for (bi, p), g, i0, u, v in zip(chains, G, I0, Us, Vs):
            ys = i0[128:] + dot(jnp.concatenate([g[128:, :128] * mli, g[128:, 128:] * mli], axis=1).astype(BF16),
                                jnp.concatenate([u, v], axis=0).astype(BF16))
            y_ref[bi, rows, p * 128:(p + 1) * 128] = ys[:C] + ys[C:]
        for (bi, p), u, v, kb, d in zip(chains, Us, Vs, KB, dec):
            vu = jnp.concatenate([v, u], axis=0)
            s_scr[bi, p] = s_scr[bi, p] * d + dot(vu.T.astype(BF16), kb)
        return carry

    lax.fori_loop(0, n_chunks, chunk, 0)

    @pl.when(pl.program_id(1) == pl.num_programs(1) - 1)
    def _():
        so_ref[...] = s_scr[...]


def _rwkv_chunk_tables():
    C = RW_CHUNK
    t = np.arange(C)
    tri = (t[:, None] >= t[None, :]).astype(np.float32)
    tri3 = np.concatenate([tri, tri, tri], axis=1)
    i = np.arange(2 * C)
    same = (i[:, None] // C) == (i[None, :] // C)
    msl = (same & (i[:, None] > i[None, :])).astype(np.float32)
    mli = (same & (i[:, None] >= i[None, :])).astype(np.float32)
    levels = []
    n = 1
    while n < C:
        levels.append(same & ((i[:, None] // (2 * n)) == (i[None, :] // (2 * n)))
                      & ((i[:, None] // n) % 2 == 1) & ((i[None, :] // n) % 2 == 0))
        n *= 2
    return jnp.asarray(tri3, BF16), msl, mli, jnp.asarray(np.stack(levels).astype(np.float32), BF16)


def rwkv_chunk_scan(r, lw, k, v, a, b, tb, nb):
    B, L, _ = r.shape
    tri3, msl, mli, lvl = _rwkv_chunk_tables()
    blk = pl.BlockSpec((nb, tb, W_C), lambda bi, l: (bi, l, 0))
    c2 = lambda s: pl.BlockSpec(s, lambda bi, l: tuple(0 for _ in s))
    return pl.pallas_call(
        functools.partial(_rwkv_chunk_kernel, n_chunks=tb // RW_CHUNK, nb=nb),
        grid=(B // nb, L // tb),
        in_specs=[blk] * 6 + [c2(tri3.shape), c2(msl.shape), c2(mli.shape), c2(lvl.shape)],
        out_specs=[blk, pl.BlockSpec((nb, H_C // 2, 128, 128), lambda bi, l: (bi, 0, 0, 0))],
        out_shape=[jax.ShapeDtypeStruct((B, L, W_C), F32),
                   jax.ShapeDtypeStruct((B, H_C // 2, 128, 128), F32)],
        scratch_shapes=[pltpu.VMEM((nb, H_C // 2, 128, 128), F32)],
        compiler_params=_cp("parallel", "arbitrary"),
        name="rwkv_chunk_scan",
    )(r, lw, k, v, a, b, tri3, msl, mli, lvl)


def _rwkv_step_kernel(s_ref, w_ref, a_ref, b_ref, k_ref, r_ref, v_ref, y_ref, so_ref):
    S = s_ref[...]
    sa = jnp.sum(S * a_ref[...], axis=-1, keepdims=True)
    Sn = S * jnp.exp(w_ref[...]) + sa * b_ref[...] + v_ref[...] * k_ref[...]
    so_ref[...] = Sn
    y_ref[...] = jnp.sum(Sn * r_ref[...], axis=-1, keepdims=True)


def rwkv_step(state, layer, lw, a, b, k, r, v):
    B = lw.shape[0]
    bb = 8
    rowspec = pl.BlockSpec((bb, H_C, 1, DH_C), lambda i: (i, 0, 0, 0))
    colspec = pl.BlockSpec((bb, H_C, DH_C, 1), lambda i: (i, 0, 0, 0))
    sspec = pl.BlockSpec((bb, H_C, DH_C, DH_C), lambda i: (i, 0, 0, 0))
    return pl.pallas_call(
        _rwkv_step_kernel,
        grid=(B // bb,),
        in_specs=[pl.BlockSpec((None, bb, H_C, DH_C, DH_C), lambda i: (layer, i, 0, 0, 0))]
                 + [rowspec] * 5 + [colspec],
        out_specs=[colspec, sspec],
        out_shape=[jax.ShapeDtypeStruct((B, H_C, DH_C, 1), F32),
                   jax.ShapeDtypeStruct((B, H_C, DH_C, DH_C), F32)],
        compiler_params=_cp("parallel"),
        name="rwkv_step",
    )(state, lw, a, b, k, r, v)


def _merge_kernel(x_ref, ya_ref, yb_ref, y_ref, r_ref, k_ref, v_ref, gg_ref, za_ref, zb_ref, zc_ref, gt_ref,
                  lng_ref, lnb_ref, rk_ref, ones_ref, wa_ref, wb_ref, wc_ref, wo_ref, o_ref):
    ones2 = ones_ref[...]
    y = y_ref[...]
    mu = _seg_sum_wide(y, ones2) * (1.0 / DH_C)
    d = y - mu
    var = _seg_sum_wide(d * d, ones2) * (1.0 / DH_C)
    ycn = d * lax.rsqrt(var + LNX_EPS) * lng_ref[...] + lnb_ref[...]
    v = v_ref[...]
    bonus = _seg_sum_wide(r_ref[...] * k_ref[...] * rk_ref[...], ones2) * v
    yc = (ycn + bonus) * gg_ref[...]
    proj = lambda t, w: jnp.dot(t.astype(BF16), w[...], preferred_element_type=F32)
    merged = (jax.nn.sigmoid(za_ref[...]) * proj(ya_ref[...], wa_ref)
              + jax.nn.sigmoid(zb_ref[...]) * proj(yb_ref[...], wb_ref)
              + jax.nn.sigmoid(zc_ref[...]) * proj(yc, wc_ref))
    o_ref[...] = x_ref[...] + gt_ref[...] * proj(merged, wo_ref)


def merge_out(x2, ya, yb, y, r, k, v, gg, z2, gt, lng, lnb, rk, wa, wb, wc, wo, tm):
    T, D = x2.shape
    G, R, _ = gt.shape
    tiles_per_group = (T // tm) // G
    tok = lambda w: pl.BlockSpec((tm, w), lambda m: (m, 0))
    gate = lambda idx: pl.BlockSpec((tm, D), lambda m: (m, Z_GATE // D + idx))
    row = lambda w: pl.BlockSpec((1, w), lambda m: (0, 0))
    full = lambda a: pl.BlockSpec(a.shape, lambda m: (0, 0))
    return pl.pallas_call(
        _merge_kernel,
        grid=(T // tm,),
        in_specs=[tok(D)] + [tok(W_C)] * 7 + [gate(0), gate(1), gate(2),
                  pl.BlockSpec((None, R, D), lambda m: (m // tiles_per_group, 0, 0)),
                  row(W_C), row(W_C), row(W_C), pl.BlockSpec((256, 128), lambda m: (0, 0)),
                  full(wa), full(wb), full(wc), full(wo)],
        out_specs=tok(D),
        out_shape=jax.ShapeDtypeStruct((T, D), F32),
        compiler_params=_cp("parallel"),
        name="merge_out",
    )(x2, ya, yb, y, r, k, v, gg, z2, z2, z2, gt, lng, lnb, rk, _block_ones2(DH_C), wa, wb, wc, wo)


def _ffn_kernel(x_ref, g_ref, sc_ref, sh_ref, gt_ref, w1_ref, w2_ref, o_ref, h_scr, acc_scr):
    f = pl.program_id(1)

    @pl.when(f == 0)
    def _():
        x = x_ref[...]
        y = x * lax.rsqrt(jnp.mean(x * x, axis=-1, keepdims=True) + EPS) * g_ref[...]
        h_scr[...] = (y * (1.0 + sc_ref[...]) + sh_ref[...]).astype(BF16)
        acc_scr[...] = jnp.zeros_like(acc_scr)

    u = jnp.maximum(jnp.dot(h_scr[...], w1_ref[...], preferred_element_type=F32), 0.0)
    acc_scr[...] += jnp.dot((u * u).astype(BF16), w2_ref[...], preferred_element_type=F32)

    @pl.when(f == pl.num_programs(1) - 1)
    def _():
        o_ref[...] = x_ref[...] + gt_ref[...] * acc_scr[...]


def ffn(x2, gain_row, sc, sh, gt, w1, w2, tm, tf):
    T, D = x2.shape
    F = w1.shape[1]
    G, R, _ = sc.shape
    tiles_per_group = (T // tm) // G
    mod_spec = pl.BlockSpec((None, R, D), lambda m, f: (m // tiles_per_group, 0, 0))
    return pl.pallas_call(
        _ffn_kernel,
        grid=(T // tm, F // tf),
        in_specs=[pl.BlockSpec((tm, D), lambda m, f: (m, 0)),
                  pl.BlockSpec((1, D), lambda m, f: (0, 0)),
                  mod_spec, mod_spec, mod_spec,
                  pl.BlockSpec((D, tf), lambda m, f: (0, f)),
                  pl.BlockSpec((tf, D), lambda m, f: (f, 0))],
        out_specs=pl.BlockSpec((tm, D), lambda m, f: (m, 0)),
        out_shape=jax.ShapeDtypeStruct((T, D), F32),
        scratch_shapes=[pltpu.VMEM((tm, D), BF16), pltpu.VMEM((tm, D), F32)],
        compiler_params=_cp("parallel", "arbitrary"),
        name="ffn",
    )(x2, gain_row, sc, sh, gt, w1, w2)


def _prep_layer(p, l):
    D = p['w_in'].shape[1]
    w_in = p['w_in'][l]
    o_diff, o_rwkv, o_gate = RET_COLS, RET_COLS + DIFF_COLS, RET_COLS + DIFF_COLS + RWKV_COLS
    w_in_p = jnp.concatenate([
        w_in[:, :RET_COLS], w_in[:, o_rwkv:o_gate], jnp.zeros((D, RWKV_PAD - RWKV_COLS), F32),
        w_in[:, o_gate:], w_in[:, o_diff:o_rwkv]], axis=1).astype(BF16)
    w_lr = jnp.zeros((LR_COLS, 3 * W_C), F32)
    w_lr = w_lr.at[:R_DECAY, :W_C].set(p['w_decay_up'][l])
    w_lr = w_lr.at[R_DECAY:R_DECAY + R_A, W_C:2 * W_C].set(p['w_a_up'][l])
    w_lr = w_lr.at[R_DECAY + R_A:R_DECAY + R_A + R_G, 2 * W_C:].set(p['w_g_up'][l])
    row = lambda t: t.reshape(1, -1)
    lp = dict(
        w_ada=p['w_ada'][l].astype(BF16), b_ada=row(p['b_ada'][l]),
        norm1=row(p['norm1'][l]), norm2=row(p['norm2'][l]), w_in=w_in_p,
        gq=row(jnp.tile(p['qk_norm_q'][l], 2 * H_B)), gk=row(jnp.tile(p['qk_norm_k'][l], 2 * H_B)),
        lam_rows=jnp.stack([p['lambda_q1'][l], p['lambda_k1'][l], p['lambda_q2'][l], p['lambda_k2'][l]]),
        subln=row(p['subln_diff'][l]),
        mu=row(jnp.pad(p['mu_shift'][l], (0, RWKV_PAD - RWKV_COLS))),
        w0=row(p['w0'][l]), a0=row(p['a0'][l]), k_k=row(p['k_k'][l]), k_a=row(p['k_a'][l]),
        w_lr=w_lr.astype(BF16), r_k=row(p['r_k'][l]), lnx_g=row(p['lnx_g'][l]), lnx_b=row(p['lnx_b'][l]),
        w_up_a=p['w_up_a'][l].astype(BF16), w_up_b=p['w_up_b'][l].astype(BF16),
        w_up_c=p['w_up_c'][l].astype(BF16), w_out=p['w_out'][l].astype(BF16),
        w_ff1=p['w_ff1'][l].astype(BF16), w_ff2=p['w_ff2'][l].astype(BF16),
        lam_init=0.8 - 0.6 * math.exp(-0.3 * l),
    )
    if l > 0:
        lp['v0'] = row(p['v0'][l - 1])
        lp['w_v_down'] = jnp.pad(p['w_v_down'][l - 1], ((0, 0), (0, 128 - R_V))).astype(BF16)
        lp['w_v_up'] = jnp.pad(p['w_v_up'][l - 1], ((0, 128 - R_V), (0, 0))).astype(BF16)
    return lp


def _modulation(c, lp, per_token):
    B, D = c.shape
    rows = -(-B // 16) * 16
    mod = ada_mod(jnp.pad(c, ((0, rows - B), (0, 0))), lp['w_ada'], lp['b_ada'])[:B]
    parts = [mod[:, i * D:(i + 1) * D] for i in range(6)]
    shape = (1, B, D) if per_token else (B, 1, D)
    return [t.reshape(shape) for t in parts]


def _trunk_prompt(x, c, layers, bias_p):
    B, L, D = x.shape
    T = B * L
    tm = min(1024, L)
    t_attn = min(512, L)
    x2 = x.reshape(T, D)
    cos_t, sin_t = _rope_tables(jnp.arange(L))
    k_rows, v_rows, ret_out, rwkv_out, shift_out = [], [], [], [], []
    v_first = None
    for l, lp in enumerate(layers):
        sh1, sc1, gt1, sh2, sc2, gt2 = _modulation(c, lp, per_token=False)
        z2 = in_proj(x2, lp['norm1'], sc1, sh1, lp['w_in'], min(2048, L), 512)
        ZC = z2.shape[1]
        z3 = z2.reshape(B, L, ZC)
        ya, s_ret = retention_prompt(z3, cos_t, sin_t, min(256, L))
        qb, k32, kb, v32, vb = qk_norm(z2, lp['gq'], lp['gk'], tm)
        r3 = lambda t: t.reshape(B, L, W_B)
        yb = diff_attention_prompt(r3(qb), r3(kb), r3(vb), bias_p, lp['lam_rows'], lp['subln'],
                                   lp['lam_init'], t_attn)
        k_rows.append(k32.reshape(B, L, H_B, 2 * DH_B))
        v_rows.append(v32.reshape(B, L, H_B, DV_B))
        vgate = None if l == 0 else (lp['v0'], lp['w_v_down'], lp['w_v_up'], v_first)
        shift0 = jnp.zeros((B, 1, RWKV_PAD), F32)
        r, lw, k, v, a, b, gg = rwkv_prep(z3, shift0, lp['mu'], lp['w0'], lp['a0'], lp['k_k'], lp['k_a'],
                                          lp['w_lr'], vgate, min(256, L), carry_shift=True)
        if l == 0:
            v_first = v
        y, s_pair = rwkv_chunk_scan(r, lw, k, v, a, b, min(256, L), B)
        shift_out.append(z3[:, L - 1:, Z_RWKV:Z_RWKV + RWKV_COLS])
        f2 = lambda t: t.reshape(T, -1)
        x2 = merge_out(x2, f2(ya), f2(yb), f2(y), f2(r), f2(k), f2(v), f2(gg), z2, gt1, lp['lnx_g'],
                       lp['lnx_b'], lp['r_k'], lp['w_up_a'], lp['w_up_b'], lp['w_up_c'], lp['w_out'],
                       min(512, L))
        x2 = ffn(x2, lp['norm2'], sc2, sh2, gt2, lp['w_ff1'], lp['w_ff2'], tm, 1024)
        ret_out.append(s_ret)
        sp = s_pair.reshape(B, H_C // 2, 2, DH_C, 2, DH_C)
        rwkv_out.append(jnp.stack([sp[:, :, 0, :, 0, :], sp[:, :, 1, :, 1, :]], axis=2)
                        .reshape(B, H_C, DH_C, DH_C))
    return (x2.reshape(B, L, D), jnp.stack(k_rows), jnp.stack(v_rows), jnp.stack(ret_out),
            jnp.stack(rwkv_out), jnp.stack(shift_out))


def _trunk_decode(x, c, layers, bias_d, state_ret, state_rwkv, state_shift, cache_k, cache_v, page_table):
    B, _, D = x.shape
    n_pages = page_table.shape[1]
    past = n_pages * PAGE_SIZE
    x2 = x.reshape(B, D)
    pt_flat = page_table.reshape(-1)
    ck = cache_k.reshape(cache_k.shape[0], cache_k.shape[1], PAGE_SIZE * H_B, 2 * DH_B)
    cv = cache_v.reshape(cache_v.shape[0], cache_v.shape[1], PAGE_SIZE * H_B, DV_B)
    k_rows, v_rows, ret_out, rwkv_out, shift_out = [], [], [], [], []
    v_first = None
    for l, lp in enumerate(layers):
        sh1, sc1, gt1, sh2, sc2, gt2 = _modulation(c, lp, per_token=True)
        z2 = in_proj(x2, lp['norm1'], sc1, sh1, lp['w_in'], B, 512)
        ZC = z2.shape[1]
        ya, s_ret = retention_step(z2, state_ret, l, past)
        qb, k32, kb, v32, vb = qk_norm(z2, lp['gq'], lp['gk'], B)
        r3 = lambda t: t.reshape(B, 1, -1)
        yb = diff_attention_decode(r3(qb), r3(k32), r3(v32), ck, cv, pt_flat, bias_d, lp['lam_rows'],
                                   lp['subln'], l, lp['lam_init'], n_pages).reshape(B, W_B)
        k_rows.append(k32.reshape(B, 1, H_B, 2 * DH_B))
        v_rows.append(v32.reshape(B, 1, H_B, DV_B))
        vgate = None if l == 0 else (lp['v0'], lp['w_v_down'], lp['w_v_up'], v_first)
        prev = jnp.pad(state_shift[l].reshape(1, B, RWKV_COLS), ((0, 0), (0, 0), (0, RWKV_PAD - RWKV_COLS)))
        r, lw, k, v, a, b, gg = rwkv_prep(z2.reshape(1, B, ZC), prev, lp['mu'], lp['w0'], lp['a0'], lp['k_k'],
                                          lp['k_a'], lp['w_lr'], vgate, B, carry_shift=False)
        if l == 0:
            v_first = v
        keyrow = lambda t: t.reshape(B, H_C, 1, DH_C)
        y_col, s_rwkv = rwkv_step(state_rwkv, l, keyrow(lw), keyrow(a), keyrow(b), keyrow(k), keyrow(r),
                                  v.reshape(B, H_C, DH_C, 1))
        shift_out.append(z2[:, Z_RWKV:Z_RWKV + RWKV_COLS].reshape(B, 1, RWKV_COLS))
        f2 = lambda t: t.reshape(B, -1)
        x2 = merge_out(x2, ya, yb, f2(y_col), f2(r), f2(k), f2(v), f2(gg), z2, gt1, lp['lnx_g'], lp['lnx_b'],
                       lp['r_k'], lp['w_up_a'], lp['w_up_b'], lp['w_up_c'], lp['w_out'], B)
        x2 = ffn(x2, lp['norm2'], sc2, sh2, gt2, lp['w_ff1'], lp['w_ff2'], B, 1024)
        ret_out.append(s_ret)
        rwkv_out.append(s_rwkv)
    return (x2.reshape(B, 1, D), jnp.stack(k_rows), jnp.stack(v_rows), jnp.stack(ret_out),
            jnp.stack(rwkv_out), jnp.stack(shift_out))


def _decode_buckets(n_pages):
    past = n_pages * PAGE_SIZE
    key = np.arange((n_pages + 1) * PAGE_SIZE)
    bk = np.where(key <= past, _bucket_np(np.maximum(past - key, 0)), -1)
    return np.broadcast_to(bk.reshape(n_pages + 1, 1, PAGE_SIZE), (n_pages + 1, 8, PAGE_SIZE)).astype(np.int32)


def kernel(x_prompt, x_sample, c_prompt, c_sample, cache_k_diff, cache_v_diff, page_table, state_ret, state_rwkv, state_shift, rel_bias, w_ada, b_ada, norm1, norm2, w_in, qk_norm_q, qk_norm_k, lambda_q1, lambda_k1, lambda_q2, lambda_k2, subln_diff, mu_shift, w0, w_decay_up, a0, w_a_up, w_g_up, v0, w_v_down, w_v_up, k_k, k_a, r_k, lnx_g, lnx_b, w_up_a, w_up_b, w_up_c, w_out, w_ff1, w_ff2):
    p = dict(w_ada=w_ada, b_ada=b_ada, norm1=norm1, norm2=norm2, w_in=w_in,
             qk_norm_q=qk_norm_q, qk_norm_k=qk_norm_k, lambda_q1=lambda_q1, lambda_k1=lambda_k1,
             lambda_q2=lambda_q2, lambda_k2=lambda_k2, subln_diff=subln_diff, mu_shift=mu_shift,
             w0=w0, w_decay_up=w_decay_up, a0=a0, w_a_up=w_a_up, w_g_up=w_g_up, v0=v0,
             w_v_down=w_v_down, w_v_up=w_v_up, k_k=k_k, k_a=k_a, r_k=r_k, lnx_g=lnx_g, lnx_b=lnx_b,
             w_up_a=w_up_a, w_up_b=w_up_b, w_up_c=w_up_c, w_out=w_out, w_ff1=w_ff1, w_ff2=w_ff2)
    depth = w_in.shape[0]
    layers = [_prep_layer(p, l) for l in range(depth)]
    L = x_prompt.shape[1]
    n_pages = page_table.shape[1]
    t_attn = min(512, L)
    bias_p = bias_tiles(rel_bias, _prompt_buckets(t_attn))
    bd = bias_tiles(rel_bias, _decode_buckets(n_pages).reshape((n_pages + 1) * 8, PAGE_SIZE))
    bias_d = bd.reshape(H_B, n_pages + 1, 8, PAGE_SIZE)

    y_p, k_p, v_p, ret_p, rwkv_p, shift_p = _trunk_prompt(x_prompt, c_prompt, layers, bias_p)
    y_s, k_s, v_s, ret_s, rwkv_s, shift_s = _trunk_decode(
        x_sample, c_sample, layers, bias_d, state_ret, state_rwkv, state_shift,
        cache_k_diff, cache_v_diff, page_table)
    return (y_p, y_s, k_p, v_p, k_s, v_s, ret_p, ret_s, rwkv_p, rwkv_s, shift_p, shift_s)
```

```python
import functools
import math

import numpy as np
import jax
import jax.numpy as jnp
from jax import lax
from jax.experimental import pallas as pl
from jax.experimental.pallas import tpu as pltpu

F32 = jnp.float32
BF16 = jnp.bfloat16
HIGHEST = lax.Precision.HIGHEST

H_A, DK_A, DV_A = 4, 128, 128
RET_CHUNK = 128
ROPE_BASE = 10000.0
H_B, DH_B, DV_B = 4, 64, 128
N_BUCKETS, MAX_DISTANCE = 32, 128
H_C, DH_C = 8, 64
R_DECAY, R_A, R_V, R_G = 64, 64, 32, 160
LNX_EPS = 64e-5
EPS = 1e-6
PAGE_SIZE = 128
W_AQ, W_A = H_A * DK_A, H_A * DV_A
W_BQK, W_B = H_B * 2 * DH_B, H_B * DV_B
W_C = H_C * DH_C
RET_COLS = 2 * W_AQ + 2 * W_A
DIFF_COLS = 2 * W_BQK + W_B
RWKV_COLS = 3 * W_C + R_DECAY + R_A + R_G
RWKV_PAD = 2048
LR_COLS = RWKV_PAD - 3 * W_C

Z_RET, Z_RWKV = 0, RET_COLS
Z_GATE = Z_RWKV + RWKV_PAD
NEG = -1e30
RW_CHUNK = 64
VMEM_LIMIT = 56 * 1024 * 1024


def _cp(*sem):
    return pltpu.CompilerParams(dimension_semantics=sem, vmem_limit_bytes=VMEM_LIMIT)


def _silu(x):
    return x * jax.nn.sigmoid(x)


def _split2(x):
    hi = x.astype(BF16)
    lo = (x - hi.astype(F32)).astype(BF16)
    return hi, lo


def _seg_sum(x, ones2):
    hi, lo = _split2(x)
    return jnp.dot(jnp.concatenate([hi, lo], axis=1), ones2, preferred_element_type=F32)


def _seg_sum_wide(x, ones2):
    return jnp.concatenate(
        [_seg_sum(x[:, c * 128:(c + 1) * 128], ones2) for c in range(x.shape[1] // 128)], axis=1)


def _mod_kernel(c_ref, w_ref, b_ref, o_ref):
    s = _silu(c_ref[...])
    o_ref[...] = jnp.dot(s.astype(BF16), w_ref[...], preferred_element_type=F32) + b_ref[...]


def ada_mod(c_pad, w_bf, b_row):
    R, D = c_pad.shape
    N = w_bf.shape[1]
    tn = 1536
    return pl.pallas_call(
        _mod_kernel,
        grid=(N // tn,),
        in_specs=[pl.BlockSpec((R, D), lambda n: (0, 0)),
                  pl.BlockSpec((D, tn), lambda n: (0, n)),
                  pl.BlockSpec((1, tn), lambda n: (0, n))],
        out_specs=pl.BlockSpec((R, tn), lambda n: (0, n)),
        out_shape=jax.ShapeDtypeStruct((R, N), F32),
        compiler_params=_cp("parallel"),
        name="ada_mod",
    )(c_pad, w_bf, b_row)


def _inproj_kernel(x_ref, g_ref, sc_ref, sh_ref, w_ref, z_ref, h_scr):
    @pl.when(pl.program_id(1) == 0)
    def _():
        x = x_ref[...]
        y = x * lax.rsqrt(jnp.mean(x * x, axis=-1, keepdims=True) + EPS) * g_ref[...]
        h_scr[...] = (y * (1.0 + sc_ref[...]) + sh_ref[...]).astype(BF16)

    z_ref[...] = jnp.dot(h_scr[...], w_ref[...], preferred_element_type=F32)


def in_proj(x2, gain_row, sc, sh, w_bf, tm, tn):
    T, D = x2.shape
    N = w_bf.shape[1]
    G, R, _ = sc.shape
    tiles_per_group = (T // tm) // G
    mod_spec = pl.BlockSpec((None, R, D), lambda m, n: (m // tiles_per_group, 0, 0))
    return pl.pallas_call(
        _inproj_kernel,
        grid=(T // tm, N // tn),
        in_specs=[pl.BlockSpec((tm, D), lambda m, n: (m, 0)),
                  pl.BlockSpec((1, D), lambda m, n: (0, 0)),
                  mod_spec, mod_spec,
                  pl.BlockSpec((D, tn), lambda m, n: (0, n))],
        out_specs=pl.BlockSpec((tm, tn), lambda m, n: (m, n)),
        out_shape=jax.ShapeDtypeStruct((T, N), F32),
        scratch_shapes=[pltpu.VMEM((tm, D), BF16)],
        compiler_params=_cp("parallel", "arbitrary"),
        name="in_proj",
    )(x2, gain_row, sc, sh, w_bf)


def _rope(x, cos, sin_signed):
    return x * cos + pltpu.roll(x, DK_A // 2, 1) * sin_signed


def _ret_kernel(q_ref, k_ref, v_ref, g_ref, cos_ref, sin_ref, inner_ref, cross_ref, tail_ref, cd_ref,
                ya_ref, so_ref, s_scr, *, n_sub):
    lt = pl.program_id(1)

    @pl.when(lt == 0)
    def _():
        s_scr[...] = jnp.zeros_like(s_scr)

    C = RET_CHUNK
    for c in range(n_sub):
        rows = slice(c * C, (c + 1) * C)
        cos = cos_ref[rows, :]
        sin = sin_ref[rows, :]
        for h in range(H_A):
            cols = slice(h * DK_A, (h + 1) * DK_A)
            q = _rope(q_ref[rows, cols], cos, sin)
            k = _rope(k_ref[rows, cols], cos, sin) * (DK_A ** -0.5)
            qb, kb, vb = q.astype(BF16), k.astype(BF16), v_ref[rows, cols].astype(BF16)
            S = s_scr[h]
            sc = lax.dot_general(qb, kb, (((1,), (1,)), ((), ())), preferred_element_type=F32) * inner_ref[h]
            o = (jnp.dot(sc.astype(BF16), vb, preferred_element_type=F32)
                 + jnp.dot(qb, S.astype(BF16), preferred_element_type=F32) * cross_ref[h])
            kt = (k * tail_ref[h]).T.astype(BF16)
            s_scr[h] = S * cd_ref[h] + jnp.dot(kt, vb, preferred_element_type=F32)
            on = o * lax.rsqrt(jnp.mean(o * o, axis=-1, keepdims=True) + EPS)
            ya_ref[rows, cols] = on * _silu(g_ref[rows, cols])

    @pl.when(lt == pl.num_programs(1) - 1)
    def _():
        so_ref[...] = s_scr[...]


def _ret_tables(L):
    C = math.gcd(L, RET_CHUNK)
    log_g = np.log1p(-np.exp2(-5.0 - np.arange(H_A, dtype=np.float32))).astype(np.float32)
    i = np.arange(C, dtype=np.float32)
    dist = i[:, None] - i[None, :]
    causal = dist >= 0
    inner = np.where(causal[None], np.exp(np.where(causal, dist, 0.0)[None] * log_g[:, None, None]), 0.0)
    cross = np.exp((i[None, :] + 1.0) * log_g[:, None])
    tail = np.exp((C - 1.0 - i)[None, :] * log_g[:, None])
    chunk = np.exp(C * log_g)
    bc = lambda t: np.broadcast_to(t[:, :, None], (H_A, C, 128)).astype(np.float32)
    cd = np.broadcast_to(chunk[:, None, None], (H_A, 1, 128)).astype(np.float32)
    return inner.astype(np.float32), bc(cross), bc(tail), cd


def _rope_tables(pos):
    half = DK_A // 2
    inv = ROPE_BASE ** (-jnp.arange(half, dtype=F32) / half)
    ang = pos.astype(F32)[:, None] * inv[None, :]
    cos, sin = jnp.cos(ang), jnp.sin(ang)
    return jnp.concatenate([cos, cos], axis=-1), jnp.concatenate([-sin, sin], axis=-1)


def retention_prompt(z3, cos_t, sin_t, tb):
    B, L, ZC = z3.shape
    inner, cross, tail, cd = _ret_tables(L)
    zspec = lambda idx: pl.BlockSpec((None, tb, W_AQ), lambda b, l: (b, l, idx))
    full3 = lambda s: pl.BlockSpec(s, lambda b, l: (0, 0, 0))
    return pl.pallas_call(
        functools.partial(_ret_kernel, n_sub=tb // RET_CHUNK),
        grid=(B, L // tb),
        in_specs=[zspec(0), zspec(1), zspec(2), zspec(3),
                  pl.BlockSpec((tb, 128), lambda b, l: (l, 0)),
                  pl.BlockSpec((tb, 128), lambda b, l: (l, 0)),
                  full3(inner.shape), full3(cross.shape), full3(tail.shape), full3(cd.shape)],
        out_specs=[pl.BlockSpec((None, tb, W_A), lambda b, l: (b, l, 0)),
                   pl.BlockSpec((None, H_A, DK_A, DV_A), lambda b, l: (b, 0, 0, 0))],
        out_shape=[jax.ShapeDtypeStruct((B, L, W_A), F32),
                   jax.ShapeDtypeStruct((B, H_A, DK_A, DV_A), F32)],
        scratch_shapes=[pltpu.VMEM((H_A, DK_A, DV_A), F32)],
        compiler_params=_cp("parallel", "arbitrary"),
        name="retention_prompt",
    )(z3, z3, z3, z3, cos_t, sin_t, inner, cross, tail, cd)


def _ret_step_kernel(q_ref, k_ref, v_ref, g_ref, cos_ref, sin_ref, gam_ref, s_ref, ya_ref, so_ref, *, bb):
    cos, sin = cos_ref[...], sin_ref[...]
    row = lax.broadcasted_iota(jnp.int32, (bb, 128), 0)
    for h in range(H_A):
        cols = slice(h * DK_A, (h + 1) * DK_A)
        gam = gam_ref[h]
        q = _rope(q_ref[:, cols], cos, sin)
        k = _rope(k_ref[:, cols], cos, sin) * (DK_A ** -0.5)
        v = v_ref[:, cols]
        qb = q.astype(BF16)
        qk = jnp.sum(qb.astype(F32) * k.astype(BF16).astype(F32), axis=-1, keepdims=True)
        qs = jnp.zeros((bb, 128), F32)
        for b in range(bb):
            S = s_ref[b, h]
            onehot = row == b
            qs = jnp.where(onehot, jnp.dot(qb, S.astype(BF16), preferred_element_type=F32), qs)
            kb_t = jnp.where(onehot, k, 0.0).T.astype(BF16)
            so_ref[b, h] = S * gam + jnp.dot(kb_t, v.astype(BF16), preferred_element_type=F32)
        o = qk * v.astype(BF16).astype(F32) + qs * gam
        on = o * lax.rsqrt(jnp.mean(o * o, axis=-1, keepdims=True) + EPS)
        ya_ref[:, cols] = on * _silu(g_ref[:, cols])


def retention_step(z2, state, layer, pos):
    B = z2.shape[0]
    bb = 8
    cos_t, sin_t = _rope_tables(jnp.full((1,), pos))
    log_g = np.log1p(-np.exp2(-5.0 - np.arange(H_A, dtype=np.float32))).astype(np.float32)
    gam = np.broadcast_to(np.exp(log_g)[:, None, None], (H_A, 1, 128)).astype(np.float32)
    zspec = lambda idx: pl.BlockSpec((bb, W_AQ), lambda i: (i, idx))
    return pl.pallas_call(
        functools.partial(_ret_step_kernel, bb=bb),
        grid=(B // bb,),
        in_specs=[zspec(0), zspec(1), zspec(2), zspec(3),
                  pl.BlockSpec((1, 128), lambda i: (0, 0)),
                  pl.BlockSpec((1, 128), lambda i: (0, 0)),
                  pl.BlockSpec((H_A, 1, 128), lambda i: (0, 0, 0)),
                  pl.BlockSpec((None, bb, H_A, DK_A, DV_A), lambda i: (layer, i, 0, 0, 0))],
        out_specs=[pl.BlockSpec((bb, W_A), lambda i: (i, 0)),
                   pl.BlockSpec((bb, H_A, DK_A, DV_A), lambda i: (i, 0, 0, 0))],
        out_shape=[jax.ShapeDtypeStruct((B, W_A), F32),
                   jax.ShapeDtypeStruct((B, H_A, DK_A, DV_A), F32)],
        compiler_params=_cp("parallel"),
        name="retention_step",
    )(z2, z2, z2, z2, cos_t, sin_t, gam, state)


def _qknorm_kernel(q_ref, k_ref, v_ref, gq_ref, gk_ref, ones_ref, qb_ref, k32_ref, kb_ref, v32_ref, vb_ref,
                   *, q_scale):
    ones2 = ones_ref[...]
    q, k, v = q_ref[...], k_ref[...], v_ref[...]
    tm = q.shape[0]
    qn = q * lax.rsqrt(_seg_sum_wide(q * q, ones2) * (1.0 / DH_B) + EPS) * gq_ref[...] * q_scale
    kn = k * lax.rsqrt(_seg_sum_wide(k * k, ones2) * (1.0 / DH_B) + EPS) * gk_ref[...]
    qb_ref[...] = qn.astype(BF16)
    kb_ref[...] = kn.astype(BF16)
    vb_ref[...] = v.astype(BF16)
    for h in range(H_B):
        head_rows = pl.ds(h, tm, stride=H_B)
        k32_ref[head_rows, :] = kn[:, h * DV_B:(h + 1) * DV_B]
        v32_ref[head_rows, :] = v[:, h * DV_B:(h + 1) * DV_B]


def _block_ones2(group):
    i = np.arange(128)
    m = (i[:, None] // group == i[None, :] // group).astype(np.float32)
    return jnp.asarray(np.concatenate([m, m], axis=0), BF16)


def qk_norm(z2, gq_row, gk_row, tm, q_scale):
    T = z2.shape[0]
    base = (Z_GATE + 3 * 1024) // W_BQK
    zspec = lambda idx: pl.BlockSpec((tm, W_BQK), lambda m: (m, base + idx))
    row = pl.BlockSpec((1, W_BQK), lambda m: (0, 0))
    out = pl.BlockSpec((tm, W_BQK), lambda m: (m, 0))
    out_rows = pl.BlockSpec((tm * H_B, DV_B), lambda m: (m, 0))
    return pl.pallas_call(
        functools.partial(_qknorm_kernel, q_scale=q_scale),
        grid=(T // tm,),
        in_specs=[zspec(0), zspec(1), zspec(2), row, row, pl.BlockSpec((256, 128), lambda m: (0, 0))],
        out_specs=[out, out_rows, out, out_rows, out],
        out_shape=[jax.ShapeDtypeStruct((T, W_BQK), BF16), jax.ShapeDtypeStruct((T * H_B, DV_B), F32),
                   jax.ShapeDtypeStruct((T, W_BQK), BF16), jax.ShapeDtypeStruct((T * H_B, DV_B), F32),
                   jax.ShapeDtypeStruct((T, W_B), BF16)],
        compiler_params=_cp("parallel"),
        name="qk_norm",
    )(z2, z2, z2, gq_row, gk_row, _block_ones2(DH_B))


def _bucket_np(n):
    max_exact = N_BUCKETS // 2
    nf = np.maximum(n, 1).astype(np.float32)
    large = max_exact + (np.log(nf / np.float32(max_exact)) / np.float32(math.log(MAX_DISTANCE / max_exact))
                         * np.float32(N_BUCKETS - max_exact)).astype(np.int32)
    large = np.minimum(large, N_BUCKETS - 1)
    return np.where(n < max_exact, n, large).astype(np.int32)


def _bias_kernel(tab_ref, bkt_ref, o_ref, *, log2_far_shift):
    h = pl.program_id(0)
    bk = bkt_ref[...]
    acc = jnp.zeros(bk.shape, F32)
    for b in range(N_BUCKETS):
        acc = jnp.where(bk == b, tab_ref[b, h], acc)
    if log2_far_shift:
        acc = (acc - tab_ref[N_BUCKETS - 1, h]) * math.log2(math.e)
    o_ref[...] = jnp.where(bk < 0, NEG, acc)


def bias_tiles(rel_bias, buckets, log2_far_shift):
    R, C = buckets.shape
    tr = min(R, 512)
    return pl.pallas_call(
        functools.partial(_bias_kernel, log2_far_shift=log2_far_shift),
        grid=(H_B, R // tr),
        in_specs=[pl.BlockSpec(memory_space=pltpu.SMEM),
                  pl.BlockSpec((tr, C), lambda h, r: (r, 0))],
        out_specs=pl.BlockSpec((None, tr, C), lambda h, r: (h, r, 0)),
        out_shape=jax.ShapeDtypeStruct((H_B, R, C), F32),
        compiler_params=_cp("parallel", "parallel"),
        name="bias_tiles",
    )(rel_bias, jnp.asarray(buckets))


def _prompt_buckets(t):
    r = np.arange(t)[:, None]
    c = np.arange(t)[None, :]
    diag = np.where(c <= r, _bucket_np(np.maximum(r - c, 0)), -1)
    off1 = _bucket_np(t + r - c)
    assert t >= MAX_DISTANCE
    return np.concatenate([diag, off1], axis=0).astype(np.int32)


def _lambda_full(lam_ref, lam_init):
    lv = lam_ref[...]
    s1 = jnp.sum(lv[0:1] * lv[1:2], axis=-1, keepdims=True)
    s2 = jnp.sum(lv[2:3] * lv[3:4], axis=-1, keepdims=True)
    return jnp.exp(s1) - jnp.exp(s2) + lam_init


def _flash_kernel(q_ref, k_ref, v_ref, bias_ref, lam_ref, gain_ref, o_ref, qq_scr, m_scr, acc_scr,
                  *, lam_init, sub):
    i = pl.program_id(2)
    t = q_ref.shape[0]

    q = q_ref[...]
    first = lax.broadcasted_iota(jnp.int32, q.shape, 1) < DH_B
    qq_scr[:t] = jnp.where(first, q, jnp.zeros_like(q))
    qq_scr[t:] = jnp.where(first, jnp.zeros_like(q), q)
    m_scr[...] = jnp.full(m_scr.shape, NEG, F32)
    acc_scr[...] = jnp.zeros_like(acc_scr)

    def kv_block(j, tile):
        krows = pl.ds(pl.multiple_of(j * t, t), t)
        k = k_ref[krows, :]
        v1 = jnp.concatenate([v_ref[krows, :], jnp.ones((t, DV_B), BF16)], axis=1)

        def scores(u):
            rows = slice(u * sub, (u + 1) * sub)
            s = lax.dot_general(qq_scr[rows], k, (((1,), (1,)), ((), ())), preferred_element_type=F32)
            if tile is None:
                return s
            brow = tile * t + (u * sub) % t
            return s + bias_ref[brow:brow + sub, :]

        n_sub = 2 * t // sub
        s_next = scores(0)
        for u in range(n_sub):
            s = s_next
            if u + 1 < n_sub:
                s_next = scores(u + 1)
            rows = slice(u * sub, (u + 1) * sub)
            m_old = m_scr[rows]
            m_new = jnp.maximum(m_old, jnp.max(s, axis=-1, keepdims=True))
            alpha = jnp.exp2(m_old - m_new)
            p = jnp.concatenate([jnp.exp2(s[:, c * 128:(c + 1) * 128] - m_new) for c in range(t // 128)],
                                axis=1).astype(BF16)
            acc_scr[rows] = (jnp.concatenate([alpha, alpha], axis=1) * acc_scr[rows]
                             + jnp.dot(p, v1, preferred_element_type=F32))
            m_scr[rows] = m_new

    def far_block(j, carry):
        kv_block(j, None)
        return carry

    lax.fori_loop(0, i - 1, far_block, 0)

    @pl.when(i >= 1)
    def _():
        kv_block(i - 1, 1)

    kv_block(i, 0)

    lam = _lambda_full(lam_ref, lam_init)
    acc = acc_scr[...]
    ob = acc[:t, :DV_B] / acc[:t, DV_B:] - lam * (acc[t:, :DV_B] / acc[t:, DV_B:])
    on = ob * lax.rsqrt(jnp.mean(ob * ob, axis=-1, keepdims=True) + EPS) * gain_ref[...]
    o_ref[...] = on * (1.0 - lam_init)


def diff_attention_prompt(qb, kb, vb, bias, lam_rows, gain_row, lam_init, t):
    B, L, _ = qb.shape
    n = L // t
    kv_spec = pl.BlockSpec((None, L, DV_B), lambda b, h, i: (b, 0, h))
    return pl.pallas_call(
        functools.partial(_flash_kernel, lam_init=lam_init, sub=min(256, t)),
        grid=(B, H_B, n),
        in_specs=[pl.BlockSpec((None, t, DV_B), lambda b, h, i: (b, i, h)),
                  kv_spec, kv_spec,
                  pl.BlockSpec((None, 2 * t, t), lambda b, h, i: (h, 0, 0)),
                  pl.BlockSpec((4, DH_B), lambda b, h, i: (0, 0)),
                  pl.BlockSpec((1, DV_B), lambda b, h, i: (0, 0))],
        out_specs=pl.BlockSpec((None, t, DV_B), lambda b, h, i: (b, i, h)),
        out_shape=jax.ShapeDtypeStruct((B, L, W_B), F32),
        scratch_shapes=[pltpu.VMEM((2 * t, DV_B), BF16), pltpu.VMEM((2 * t, 128), F32),
                        pltpu.VMEM((2 * t, 2 * DV_B), F32)],
        compiler_params=_cp("parallel", "parallel", "arbitrary"),
        name="diff_attention_prompt",
    )(qb, kb, vb, bias, lam_rows, gain_row)


def _decode_kernel(pt_ref, q_ref, kn_ref, vn_ref, bias_ref, lam_ref, gain_ref, *rest, n_pages, lam_init):
    k_refs, v_refs, o_ref = rest[:n_pages], rest[n_pages:2 * n_pages], rest[2 * n_pages]
    lam = _lambda_full(lam_ref, lam_init)
    gain = gain_ref[...]
    r8 = lax.broadcasted_iota(jnp.int32, (8, 2 * DH_B), 0)
    l8 = lax.broadcasted_iota(jnp.int32, (8, 2 * DH_B), 1)
    comp_rows = (l8 // DH_B) == r8
    row0 = lax.broadcasted_iota(jnp.int32, (PAGE_SIZE, DV_B), 0) == 0
    nt = lambda x, y: lax.dot_general(x, y, (((1,), (1,)), ((), ())), preferred_element_type=F32)
    heads = range(H_B)
    cols = [slice(h * DV_B, (h + 1) * DV_B) for h in heads]
    head_rows = [pl.ds(h, PAGE_SIZE, stride=H_B) for h in heads]
    ss = []
    for h in heads:
        qrows = jnp.where(comp_rows, q_ref[:, cols[h]].astype(F32), 0.0).astype(BF16)
        ks = [r[head_rows[h], :].astype(BF16) for r in k_refs]
        ks.append(jnp.where(row0, kn_ref[h:h + 1, :], 0.0).astype(BF16))
        ss.append([nt(qrows, kp) + bias_ref[h, p] for p, kp in enumerate(ks)])
    ps, ls = [], []
    for h in heads:
        m = functools.reduce(jnp.maximum, [jnp.max(s, axis=-1, keepdims=True) for s in ss[h]])
        p = [jnp.exp(s - m) for s in ss[h]]
        ls.append(functools.reduce(jnp.add, [jnp.sum(x, axis=-1, keepdims=True) for x in p]))
        ps.append([x.astype(BF16) for x in p])
    for h in heads:
        vs = [r[head_rows[h], :].astype(BF16) for r in v_refs]
        vs.append(jnp.where(row0, vn_ref[h:h + 1, :], 0.0).astype(BF16))
        acc = functools.reduce(jnp.add, [jnp.dot(p, vp, preferred_element_type=F32)
                                         for p, vp in zip(ps[h], vs)])
        outn = acc / ls[h]
        oh = outn[0:1] - lam * outn[1:2]
        on = oh * lax.rsqrt(jnp.mean(oh * oh, axis=-1, keepdims=True) + EPS) * gain
        o_ref[:, cols[h]] = on * (1.0 - lam_init)


def diff_attention_decode(qb, k32, v32, cache_k, cache_v, pt_flat, bias_dec, lam_rows, gain_row, layer,
                          lam_init, n_pages):
    B = qb.shape[0]
    rowspec = pl.BlockSpec((None, 1, W_BQK), lambda b, pt: (b, 0, 0))
    headspec = pl.BlockSpec((None, H_B, DV_B), lambda b, pt: (b, 0, 0))
    const = lambda s: pl.BlockSpec(s, lambda b, pt: tuple(0 for _ in s))
    page_spec = lambda p: pl.BlockSpec((None, None, PAGE_SIZE * H_B, DV_B),
                                       lambda b, pt, p=p: (layer, pt[b * n_pages + p], 0, 0))
    grid_spec = pltpu.PrefetchScalarGridSpec(
        num_scalar_prefetch=1,
        grid=(B,),
        in_specs=[rowspec, headspec, headspec, const(bias_dec.shape), const((4, DH_B)), const((1, DV_B))]
                 + [page_spec(p) for p in range(n_pages)] * 2,
        out_specs=pl.BlockSpec((None, 1, W_B), lambda b, pt: (b, 0, 0)),
    )
    return pl.pallas_call(
        functools.partial(_decode_kernel, n_pages=n_pages, lam_init=lam_init),
        grid_spec=grid_spec,
        out_shape=jax.ShapeDtypeStruct((B, 1, W_B), F32),
        compiler_params=_cp("arbitrary"),
        name="diff_attention_decode",
    )(pt_flat, qb, k32, v32, bias_dec, lam_rows, gain_row,
      *([cache_k] * n_pages), *([cache_v] * n_pages))


def _rwkv_prep_kernel(*refs, carry_shift, gate_v):
    it = iter(refs)
    z_ref, prev_ref, mu_ref, w0_ref, a0_ref, kk_ref, ka_ref, wlr_ref, ones_ref = (next(it) for _ in range(9))
    if gate_v:
        v0_ref, wvd_ref, wvu_ref, vf_ref = (next(it) for _ in range(4))
    r_ref, lw_ref, k_ref, v_ref, a_ref, b_ref, gg_ref = (next(it) for _ in range(7))
    z = z_ref[...]
    if carry_shift:
        carry_scr = next(it)

        @pl.when(pl.program_id(1) == 0)
        def _():
            carry_scr[...] = prev_ref[...]

        first = lax.broadcasted_iota(jnp.int32, z.shape, 0) == 0
        zprev = jnp.where(first, carry_scr[...], pltpu.roll(z, 1, 0))
        carry_scr[...] = z_ref[z.shape[0] - 1:z.shape[0], :]
    else:
        zprev = prev_ref[...]
    zs = z + (zprev - z) * mu_ref[...]
    rc, kc, vc = zs[:, :W_C], zs[:, W_C:2 * W_C], zs[:, 2 * W_C:3 * W_C]
    lr = zs[:, 3 * W_C:]
    col = lax.broadcasted_iota(jnp.int32, lr.shape, 1)
    act = jnp.where(col < R_DECAY, jnp.tanh(lr), jnp.where(col < R_DECAY + R_A, lr, jax.nn.sigmoid(lr)))
    up = jnp.dot(act.astype(BF16), wlr_ref[...], preferred_element_type=F32)
    y = -(w0_ref[...] + up[:, :W_C])
    softplus = jnp.maximum(y, 0.0) + jnp.log(1.0 + jnp.exp(-jnp.abs(y)))
    lw_ref[...] = -jnp.exp(-softplus - 0.5)
    if gate_v:
        down = jnp.dot(vc.astype(BF16), wvd_ref[...], preferred_element_type=F32)
        vgate = jax.nn.sigmoid(v0_ref[...] + jnp.dot(down.astype(BF16), wvu_ref[...], preferred_element_type=F32))
        vc = vc + (vf_ref[...] - vc) * vgate
    a = jax.nn.sigmoid(a0_ref[...] + up[:, W_C:2 * W_C])
    gg_ref[...] = up[:, 2 * W_C:]
    kk = kc * kk_ref[...]
    norm = jnp.sqrt(_seg_sum_wide(kk * kk, ones_ref[...]))
    kk = kk / jnp.maximum(norm, 1e-12)
    r_ref[...] = rc
    k_ref[...] = kc * (1.0 + (a - 1.0) * ka_ref[...])
    v_ref[...] = vc
    a_ref[...] = -kk
    b_ref[...] = kk * a


def rwkv_prep(z3, prev, mu_row, w0, a0, k_k, k_a, w_lr, vgate, tm, carry_shift):
    G, Lg, _ = z3.shape
    blk = lambda w, idx: pl.BlockSpec((None, tm, w), lambda g, l: (g, l, idx))
    row = lambda w: pl.BlockSpec((1, w), lambda g, l: (0, 0))
    prev_spec = (pl.BlockSpec((None, 1, RWKV_PAD), lambda g, l: (g, 0, 0)) if carry_shift
                 else blk(RWKV_PAD, 0))
    in_specs = [blk(RWKV_PAD, Z_RWKV // RWKV_PAD), prev_spec, row(RWKV_PAD), row(W_C), row(W_C), row(W_C),
                row(W_C), pl.BlockSpec(w_lr.shape, lambda g, l: (0, 0)),
                pl.BlockSpec((256, 128), lambda g, l: (0, 0))]
    args = [z3, prev, mu_row, w0, a0, k_k, k_a, w_lr, _block_ones2(DH_C)]
    if vgate is not None:
        v0, wvd, wvu, vfirst = vgate
        in_specs += [row(W_C), pl.BlockSpec(wvd.shape, lambda g, l: (0, 0)),
                     pl.BlockSpec(wvu.shape, lambda g, l: (0, 0)), blk(W_C, 0)]
        args += [v0, wvd, wvu, vfirst]
    return pl.pallas_call(
        functools.partial(_rwkv_prep_kernel, carry_shift=carry_shift, gate_v=vgate is not None),
        grid=(G, Lg // tm),
        in_specs=in_specs,
        out_specs=[blk(W_C, 0)] * 7,
        out_shape=[jax.ShapeDtypeStruct((G, Lg, W_C), F32)] * 7,
        scratch_shapes=[pltpu.VMEM((1, RWKV_PAD), F32)] if carry_shift else [],
        compiler_params=_cp("parallel", "arbitrary"),
        name="rwkv_prep",
    )(*args)


def _rwkv_chunk_kernel(r_ref, lw_ref, k_ref, v_ref, a_ref, b_ref, tri_ref, msl_ref, mli_ref, lvl_ref,
                       y_ref, so_ref, s_scr, *, n_chunks, nb):
    C = RW_CHUNK

    @pl.when(pl.program_id(1) == 0)
    def _():
        s_scr[...] = jnp.zeros_like(s_scr)

    head0 = lax.broadcasted_iota(jnp.int32, (C, 128), 1) < DH_C
    tri3, msl, mli = tri_ref[...], msl_ref[...], mli_ref[...]
    ri = lax.broadcasted_iota(jnp.int32, (128, 128), 0)
    ci = lax.broadcasted_iota(jnp.int32, (128, 128), 1)
    eye = (ri == ci).astype(F32)
    n_levels = lvl_ref.shape[0]
    chains = [(bi, p) for bi in range(nb) for p in range(H_C // 2)]

    def stack(x):
        return jnp.concatenate([jnp.where(head0, x, 0.0), jnp.where(head0, 0.0, x)], axis=0)

    def nt(x, y):
        return lax.dot_general(x, y, (((1,), (1,)), ((), ())), preferred_element_type=F32)

    def dot(x, y):
        return jnp.dot(x, y, preferred_element_type=F32)

    def lhs2(h, l):
        return jnp.concatenate([h, l], axis=1)

    def rhs2(x):
        h = x.astype(BF16)
        return jnp.concatenate([h, h], axis=0)

    def chunk(ci_, carry):
        rows = pl.ds(pl.multiple_of(ci_ * C, C), C)
        ld = lambda ref, bi, p: ref[bi, rows, p * 128:(p + 1) * 128]
        lams = []
        for bi, p in chains:
            lw = ld(lw_ref, bi, p)
            hi = lw.astype(BF16)
            mid = (lw - hi.astype(F32)).astype(BF16)
            lo = (lw - hi.astype(F32) - mid.astype(F32)).astype(BF16)
            lams.append(dot(tri3, jnp.concatenate([hi, mid, lo], axis=0)))
        AR, BK, Vs, KB, dec = [], [], [], [], []
        for (bi, p), lam in zip(chains, lams):
            r, lw, k = ld(r_ref, bi, p), ld(lw_ref, bi, p), ld(k_ref, bi, p)
            v, a, b = ld(v_ref, bi, p), ld(a_ref, bi, p), ld(b_ref, bi, p)
            lam_c = lam[C - 1:C, :]
            e_neg = jnp.exp(-lam)
            e_tail = jnp.exp(lam_c - lam)
            AR.append(jnp.concatenate([stack(a * jnp.exp(lam - lw)), stack(r * jnp.exp(lam))],
                                      axis=0).astype(BF16))
            BK.append(jnp.concatenate([stack(b * e_neg), stack(k * e_neg)], axis=0).astype(BF16))
            KB.append(jnp.concatenate([stack(k * e_tail), stack(b * e_tail)], axis=0).astype(BF16))
            Vs.append(stack(v))
            dec.append(jnp.exp(lam_c))
        G = [nt(x, y) for x, y in zip(AR, BK)]
        I0 = [nt(x, s_scr[bi, p].astype(BF16)) for x, (bi, p) in zip(AR, chains)]
        rhs = [i0[:128] + dot((g[:128, 128:] * msl).astype(BF16), v.astype(BF16))
               for g, i0, v in zip(G, I0, Vs)]
        Nh, Nl, D = [], [], []
        for g in G:
            h, l = _split2(g[:128, :128] * msl)
            Nh.append(h)
            Nl.append(l)
            D.append(eye + (h * lvl_ref[0]).astype(F32) + (l * lvl_ref[0]).astype(F32))
        for lv in range(1, n_levels):
            m = lvl_ref[lv]
            Ds = [_split2(d) for d in D]
            X = [dot(lhs2(h * m, l * m), jnp.concatenate([dh, dh], axis=0))
                 for h, l, (dh, dl) in zip(Nh, Nl, Ds)]
            D = [d + dot(lhs2(dh, dl), rhs2(x)) for d, (dh, dl), x in zip(D, Ds, X)]
        Ds = [_split2(d) for d in D]
        Us = [dot(lhs2(dh, dl), rhs2(x)) for (dh, dl), x in zip(Ds, rhs)]
        for (bi, p), g, i0, u, v in zip(chains, G, I0, Us, Vs):
            ys = i0[128:] + dot(jnp.concatenate([g[128:, :128] * mli, g[128:, 128:] * mli], axis=1).astype(BF16),
                                jnp.concatenate([u, v], axis=0).astype(BF16))
            y_ref[bi, rows, p * 128:(p + 1) * 128] = ys[:C] + ys[C:]
        for (bi, p), u, v, kb, d in zip(chains, Us, Vs, KB, dec):
            vu = jnp.concatenate([v, u], axis=0)
            s_scr[bi, p] = s_scr[bi, p] * d + dot(vu.T.astype(BF16), kb)
        return carry

    lax.fori_loop(0, n_chunks, chunk, 0)

    @pl.when(pl.program_id(1) == pl.num_programs(1) - 1)
    def _():
        so_ref[...] = s_scr[...]


def _rwkv_chunk_tables():
    C = RW_CHUNK
    t = np.arange(C)
    tri = (t[:, None] >= t[None, :]).astype(np.float32)
    tri3 = np.concatenate([tri, tri, tri], axis=1)
    i = np.arange(2 * C)
    same = (i[:, None] // C) == (i[None, :] // C)
    msl = (same & (i[:, None] > i[None, :])).astype(np.float32)
    mli = (same & (i[:, None] >= i[None, :])).astype(np.float32)
    levels = []
    n = 1
    while n < C:
        levels.append(same & ((i[:, None] // (2 * n)) == (i[None, :] // (2 * n)))
                      & ((i[:, None] // n) % 2 == 1) & ((i[None, :] // n) % 2 == 0))
        n *= 2
    return jnp.asarray(tri3, BF16), msl, mli, jnp.asarray(np.stack(levels).astype(np.float32), BF16)


def rwkv_chunk_scan(r, lw, k, v, a, b, tb, nb):
    B, L, _ = r.shape
    tri3, msl, mli, lvl = _rwkv_chunk_tables()
    blk = pl.BlockSpec((nb, tb, W_C), lambda bi, l: (bi, l, 0))
    c2 = lambda s: pl.BlockSpec(s, lambda bi, l: tuple(0 for _ in s))
    return pl.pallas_call(
        functools.partial(_rwkv_chunk_kernel, n_chunks=tb // RW_CHUNK, nb=nb),
        grid=(B // nb, L // tb),
        in_specs=[blk] * 6 + [c2(tri3.shape), c2(msl.shape), c2(mli.shape), c2(lvl.shape)],
        out_specs=[blk, pl.BlockSpec((nb, H_C // 2, 128, 128), lambda bi, l: (bi, 0, 0, 0))],
        out_shape=[jax.ShapeDtypeStruct((B, L, W_C), F32),
                   jax.ShapeDtypeStruct((B, H_C // 2, 128, 128), F32)],
        scratch_shapes=[pltpu.VMEM((nb, H_C // 2, 128, 128), F32)],
        compiler_params=_cp("parallel", "arbitrary"),
        name="rwkv_chunk_scan",
    )(r, lw, k, v, a, b, tri3, msl, mli, lvl)


def _rwkv_step_kernel(s_ref, w_ref, a_ref, b_ref, k_ref, r_ref, v_ref, y_ref, so_ref, vt_scr, y_scr):
    tr = lambda ref: ref[...].T
    wT, aT, bT, kT, rT = jnp.exp(tr(w_ref)), tr(a_ref), tr(b_ref), tr(k_ref), tr(r_ref)
    vt_scr[...] = tr(v_ref)
    for hh in range(2):
        ch = slice(hh * DH_C, (hh + 1) * DH_C)
        w, a, b, k, r = wT[ch], aT[ch], bT[ch], kT[ch], rT[ch]

        def value_row(i, carry):
            row = pl.ds(hh * DH_C + i, 1)
            S = s_ref[hh, i]
            sa = jnp.sum(S * a, axis=0, keepdims=True)
            Sn = S * w + sa * b + vt_scr[row, :] * k
            so_ref[hh, i] = Sn
            y_scr[row, :] = jnp.sum(Sn * r, axis=0, keepdims=True)
            return carry

        lax.fori_loop(0, DH_C, value_row, 0)
    y_ref[...] = y_scr[...].T


def rwkv_step(state_t, layer, lw, a, b, k, r, v):
    B = lw.shape[0]
    vec = pl.BlockSpec((B, 2 * DH_C), lambda p: (0, p))
    return pl.pallas_call(
        _rwkv_step_kernel,
        grid=(H_C // 2,),
        in_specs=[pl.BlockSpec((None, 2, DH_C, DH_C, B), lambda p: (layer, p, 0, 0, 0))] + [vec] * 6,
        out_specs=[vec, pl.BlockSpec((2, DH_C, DH_C, B), lambda p: (p, 0, 0, 0))],
        out_shape=[jax.ShapeDtypeStruct((B, W_C), F32),
                   jax.ShapeDtypeStruct((H_C, DH_C, DH_C, B), F32)],
        scratch_shapes=[pltpu.VMEM((2 * DH_C, B), F32), pltpu.VMEM((2 * DH_C, B), F32)],
        compiler_params=_cp("parallel"),
        name="rwkv_step",
    )(state_t, lw, a, b, k, r, v)


def _merge_kernel(x_ref, ya_ref, yb_ref, y_ref, r_ref, k_ref, v_ref, gg_ref, za_ref, zb_ref, zc_ref, gt_ref,
                  lng_ref, lnb_ref, rk_ref, ones_ref, wa_ref, wb_ref, wc_ref, wo_ref, o_ref):
    ones2 = ones_ref[...]
    y = y_ref[...]
    mu = _seg_sum_wide(y, ones2) * (1.0 / DH_C)
    d = y - mu
    var = _seg_sum_wide(d * d, ones2) * (1.0 / DH_C)
    ycn = d * lax.rsqrt(var + LNX_EPS) * lng_ref[...] + lnb_ref[...]
    v = v_ref[...]
    bonus = _seg_sum_wide(r_ref[...] * k_ref[...] * rk_ref[...], ones2) * v
    yc = (ycn + bonus) * gg_ref[...]
    proj = lambda t, w: jnp.dot(t.astype(BF16), w[...], preferred_element_type=F32)
    merged = (jax.nn.sigmoid(za_ref[...]) * proj(ya_ref[...], wa_ref)
              + jax.nn.sigmoid(zb_ref[...]) * proj(yb_ref[...], wb_ref)
              + jax.nn.sigmoid(zc_ref[...]) * proj(yc, wc_ref))
    o_ref[...] = x_ref[...] + gt_ref[...] * proj(merged, wo_ref)


def merge_out(x2, ya, yb, y, r, k, v, gg, z2, gt, lng, lnb, rk, wa, wb, wc, wo, tm):
    T, D = x2.shape
    G, R, _ = gt.shape
    tiles_per_group = (T // tm) // G
    tok = lambda w: pl.BlockSpec((tm, w), lambda m: (m, 0))
    gate = lambda idx: pl.BlockSpec((tm, D), lambda m: (m, Z_GATE // D + idx))
    row = lambda w: pl.BlockSpec((1, w), lambda m: (0, 0))
    full = lambda a: pl.BlockSpec(a.shape, lambda m: (0, 0))
    return pl.pallas_call(
        _merge_kernel,
        grid=(T // tm,),
        in_specs=[tok(D)] + [tok(W_C)] * 7 + [gate(0), gate(1), gate(2),
                  pl.BlockSpec((None, R, D), lambda m: (m // tiles_per_group, 0, 0)),
                  row(W_C), row(W_C), row(W_C), pl.BlockSpec((256, 128), lambda m: (0, 0)),
                  full(wa), full(wb), full(wc), full(wo)],
        out_specs=tok(D),
        out_shape=jax.ShapeDtypeStruct((T, D), F32),
        compiler_params=_cp("parallel"),
        name="merge_out",
    )(x2, ya, yb, y, r, k, v, gg, z2, z2, z2, gt, lng, lnb, rk, _block_ones2(DH_C), wa, wb, wc, wo)


def _ffn_kernel(x_ref, g_ref, sc_ref, sh_ref, gt_ref, w1_ref, w2_ref, o_ref, h_scr, acc_scr):
    f = pl.program_id(1)

    @pl.when(f == 0)
    def _():
        x = x_ref[...]
        y = x * lax.rsqrt(jnp.mean(x * x, axis=-1, keepdims=True) + EPS) * g_ref[...]
        h_scr[...] = (y * (1.0 + sc_ref[...]) + sh_ref[...]).astype(BF16)
        acc_scr[...] = jnp.zeros_like(acc_scr)

    u = jnp.maximum(jnp.dot(h_scr[...], w1_ref[...], preferred_element_type=F32), 0.0)
    acc_scr[...] += jnp.dot((u * u).astype(BF16), w2_ref[...], preferred_element_type=F32)

    @pl.when(f == pl.num_programs(1) - 1)
    def _():
        o_ref[...] = x_ref[...] + gt_ref[...] * acc_scr[...]


def ffn(x2, gain_row, sc, sh, gt, w1, w2, tm, tf):
    T, D = x2.shape
    F = w1.shape[1]
    G, R, _ = sc.shape
    tiles_per_group = (T // tm) // G
    mod_spec = pl.BlockSpec((None, R, D), lambda m, f: (m // tiles_per_group, 0, 0))
    return pl.pallas_call(
        _ffn_kernel,
        grid=(T // tm, F // tf),
        in_specs=[pl.BlockSpec((tm, D), lambda m, f: (m, 0)),
                  pl.BlockSpec((1, D), lambda m, f: (0, 0)),
                  mod_spec, mod_spec, mod_spec,
                  pl.BlockSpec((D, tf), lambda m, f: (0, f)),
                  pl.BlockSpec((tf, D), lambda m, f: (f, 0))],
        out_specs=pl.BlockSpec((tm, D), lambda m, f: (m, 0)),
        out_shape=jax.ShapeDtypeStruct((T, D), F32),
        scratch_shapes=[pltpu.VMEM((tm, D), BF16), pltpu.VMEM((tm, D), F32)],
        compiler_params=_cp("parallel", "arbitrary"),
        name="ffn",
    )(x2, gain_row, sc, sh, gt, w1, w2)


def _prep_layer(p, l):
    D = p['w_in'].shape[1]
    w_in = p['w_in'][l]
    o_diff, o_rwkv, o_gate = RET_COLS, RET_COLS + DIFF_COLS, RET_COLS + DIFF_COLS + RWKV_COLS
    w_in_p = jnp.concatenate([
        w_in[:, :RET_COLS], w_in[:, o_rwkv:o_gate], jnp.zeros((D, RWKV_PAD - RWKV_COLS), F32),
        w_in[:, o_gate:], w_in[:, o_diff:o_rwkv]], axis=1).astype(BF16)
    w_lr = jnp.zeros((LR_COLS, 3 * W_C), F32)
    w_lr = w_lr.at[:R_DECAY, :W_C].set(p['w_decay_up'][l])
    w_lr = w_lr.at[R_DECAY:R_DECAY + R_A, W_C:2 * W_C].set(p['w_a_up'][l])
    w_lr = w_lr.at[R_DECAY + R_A:R_DECAY + R_A + R_G, 2 * W_C:].set(p['w_g_up'][l])
    row = lambda t: t.reshape(1, -1)
    lp = dict(
        w_ada=p['w_ada'][l].astype(BF16), b_ada=row(p['b_ada'][l]),
        norm1=row(p['norm1'][l]), norm2=row(p['norm2'][l]), w_in=w_in_p,
        gq=row(jnp.tile(p['qk_norm_q'][l], 2 * H_B)), gk=row(jnp.tile(p['qk_norm_k'][l], 2 * H_B)),
        lam_rows=jnp.stack([p['lambda_q1'][l], p['lambda_k1'][l], p['lambda_q2'][l], p['lambda_k2'][l]]),
        subln=row(p['subln_diff'][l]),
        mu=row(jnp.pad(p['mu_shift'][l], (0, RWKV_PAD - RWKV_COLS))),
        w0=row(p['w0'][l]), a0=row(p['a0'][l]), k_k=row(p['k_k'][l]), k_a=row(p['k_a'][l]),
        w_lr=w_lr.astype(BF16), r_k=row(p['r_k'][l]), lnx_g=row(p['lnx_g'][l]), lnx_b=row(p['lnx_b'][l]),
        w_up_a=p['w_up_a'][l].astype(BF16), w_up_b=p['w_up_b'][l].astype(BF16),
        w_up_c=p['w_up_c'][l].astype(BF16), w_out=p['w_out'][l].astype(BF16),
        w_ff1=p['w_ff1'][l].astype(BF16), w_ff2=p['w_ff2'][l].astype(BF16),
        lam_init=0.8 - 0.6 * math.exp(-0.3 * l),
    )
    if l > 0:
        lp['v0'] = row(p['v0'][l - 1])
        lp['w_v_down'] = jnp.pad(p['w_v_down'][l - 1], ((0, 0), (0, 128 - R_V))).astype(BF16)
        lp['w_v_up'] = jnp.pad(p['w_v_up'][l - 1], ((0, 128 - R_V), (0, 0))).astype(BF16)
    return lp


def _modulation(c, lp, per_token):
    B, D = c.shape
    rows = -(-B // 16) * 16
    mod = ada_mod(jnp.pad(c, ((0, rows - B), (0, 0))), lp['w_ada'], lp['b_ada'])[:B]
    parts = [mod[:, i * D:(i + 1) * D] for i in range(6)]
    shape = (1, B, D) if per_token else (B, 1, D)
    return [t.reshape(shape) for t in parts]


def _trunk_prompt(x, c, layers, bias_p):
    B, L, D = x.shape
    T = B * L
    tm = min(1024, L)
    t_attn = min(512, L)
    x2 = x.reshape(T, D)
    cos_t, sin_t = _rope_tables(jnp.arange(L))
    k_rows, v_rows, ret_out, rwkv_out, shift_out = [], [], [], [], []
    v_first = None
    for l, lp in enumerate(layers):
        sh1, sc1, gt1, sh2, sc2, gt2 = _modulation(c, lp, per_token=False)
        z2 = in_proj(x2, lp['norm1'], sc1, sh1, lp['w_in'], min(2048, L), 512)
        ZC = z2.shape[1]
        z3 = z2.reshape(B, L, ZC)
        ya, s_ret = retention_prompt(z3, cos_t, sin_t, min(256, L))
        qb, k32, kb, v32, vb = qk_norm(z2, lp['gq'], lp['gk'], tm, DH_B ** -0.5 * math.log2(math.e))
        r3 = lambda t: t.reshape(B, L, W_B)
        yb = diff_attention_prompt(r3(qb), r3(kb), r3(vb), bias_p, lp['lam_rows'], lp['subln'],
                                   lp['lam_init'], t_attn)
        k_rows.append(k32.reshape(B, L, H_B, 2 * DH_B))
        v_rows.append(v32.reshape(B, L, H_B, DV_B))
        vgate = None if l == 0 else (lp['v0'], lp['w_v_down'], lp['w_v_up'], v_first)
        shift0 = jnp.zeros((B, 1, RWKV_PAD), F32)
        r, lw, k, v, a, b, gg = rwkv_prep(z3, shift0, lp['mu'], lp['w0'], lp['a0'], lp['k_k'], lp['k_a'],
                                          lp['w_lr'], vgate, min(256, L), carry_shift=True)
        if l == 0:
            v_first = v
        y, s_pair = rwkv_chunk_scan(r, lw, k, v, a, b, min(256, L), B)
        shift_out.append(z3[:, L - 1:, Z_RWKV:Z_RWKV + RWKV_COLS])
        f2 = lambda t: t.reshape(T, -1)
        x2 = merge_out(x2, f2(ya), f2(yb), f2(y), f2(r), f2(k), f2(v), f2(gg), z2, gt1, lp['lnx_g'],
                       lp['lnx_b'], lp['r_k'], lp['w_up_a'], lp['w_up_b'], lp['w_up_c'], lp['w_out'],
                       min(512, L))
        x2 = ffn(x2, lp['norm2'], sc2, sh2, gt2, lp['w_ff1'], lp['w_ff2'], tm, 1024)
        ret_out.append(s_ret)
        sp = s_pair.reshape(B, H_C // 2, 2, DH_C, 2, DH_C)
        rwkv_out.append(jnp.stack([sp[:, :, 0, :, 0, :], sp[:, :, 1, :, 1, :]], axis=2)
                        .reshape(B, H_C, DH_C, DH_C))
    return (x2.reshape(B, L, D), jnp.stack(k_rows), jnp.stack(v_rows), jnp.stack(ret_out),
            jnp.stack(rwkv_out), jnp.stack(shift_out))


def _trunk_decode(x, c, layers, bias_d, state_ret, state_rwkv, state_shift, cache_k, cache_v, page_table):
    B, _, D = x.shape
    n_pages = page_table.shape[1]
    past = n_pages * PAGE_SIZE
    x2 = x.reshape(B, D)
    pt_flat = page_table.reshape(-1)
    ck = cache_k.reshape(cache_k.shape[0], cache_k.shape[1], PAGE_SIZE * H_B, 2 * DH_B)
    cv = cache_v.reshape(cache_v.shape[0], cache_v.shape[1], PAGE_SIZE * H_B, DV_B)
    state_t = jnp.transpose(state_rwkv, (0, 2, 3, 4, 1))
    k_rows, v_rows, ret_out, rwkv_out, shift_out = [], [], [], [], []
    v_first = None
    for l, lp in enumerate(layers):
        sh1, sc1, gt1, sh2, sc2, gt2 = _modulation(c, lp, per_token=True)
        z2 = in_proj(x2, lp['norm1'], sc1, sh1, lp['w_in'], B, 512)
        ZC = z2.shape[1]
        ya, s_ret = retention_step(z2, state_ret, l, past)
        qb, k32, kb, v32, vb = qk_norm(z2, lp['gq'], lp['gk'], B, DH_B ** -0.5)
        h3 = lambda t: t.reshape(B, H_B, DV_B)
        yb = diff_attention_decode(qb.reshape(B, 1, W_BQK), h3(k32), h3(v32), ck, cv, pt_flat, bias_d,
                                   lp['lam_rows'], lp['subln'], l, lp['lam_init'], n_pages).reshape(B, W_B)
        k_rows.append(k32.reshape(B, 1, H_B, 2 * DH_B))
        v_rows.append(v32.reshape(B, 1, H_B, DV_B))
        vgate = None if l == 0 else (lp['v0'], lp['w_v_down'], lp['w_v_up'], v_first)
        prev = jnp.pad(state_shift[l].reshape(1, B, RWKV_COLS), ((0, 0), (0, 0), (0, RWKV_PAD - RWKV_COLS)))
        r, lw, k, v, a, b, gg = rwkv_prep(z2.reshape(1, B, ZC), prev, lp['mu'], lp['w0'], lp['a0'], lp['k_k'],
                                          lp['k_a'], lp['w_lr'], vgate, B, carry_shift=False)
        if l == 0:
            v_first = v
        f2 = lambda t: t.reshape(B, -1)
        y_dec, s_rwkv = rwkv_step(state_t, l, f2(lw), f2(a), f2(b), f2(k), f2(r), f2(v))
        shift_out.append(z2[:, Z_RWKV:Z_RWKV + RWKV_COLS].reshape(B, 1, RWKV_COLS))
        x2 = merge_out(x2, ya, yb, y_dec, f2(r), f2(k), f2(v), f2(gg), z2, gt1, lp['lnx_g'], lp['lnx_b'],
                       lp['r_k'], lp['w_up_a'], lp['w_up_b'], lp['w_up_c'], lp['w_out'], B)
        x2 = ffn(x2, lp['norm2'], sc2, sh2, gt2, lp['w_ff1'], lp['w_ff2'], B, 1024)
        ret_out.append(s_ret)
        rwkv_out.append(s_rwkv)
    rwkv_state = jnp.transpose(jnp.stack(rwkv_out), (0, 4, 1, 2, 3))
    return (x2.reshape(B, 1, D), jnp.stack(k_rows), jnp.stack(v_rows), jnp.stack(ret_out),
            rwkv_state, jnp.stack(shift_out))


def _decode_buckets(n_pages):
    past = n_pages * PAGE_SIZE
    key = np.arange((n_pages + 1) * PAGE_SIZE)
    bk = np.where(key <= past, _bucket_np(np.maximum(past - key, 0)), -1)
    return np.broadcast_to(bk.reshape(n_pages + 1, 1, PAGE_SIZE), (n_pages + 1, 8, PAGE_SIZE)).astype(np.int32)


def kernel(x_prompt, x_sample, c_prompt, c_sample, cache_k_diff, cache_v_diff, page_table, state_ret, state_rwkv, state_shift, rel_bias, w_ada, b_ada, norm1, norm2, w_in, qk_norm_q, qk_norm_k, lambda_q1, lambda_k1, lambda_q2, lambda_k2, subln_diff, mu_shift, w0, w_decay_up, a0, w_a_up, w_g_up, v0, w_v_down, w_v_up, k_k, k_a, r_k, lnx_g, lnx_b, w_up_a, w_up_b, w_up_c, w_out, w_ff1, w_ff2):
    p = dict(w_ada=w_ada, b_ada=b_ada, norm1=norm1, norm2=norm2, w_in=w_in,
             qk_norm_q=qk_norm_q, qk_norm_k=qk_norm_k, lambda_q1=lambda_q1, lambda_k1=lambda_k1,
             lambda_q2=lambda_q2, lambda_k2=lambda_k2, subln_diff=subln_diff, mu_shift=mu_shift,
             w0=w0, w_decay_up=w_decay_up, a0=a0, w_a_up=w_a_up, w_g_up=w_g_up, v0=v0,
             w_v_down=w_v_down, w_v_up=w_v_up, k_k=k_k, k_a=k_a, r_k=r_k, lnx_g=lnx_g, lnx_b=lnx_b,
             w_up_a=w_up_a, w_up_b=w_up_b, w_up_c=w_up_c, w_out=w_out, w_ff1=w_ff1, w_ff2=w_ff2)
    depth = w_in.shape[0]
    layers = [_prep_layer(p, l) for l in range(depth)]
    L = x_prompt.shape[1]
    n_pages = page_table.shape[1]
    t_attn = min(512, L)
    bias_p = bias_tiles(rel_bias, _prompt_buckets(t_attn), True)
    bd = bias_tiles(rel_bias, _decode_buckets(n_pages).reshape((n_pages + 1) * 8, PAGE_SIZE), False)
    bias_d = bd.reshape(H_B, n_pages + 1, 8, PAGE_SIZE)

    y_p, k_p, v_p, ret_p, rwkv_p, shift_p = _trunk_prompt(x_prompt, c_prompt, layers, bias_p)
    y_s, k_s, v_s, ret_s, rwkv_s, shift_s = _trunk_decode(
        x_sample, c_sample, layers, bias_d, state_ret, state_rwkv, state_shift,
        cache_k_diff, cache_v_diff, page_table)
    return (y_p, y_s, k_p, v_p, k_s, v_s, ret_p, ret_s, rwkv_p, rwkv_s, shift_p, shift_s)
```

```python
import functools
import math

import numpy as np
import jax
import jax.numpy as jnp
from jax import lax
from jax.experimental import pallas as pl
from jax.experimental.pallas import tpu as pltpu

F32 = jnp.float32
BF16 = jnp.bfloat16
HIGHEST = lax.Precision.HIGHEST

H_A, DK_A, DV_A = 4, 128, 128
RET_CHUNK = 128
ROPE_BASE = 10000.0
H_B, DH_B, DV_B = 4, 64, 128
N_BUCKETS, MAX_DISTANCE = 32, 128
H_C, DH_C = 8, 64
R_DECAY, R_A, R_V, R_G = 64, 64, 32, 160
LNX_EPS = 64e-5
EPS = 1e-6
PAGE_SIZE = 128
W_AQ, W_A = H_A * DK_A, H_A * DV_A
W_BQK, W_B = H_B * 2 * DH_B, H_B * DV_B
W_C = H_C * DH_C
RET_COLS = 2 * W_AQ + 2 * W_A
DIFF_COLS = 2 * W_BQK + W_B
RWKV_COLS = 3 * W_C + R_DECAY + R_A + R_G
RWKV_PAD = 2048
LR_COLS = RWKV_PAD - 3 * W_C

Z_RET, Z_RWKV = 0, RET_COLS
Z_GATE = Z_RWKV + RWKV_PAD
NEG = -1e30
RW_CHUNK = 64
VMEM_LIMIT = 56 * 1024 * 1024


def _cp(*sem):
    return pltpu.CompilerParams(dimension_semantics=sem, vmem_limit_bytes=VMEM_LIMIT)


def _silu(x):
    return x * jax.nn.sigmoid(x)


def _split2(x):
    hi = x.astype(BF16)
    lo = (x - hi.astype(F32)).astype(BF16)
    return hi, lo


def _seg_sum(x, ones2):
    hi, lo = _split2(x)
    return jnp.dot(jnp.concatenate([hi, lo], axis=1), ones2, preferred_element_type=F32)


def _seg_sum_wide(x, ones2):
    return jnp.concatenate(
        [_seg_sum(x[:, c * 128:(c + 1) * 128], ones2) for c in range(x.shape[1] // 128)], axis=1)


def _mod_kernel(c_ref, w_ref, b_ref, o_ref):
    s = _silu(c_ref[...])
    o_ref[...] = jnp.dot(s.astype(BF16), w_ref[...].astype(BF16), preferred_element_type=F32) + b_ref[...]


def ada_mod(c_pad, w_stack, b_row, layer):
    R, D = c_pad.shape
    N = w_stack.shape[2]
    tn = 1536
    return pl.pallas_call(
        _mod_kernel,
        grid=(N // tn,),
        in_specs=[pl.BlockSpec((R, D), lambda n: (0, 0)),
                  pl.BlockSpec((None, D, tn), lambda n: (layer, 0, n)),
                  pl.BlockSpec((1, tn), lambda n: (0, n))],
        out_specs=pl.BlockSpec((R, tn), lambda n: (0, n)),
        out_shape=jax.ShapeDtypeStruct((R, N), F32),
        compiler_params=_cp("parallel"),
        name="ada_mod",
    )(c_pad, w_stack, b_row)


def _inproj_kernel(x_ref, g_ref, sc_ref, sh_ref, w_ref, z_ref, h_scr):
    @pl.when(pl.program_id(1) == 0)
    def _():
        x = x_ref[...]
        y = x * lax.rsqrt(jnp.mean(x * x, axis=-1, keepdims=True) + EPS) * g_ref[...]
        h_scr[...] = (y * (1.0 + sc_ref[...]) + sh_ref[...]).astype(BF16)

    z_ref[...] = lax.dot_general(h_scr[...], w_ref[...], (((1,), (1,)), ((), ())),
                                 preferred_element_type=F32)


def in_proj(x2, gain_row, sc, sh, w_bf, layer, tm, tn):
    T, D = x2.shape
    N = w_bf.shape[1]
    G, R, _ = sc.shape
    tiles_per_group = (T // tm) // G
    mod_spec = pl.BlockSpec((None, R, D), lambda m, n: (m // tiles_per_group, 0, 0))
    return pl.pallas_call(
        _inproj_kernel,
        grid=(T // tm, N // tn),
        in_specs=[pl.BlockSpec((tm, D), lambda m, n: (m, 0)),
                  pl.BlockSpec((1, D), lambda m, n: (0, 0)),
                  mod_spec, mod_spec,
                  pl.BlockSpec((None, tn, D), lambda m, n: (layer, n, 0))],
        out_specs=pl.BlockSpec((tm, tn), lambda m, n: (m, n)),
        out_shape=jax.ShapeDtypeStruct((T, N), F32),
        scratch_shapes=[pltpu.VMEM((tm, D), BF16)],
        compiler_params=_cp("parallel", "arbitrary"),
        name="in_proj",
    )(x2, gain_row, sc, sh, w_bf)


def _rope(x, cos, sin_signed):
    return x * cos + pltpu.roll(x, DK_A // 2, 1) * sin_signed


def _ret_kernel(q_ref, k_ref, v_ref, g_ref, cos_ref, sin_ref, inner_ref, cross_ref, tail_ref, cd_ref,
                ya_ref, so_ref, s_scr, *, n_sub):
    lt = pl.program_id(1)

    @pl.when(lt == 0)
    def _():
        s_scr[...] = jnp.zeros_like(s_scr)

    C = RET_CHUNK
    for c in range(n_sub):
        rows = slice(c * C, (c + 1) * C)
        cos = cos_ref[rows, :]
        sin = sin_ref[rows, :]
        for h in range(H_A):
            cols = slice(h * DK_A, (h + 1) * DK_A)
            q = _rope(q_ref[rows, cols], cos, sin)
            k = _rope(k_ref[rows, cols], cos, sin) * (DK_A ** -0.5)
            qb, kb, vb = q.astype(BF16), k.astype(BF16), v_ref[rows, cols].astype(BF16)
            S = s_scr[h]
            sc = lax.dot_general(qb, kb, (((1,), (1,)), ((), ())), preferred_element_type=F32) * inner_ref[h]
            o = (jnp.dot(sc.astype(BF16), vb, preferred_element_type=F32)
                 + jnp.dot(qb, S.astype(BF16), preferred_element_type=F32) * cross_ref[h])
            kt = (k * tail_ref[h]).T.astype(BF16)
            s_scr[h] = S * cd_ref[h] + jnp.dot(kt, vb, preferred_element_type=F32)
            on = o * lax.rsqrt(jnp.mean(o * o, axis=-1, keepdims=True) + EPS)
            ya_ref[rows, cols] = (on * _silu(g_ref[rows, cols])).astype(ya_ref.dtype)

    @pl.when(lt == pl.num_programs(1) - 1)
    def _():
        so_ref[...] = s_scr[...]


def _ret_tables(L):
    C = math.gcd(L, RET_CHUNK)
    log_g = np.log1p(-np.exp2(-5.0 - np.arange(H_A, dtype=np.float32))).astype(np.float32)
    i = np.arange(C, dtype=np.float32)
    dist = i[:, None] - i[None, :]
    causal = dist >= 0
    inner = np.where(causal[None], np.exp(np.where(causal, dist, 0.0)[None] * log_g[:, None, None]), 0.0)
    cross = np.exp((i[None, :] + 1.0) * log_g[:, None])
    tail = np.exp((C - 1.0 - i)[None, :] * log_g[:, None])
    chunk = np.exp(C * log_g)
    bc = lambda t: np.broadcast_to(t[:, :, None], (H_A, C, 128)).astype(np.float32)
    cd = np.broadcast_to(chunk[:, None, None], (H_A, 1, 128)).astype(np.float32)
    return inner.astype(np.float32), bc(cross), bc(tail), cd


def _rope_tables(pos):
    half = DK_A // 2
    inv = ROPE_BASE ** (-jnp.arange(half, dtype=F32) / half)
    ang = pos.astype(F32)[:, None] * inv[None, :]
    cos, sin = jnp.cos(ang), jnp.sin(ang)
    return jnp.concatenate([cos, cos], axis=-1), jnp.concatenate([-sin, sin], axis=-1)


def retention_prompt(z3, cos_t, sin_t, tb):
    B, L, ZC = z3.shape
    inner, cross, tail, cd = _ret_tables(L)
    zspec = lambda idx: pl.BlockSpec((None, tb, W_AQ), lambda b, l: (b, l, idx))
    full3 = lambda s: pl.BlockSpec(s, lambda b, l: (0, 0, 0))
    return pl.pallas_call(
        functools.partial(_ret_kernel, n_sub=tb // RET_CHUNK),
        grid=(B, L // tb),
        in_specs=[zspec(0), zspec(1), zspec(2), zspec(3),
                  pl.BlockSpec((tb, 128), lambda b, l: (l, 0)),
                  pl.BlockSpec((tb, 128), lambda b, l: (l, 0)),
                  full3(inner.shape), full3(cross.shape), full3(tail.shape), full3(cd.shape)],
        out_specs=[pl.BlockSpec((None, tb, W_A), lambda b, l: (b, l, 0)),
                   pl.BlockSpec((None, H_A, DK_A, DV_A), lambda b, l: (b, 0, 0, 0))],
        out_shape=[jax.ShapeDtypeStruct((B, L, W_A), BF16),
                   jax.ShapeDtypeStruct((B, H_A, DK_A, DV_A), F32)],
        scratch_shapes=[pltpu.VMEM((H_A, DK_A, DV_A), F32)],
        compiler_params=_cp("parallel", "arbitrary"),
        name="retention_prompt",
    )(z3, z3, z3, z3, cos_t, sin_t, inner, cross, tail, cd)


def _ret_step_kernel(q_ref, k_ref, v_ref, g_ref, cos_ref, sin_ref, gam_ref, s_ref, ya_ref, so_ref, *, bb):
    cos, sin = cos_ref[...], sin_ref[...]
    row = lax.broadcasted_iota(jnp.int32, (bb, 128), 0)
    for h in range(H_A):
        cols = slice(h * DK_A, (h + 1) * DK_A)
        gam = gam_ref[h]
        q = _rope(q_ref[:, cols], cos, sin)
        k = _rope(k_ref[:, cols], cos, sin) * (DK_A ** -0.5)
        v = v_ref[:, cols]
        qb = q.astype(BF16)
        qk = jnp.sum(qb.astype(F32) * k.astype(BF16).astype(F32), axis=-1, keepdims=True)
        qs = jnp.zeros((bb, 128), F32)
        for b in range(bb):
            S = s_ref[b, h]
            onehot = row == b
            qs = jnp.where(onehot, jnp.dot(qb, S.astype(BF16), preferred_element_type=F32), qs)
            kb_t = jnp.where(onehot, k, 0.0).T.astype(BF16)
            so_ref[b, h] = S * gam + jnp.dot(kb_t, v.astype(BF16), preferred_element_type=F32)
        o = qk * v.astype(BF16).astype(F32) + qs * gam
        on = o * lax.rsqrt(jnp.mean(o * o, axis=-1, keepdims=True) + EPS)
        ya_ref[:, cols] = on * _silu(g_ref[:, cols])


def retention_step(z2, state, layer, pos):
    B = z2.shape[0]
    bb = 8
    cos_t, sin_t = _rope_tables(jnp.full((1,), pos))
    log_g = np.log1p(-np.exp2(-5.0 - np.arange(H_A, dtype=np.float32))).astype(np.float32)
    gam = np.broadcast_to(np.exp(log_g)[:, None, None], (H_A, 1, 128)).astype(np.float32)
    zspec = lambda idx: pl.BlockSpec((bb, W_AQ), lambda i: (i, idx))
    return pl.pallas_call(
        functools.partial(_ret_step_kernel, bb=bb),
        grid=(B // bb,),
        in_specs=[zspec(0), zspec(1), zspec(2), zspec(3),
                  pl.BlockSpec((1, 128), lambda i: (0, 0)),
                  pl.BlockSpec((1, 128), lambda i: (0, 0)),
                  pl.BlockSpec((H_A, 1, 128), lambda i: (0, 0, 0)),
                  pl.BlockSpec((None, bb, H_A, DK_A, DV_A), lambda i: (layer, i, 0, 0, 0))],
        out_specs=[pl.BlockSpec((bb, W_A), lambda i: (i, 0)),
                   pl.BlockSpec((bb, H_A, DK_A, DV_A), lambda i: (i, 0, 0, 0))],
        out_shape=[jax.ShapeDtypeStruct((B, W_A), F32),
                   jax.ShapeDtypeStruct((B, H_A, DK_A, DV_A), F32)],
        compiler_params=_cp("parallel"),
        name="retention_step",
    )(z2, z2, z2, z2, cos_t, sin_t, gam, state)


def _qknorm_kernel(q_ref, k_ref, v_ref, gq_ref, gk_ref, ones_ref, qb_ref, k32_ref, kb_ref, v32_ref, vb_ref,
                   *, q_scale):
    ones2 = ones_ref[...]
    q, k, v = q_ref[...], k_ref[...], v_ref[...]
    tm = q.shape[0]
    qn = q * lax.rsqrt(_seg_sum_wide(q * q, ones2) * (1.0 / DH_B) + EPS) * gq_ref[...] * q_scale
    kn = k * lax.rsqrt(_seg_sum_wide(k * k, ones2) * (1.0 / DH_B) + EPS) * gk_ref[...]
    qb_ref[...] = qn.astype(BF16)
    kb_ref[...] = kn.astype(BF16)
    vb_ref[...] = v.astype(BF16)
    for h in range(H_B):
        head_rows = pl.ds(h, tm, stride=H_B)
        k32_ref[head_rows, :] = kn[:, h * DV_B:(h + 1) * DV_B]
        v32_ref[head_rows, :] = v[:, h * DV_B:(h + 1) * DV_B]


def _block_ones2(group):
    i = np.arange(128)
    m = (i[:, None] // group == i[None, :] // group).astype(np.float32)
    return jnp.asarray(np.concatenate([m, m], axis=0), BF16)


def qk_norm(z2, gq_row, gk_row, tm, q_scale):
    T = z2.shape[0]
    base = (Z_GATE + 3 * 1024) // W_BQK
    zspec = lambda idx: pl.BlockSpec((tm, W_BQK), lambda m: (m, base + idx))
    row = pl.BlockSpec((1, W_BQK), lambda m: (0, 0))
    out = pl.BlockSpec((tm, W_BQK), lambda m: (m, 0))
    out_rows = pl.BlockSpec((tm * H_B, DV_B), lambda m: (m, 0))
    return pl.pallas_call(
        functools.partial(_qknorm_kernel, q_scale=q_scale),
        grid=(T // tm,),
        in_specs=[zspec(0), zspec(1), zspec(2), row, row, pl.BlockSpec((256, 128), lambda m: (0, 0))],
        out_specs=[out, out_rows, out, out_rows, out],
        out_shape=[jax.ShapeDtypeStruct((T, W_BQK), BF16), jax.ShapeDtypeStruct((T * H_B, DV_B), F32),
                   jax.ShapeDtypeStruct((T, W_BQK), BF16), jax.ShapeDtypeStruct((T * H_B, DV_B), F32),
                   jax.ShapeDtypeStruct((T, W_B), BF16)],
        compiler_params=_cp("parallel"),
        name="qk_norm",
    )(z2, z2, z2, gq_row, gk_row, _block_ones2(DH_B))


def _bucket_np(n):
    max_exact = N_BUCKETS // 2
    nf = np.maximum(n, 1).astype(np.float32)
    large = max_exact + (np.log(nf / np.float32(max_exact)) / np.float32(math.log(MAX_DISTANCE / max_exact))
                         * np.float32(N_BUCKETS - max_exact)).astype(np.int32)
    large = np.minimum(large, N_BUCKETS - 1)
    return np.where(n < max_exact, n, large).astype(np.int32)


def _bias_kernel(tab_ref, bkt_ref, o_ref, *, log2_far_shift):
    h = pl.program_id(0)
    bk = bkt_ref[...]
    acc = jnp.zeros(bk.shape, F32)
    for b in range(N_BUCKETS):
        acc = jnp.where(bk == b, tab_ref[b, h], acc)
    if log2_far_shift:
        acc = (acc - tab_ref[N_BUCKETS - 1, h]) * math.log2(math.e)
    o_ref[...] = jnp.where(bk < 0, NEG, acc)


def bias_tiles(rel_bias, buckets, log2_far_shift):
    R, C = buckets.shape
    tr = min(R, 512)
    return pl.pallas_call(
        functools.partial(_bias_kernel, log2_far_shift=log2_far_shift),
        grid=(H_B, R // tr),
        in_specs=[pl.BlockSpec(memory_space=pltpu.SMEM),
                  pl.BlockSpec((tr, C), lambda h, r: (r, 0))],
        out_specs=pl.BlockSpec((None, tr, C), lambda h, r: (h, r, 0)),
        out_shape=jax.ShapeDtypeStruct((H_B, R, C), F32),
        compiler_params=_cp("parallel", "parallel"),
        name="bias_tiles",
    )(rel_bias, jnp.asarray(buckets))


def _prompt_buckets(t):
    r = np.arange(t)[:, None]
    c = np.arange(t)[None, :]
    diag = np.where(c <= r, _bucket_np(np.maximum(r - c, 0)), -1)
    off1 = _bucket_np(t + r - c)
    assert t >= MAX_DISTANCE
    return np.concatenate([diag, off1], axis=0).astype(np.int32)


def _lambda_full(lam_ref, lam_init):
    lv = lam_ref[...]
    s1 = jnp.sum(lv[0:1] * lv[1:2], axis=-1, keepdims=True)
    s2 = jnp.sum(lv[2:3] * lv[3:4], axis=-1, keepdims=True)
    return jnp.exp(s1) - jnp.exp(s2) + lam_init


def _flash_kernel(q_ref, k_ref, v_ref, bias_ref, lam_ref, gain_ref, o_ref, qq_scr, m_scr, acc_scr,
                  *, lam_init, sub):
    i = pl.program_id(2)
    t = q_ref.shape[0]

    q = q_ref[...]
    first = lax.broadcasted_iota(jnp.int32, q.shape, 1) < DH_B
    qq_scr[:t] = jnp.where(first, q, jnp.zeros_like(q))
    qq_scr[t:] = jnp.where(first, jnp.zeros_like(q), q)
    m_scr[...] = jnp.full(m_scr.shape, NEG, F32)
    acc_scr[...] = jnp.zeros_like(acc_scr)

    def kv_block(j, tile):
        krows = pl.ds(pl.multiple_of(j * t, t), t)
        k = k_ref[krows, :]
        v1 = jnp.concatenate([v_ref[krows, :], jnp.ones((t, DV_B), BF16)], axis=1)

        def scores(u):
            rows = slice(u * sub, (u + 1) * sub)
            s = lax.dot_general(qq_scr[rows], k, (((1,), (1,)), ((), ())), preferred_element_type=F32)
            if tile is None:
                return s
            brow = tile * t + (u * sub) % t
            return s + bias_ref[brow:brow + sub, :]

        n_sub = 2 * t // sub
        s_next = scores(0)
        for u in range(n_sub):
            s = s_next
            if u + 1 < n_sub:
                s_next = scores(u + 1)
            rows = slice(u * sub, (u + 1) * sub)
            m_old = m_scr[rows]
            m_new = jnp.maximum(m_old, jnp.max(s, axis=-1, keepdims=True))
            alpha = jnp.exp2(m_old - m_new)
            p = jnp.concatenate([jnp.exp2(s[:, c * 128:(c + 1) * 128] - m_new) for c in range(t // 128)],
                                axis=1).astype(BF16)
            acc_scr[rows] = (jnp.concatenate([alpha, alpha], axis=1) * acc_scr[rows]
                             + jnp.dot(p, v1, preferred_element_type=F32))
            m_scr[rows] = m_new

    def far_pair(jj, carry):
        kv_block(2 * jj, None)
        kv_block(2 * jj + 1, None)
        return carry

    n_far = jnp.maximum(i - 1, 0)
    lax.fori_loop(0, n_far // 2, far_pair, 0)

    @pl.when(n_far % 2 == 1)
    def _():
        kv_block(n_far - 1, None)

    @pl.when(i >= 1)
    def _():
        kv_block(i - 1, 1)

    kv_block(i, 0)

    lam = _lambda_full(lam_ref, lam_init)
    acc = acc_scr[...]
    ob = acc[:t, :DV_B] / acc[:t, DV_B:] - lam * (acc[t:, :DV_B] / acc[t:, DV_B:])
    on = ob * lax.rsqrt(jnp.mean(ob * ob, axis=-1, keepdims=True) + EPS) * gain_ref[...]
    o_ref[...] = (on * (1.0 - lam_init)).astype(o_ref.dtype)


def diff_attention_prompt(qb, kb, vb, bias, lam_rows, gain_row, lam_init, t):
    B, L, _ = qb.shape
    n = L // t
    kv_spec = pl.BlockSpec((None, L, DV_B), lambda b, h, i: (b, 0, h))
    return pl.pallas_call(
        functools.partial(_flash_kernel, lam_init=lam_init, sub=min(512, t)),
        grid=(B, H_B, n),
        in_specs=[pl.BlockSpec((None, t, DV_B), lambda b, h, i: (b, i, h)),
                  kv_spec, kv_spec,
                  pl.BlockSpec((None, 2 * t, t), lambda b, h, i: (h, 0, 0)),
                  pl.BlockSpec((4, DH_B), lambda b, h, i: (0, 0)),
                  pl.BlockSpec((1, DV_B), lambda b, h, i: (0, 0))],
        out_specs=pl.BlockSpec((None, t, DV_B), lambda b, h, i: (b, i, h)),
        out_shape=jax.ShapeDtypeStruct((B, L, W_B), BF16),
        scratch_shapes=[pltpu.VMEM((2 * t, DV_B), BF16), pltpu.VMEM((2 * t, 128), F32),
                        pltpu.VMEM((2 * t, 2 * DV_B), F32)],
        compiler_params=_cp("parallel", "parallel", "arbitrary"),
        name="diff_attention_prompt",
    )(qb, kb, vb, bias, lam_rows, gain_row)


def _decode_kernel(pt_ref, q_ref, kn_ref, vn_ref, bias_ref, lam_ref, gain_ref, *rest, n_pages, lam_init):
    k_refs, v_refs, o_ref = rest[:n_pages], rest[n_pages:2 * n_pages], rest[2 * n_pages]
    lam = _lambda_full(lam_ref, lam_init)
    gain = gain_ref[...]
    r8 = lax.broadcasted_iota(jnp.int32, (8, 2 * DH_B), 0)
    l8 = lax.broadcasted_iota(jnp.int32, (8, 2 * DH_B), 1)
    comp_rows = (l8 // DH_B) == r8
    row0 = lax.broadcasted_iota(jnp.int32, (PAGE_SIZE, DV_B), 0) == 0
    nt = lambda x, y: lax.dot_general(x, y, (((1,), (1,)), ((), ())), preferred_element_type=F32)
    heads = range(H_B)
    cols = [slice(h * DV_B, (h + 1) * DV_B) for h in heads]
    head_rows = [pl.ds(h, PAGE_SIZE, stride=H_B) for h in heads]
    ss = []
    for h in heads:
        qrows = jnp.where(comp_rows, q_ref[:, cols[h]].astype(F32), 0.0).astype(BF16)
        ks = [r[head_rows[h], :].astype(BF16) for r in k_refs]
        ks.append(jnp.where(row0, kn_ref[h:h + 1, :], 0.0).astype(BF16))
        ss.append([nt(qrows, kp) + bias_ref[h, p] for p, kp in enumerate(ks)])
    ps, ls = [], []
    for h in heads:
        m = functools.reduce(jnp.maximum, [jnp.max(s, axis=-1, keepdims=True) for s in ss[h]])
        p = [jnp.exp(s - m) for s in ss[h]]
        ls.append(functools.reduce(jnp.add, [jnp.sum(x, axis=-1, keepdims=True) for x in p]))
        ps.append([x.astype(BF16) for x in p])
    for h in heads:
        vs = [r[head_rows[h], :].astype(BF16) for r in v_refs]
        vs.append(jnp.where(row0, vn_ref[h:h + 1, :], 0.0).astype(BF16))
        acc = functools.reduce(jnp.add, [jnp.dot(p, vp, preferred_element_type=F32)
                                         for p, vp in zip(ps[h], vs)])
        outn = acc / ls[h]
        oh = outn[0:1] - lam * outn[1:2]
        on = oh * lax.rsqrt(jnp.mean(oh * oh, axis=-1, keepdims=True) + EPS) * gain
        o_ref[:, cols[h]] = on * (1.0 - lam_init)


def diff_attention_decode(qb, k32, v32, cache_k, cache_v, pt_flat, bias_dec, lam_rows, gain_row, layer,
                          lam_init, n_pages):
    B = qb.shape[0]
    rowspec = pl.BlockSpec((None, 1, W_BQK), lambda b, pt: (b, 0, 0))
    headspec = pl.BlockSpec((None, H_B, DV_B), lambda b, pt: (b, 0, 0))
    const = lambda s: pl.BlockSpec(s, lambda b, pt: tuple(0 for _ in s))
    page_spec = lambda p: pl.BlockSpec((None, None, PAGE_SIZE * H_B, DV_B),
                                       lambda b, pt, p=p: (layer, pt[b * n_pages + p], 0, 0))
    grid_spec = pltpu.PrefetchScalarGridSpec(
        num_scalar_prefetch=1,
        grid=(B,),
        in_specs=[rowspec, headspec, headspec, const(bias_dec.shape), const((4, DH_B)), const((1, DV_B))]
                 + [page_spec(p) for p in range(n_pages)] * 2,
        out_specs=pl.BlockSpec((None, 1, W_B), lambda b, pt: (b, 0, 0)),
    )
    return pl.pallas_call(
        functools.partial(_decode_kernel, n_pages=n_pages, lam_init=lam_init),
        grid_spec=grid_spec,
        out_shape=jax.ShapeDtypeStruct((B, 1, W_B), F32),
        compiler_params=_cp("arbitrary"),
        name="diff_attention_decode",
    )(pt_flat, qb, k32, v32, bias_dec, lam_rows, gain_row,
      *([cache_k] * n_pages), *([cache_v] * n_pages))


def _rwkv_prep_kernel(*refs, carry_shift, gate_v):
    it = iter(refs)
    z_ref, prev_ref, mu_ref, w0_ref, a0_ref, kk_ref, ka_ref, wlr_ref, ones_ref = (next(it) for _ in range(9))
    if gate_v:
        v0_ref, wvd_ref, wvu_ref, vf_ref = (next(it) for _ in range(4))
    r_ref, lw_ref, k_ref, v_ref, a_ref, b_ref, gg_ref = (next(it) for _ in range(7))
    z = z_ref[...]
    if carry_shift:
        carry_scr = next(it)

        @pl.when(pl.program_id(1) == 0)
        def _():
            carry_scr[...] = prev_ref[...]

        first = lax.broadcasted_iota(jnp.int32, z.shape, 0) == 0
        zprev = jnp.where(first, carry_scr[...], pltpu.roll(z, 1, 0))
        carry_scr[...] = z_ref[z.shape[0] - 1:z.shape[0], :]
    else:
        zprev = prev_ref[...]
    zs = z + (zprev - z) * mu_ref[...]
    rc, kc, vc = zs[:, :W_C], zs[:, W_C:2 * W_C], zs[:, 2 * W_C:3 * W_C]
    lr = zs[:, 3 * W_C:]
    col = lax.broadcasted_iota(jnp.int32, lr.shape, 1)
    act = jnp.where(col < R_DECAY, jnp.tanh(lr), jnp.where(col < R_DECAY + R_A, lr, jax.nn.sigmoid(lr)))
    up = jnp.dot(act.astype(BF16), wlr_ref[...], preferred_element_type=F32)
    y = -(w0_ref[...] + up[:, :W_C])
    softplus = jnp.maximum(y, 0.0) + jnp.log(1.0 + jnp.exp(-jnp.abs(y)))
    lw_ref[...] = -jnp.exp(-softplus - 0.5)
    if gate_v:
        down = jnp.dot(vc.astype(BF16), wvd_ref[...], preferred_element_type=F32)
        vgate = jax.nn.sigmoid(v0_ref[...] + jnp.dot(down.astype(BF16), wvu_ref[...], preferred_element_type=F32))
        vc = vc + (vf_ref[...].astype(F32) - vc) * vgate
    a = jax.nn.sigmoid(a0_ref[...] + up[:, W_C:2 * W_C])
    gg_ref[...] = up[:, 2 * W_C:].astype(gg_ref.dtype)
    kk = kc * kk_ref[...]
    norm = jnp.sqrt(_seg_sum_wide(kk * kk, ones_ref[...]))
    kk = kk / jnp.maximum(norm, 1e-12)
    r_ref[...] = rc.astype(r_ref.dtype)
    k_ref[...] = (kc * (1.0 + (a - 1.0) * ka_ref[...])).astype(k_ref.dtype)
    v_ref[...] = vc.astype(v_ref.dtype)
    a_ref[...] = (-kk).astype(a_ref.dtype)
    b_ref[...] = (kk * a).astype(b_ref.dtype)


def rwkv_prep(z3, prev, mu_row, w0, a0, k_k, k_a, w_lr, vgate, tm, carry_shift):
    G, Lg, _ = z3.shape
    blk = lambda w, idx: pl.BlockSpec((None, tm, w), lambda g, l: (g, l, idx))
    row = lambda w: pl.BlockSpec((1, w), lambda g, l: (0, 0))
    prev_spec = (pl.BlockSpec((None, 1, RWKV_PAD), lambda g, l: (g, 0, 0)) if carry_shift
                 else blk(RWKV_PAD, 0))
    in_specs = [blk(RWKV_PAD, Z_RWKV // RWKV_PAD), prev_spec, row(RWKV_PAD), row(W_C), row(W_C), row(W_C),
                row(W_C), pl.BlockSpec(w_lr.shape, lambda g, l: (0, 0)),
                pl.BlockSpec((256, 128), lambda g, l: (0, 0))]
    args = [z3, prev, mu_row, w0, a0, k_k, k_a, w_lr, _block_ones2(DH_C)]
    if vgate is not None:
        v0, wvd, wvu, vfirst = vgate
        in_specs += [row(W_C), pl.BlockSpec(wvd.shape, lambda g, l: (0, 0)),
                     pl.BlockSpec(wvu.shape, lambda g, l: (0, 0)), blk(W_C, 0)]
        args += [v0, wvd, wvu, vfirst]
    return pl.pallas_call(
        functools.partial(_rwkv_prep_kernel, carry_shift=carry_shift, gate_v=vgate is not None),
        grid=(G, Lg // tm),
        in_specs=in_specs,
        out_specs=[blk(W_C, 0)] * 7,
        out_shape=[jax.ShapeDtypeStruct((G, Lg, W_C), F32 if i == 1 else BF16) for i in range(7)],
        scratch_shapes=[pltpu.VMEM((1, RWKV_PAD), F32)] if carry_shift else [],
        compiler_params=_cp("parallel", "arbitrary"),
        name="rwkv_prep",
    )(*args)


def _rwkv_chunk_kernel(r_ref, lw_ref, k_ref, v_ref, a_ref, b_ref, tri_ref, msl_ref, mli_ref, lvl_ref,
                       y_ref, so_ref, s_scr, *, n_chunks, nb):
    C = RW_CHUNK

    @pl.when(pl.program_id(1) == 0)
    def _():
        s_scr[...] = jnp.zeros_like(s_scr)

    head0 = lax.broadcasted_iota(jnp.int32, (C, 128), 1) < DH_C
    tri3, msl, mli = tri_ref[...], msl_ref[...], mli_ref[...]
    ri = lax.broadcasted_iota(jnp.int32, (128, 128), 0)
    ci = lax.broadcasted_iota(jnp.int32, (128, 128), 1)
    eye = (ri == ci).astype(F32)
    n_levels = lvl_ref.shape[0]
    chains = [(bi, p) for bi in range(nb) for p in range(H_C // 2)]

    def stack(x):
        return jnp.concatenate([jnp.where(head0, x, 0.0), jnp.where(head0, 0.0, x)], axis=0)

    def nt(x, y):
        return lax.dot_general(x, y, (((1,), (1,)), ((), ())), preferred_element_type=F32)

    def dot(x, y):
        return jnp.dot(x, y, preferred_element_type=F32)

    def lhs2(h, l):
        return jnp.concatenate([h, l], axis=1)

    def rhs2(x):
        h = x.astype(BF16)
        return jnp.concatenate([h, h], axis=0)

    def chunk(ci_, carry):
        rows = pl.ds(pl.multiple_of(ci_ * C, C), C)
        ld = lambda ref, bi, p: ref[bi, rows, p * 128:(p + 1) * 128].astype(F32)
        lams = []
        for bi, p in chains:
            lw = ld(lw_ref, bi, p)
            hi = lw.astype(BF16)
            mid = (lw - hi.astype(F32)).astype(BF16)
            lo = (lw - hi.astype(F32) - mid.astype(F32)).astype(BF16)
            lams.append(dot(tri3, jnp.concatenate([hi, mid, lo], axis=0)))
        AR, BK, Vs, KB, dec = [], [], [], [], []
        for (bi, p), lam in zip(chains, lams):
            r, lw, k = ld(r_ref, bi, p), ld(lw_ref, bi, p), ld(k_ref, bi, p)
            v, a, b = ld(v_ref, bi, p), ld(a_ref, bi, p), ld(b_ref, bi, p)
            lam_c = lam[C - 1:C, :]
            e_neg = jnp.exp(-lam)
            e_tail = jnp.exp(lam_c - lam)
            AR.append(jnp.concatenate([stack(a * jnp.exp(lam - lw)), stack(r * jnp.exp(lam))],
                                      axis=0).astype(BF16))
            BK.append(jnp.concatenate([stack(b * e_neg), stack(k * e_neg)], axis=0).astype(BF16))
            KB.append(jnp.concatenate([stack(k * e_tail), stack(b * e_tail)], axis=0).astype(BF16))
            Vs.append(stack(v))
            dec.append(jnp.exp(lam_c))
        G = [nt(x, y) for x, y in zip(AR, BK)]
        I0 = [nt(x, s_scr[bi, p].astype(BF16)) for x, (bi, p) in zip(AR, chains)]
        rhs = [i0[:128] + dot((g[:128, 128:] * msl).astype(BF16), v.astype(BF16))
               for g, i0, v in zip(G, I0, Vs)]
        Nh, Nl, D = [], [], []
        for g in G:
            h, l = _split2(g[:128, :128] * msl)
            Nh.append(h)
            Nl.append(l)
            D.append(eye + (h * lvl_ref[0]).astype(F32) + (l * lvl_ref[0]).astype(F32))
        for lv in range(1, n_levels):
            m = lvl_ref[lv]
            Ds = [_split2(d) for d in D]
            X = [dot(lhs2(h * m, l * m), jnp.concatenate([dh, dh], axis=0))
                 for h, l, (dh, dl) in zip(Nh, Nl, Ds)]
            D = [d + dot(lhs2(dh, dl), rhs2(x)) for d, (dh, dl), x in zip(D, Ds, X)]
        Ds = [_split2(d) for d in D]
        Us = [dot(lhs2(dh, dl), rhs2(x)) for (dh, dl), x in zip(Ds, rhs)]
        for (bi, p), g, i0, u, v in zip(chains, G, I0, Us, Vs):
            ys = i0[128:] + dot(jnp.concatenate([g[128:, :128] * mli, g[128:, 128:] * mli], axis=1).astype(BF16),
                                jnp.concatenate([u, v], axis=0).astype(BF16))
            y_ref[bi, rows, p * 128:(p + 1) * 128] = ys[:C] + ys[C:]
        for (bi, p), u, v, kb, d in zip(chains, Us, Vs, KB, dec):
            vu = jnp.concatenate([v, u], axis=0)
            s_scr[bi, p] = s_scr[bi, p] * d + dot(vu.T.astype(BF16), kb)
        return carry

    lax.fori_loop(0, n_chunks, chunk, 0)

    @pl.when(pl.program_id(1) == pl.num_programs(1) - 1)
    def _():
        so_ref[...] = s_scr[...]


def _rwkv_chunk_tables():
    C = RW_CHUNK
    t = np.arange(C)
    tri = (t[:, None] >= t[None, :]).astype(np.float32)
    tri3 = np.concatenate([tri, tri, tri], axis=1)
    i = np.arange(2 * C)
    same = (i[:, None] // C) == (i[None, :] // C)
    msl = (same & (i[:, None] > i[None, :])).astype(np.float32)
    mli = (same & (i[:, None] >= i[None, :])).astype(np.float32)
    levels = []
    n = 1
    while n < C:
        levels.append(same & ((i[:, None] // (2 * n)) == (i[None, :] // (2 * n)))
                      & ((i[:, None] // n) % 2 == 1) & ((i[None, :] // n) % 2 == 0))
        n *= 2
    return jnp.asarray(tri3, BF16), msl, mli, jnp.asarray(np.stack(levels).astype(np.float32), BF16)


def rwkv_chunk_scan(r, lw, k, v, a, b, tb, nb):
    B, L, _ = r.shape
    tri3, msl, mli, lvl = _rwkv_chunk_tables()
    blk = pl.BlockSpec((nb, tb, W_C), lambda bi, l: (bi, l, 0))
    c2 = lambda s: pl.BlockSpec(s, lambda bi, l: tuple(0 for _ in s))
    return pl.pallas_call(
        functools.partial(_rwkv_chunk_kernel, n_chunks=tb // RW_CHUNK, nb=nb),
        grid=(B // nb, L // tb),
        in_specs=[blk] * 6 + [c2(tri3.shape), c2(msl.shape), c2(mli.shape), c2(lvl.shape)],
        out_specs=[blk, pl.BlockSpec((nb, H_C // 2, 128, 128), lambda bi, l: (bi, 0, 0, 0))],
        out_shape=[jax.ShapeDtypeStruct((B, L, W_C), F32),
                   jax.ShapeDtypeStruct((B, H_C // 2, 128, 128), F32)],
        scratch_shapes=[pltpu.VMEM((nb, H_C // 2, 128, 128), F32)],
        compiler_params=_cp("parallel", "arbitrary"),
        name="rwkv_chunk_scan",
    )(r, lw, k, v, a, b, tri3, msl, mli, lvl)


def _rwkv_step_kernel(s_ref, w_ref, a_ref, b_ref, k_ref, r_ref, v_ref, y_ref, so_ref, vt_scr, y_scr):
    tr = lambda ref: ref[...].astype(F32).T
    wT, aT, bT, kT, rT = jnp.exp(tr(w_ref)), tr(a_ref), tr(b_ref), tr(k_ref), tr(r_ref)
    vt_scr[...] = tr(v_ref)
    for hh in range(2):
        ch = slice(hh * DH_C, (hh + 1) * DH_C)
        w, a, b, k, r = wT[ch], aT[ch], bT[ch], kT[ch], rT[ch]

        def value_row(i, carry):
            row = pl.ds(hh * DH_C + i, 1)
            S = s_ref[hh, i]
            sa = jnp.sum(S * a, axis=0, keepdims=True)
            Sn = S * w + sa * b + vt_scr[row, :] * k
            so_ref[hh, i] = Sn
            y_scr[row, :] = jnp.sum(Sn * r, axis=0, keepdims=True)
            return carry

        lax.fori_loop(0, DH_C, value_row, 0)
    y_ref[...] = y_scr[...].T


def rwkv_step(state_t, layer, lw, a, b, k, r, v):
    B = lw.shape[0]
    vec = pl.BlockSpec((B, 2 * DH_C), lambda p: (0, p))
    return pl.pallas_call(
        _rwkv_step_kernel,
        grid=(H_C // 2,),
        in_specs=[pl.BlockSpec((None, 2, DH_C, DH_C, B), lambda p: (layer, p, 0, 0, 0))] + [vec] * 6,
        out_specs=[vec, pl.BlockSpec((2, DH_C, DH_C, B), lambda p: (p, 0, 0, 0))],
        out_shape=[jax.ShapeDtypeStruct((B, W_C), F32),
                   jax.ShapeDtypeStruct((H_C, DH_C, DH_C, B), F32)],
        scratch_shapes=[pltpu.VMEM((2 * DH_C, B), F32), pltpu.VMEM((2 * DH_C, B), F32)],
        compiler_params=_cp("parallel"),
        name="rwkv_step",
    )(state_t, lw, a, b, k, r, v)


def _merge_kernel(x_ref, ya_ref, yb_ref, y_ref, r_ref, k_ref, v_ref, gg_ref, za_ref, zb_ref, zc_ref, gt_ref,
                  lng_ref, lnb_ref, rk_ref, ones_ref, wa_ref, wb_ref, wc_ref, wo_ref, o_ref):
    ones2 = ones_ref[...]
    y = y_ref[...]
    mu = _seg_sum_wide(y, ones2) * (1.0 / DH_C)
    d = y - mu
    var = _seg_sum_wide(d * d, ones2) * (1.0 / DH_C)
    ycn = d * lax.rsqrt(var + LNX_EPS) * lng_ref[...] + lnb_ref[...]
    f32 = lambda ref: ref[...].astype(F32)
    bonus = _seg_sum_wide(f32(r_ref) * f32(k_ref) * rk_ref[...], ones2) * f32(v_ref)
    yc = (ycn + bonus) * f32(gg_ref)
    proj = lambda t, w: jnp.dot(t.astype(BF16), w[...], preferred_element_type=F32)
    merged = (jax.nn.sigmoid(za_ref[...]) * proj(ya_ref[...], wa_ref)
              + jax.nn.sigmoid(zb_ref[...]) * proj(yb_ref[...], wb_ref)
              + jax.nn.sigmoid(zc_ref[...]) * proj(yc, wc_ref))
    o_ref[...] = x_ref[...] + gt_ref[...] * proj(merged, wo_ref)


def merge_out(x2, ya, yb, y, r, k, v, gg, z2, gt, lng, lnb, rk, wa, wb, wc, wo, tm):
    T, D = x2.shape
    G, R, _ = gt.shape
    tiles_per_group = (T // tm) // G
    tok = lambda w: pl.BlockSpec((tm, w), lambda m: (m, 0))
    gate = lambda idx: pl.BlockSpec((tm, D), lambda m: (m, Z_GATE // D + idx))
    row = lambda w: pl.BlockSpec((1, w), lambda m: (0, 0))
    full = lambda a: pl.BlockSpec(a.shape, lambda m: (0, 0))
    return pl.pallas_call(
        _merge_kernel,
        grid=(T // tm,),
        in_specs=[tok(D)] + [tok(W_C)] * 7 + [gate(0), gate(1), gate(2),
                  pl.BlockSpec((None, R, D), lambda m: (m // tiles_per_group, 0, 0)),
                  row(W_C), row(W_C), row(W_C), pl.BlockSpec((256, 128), lambda m: (0, 0)),
                  full(wa), full(wb), full(wc), full(wo)],
        out_specs=tok(D),
        out_shape=jax.ShapeDtypeStruct((T, D), F32),
        compiler_params=_cp("parallel"),
        name="merge_out",
    )(x2, ya, yb, y, r, k, v, gg, z2, z2, z2, gt, lng, lnb, rk, _block_ones2(DH_C), wa, wb, wc, wo)


def _ffn_kernel(x_ref, g_ref, sc_ref, sh_ref, gt_ref, w1_ref, w2_ref, o_ref, h_scr, acc_scr):
    f = pl.program_id(1)

    @pl.when(f == 0)
    def _():
        x = x_ref[...]
        y = x * lax.rsqrt(jnp.mean(x * x, axis=-1, keepdims=True) + EPS) * g_ref[...]
        h_scr[...] = (y * (1.0 + sc_ref[...]) + sh_ref[...]).astype(BF16)
        acc_scr[...] = jnp.zeros_like(acc_scr)

    u = jnp.maximum(jnp.dot(h_scr[...], w1_ref[...], preferred_element_type=F32), 0.0)
    acc_scr[...] += jnp.dot((u * u).astype(BF16), w2_ref[...], preferred_element_type=F32)

    @pl.when(f == pl.num_programs(1) - 1)
    def _():
        o_ref[...] = x_ref[...] + gt_ref[...] * acc_scr[...]


def ffn(x2, gain_row, sc, sh, gt, w1, w2, tm, tf):
    T, D = x2.shape
    F = w1.shape[1]
    G, R, _ = sc.shape
    tiles_per_group = (T // tm) // G
    mod_spec = pl.BlockSpec((None, R, D), lambda m, f: (m // tiles_per_group, 0, 0))
    return pl.pallas_call(
        _ffn_kernel,
        grid=(T // tm, F // tf),
        in_specs=[pl.BlockSpec((tm, D), lambda m, f: (m, 0)),
                  pl.BlockSpec((1, D), lambda m, f: (0, 0)),
                  mod_spec, mod_spec, mod_spec,
                  pl.BlockSpec((D, tf), lambda m, f: (0, f)),
                  pl.BlockSpec((tf, D), lambda m, f: (f, 0))],
        out_specs=pl.BlockSpec((tm, D), lambda m, f: (m, 0)),
        out_shape=jax.ShapeDtypeStruct((T, D), F32),
        scratch_shapes=[pltpu.VMEM((tm, D), BF16), pltpu.VMEM((tm, D), F32)],
        compiler_params=_cp("parallel", "arbitrary"),
        name="ffn",
    )(x2, gain_row, sc, sh, gt, w1, w2)


def prep_w_in(w_in):
    depth, D, _ = w_in.shape
    wt = jnp.transpose(w_in, (0, 2, 1))
    o_diff, o_rwkv, o_gate = RET_COLS, RET_COLS + DIFF_COLS, RET_COLS + DIFF_COLS + RWKV_COLS
    return jnp.concatenate([
        wt[:, :RET_COLS], wt[:, o_rwkv:o_gate], jnp.zeros((depth, RWKV_PAD - RWKV_COLS, D), F32),
        wt[:, o_gate:], wt[:, o_diff:o_rwkv]], axis=1).astype(BF16)


def _prep_layer(p, l, w_in_p):
    w_lr = jnp.zeros((LR_COLS, 3 * W_C), F32)
    w_lr = w_lr.at[:R_DECAY, :W_C].set(p['w_decay_up'][l])
    w_lr = w_lr.at[R_DECAY:R_DECAY + R_A, W_C:2 * W_C].set(p['w_a_up'][l])
    w_lr = w_lr.at[R_DECAY + R_A:R_DECAY + R_A + R_G, 2 * W_C:].set(p['w_g_up'][l])
    row = lambda t: t.reshape(1, -1)
    lp = dict(
        layer=l, w_ada=p['w_ada'], b_ada=row(p['b_ada'][l]),
        norm1=row(p['norm1'][l]), norm2=row(p['norm2'][l]), w_in=w_in_p,
        gq=row(jnp.tile(p['qk_norm_q'][l], 2 * H_B)), gk=row(jnp.tile(p['qk_norm_k'][l], 2 * H_B)),
        lam_rows=jnp.stack([p['lambda_q1'][l], p['lambda_k1'][l], p['lambda_q2'][l], p['lambda_k2'][l]]),
        subln=row(p['subln_diff'][l]),
        mu=row(jnp.pad(p['mu_shift'][l], (0, RWKV_PAD - RWKV_COLS))),
        w0=row(p['w0'][l]), a0=row(p['a0'][l]), k_k=row(p['k_k'][l]), k_a=row(p['k_a'][l]),
        w_lr=w_lr.astype(BF16), r_k=row(p['r_k'][l]), lnx_g=row(p['lnx_g'][l]), lnx_b=row(p['lnx_b'][l]),
        w_up_a=p['w_up_a'][l].astype(BF16), w_up_b=p['w_up_b'][l].astype(BF16),
        w_up_c=p['w_up_c'][l].astype(BF16), w_out=p['w_out'][l].astype(BF16),
        w_ff1=p['w_ff1'][l].astype(BF16), w_ff2=p['w_ff2'][l].astype(BF16),
        lam_init=0.8 - 0.6 * math.exp(-0.3 * l),
    )
    if l > 0:
        lp['v0'] = row(p['v0'][l - 1])
        lp['w_v_down'] = jnp.pad(p['w_v_down'][l - 1], ((0, 0), (0, 128 - R_V))).astype(BF16)
        lp['w_v_up'] = jnp.pad(p['w_v_up'][l - 1], ((0, 128 - R_V), (0, 0))).astype(BF16)
    return lp


def _modulation(c, lp, per_token):
    B, D = c.shape
    rows = -(-B // 16) * 16
    mod = ada_mod(jnp.pad(c, ((0, rows - B), (0, 0))), lp['w_ada'], lp['b_ada'], lp['layer'])[:B]
    parts = [mod[:, i * D:(i + 1) * D] for i in range(6)]
    shape = (1, B, D) if per_token else (B, 1, D)
    return [t.reshape(shape) for t in parts]


def _trunk_prompt(x, c, layers, bias_p):
    B, L, D = x.shape
    T = B * L
    tm = min(1024, L)
    t_attn = min(512, L)
    x2 = x.reshape(T, D)
    cos_t, sin_t = _rope_tables(jnp.arange(L))
    k_rows, v_rows, ret_out, rwkv_out, shift_out = [], [], [], [], []
    v_first = None
    for l, lp in enumerate(layers):
        sh1, sc1, gt1, sh2, sc2, gt2 = _modulation(c, lp, per_token=False)
        z2 = in_proj(x2, lp['norm1'], sc1, sh1, lp['w_in'], l, min(2048, L), 512)
        ZC = z2.shape[1]
        z3 = z2.reshape(B, L, ZC)
        ya, s_ret = retention_prompt(z3, cos_t, sin_t, min(256, L))
        qb, k32, kb, v32, vb = qk_norm(z2, lp['gq'], lp['gk'], tm, DH_B ** -0.5 * math.log2(math.e))
        r3 = lambda t: t.reshape(B, L, W_B)
        yb = diff_attention_prompt(r3(qb), r3(kb), r3(vb), bias_p, lp['lam_rows'], lp['subln'],
                                   lp['lam_init'], t_attn)
        k_rows.append(k32.reshape(B, L, H_B, 2 * DH_B))
        v_rows.append(v32.reshape(B, L, H_B, DV_B))
        vgate = None if l == 0 else (lp['v0'], lp['w_v_down'], lp['w_v_up'], v_first)
        shift0 = jnp.zeros((B, 1, RWKV_PAD), F32)
        r, lw, k, v, a, b, gg = rwkv_prep(z3, shift0, lp['mu'], lp['w0'], lp['a0'], lp['k_k'], lp['k_a'],
                                          lp['w_lr'], vgate, min(256, L), carry_shift=True)
        if l == 0:
            v_first = v
        y, s_pair = rwkv_chunk_scan(r, lw, k, v, a, b, min(256, L), B)
        shift_out.append(z3[:, L - 1:, Z_RWKV:Z_RWKV + RWKV_COLS])
        f2 = lambda t: t.reshape(T, -1)
        x2 = merge_out(x2, f2(ya), f2(yb), f2(y), f2(r), f2(k), f2(v), f2(gg), z2, gt1, lp['lnx_g'],
                       lp['lnx_b'], lp['r_k'], lp['w_up_a'], lp['w_up_b'], lp['w_up_c'], lp['w_out'],
                       min(512, L))
        x2 = ffn(x2, lp['norm2'], sc2, sh2, gt2, lp['w_ff1'], lp['w_ff2'], tm, 1024)
        ret_out.append(s_ret)
        sp = s_pair.reshape(B, H_C // 2, 2, DH_C, 2, DH_C)
        rwkv_out.append(jnp.stack([sp[:, :, 0, :, 0, :], sp[:, :, 1, :, 1, :]], axis=2)
                        .reshape(B, H_C, DH_C, DH_C))
    return (x2.reshape(B, L, D), jnp.stack(k_rows), jnp.stack(v_rows), jnp.stack(ret_out),
            jnp.stack(rwkv_out), jnp.stack(shift_out))


def _trunk_decode(x, c, layers, bias_d, state_ret, state_rwkv, state_shift, cache_k, cache_v, page_table):
    B, _, D = x.shape
    n_pages = page_table.shape[1]
    past = n_pages * PAGE_SIZE
    x2 = x.reshape(B, D)
    pt_flat = page_table.reshape(-1)
    ck = cache_k.reshape(cache_k.shape[0], cache_k.shape[1], PAGE_SIZE * H_B, 2 * DH_B)
    cv = cache_v.reshape(cache_v.shape[0], cache_v.shape[1], PAGE_SIZE * H_B, DV_B)
    state_t = jnp.transpose(state_rwkv, (0, 2, 3, 4, 1))
    k_rows, v_rows, ret_out, rwkv_out, shift_out = [], [], [], [], []
    v_first = None
    for l, lp in enumerate(layers):
        sh1, sc1, gt1, sh2, sc2, gt2 = _modulation(c, lp, per_token=True)
        z2 = in_proj(x2, lp['norm1'], sc1, sh1, lp['w_in'], l, B, 512)
        ZC = z2.shape[1]
        ya, s_ret = retention_step(z2, state_ret, l, past)
        qb, k32, kb, v32, vb = qk_norm(z2, lp['gq'], lp['gk'], B, DH_B ** -0.5)
        h3 = lambda t: t.reshape(B, H_B, DV_B)
        yb = diff_attention_decode(qb.reshape(B, 1, W_BQK), h3(k32), h3(v32), ck, cv, pt_flat, bias_d,
                                   lp['lam_rows'], lp['subln'], l, lp['lam_init'], n_pages).reshape(B, W_B)
        k_rows.append(k32.reshape(B, 1, H_B, 2 * DH_B))
        v_rows.append(v32.reshape(B, 1, H_B, DV_B))
        vgate = None if l == 0 else (lp['v0'], lp['w_v_down'], lp['w_v_up'], v_first)
        prev = jnp.pad(state_shift[l].reshape(1, B, RWKV_COLS), ((0, 0), (0, 0), (0, RWKV_PAD - RWKV_COLS)))
        r, lw, k, v, a, b, gg = rwkv_prep(z2.reshape(1, B, ZC), prev, lp['mu'], lp['w0'], lp['a0'], lp['k_k'],
                                          lp['k_a'], lp['w_lr'], vgate, B, carry_shift=False)
        if l == 0:
            v_first = v
        f2 = lambda t: t.reshape(B, -1)
        y_dec, s_rwkv = rwkv_step(state_t, l, f2(lw), f2(a), f2(b), f2(k), f2(r), f2(v))
        shift_out.append(z2[:, Z_RWKV:Z_RWKV + RWKV_COLS].reshape(B, 1, RWKV_COLS))
        x2 = merge_out(x2, ya, yb, y_dec, f2(r), f2(k), f2(v), f2(gg), z2, gt1, lp['lnx_g'], lp['lnx_b'],
                       lp['r_k'], lp['w_up_a'], lp['w_up_b'], lp['w_up_c'], lp['w_out'], B)
        x2 = ffn(x2, lp['norm2'], sc2, sh2, gt2, lp['w_ff1'], lp['w_ff2'], B, 1024)
        ret_out.append(s_ret)
        rwkv_out.append(s_rwkv)
    rwkv_state = jnp.transpose(jnp.stack(rwkv_out), (0, 4, 1, 2, 3))
    return (x2.reshape(B, 1, D), jnp.stack(k_rows), jnp.stack(v_rows), jnp.stack(ret_out),
            rwkv_state, jnp.stack(shift_out))


def _decode_buckets(n_pages):
    past = n_pages * PAGE_SIZE
    key = np.arange((n_pages + 1) * PAGE_SIZE)
    bk = np.where(key <= past, _bucket_np(np.maximum(past - key, 0)), -1)
    return np.broadcast_to(bk.reshape(n_pages + 1, 1, PAGE_SIZE), (n_pages + 1, 8, PAGE_SIZE)).astype(np.int32)


def kernel(x_prompt, x_sample, c_prompt, c_sample, cache_k_diff, cache_v_diff, page_table, state_ret, state_rwkv, state_shift, rel_bias, w_ada, b_ada, norm1, norm2, w_in, qk_norm_q, qk_norm_k, lambda_q1, lambda_k1, lambda_q2, lambda_k2, subln_diff, mu_shift, w0, w_decay_up, a0, w_a_up, w_g_up, v0, w_v_down, w_v_up, k_k, k_a, r_k, lnx_g, lnx_b, w_up_a, w_up_b, w_up_c, w_out, w_ff1, w_ff2):
    p = dict(w_ada=w_ada, b_ada=b_ada, norm1=norm1, norm2=norm2, w_in=w_in,
             qk_norm_q=qk_norm_q, qk_norm_k=qk_norm_k, lambda_q1=lambda_q1, lambda_k1=lambda_k1,
             lambda_q2=lambda_q2, lambda_k2=lambda_k2, subln_diff=subln_diff, mu_shift=mu_shift,
             w0=w0, w_decay_up=w_decay_up, a0=a0, w_a_up=w_a_up, w_g_up=w_g_up, v0=v0,
             w_v_down=w_v_down, w_v_up=w_v_up, k_k=k_k, k_a=k_a, r_k=r_k, lnx_g=lnx_g, lnx_b=lnx_b,
             w_up_a=w_up_a, w_up_b=w_up_b, w_up_c=w_up_c, w_out=w_out, w_ff1=w_ff1, w_ff2=w_ff2)
    depth = w_in.shape[0]
    w_in_p = prep_w_in(w_in)
    layers = [_prep_layer(p, l, w_in_p) for l in range(depth)]
    L = x_prompt.shape[1]
    n_pages = page_table.shape[1]
    t_attn = min(512, L)
    bias_p = bias_tiles(rel_bias, _prompt_buckets(t_attn), True)
    bd = bias_tiles(rel_bias, _decode_buckets(n_pages).reshape((n_pages + 1) * 8, PAGE_SIZE), False)
    bias_d = bd.reshape(H_B, n_pages + 1, 8, PAGE_SIZE)

    y_p, k_p, v_p, ret_p, rwkv_p, shift_p = _trunk_prompt(x_prompt, c_prompt, layers, bias_p)
    y_s, k_s, v_s, ret_s, rwkv_s, shift_s = _trunk_decode(
        x_sample, c_sample, layers, bias_d, state_ret, state_rwkv, state_shift,
        cache_k_diff, cache_v_diff, page_table)
    return (y_p, y_s, k_p, v_p, k_s, v_s, ret_p, ret_s, rwkv_p, rwkv_s, shift_p, shift_s)
```

```python
import functools
import math

import numpy as np
import jax
import jax.numpy as jnp
from jax import lax
from jax.experimental import pallas as pl
from jax.experimental.pallas import tpu as pltpu

F32 = jnp.float32
BF16 = jnp.bfloat16

H_A, DK_A, DV_A = 4, 128, 128
RET_CHUNK = 128
ROPE_BASE = 10000.0
H_B, DH_B, DV_B = 4, 64, 128
N_BUCKETS, MAX_DISTANCE = 32, 128
H_C, DH_C = 8, 64
R_DECAY, R_A, R_V, R_G = 64, 64, 32, 160
LNX_EPS = 64e-5
EPS = 1e-6
PAGE_SIZE = 128
W_AQ, W_A = H_A * DK_A, H_A * DV_A
W_BQK, W_B = H_B * 2 * DH_B, H_B * DV_B
W_C = H_C * DH_C
RET_COLS = 2 * W_AQ + 2 * W_A
DIFF_COLS = 2 * W_BQK + W_B
RWKV_COLS = 3 * W_C + R_DECAY + R_A + R_G
RWKV_PAD = 2048
LR_COLS = RWKV_PAD - 3 * W_C

Z_RET, Z_RWKV = 0, RET_COLS
Z_GATE = Z_RWKV + RWKV_PAD
NEG = -1e30
RW_CHUNK = 64
VMEM_LIMIT = 56 * 1024 * 1024


def _cp(*sem):
    return pltpu.CompilerParams(dimension_semantics=sem, vmem_limit_bytes=VMEM_LIMIT)


def _silu(x):
    return x * jax.nn.sigmoid(x)


def _split2(x):
    hi = x.astype(BF16)
    lo = (x - hi.astype(F32)).astype(BF16)
    return hi, lo


def _seg_sum(x, ones2):
    hi, lo = _split2(x)
    return jnp.dot(jnp.concatenate([hi, lo], axis=1), ones2, preferred_element_type=F32)


def _seg_sum_wide(x, ones2):
    return jnp.concatenate(
        [_seg_sum(x[:, c * 128:(c + 1) * 128], ones2) for c in range(x.shape[1] // 128)], axis=1)


def _mod_kernel(c_ref, w_ref, b_ref, o_ref):
    s = _silu(c_ref[...])
    o_ref[...] = jnp.dot(s.astype(BF16), w_ref[...].astype(BF16), preferred_element_type=F32) + b_ref[...]


def ada_mod(c_pad, w_stack, b_row, layer):
    R, D = c_pad.shape
    N = w_stack.shape[2]
    tn = 1536
    return pl.pallas_call(
        _mod_kernel,
        grid=(N // tn,),
        in_specs=[pl.BlockSpec((R, D), lambda n: (0, 0)),
                  pl.BlockSpec((None, D, tn), lambda n: (layer, 0, n)),
                  pl.BlockSpec((1, tn), lambda n: (0, n))],
        out_specs=pl.BlockSpec((R, tn), lambda n: (0, n)),
        out_shape=jax.ShapeDtypeStruct((R, N), F32),
        compiler_params=_cp("parallel"),
        name="ada_mod",
    )(c_pad, w_stack, b_row)


def _inproj_kernel(x_ref, g_ref, sc_ref, sh_ref, w_ref, z_ref, h_scr):
    @pl.when(pl.program_id(1) == 0)
    def _():
        x = x_ref[...]
        y = x * lax.rsqrt(jnp.mean(x * x, axis=-1, keepdims=True) + EPS) * g_ref[...]
        h_scr[...] = (y * (1.0 + sc_ref[...]) + sh_ref[...]).astype(BF16)

    z_ref[...] = lax.dot_general(h_scr[...], w_ref[...], (((1,), (1,)), ((), ())),
                                 preferred_element_type=F32)


def in_proj(x2, gain_row, sc, sh, w_bf, layer, tm, tn):
    T, D = x2.shape
    N = w_bf.shape[1]
    G, R, _ = sc.shape
    tiles_per_group = (T // tm) // G
    mod_spec = pl.BlockSpec((None, R, D), lambda m, n: (m // tiles_per_group, 0, 0))
    return pl.pallas_call(
        _inproj_kernel,
        grid=(T // tm, N // tn),
        in_specs=[pl.BlockSpec((tm, D), lambda m, n: (m, 0)),
                  pl.BlockSpec((1, D), lambda m, n: (0, 0)),
                  mod_spec, mod_spec,
                  pl.BlockSpec((None, tn, D), lambda m, n: (layer, n, 0))],
        out_specs=pl.BlockSpec((tm, tn), lambda m, n: (m, n)),
        out_shape=jax.ShapeDtypeStruct((T, N), F32),
        scratch_shapes=[pltpu.VMEM((tm, D), BF16)],
        compiler_params=_cp("parallel", "arbitrary"),
        name="in_proj",
    )(x2, gain_row, sc, sh, w_bf)


def _rope(x, cos, sin_signed):
    return x * cos + pltpu.roll(x, DK_A // 2, 1) * sin_signed


def _ret_kernel(q_ref, k_ref, v_ref, g_ref, cos_ref, sin_ref, inner_ref, cross_ref, tail_ref, cd_ref,
                ya_ref, so_ref, s_scr, *, n_sub):
    lt = pl.program_id(1)

    @pl.when(lt == 0)
    def _():
        s_scr[...] = jnp.zeros_like(s_scr)

    C = RET_CHUNK
    heads = range(H_A)
    cols = [slice(h * DK_A, (h + 1) * DK_A) for h in heads]
    dot = lambda x, y: jnp.dot(x, y, preferred_element_type=F32)
    for c in range(n_sub):
        rows = slice(c * C, (c + 1) * C)
        cos = cos_ref[rows, :]
        sin = sin_ref[rows, :]
        qb, kb, vb, kt = [], [], [], []
        for h in heads:
            q = _rope(q_ref[rows, cols[h]], cos, sin)
            k = _rope(k_ref[rows, cols[h]], cos, sin) * (DK_A ** -0.5)
            qb.append(q.astype(BF16))
            kb.append(k.astype(BF16))
            vb.append(v_ref[rows, cols[h]].astype(BF16))
            kt.append((k * tail_ref[h]).T.astype(BF16))
        S = [s_scr[h] for h in heads]
        sc = [lax.dot_general(qb[h], kb[h], (((1,), (1,)), ((), ())), preferred_element_type=F32) * inner_ref[h]
              for h in heads]
        qs = [dot(qb[h], S[h].astype(BF16)) * cross_ref[h] for h in heads]
        kv = [dot(kt[h], vb[h]) for h in heads]
        o = [dot(sc[h].astype(BF16), vb[h]) + qs[h] for h in heads]
        for h in heads:
            s_scr[h] = S[h] * cd_ref[h] + kv[h]
            on = o[h] * lax.rsqrt(jnp.mean(o[h] * o[h], axis=-1, keepdims=True) + EPS)
            ya_ref[rows, cols[h]] = (on * _silu(g_ref[rows, cols[h]])).astype(ya_ref.dtype)

    @pl.when(lt == pl.num_programs(1) - 1)
    def _():
        so_ref[...] = s_scr[...]


def _ret_tables(L):
    C = math.gcd(L, RET_CHUNK)
    log_g = np.log1p(-np.exp2(-5.0 - np.arange(H_A, dtype=np.float32))).astype(np.float32)
    i = np.arange(C, dtype=np.float32)
    dist = i[:, None] - i[None, :]
    causal = dist >= 0
    inner = np.where(causal[None], np.exp(np.where(causal, dist, 0.0)[None] * log_g[:, None, None]), 0.0)
    cross = np.exp((i[None, :] + 1.0) * log_g[:, None])
    tail = np.exp((C - 1.0 - i)[None, :] * log_g[:, None])
    chunk = np.exp(C * log_g)
    bc = lambda t: np.broadcast_to(t[:, :, None], (H_A, C, 128)).astype(np.float32)
    cd = np.broadcast_to(chunk[:, None, None], (H_A, 1, 128)).astype(np.float32)
    return inner.astype(np.float32), bc(cross), bc(tail), cd


def _rope_tables(pos):
    half = DK_A // 2
    inv = ROPE_BASE ** (-jnp.arange(half, dtype=F32) / half)
    ang = pos.astype(F32)[:, None] * inv[None, :]
    cos, sin = jnp.cos(ang), jnp.sin(ang)
    return jnp.concatenate([cos, cos], axis=-1), jnp.concatenate([-sin, sin], axis=-1)


def retention_prompt(z3, cos_t, sin_t, tb):
    B, L, ZC = z3.shape
    inner, cross, tail, cd = _ret_tables(L)
    zspec = lambda idx: pl.BlockSpec((None, tb, W_AQ), lambda b, l: (b, l, idx))
    full3 = lambda s: pl.BlockSpec(s, lambda b, l: (0, 0, 0))
    return pl.pallas_call(
        functools.partial(_ret_kernel, n_sub=tb // RET_CHUNK),
        grid=(B, L // tb),
        in_specs=[zspec(0), zspec(1), zspec(2), zspec(3),
                  pl.BlockSpec((tb, 128), lambda b, l: (l, 0)),
                  pl.BlockSpec((tb, 128), lambda b, l: (l, 0)),
                  full3(inner.shape), full3(cross.shape), full3(tail.shape), full3(cd.shape)],
        out_specs=[pl.BlockSpec((None, tb, W_A), lambda b, l: (b, l, 0)),
                   pl.BlockSpec((None, H_A, DK_A, DV_A), lambda b, l: (b, 0, 0, 0))],
        out_shape=[jax.ShapeDtypeStruct((B, L, W_A), BF16),
                   jax.ShapeDtypeStruct((B, H_A, DK_A, DV_A), F32)],
        scratch_shapes=[pltpu.VMEM((H_A, DK_A, DV_A), F32)],
        compiler_params=_cp("parallel", "arbitrary"),
        name="retention_prompt",
    )(z3, z3, z3, z3, cos_t, sin_t, inner, cross, tail, cd)


def _ret_step_kernel(q_ref, k_ref, v_ref, g_ref, cos_ref, sin_ref, gam_ref, s_ref, ya_ref, so_ref, *, bb):
    cos, sin = cos_ref[...], sin_ref[...]
    row = lax.broadcasted_iota(jnp.int32, (bb, 128), 0)
    for h in range(H_A):
        cols = slice(h * DK_A, (h + 1) * DK_A)
        gam = gam_ref[h]
        q = _rope(q_ref[:, cols], cos, sin)
        k = _rope(k_ref[:, cols], cos, sin) * (DK_A ** -0.5)
        v = v_ref[:, cols]
        qb = q.astype(BF16)
        qk = jnp.sum(qb.astype(F32) * k.astype(BF16).astype(F32), axis=-1, keepdims=True)
        qs = jnp.zeros((bb, 128), F32)
        for b in range(bb):
            S = s_ref[b, h]
            onehot = row == b
            qs = jnp.where(onehot, jnp.dot(qb, S.astype(BF16), preferred_element_type=F32), qs)
            kb_t = jnp.where(onehot, k, 0.0).T.astype(BF16)
            so_ref[b, h] = S * gam + jnp.dot(kb_t, v.astype(BF16), preferred_element_type=F32)
        o = qk * v.astype(BF16).astype(F32) + qs * gam
        on = o * lax.rsqrt(jnp.mean(o * o, axis=-1, keepdims=True) + EPS)
        ya_ref[:, cols] = on * _silu(g_ref[:, cols])


def retention_step(z2, state, layer, pos):
    B = z2.shape[0]
    bb = 8
    cos_t, sin_t = _rope_tables(jnp.full((1,), pos))
    log_g = np.log1p(-np.exp2(-5.0 - np.arange(H_A, dtype=np.float32))).astype(np.float32)
    gam = np.broadcast_to(np.exp(log_g)[:, None, None], (H_A, 1, 128)).astype(np.float32)
    zspec = lambda idx: pl.BlockSpec((bb, W_AQ), lambda i: (i, idx))
    return pl.pallas_call(
        functools.partial(_ret_step_kernel, bb=bb),
        grid=(B // bb,),
        in_specs=[zspec(0), zspec(1), zspec(2), zspec(3),
                  pl.BlockSpec((1, 128), lambda i: (0, 0)),
                  pl.BlockSpec((1, 128), lambda i: (0, 0)),
                  pl.BlockSpec((H_A, 1, 128), lambda i: (0, 0, 0)),
                  pl.BlockSpec((None, bb, H_A, DK_A, DV_A), lambda i: (layer, i, 0, 0, 0))],
        out_specs=[pl.BlockSpec((bb, W_A), lambda i: (i, 0)),
                   pl.BlockSpec((bb, H_A, DK_A, DV_A), lambda i: (i, 0, 0, 0))],
        out_shape=[jax.ShapeDtypeStruct((B, W_A), F32),
                   jax.ShapeDtypeStruct((B, H_A, DK_A, DV_A), F32)],
        compiler_params=_cp("parallel"),
        name="retention_step",
    )(z2, z2, z2, z2, cos_t, sin_t, gam, state)


def _qknorm_kernel(q_ref, k_ref, v_ref, gq_ref, gk_ref, ones_ref, qb_ref, k32_ref, kb_ref, v32_ref, vb_ref,
                   *, q_scale):
    ones2 = ones_ref[...]
    q, k, v = q_ref[...], k_ref[...], v_ref[...]
    tm = q.shape[0]
    qn = q * lax.rsqrt(_seg_sum_wide(q * q, ones2) * (1.0 / DH_B) + EPS) * gq_ref[...] * q_scale
    kn = k * lax.rsqrt(_seg_sum_wide(k * k, ones2) * (1.0 / DH_B) + EPS) * gk_ref[...]
    qb_ref[...] = qn.astype(BF16)
    kb_ref[...] = kn.astype(BF16)
    vb_ref[...] = v.astype(BF16)
    for h in range(H_B):
        head_rows = pl.ds(h, tm, stride=H_B)
        k32_ref[head_rows, :] = kn[:, h * DV_B:(h + 1) * DV_B]
        v32_ref[head_rows, :] = v[:, h * DV_B:(h + 1) * DV_B]


def _block_ones2(group):
    i = np.arange(128)
    m = (i[:, None] // group == i[None, :] // group).astype(np.float32)
    return jnp.asarray(np.concatenate([m, m], axis=0), BF16)


def qk_norm(z2, gq_row, gk_row, tm, q_scale):
    T = z2.shape[0]
    base = (Z_GATE + 3 * 1024) // W_BQK
    zspec = lambda idx: pl.BlockSpec((tm, W_BQK), lambda m: (m, base + idx))
    row = pl.BlockSpec((1, W_BQK), lambda m: (0, 0))
    out = pl.BlockSpec((tm, W_BQK), lambda m: (m, 0))
    out_rows = pl.BlockSpec((tm * H_B, DV_B), lambda m: (m, 0))
    return pl.pallas_call(
        functools.partial(_qknorm_kernel, q_scale=q_scale),
        grid=(T // tm,),
        in_specs=[zspec(0), zspec(1), zspec(2), row, row, pl.BlockSpec((256, 128), lambda m: (0, 0))],
        out_specs=[out, out_rows, out, out_rows, out],
        out_shape=[jax.ShapeDtypeStruct((T, W_BQK), BF16), jax.ShapeDtypeStruct((T * H_B, DV_B), F32),
                   jax.ShapeDtypeStruct((T, W_BQK), BF16), jax.ShapeDtypeStruct((T * H_B, DV_B), F32),
                   jax.ShapeDtypeStruct((T, W_B), BF16)],
        compiler_params=_cp("parallel"),
        name="qk_norm",
    )(z2, z2, z2, gq_row, gk_row, _block_ones2(DH_B))


def _bucket_np(n):
    max_exact = N_BUCKETS // 2
    nf = np.maximum(n, 1).astype(np.float32)
    large = max_exact + (np.log(nf / np.float32(max_exact)) / np.float32(math.log(MAX_DISTANCE / max_exact))
                         * np.float32(N_BUCKETS - max_exact)).astype(np.int32)
    large = np.minimum(large, N_BUCKETS - 1)
    return np.where(n < max_exact, n, large).astype(np.int32)


def _bias_kernel(tab_ref, bkt_ref, o_ref, *, log2_far_shift):
    h = pl.program_id(0)
    bk = bkt_ref[...]
    acc = jnp.zeros(bk.shape, F32)
    for b in range(N_BUCKETS):
        acc = jnp.where(bk == b, tab_ref[b, h], acc)
    if log2_far_shift:
        acc = (acc - tab_ref[N_BUCKETS - 1, h]) * math.log2(math.e)
    o_ref[...] = jnp.where(bk < 0, NEG, acc)


def bias_tiles(rel_bias, buckets, log2_far_shift):
    R, C = buckets.shape
    tr = min(R, 512)
    return pl.pallas_call(
        functools.partial(_bias_kernel, log2_far_shift=log2_far_shift),
        grid=(H_B, R // tr),
        in_specs=[pl.BlockSpec(memory_space=pltpu.SMEM),
                  pl.BlockSpec((tr, C), lambda h, r: (r, 0))],
        out_specs=pl.BlockSpec((None, tr, C), lambda h, r: (h, r, 0)),
        out_shape=jax.ShapeDtypeStruct((H_B, R, C), F32),
        compiler_params=_cp("parallel", "parallel"),
        name="bias_tiles",
    )(rel_bias, jnp.asarray(buckets))


def _prompt_buckets(t):
    r = np.arange(t)[:, None]
    c = np.arange(t)[None, :]
    diag = np.where(c <= r, _bucket_np(np.maximum(r - c, 0)), -1)
    off1 = _bucket_np(t + r - c)
    assert t >= MAX_DISTANCE
    return np.concatenate([diag, off1], axis=0).astype(np.int32)


def _lambda_full(lam_ref, lam_init):
    lv = lam_ref[...]
    s1 = jnp.sum(lv[0:1] * lv[1:2], axis=-1, keepdims=True)
    s2 = jnp.sum(lv[2:3] * lv[3:4], axis=-1, keepdims=True)
    return jnp.exp(s1) - jnp.exp(s2) + lam_init


def _flash_kernel(q_ref, k_ref, v_ref, bias_ref, lam_ref, gain_ref, o_ref, qq_scr, m_scr, acc_scr, s_scr,
                  *, lam_init):
    i = pl.program_id(2)
    t = q_ref.shape[0]

    q = q_ref[...]
    first = lax.broadcasted_iota(jnp.int32, q.shape, 1) < DH_B
    qq_scr[:t] = jnp.where(first, q, jnp.zeros_like(q))
    qq_scr[t:] = jnp.where(first, jnp.zeros_like(q), q)
    m_scr[...] = jnp.full(m_scr.shape, NEG, F32)
    acc_scr[...] = jnp.zeros_like(acc_scr)

    def raw_scores(comp, j):
        krows = pl.ds(pl.multiple_of(j * t, t), t)
        return lax.dot_general(qq_scr[comp * t:(comp + 1) * t], k_ref[krows, :], (((1,), (1,)), ((), ())),
                               preferred_element_type=F32)

    def accumulate(comp, s, v1):
        rows = slice(comp * t, (comp + 1) * t)
        m_old = m_scr[rows]
        m_new = jnp.maximum(m_old, jnp.max(s, axis=-1, keepdims=True))
        alpha = jnp.exp2(m_old - m_new)
        p = jnp.concatenate([jnp.exp2(s[:, c * 128:(c + 1) * 128] - m_new) for c in range(t // 128)],
                            axis=1).astype(BF16)
        acc_scr[rows] = (jnp.concatenate([alpha, alpha], axis=1) * acc_scr[rows]
                         + jnp.dot(p, v1, preferred_element_type=F32))
        m_scr[rows] = m_new

    def kv_block(j, tile, has_next):
        krows = pl.ds(pl.multiple_of(j * t, t), t)
        v1 = jnp.concatenate([v_ref[krows, :], jnp.ones((t, DV_B), BF16)], axis=1)
        with_bias = (lambda s: s) if tile is None else (lambda s: s + bias_ref[tile * t:(tile + 1) * t, :])
        s_first = s_scr[...]
        s_second = raw_scores(1, j)
        accumulate(0, with_bias(s_first), v1)
        if has_next:
            s_scr[...] = raw_scores(0, j + 1)
        accumulate(1, with_bias(s_second), v1)

    s_scr[...] = raw_scores(0, 0)

    def far_pair(jj, carry):
        kv_block(2 * jj, None, True)
        kv_block(2 * jj + 1, None, True)
        return carry

    n_far = jnp.maximum(i - 1, 0)
    lax.fori_loop(0, n_far // 2, far_pair, 0)

    @pl.when(jnp.logical_and(i >= 1, n_far % 2 == 1))
    def _():
        kv_block(i - 2, None, True)
        kv_block(i - 1, 1, True)
        kv_block(i, 0, False)

    @pl.when(jnp.logical_and(i >= 1, n_far % 2 == 0))
    def _():
        kv_block(i - 1, 1, True)
        kv_block(i, 0, False)

    @pl.when(i == 0)
    def _():
        kv_block(i, 0, False)

    lam = _lambda_full(lam_ref, lam_init)
    acc = acc_scr[...]
    ob = acc[:t, :DV_B] / acc[:t, DV_B:] - lam * (acc[t:, :DV_B] / acc[t:, DV_B:])
    on = ob * lax.rsqrt(jnp.mean(ob * ob, axis=-1, keepdims=True) + EPS) * gain_ref[...]
    o_ref[...] = (on * (1.0 - lam_init)).astype(o_ref.dtype)


def diff_attention_prompt(qb, kb, vb, bias, lam_rows, gain_row, lam_init, t):
    B, L, _ = qb.shape
    n = L // t
    kv_spec = pl.BlockSpec((None, L, DV_B), lambda b, h, i: (b, 0, h))
    return pl.pallas_call(
        functools.partial(_flash_kernel, lam_init=lam_init),
        grid=(B, H_B, n),
        in_specs=[pl.BlockSpec((None, t, DV_B), lambda b, h, i: (b, i, h)),
                  kv_spec, kv_spec,
                  pl.BlockSpec((None, 2 * t, t), lambda b, h, i: (h, 0, 0)),
                  pl.BlockSpec((4, DH_B), lambda b, h, i: (0, 0)),
                  pl.BlockSpec((1, DV_B), lambda b, h, i: (0, 0))],
        out_specs=pl.BlockSpec((None, t, DV_B), lambda b, h, i: (b, i, h)),
        out_shape=jax.ShapeDtypeStruct((B, L, W_B), BF16),
        scratch_shapes=[pltpu.VMEM((2 * t, DV_B), BF16), pltpu.VMEM((2 * t, 128), F32),
                        pltpu.VMEM((2 * t, 2 * DV_B), F32), pltpu.VMEM((t, t), F32)],
        compiler_params=_cp("parallel", "parallel", "arbitrary"),
        name="diff_attention_prompt",
    )(qb, kb, vb, bias, lam_rows, gain_row)


def _decode_kernel(pt_ref, q_ref, kn_ref, vn_ref, bias_ref, lam_ref, gain_ref, *rest, n_pages, lam_init):
    k_refs, v_refs, o_ref = rest[:n_pages], rest[n_pages:2 * n_pages], rest[2 * n_pages]
    lam = _lambda_full(lam_ref, lam_init)
    gain = gain_ref[...]
    r8 = lax.broadcasted_iota(jnp.int32, (8, 2 * DH_B), 0)
    l8 = lax.broadcasted_iota(jnp.int32, (8, 2 * DH_B), 1)
    comp_rows = (l8 // DH_B) == r8
    row0 = lax.broadcasted_iota(jnp.int32, (PAGE_SIZE, DV_B), 0) == 0
    nt = lambda x, y: lax.dot_general(x, y, (((1,), (1,)), ((), ())), preferred_element_type=F32)
    heads = range(H_B)
    cols = [slice(h * DV_B, (h + 1) * DV_B) for h in heads]
    head_rows = [pl.ds(h, PAGE_SIZE, stride=H_B) for h in heads]
    ss = []
    for h in heads:
        qrows = jnp.where(comp_rows, q_ref[:, cols[h]].astype(F32), 0.0).astype(BF16)
        ks = [r[head_rows[h], :].astype(BF16) for r in k_refs]
        ks.append(jnp.where(row0, kn_ref[h:h + 1, :], 0.0).astype(BF16))
        ss.append([nt(qrows, kp) + bias_ref[h, p] for p, kp in enumerate(ks)])
    ps, ls = [], []
    for h in heads:
        m = functools.reduce(jnp.maximum, [jnp.max(s, axis=-1, keepdims=True) for s in ss[h]])
        p = [jnp.exp(s - m) for s in ss[h]]
        ls.append(functools.reduce(jnp.add, [jnp.sum(x, axis=-1, keepdims=True) for x in p]))
        ps.append([x.astype(BF16) for x in p])
    for h in heads:
        vs = [r[head_rows[h], :].astype(BF16) for r in v_refs]
        vs.append(jnp.where(row0, vn_ref[h:h + 1, :], 0.0).astype(BF16))
        acc = functools.reduce(jnp.add, [jnp.dot(p, vp, preferred_element_type=F32)
                                         for p, vp in zip(ps[h], vs)])
        outn = acc / ls[h]
        oh = outn[0:1] - lam * outn[1:2]
        on = oh * lax.rsqrt(jnp.mean(oh * oh, axis=-1, keepdims=True) + EPS) * gain
        o_ref[:, cols[h]] = on * (1.0 - lam_init)


def diff_attention_decode(qb, k32, v32, cache_k, cache_v, pt_flat, bias_dec, lam_rows, gain_row, layer,
                          lam_init, n_pages):
    B = qb.shape[0]
    rowspec = pl.BlockSpec((None, 1, W_BQK), lambda b, pt: (b, 0, 0))
    headspec = pl.BlockSpec((None, H_B, DV_B), lambda b, pt: (b, 0, 0))
    const = lambda s: pl.BlockSpec(s, lambda b, pt: tuple(0 for _ in s))
    page_spec = lambda p: pl.BlockSpec((None, None, PAGE_SIZE * H_B, DV_B),
                                       lambda b, pt, p=p: (layer, pt[b * n_pages + p], 0, 0))
    grid_spec = pltpu.PrefetchScalarGridSpec(
        num_scalar_prefetch=1,
        grid=(B,),
        in_specs=[rowspec, headspec, headspec, const(bias_dec.shape), const((4, DH_B)), const((1, DV_B))]
                 + [page_spec(p) for p in range(n_pages)] * 2,
        out_specs=pl.BlockSpec((None, 1, W_B), lambda b, pt: (b, 0, 0)),
    )
    return pl.pallas_call(
        functools.partial(_decode_kernel, n_pages=n_pages, lam_init=lam_init),
        grid_spec=grid_spec,
        out_shape=jax.ShapeDtypeStruct((B, 1, W_B), F32),
        compiler_params=_cp("arbitrary"),
        name="diff_attention_decode",
    )(pt_flat, qb, k32, v32, bias_dec, lam_rows, gain_row,
      *([cache_k] * n_pages), *([cache_v] * n_pages))


def _rwkv_prep_kernel(*refs, carry_shift, gate_v):
    it = iter(refs)
    z_ref, prev_ref, mu_ref, w0_ref, a0_ref, kk_ref, ka_ref, wlr_ref, ones_ref = (next(it) for _ in range(9))
    if gate_v:
        v0_ref, wvd_ref, wvu_ref, vf_ref = (next(it) for _ in range(4))
    r_ref, lw_ref, k_ref, v_ref, a_ref, b_ref, gg_ref = (next(it) for _ in range(7))
    z = z_ref[...]
    if carry_shift:
        carry_scr = next(it)

        @pl.when(pl.program_id(1) == 0)
        def _():
            carry_scr[...] = prev_ref[...]

        first = lax.broadcasted_iota(jnp.int32, z.shape, 0) == 0
        zprev = jnp.where(first, carry_scr[...], pltpu.roll(z, 1, 0))
        carry_scr[...] = z_ref[z.shape[0] - 1:z.shape[0], :]
    else:
        zprev = prev_ref[...]
    zs = z + (zprev - z) * mu_ref[...]
    rc, kc, vc = zs[:, :W_C], zs[:, W_C:2 * W_C], zs[:, 2 * W_C:3 * W_C]
    lr = zs[:, 3 * W_C:]
    col = lax.broadcasted_iota(jnp.int32, lr.shape, 1)
    act = jnp.where(col < R_DECAY, jnp.tanh(lr), jnp.where(col < R_DECAY + R_A, lr, jax.nn.sigmoid(lr)))
    up = jnp.dot(act.astype(BF16), wlr_ref[...], preferred_element_type=F32)
    y = -(w0_ref[...] + up[:, :W_C])
    softplus = jnp.maximum(y, 0.0) + jnp.log(1.0 + jnp.exp(-jnp.abs(y)))
    lw_ref[...] = -jnp.exp(-softplus - 0.5)
    if gate_v:
        down = jnp.dot(vc.astype(BF16), wvd_ref[...], preferred_element_type=F32)
        vgate = jax.nn.sigmoid(v0_ref[...] + jnp.dot(down.astype(BF16), wvu_ref[...], preferred_element_type=F32))
        vc = vc + (vf_ref[...].astype(F32) - vc) * vgate
    a = jax.nn.sigmoid(a0_ref[...] + up[:, W_C:2 * W_C])
    gg_ref[...] = up[:, 2 * W_C:].astype(gg_ref.dtype)
    kk = kc * kk_ref[...]
    norm = jnp.sqrt(_seg_sum_wide(kk * kk, ones_ref[...]))
    kk = kk / jnp.maximum(norm, 1e-12)
    r_ref[...] = rc.astype(r_ref.dtype)
    k_ref[...] = (kc * (1.0 + (a - 1.0) * ka_ref[...])).astype(k_ref.dtype)
    v_ref[...] = vc.astype(v_ref.dtype)
    a_ref[...] = (-kk).astype(a_ref.dtype)
    b_ref[...] = (kk * a).astype(b_ref.dtype)


def rwkv_prep(z3, prev, mu_row, w0, a0, k_k, k_a, w_lr, vgate, tm, carry_shift):
    G, Lg, _ = z3.shape
    blk = lambda w, idx: pl.BlockSpec((None, tm, w), lambda g, l: (g, l, idx))
    row = lambda w: pl.BlockSpec((1, w), lambda g, l: (0, 0))
    prev_spec = (pl.BlockSpec((None, 1, RWKV_PAD), lambda g, l: (g, 0, 0)) if carry_shift
                 else blk(RWKV_PAD, 0))
    in_specs = [blk(RWKV_PAD, Z_RWKV // RWKV_PAD), prev_spec, row(RWKV_PAD), row(W_C), row(W_C), row(W_C),
                row(W_C), pl.BlockSpec(w_lr.shape, lambda g, l: (0, 0)),
                pl.BlockSpec((256, 128), lambda g, l: (0, 0))]
    args = [z3, prev, mu_row, w0, a0, k_k, k_a, w_lr, _block_ones2(DH_C)]
    if vgate is not None:
        v0, wvd, wvu, vfirst = vgate
        in_specs += [row(W_C), pl.BlockSpec(wvd.shape, lambda g, l: (0, 0)),
                     pl.BlockSpec(wvu.shape, lambda g, l: (0, 0)), blk(W_C, 0)]
        args += [v0, wvd, wvu, vfirst]
    return pl.pallas_call(
        functools.partial(_rwkv_prep_kernel, carry_shift=carry_shift, gate_v=vgate is not None),
        grid=(G, Lg // tm),
        in_specs=in_specs,
        out_specs=[blk(W_C, 0)] * 7,
        out_shape=[jax.ShapeDtypeStruct((G, Lg, W_C), F32 if i == 1 else BF16) for i in range(7)],
        scratch_shapes=[pltpu.VMEM((1, RWKV_PAD), F32)] if carry_shift else [],
        compiler_params=_cp("parallel", "arbitrary"),
        name="rwkv_prep",
    )(*args)


def _rwkv_chunk_kernel(r_ref, lw_ref, k_ref, v_ref, a_ref, b_ref, tri_ref, msl_ref, mli_ref, lvl_ref,
                       y_ref, so_ref, s_scr, *, n_chunks, nb):
    C = RW_CHUNK

    @pl.when(pl.program_id(1) == 0)
    def _():
        s_scr[...] = jnp.zeros_like(s_scr)

    head0 = lax.broadcasted_iota(jnp.int32, (C, 128), 1) < DH_C
    tri3, msl, mli = tri_ref[...], msl_ref[...], mli_ref[...]
    ri = lax.broadcasted_iota(jnp.int32, (128, 128), 0)
    ci = lax.broadcasted_iota(jnp.int32, (128, 128), 1)
    eye = (ri == ci).astype(F32)
    n_levels = lvl_ref.shape[0]
    chains = [(bi, p) for bi in range(nb) for p in range(H_C // 2)]

    def stack(x):
        return jnp.concatenate([jnp.where(head0, x, 0.0), jnp.where(head0, 0.0, x)], axis=0)

    def nt(x, y):
        return lax.dot_general(x, y, (((1,), (1,)), ((), ())), preferred_element_type=F32)

    def dot(x, y):
        return jnp.dot(x, y, preferred_element_type=F32)

    def chunk(ci_, carry):
        rows = pl.ds(pl.multiple_of(ci_ * C, C), C)
        ld = lambda ref, bi, p: ref[bi, rows, p * 128:(p + 1) * 128].astype(F32)
        lams = []
        for bi, p in chains:
            lw = ld(lw_ref, bi, p)
            hi = lw.astype(BF16)
            mid = (lw - hi.astype(F32)).astype(BF16)
            lo = (lw - hi.astype(F32) - mid.astype(F32)).astype(BF16)
            lams.append(dot(tri3, jnp.concatenate([hi, mid, lo], axis=0)))
        AR, BK, Vs, KB, dec = [], [], [], [], []
        for (bi, p), lam in zip(chains, lams):
            r, lw, k = ld(r_ref, bi, p), ld(lw_ref, bi, p), ld(k_ref, bi, p)
            v, a, b = ld(v_ref, bi, p), ld(a_ref, bi, p), ld(b_ref, bi, p)
            lam_c = lam[C - 1:C, :]
            e_neg = jnp.exp(-lam)
            e_tail = jnp.exp(lam_c - lam)
            AR.append(jnp.concatenate([stack(a * jnp.exp(lam - lw)), stack(r * jnp.exp(lam))],
                                      axis=0).astype(BF16))
            BK.append(jnp.concatenate([stack(b * e_neg), stack(k * e_neg)], axis=0).astype(BF16))
            KB.append(jnp.concatenate([stack(k * e_tail), stack(b * e_tail)], axis=0).astype(BF16))
            Vs.append(stack(v))
            dec.append(jnp.exp(lam_c))
        G = [nt(x, y) for x, y in zip(AR, BK)]
        I0 = [nt(x, s_scr[bi, p].astype(BF16)) for x, (bi, p) in zip(AR, chains)]
        rhs = [i0[:128] + dot((g[:128, 128:] * msl).astype(BF16), v.astype(BF16))
               for g, i0, v in zip(G, I0, Vs)]
        Nh, Nl, D = [], [], []
        for g in G:
            h, l = _split2(g[:128, :128] * msl)
            Nh.append(h)
            Nl.append(l)
            D.append(eye + (h * lvl_ref[0]).astype(F32) + (l * lvl_ref[0]).astype(F32))
        for lv in range(1, n_levels):
            m = lvl_ref[lv]
            Db = [d.astype(BF16) for d in D]
            X = [dot(h * m, db) for h, db in zip(Nh, Db)]
            D = [d + dot(db, x.astype(BF16)) for d, db, x in zip(D, Db, X)]
        Us = [dot(d.astype(BF16), x.astype(BF16)) for d, x in zip(D, rhs)]
        for (bi, p), g, i0, u, v in zip(chains, G, I0, Us, Vs):
            ys = i0[128:] + dot(jnp.concatenate([g[128:, :128] * mli, g[128:, 128:] * mli], axis=1).astype(BF16),
                                jnp.concatenate([u, v], axis=0).astype(BF16))
            y_ref[bi, rows, p * 128:(p + 1) * 128] = ys[:C] + ys[C:]
        for (bi, p), u, v, kb, d in zip(chains, Us, Vs, KB, dec):
            vu = jnp.concatenate([v, u], axis=0)
            s_scr[bi, p] = s_scr[bi, p] * d + dot(vu.T.astype(BF16), kb)
        return carry

    lax.fori_loop(0, n_chunks, chunk, 0)

    @pl.when(pl.program_id(1) == pl.num_programs(1) - 1)
    def _():
        so_ref[...] = s_scr[...]


def _rwkv_chunk_tables():
    C = RW_CHUNK
    t = np.arange(C)
    tri = (t[:, None] >= t[None, :]).astype(np.float32)
    tri3 = np.concatenate([tri, tri, tri], axis=1)
    i = np.arange(2 * C)
    same = (i[:, None] // C) == (i[None, :] // C)
    msl = (same & (i[:, None] > i[None, :])).astype(np.float32)
    mli = (same & (i[:, None] >= i[None, :])).astype(np.float32)
    levels = []
    n = 1
    while n < C:
        levels.append(same & ((i[:, None] // (2 * n)) == (i[None, :] // (2 * n)))
                      & ((i[:, None] // n) % 2 == 1) & ((i[None, :] // n) % 2 == 0))
        n *= 2
    return jnp.asarray(tri3, BF16), msl, mli, jnp.asarray(np.stack(levels).astype(np.float32), BF16)


def rwkv_chunk_scan(r, lw, k, v, a, b, tb, nb):
    B, L, _ = r.shape
    tri3, msl, mli, lvl = _rwkv_chunk_tables()
    blk = pl.BlockSpec((nb, tb, W_C), lambda bi, l: (bi, l, 0))
    c2 = lambda s: pl.BlockSpec(s, lambda bi, l: tuple(0 for _ in s))
    return pl.pallas_call(
        functools.partial(_rwkv_chunk_kernel, n_chunks=tb // RW_CHUNK, nb=nb),
        grid=(B // nb, L // tb),
        in_specs=[blk] * 6 + [c2(tri3.shape), c2(msl.shape), c2(mli.shape), c2(lvl.shape)],
        out_specs=[blk, pl.BlockSpec((nb, H_C // 2, 128, 128), lambda bi, l: (bi, 0, 0, 0))],
        out_shape=[jax.ShapeDtypeStruct((B, L, W_C), F32),
                   jax.ShapeDtypeStruct((B, H_C // 2, 128, 128), F32)],
        scratch_shapes=[pltpu.VMEM((nb, H_C // 2, 128, 128), F32)],
        compiler_params=_cp("parallel", "arbitrary"),
        name="rwkv_chunk_scan",
    )(r, lw, k, v, a, b, tri3, msl, mli, lvl)


def _rwkv_step_kernel(s_ref, w_ref, a_ref, b_ref, k_ref, r_ref, v_ref, y_ref, so_ref, vt_scr, y_scr):
    tr = lambda ref: ref[...].astype(F32).T
    wT, aT, bT, kT, rT = jnp.exp(tr(w_ref)), tr(a_ref), tr(b_ref), tr(k_ref), tr(r_ref)
    vt_scr[...] = tr(v_ref)
    for hh in range(2):
        ch = slice(hh * DH_C, (hh + 1) * DH_C)
        w, a, b, k, r = wT[ch], aT[ch], bT[ch], kT[ch], rT[ch]

        def value_row(i, carry):
            row = pl.ds(hh * DH_C + i, 1)
            S = s_ref[hh, i]
            sa = jnp.sum(S * a, axis=0, keepdims=True)
            Sn = S * w + sa * b + vt_scr[row, :] * k
            so_ref[hh, i] = Sn
            y_scr[row, :] = jnp.sum(Sn * r, axis=0, keepdims=True)
            return carry

        lax.fori_loop(0, DH_C, value_row, 0)
    y_ref[...] = y_scr[...].T


def rwkv_step(state_t, layer, lw, a, b, k, r, v):
    B = lw.shape[0]
    vec = pl.BlockSpec((B, 2 * DH_C), lambda p: (0, p))
    return pl.pallas_call(
        _rwkv_step_kernel,
        grid=(H_C // 2,),
        in_specs=[pl.BlockSpec((None, 2, DH_C, DH_C, B), lambda p: (layer, p, 0, 0, 0))] + [vec] * 6,
        out_specs=[vec, pl.BlockSpec((2, DH_C, DH_C, B), lambda p: (p, 0, 0, 0))],
        out_shape=[jax.ShapeDtypeStruct((B, W_C), F32),
                   jax.ShapeDtypeStruct((H_C, DH_C, DH_C, B), F32)],
        scratch_shapes=[pltpu.VMEM((2 * DH_C, B), F32), pltpu.VMEM((2 * DH_C, B), F32)],
        compiler_params=_cp("parallel"),
        name="rwkv_step",
    )(state_t, lw, a, b, k, r, v)


def _merge_kernel(x_ref, ya_ref, yb_ref, y_ref, r_ref, k_ref, v_ref, gg_ref, za_ref, zb_ref, zc_ref, gt_ref,
                  lng_ref, lnb_ref, rk_ref, ones_ref, wa_ref, wb_ref, wc_ref, wo_ref, o_ref):
    ones2 = ones_ref[...]
    y = y_ref[...]
    mu = _seg_sum_wide(y, ones2) * (1.0 / DH_C)
    d = y - mu
    var = _seg_sum_wide(d * d, ones2) * (1.0 / DH_C)
    ycn = d * lax.rsqrt(var + LNX_EPS) * lng_ref[...] + lnb_ref[...]
    f32 = lambda ref: ref[...].astype(F32)
    bonus = _seg_sum_wide(f32(r_ref) * f32(k_ref) * rk_ref[...], ones2) * f32(v_ref)
    yc = (ycn + bonus) * f32(gg_ref)
    proj = lambda t, w: jnp.dot(t.astype(BF16), w[...], preferred_element_type=F32)
    merged = (jax.nn.sigmoid(za_ref[...]) * proj(ya_ref[...], wa_ref)
              + jax.nn.sigmoid(zb_ref[...]) * proj(yb_ref[...], wb_ref)
              + jax.nn.sigmoid(zc_ref[...]) * proj(yc, wc_ref))
    o_ref[...] = x_ref[...] + gt_ref[...] * proj(merged, wo_ref)


def merge_out(x2, ya, yb, y, r, k, v, gg, z2, gt, lng, lnb, rk, wa, wb, wc, wo, tm):
    T, D = x2.shape
    G, R, _ = gt.shape
    tiles_per_group = (T // tm) // G
    tok = lambda w: pl.BlockSpec((tm, w), lambda m: (m, 0))
    gate = lambda idx: pl.BlockSpec((tm, D), lambda m: (m, Z_GATE // D + idx))
    row = lambda w: pl.BlockSpec((1, w), lambda m: (0, 0))
    full = lambda a: pl.BlockSpec(a.shape, lambda m: (0, 0))
    return pl.pallas_call(
        _merge_kernel,
        grid=(T // tm,),
        in_specs=[tok(D)] + [tok(W_C)] * 7 + [gate(0), gate(1), gate(2),
                  pl.BlockSpec((None, R, D), lambda m: (m // tiles_per_group, 0, 0)),
                  row(W_C), row(W_C), row(W_C), pl.BlockSpec((256, 128), lambda m: (0, 0)),
                  full(wa), full(wb), full(wc), full(wo)],
        out_specs=tok(D),
        out_shape=jax.ShapeDtypeStruct((T, D), F32),
        compiler_params=_cp("parallel"),
        name="merge_out",
    )(x2, ya, yb, y, r, k, v, gg, z2, z2, z2, gt, lng, lnb, rk, _block_ones2(DH_C), wa, wb, wc, wo)


def _ffn_kernel(x_ref, g_ref, sc_ref, sh_ref, gt_ref, w1_ref, w2_ref, o_ref, h_scr, acc_scr):
    f = pl.program_id(1)

    @pl.when(f == 0)
    def _():
        x = x_ref[...]
        y = x * lax.rsqrt(jnp.mean(x * x, axis=-1, keepdims=True) + EPS) * g_ref[...]
        h_scr[...] = (y * (1.0 + sc_ref[...]) + sh_ref[...]).astype(BF16)
        acc_scr[...] = jnp.zeros_like(acc_scr)

    u = jnp.maximum(jnp.dot(h_scr[...], w1_ref[...], preferred_element_type=F32), 0.0)
    acc_scr[...] += jnp.dot((u * u).astype(BF16), w2_ref[...], preferred_element_type=F32)

    @pl.when(f == pl.num_programs(1) - 1)
    def _():
        o_ref[...] = x_ref[...] + gt_ref[...] * acc_scr[...]


def ffn(x2, gain_row, sc, sh, gt, w1, w2, tm, tf):
    T, D = x2.shape
    F = w1.shape[1]
    G, R, _ = sc.shape
    tiles_per_group = (T // tm) // G
    mod_spec = pl.BlockSpec((None, R, D), lambda m, f: (m // tiles_per_group, 0, 0))
    return pl.pallas_call(
        _ffn_kernel,
        grid=(T // tm, F // tf),
        in_specs=[pl.BlockSpec((tm, D), lambda m, f: (m, 0)),
                  pl.BlockSpec((1, D), lambda m, f: (0, 0)),
                  mod_spec, mod_spec, mod_spec,
                  pl.BlockSpec((D, tf), lambda m, f: (0, f)),
                  pl.BlockSpec((tf, D), lambda m, f: (f, 0))],
        out_specs=pl.BlockSpec((tm, D), lambda m, f: (m, 0)),
        out_shape=jax.ShapeDtypeStruct((T, D), F32),
        scratch_shapes=[pltpu.VMEM((tm, D), BF16), pltpu.VMEM((tm, D), F32)],
        compiler_params=_cp("parallel", "arbitrary"),
        name="ffn",
    )(x2, gain_row, sc, sh, gt, w1, w2)


def prep_w_in(w_in):
    depth, D, _ = w_in.shape
    wt = jnp.transpose(w_in, (0, 2, 1))
    o_diff, o_rwkv, o_gate = RET_COLS, RET_COLS + DIFF_COLS, RET_COLS + DIFF_COLS + RWKV_COLS
    return jnp.concatenate([
        wt[:, :RET_COLS], wt[:, o_rwkv:o_gate], jnp.zeros((depth, RWKV_PAD - RWKV_COLS, D), F32),
        wt[:, o_gate:], wt[:, o_diff:o_rwkv]], axis=1).astype(BF16)


def _prep_layer(p, l, w_in_p):
    place = lambda w, slot: jnp.pad(w, ((0, 0), (slot * W_C, (2 - slot) * W_C)))
    w_lr = jnp.concatenate([place(p['w_decay_up'][l], 0), place(p['w_a_up'][l], 1), place(p['w_g_up'][l], 2),
                            jnp.zeros((LR_COLS - R_DECAY - R_A - R_G, 3 * W_C), F32)], axis=0)
    row = lambda t: t.reshape(1, -1)
    lp = dict(
        layer=l, w_ada=p['w_ada'], b_ada=row(p['b_ada'][l]),
        norm1=row(p['norm1'][l]), norm2=row(p['norm2'][l]), w_in=w_in_p,
        gq=row(jnp.tile(p['qk_norm_q'][l], 2 * H_B)), gk=row(jnp.tile(p['qk_norm_k'][l], 2 * H_B)),
        lam_rows=jnp.stack([p['lambda_q1'][l], p['lambda_k1'][l], p['lambda_q2'][l], p['lambda_k2'][l]]),
        subln=row(p['subln_diff'][l]),
        mu=row(jnp.pad(p['mu_shift'][l], (0, RWKV_PAD - RWKV_COLS))),
        w0=row(p['w0'][l]), a0=row(p['a0'][l]), k_k=row(p['k_k'][l]), k_a=row(p['k_a'][l]),
        w_lr=w_lr.astype(BF16), r_k=row(p['r_k'][l]), lnx_g=row(p['lnx_g'][l]), lnx_b=row(p['lnx_b'][l]),
        w_up_a=p['w_up_a'][l].astype(BF16), w_up_b=p['w_up_b'][l].astype(BF16),
        w_up_c=p['w_up_c'][l].astype(BF16), w_out=p['w_out'][l].astype(BF16),
        w_ff1=p['w_ff1'][l].astype(BF16), w_ff2=p['w_ff2'][l].astype(BF16),
        lam_init=0.8 - 0.6 * math.exp(-0.3 * l),
    )
    if l > 0:
        lp['v0'] = row(p['v0'][l - 1])
        lp['w_v_down'] = jnp.pad(p['w_v_down'][l - 1], ((0, 0), (0, 128 - R_V))).astype(BF16)
        lp['w_v_up'] = jnp.pad(p['w_v_up'][l - 1], ((0, 128 - R_V), (0, 0))).astype(BF16)
    return lp


def _modulation(c, lp, per_token):
    B, D = c.shape
    rows = -(-B // 16) * 16
    mod = ada_mod(jnp.pad(c, ((0, rows - B), (0, 0))), lp['w_ada'], lp['b_ada'], lp['layer'])[:B]
    parts = [mod[:, i * D:(i + 1) * D] for i in range(6)]
    shape = (1, B, D) if per_token else (B, 1, D)
    return [t.reshape(shape) for t in parts]


def _trunk_prompt(x, c, layers, bias_p):
    B, L, D = x.shape
    T = B * L
    tm = min(1024, L)
    t_attn = min(512, L)
    x2 = x.reshape(T, D)
    cos_t, sin_t = _rope_tables(jnp.arange(L))
    k_rows, v_rows, ret_out, rwkv_out, shift_out = [], [], [], [], []
    v_first = None
    for l, lp in enumerate(layers):
        sh1, sc1, gt1, sh2, sc2, gt2 = _modulation(c, lp, per_token=False)
        z2 = in_proj(x2, lp['norm1'], sc1, sh1, lp['w_in'], l, min(2048, L), 512)
        ZC = z2.shape[1]
        z3 = z2.reshape(B, L, ZC)
        ya, s_ret = retention_prompt(z3, cos_t, sin_t, min(256, L))
        qb, k32, kb, v32, vb = qk_norm(z2, lp['gq'], lp['gk'], tm, DH_B ** -0.5 * math.log2(math.e))
        r3 = lambda t: t.reshape(B, L, W_B)
        yb = diff_attention_prompt(r3(qb), r3(kb), r3(vb), bias_p, lp['lam_rows'], lp['subln'],
                                   lp['lam_init'], t_attn)
        k_rows.append(k32.reshape(B, L, H_B, 2 * DH_B))
        v_rows.append(v32.reshape(B, L, H_B, DV_B))
        vgate = None if l == 0 else (lp['v0'], lp['w_v_down'], lp['w_v_up'], v_first)
        shift0 = jnp.zeros((B, 1, RWKV_PAD), F32)
        r, lw, k, v, a, b, gg = rwkv_prep(z3, shift0, lp['mu'], lp['w0'], lp['a0'], lp['k_k'], lp['k_a'],
                                          lp['w_lr'], vgate, min(256, L), carry_shift=True)
        if l == 0:
            v_first = v
        y, s_pair = rwkv_chunk_scan(r, lw, k, v, a, b, min(256, L), B)
        shift_out.append(z3[:, L - 1:, Z_RWKV:Z_RWKV + RWKV_COLS])
        f2 = lambda t: t.reshape(T, -1)
        x2 = merge_out(x2, f2(ya), f2(yb), f2(y), f2(r), f2(k), f2(v), f2(gg), z2, gt1, lp['lnx_g'],
                       lp['lnx_b'], lp['r_k'], lp['w_up_a'], lp['w_up_b'], lp['w_up_c'], lp['w_out'],
                       min(512, L))
        x2 = ffn(x2, lp['norm2'], sc2, sh2, gt2, lp['w_ff1'], lp['w_ff2'], tm, 1024)
        ret_out.append(s_ret)
        sp = s_pair.reshape(B, H_C // 2, 2, DH_C, 2, DH_C)
        rwkv_out.append(jnp.stack([sp[:, :, 0, :, 0, :], sp[:, :, 1, :, 1, :]], axis=2)
                        .reshape(B, H_C, DH_C, DH_C))
    return (x2.reshape(B, L, D), jnp.stack(k_rows), jnp.stack(v_rows), jnp.stack(ret_out),
            jnp.stack(rwkv_out), jnp.stack(shift_out))


def _trunk_decode(x, c, layers, bias_d, state_ret, state_rwkv, state_shift, cache_k, cache_v, page_table):
    B, _, D = x.shape
    n_pages = page_table.shape[1]
    past = n_pages * PAGE_SIZE
    x2 = x.reshape(B, D)
    pt_flat = page_table.reshape(-1)
    ck = cache_k.reshape(cache_k.shape[0], cache_k.shape[1], PAGE_SIZE * H_B, 2 * DH_B)
    cv = cache_v.reshape(cache_v.shape[0], cache_v.shape[1], PAGE_SIZE * H_B, DV_B)
    state_t = jnp.transpose(state_rwkv, (0, 2, 3, 4, 1))
    k_rows, v_rows, ret_out, rwkv_out, shift_out = [], [], [], [], []
    v_first = None
    for l, lp in enumerate(layers):
        sh1, sc1, gt1, sh2, sc2, gt2 = _modulation(c, lp, per_token=True)
        z2 = in_proj(x2, lp['norm1'], sc1, sh1, lp['w_in'], l, B, 512)
        ZC = z2.shape[1]
        ya, s_ret = retention_step(z2, state_ret, l, past)
        qb, k32, kb, v32, vb = qk_norm(z2, lp['gq'], lp['gk'], B, DH_B ** -0.5)
        h3 = lambda t: t.reshape(B, H_B, DV_B)
        yb = diff_attention_decode(qb.reshape(B, 1, W_BQK), h3(k32), h3(v32), ck, cv, pt_flat, bias_d,
                                   lp['lam_rows'], lp['subln'], l, lp['lam_init'], n_pages).reshape(B, W_B)
        k_rows.append(k32.reshape(B, 1, H_B, 2 * DH_B))
        v_rows.append(v32.reshape(B, 1, H_B, DV_B))
        vgate = None if l == 0 else (lp['v0'], lp['w_v_down'], lp['w_v_up'], v_first)
        prev = jnp.pad(state_shift[l].reshape(1, B, RWKV_COLS), ((0, 0), (0, 0), (0, RWKV_PAD - RWKV_COLS)))
        r, lw, k, v, a, b, gg = rwkv_prep(z2.reshape(1, B, ZC), prev, lp['mu'], lp['w0'], lp['a0'], lp['k_k'],
                                          lp['k_a'], lp['w_lr'], vgate, B, carry_shift=False)
        if l == 0:
            v_first = v
        f2 = lambda t: t.reshape(B, -1)
        y_dec, s_rwkv = rwkv_step(state_t, l, f2(lw), f2(a), f2(b), f2(k), f2(r), f2(v))
        shift_out.append(z2[:, Z_RWKV:Z_RWKV + RWKV_COLS].reshape(B, 1, RWKV_COLS))
        x2 = merge_out(x2, ya, yb, y_dec, f2(r), f2(k), f2(v), f2(gg), z2, gt1, lp['lnx_g'], lp['lnx_b'],
                       lp['r_k'], lp['w_up_a'], lp['w_up_b'], lp['w_up_c'], lp['w_out'], B)
        x2 = ffn(x2, lp['norm2'], sc2, sh2, gt2, lp['w_ff1'], lp['w_ff2'], B, 1024)
        ret_out.append(s_ret)
        rwkv_out.append(s_rwkv)
    rwkv_state = jnp.transpose(jnp.stack(rwkv_out), (0, 4, 1, 2, 3))
    return (x2.reshape(B, 1, D), jnp.stack(k_rows), jnp.stack(v_rows), jnp.stack(ret_out),
            rwkv_state, jnp.stack(shift_out))


def _decode_buckets(n_pages):
    past = n_pages * PAGE_SIZE
    key = np.arange((n_pages + 1) * PAGE_SIZE)
    bk = np.where(key <= past, _bucket_np(np.maximum(past - key, 0)), -1)
    return np.broadcast_to(bk.reshape(n_pages + 1, 1, PAGE_SIZE), (n_pages + 1, 8, PAGE_SIZE)).astype(np.int32)


def kernel(x_prompt, x_sample, c_prompt, c_sample, cache_k_diff, cache_v_diff, page_table, state_ret, state_rwkv, state_shift, rel_bias, w_ada, b_ada, norm1, norm2, w_in, qk_norm_q, qk_norm_k, lambda_q1, lambda_k1, lambda_q2, lambda_k2, subln_diff, mu_shift, w0, w_decay_up, a0, w_a_up, w_g_up, v0, w_v_down, w_v_up, k_k, k_a, r_k, lnx_g, lnx_b, w_up_a, w_up_b, w_up_c, w_out, w_ff1, w_ff2):
    p = dict(w_ada=w_ada, b_ada=b_ada, norm1=norm1, norm2=norm2, w_in=w_in,
             qk_norm_q=qk_norm_q, qk_norm_k=qk_norm_k, lambda_q1=lambda_q1, lambda_k1=lambda_k1,
             lambda_q2=lambda_q2, lambda_k2=lambda_k2, subln_diff=subln_diff, mu_shift=mu_shift,
             w0=w0, w_decay_up=w_decay_up, a0=a0, w_a_up=w_a_up, w_g_up=w_g_up, v0=v0,
             w_v_down=w_v_down, w_v_up=w_v_up, k_k=k_k, k_a=k_a, r_k=r_k, lnx_g=lnx_g, lnx_b=lnx_b,
             w_up_a=w_up_a, w_up_b=w_up_b, w_up_c=w_up_c, w_out=w_out, w_ff1=w_ff1, w_ff2=w_ff2)
    depth = w_in.shape[0]
    w_in_p = prep_w_in(w_in)
    layers = [_prep_layer(p, l, w_in_p) for l in range(depth)]
    L = x_prompt.shape[1]
    n_pages = page_table.shape[1]
    t_attn = min(512, L)
    bias_p = bias_tiles(rel_bias, _prompt_buckets(t_attn), True)
    bd = bias_tiles(rel_bias, _decode_buckets(n_pages).reshape((n_pages + 1) * 8, PAGE_SIZE), False)
    bias_d = bd.reshape(H_B, n_pages + 1, 8, PAGE_SIZE)

    y_p, k_p, v_p, ret_p, rwkv_p, shift_p = _trunk_prompt(x_prompt, c_prompt, layers, bias_p)
    y_s, k_s, v_s, ret_s, rwkv_s, shift_s = _trunk_decode(
        x_sample, c_sample, layers, bias_d, state_ret, state_rwkv, state_shift,
        cache_k_diff, cache_v_diff, page_table)
    return (y_p, y_s, k_p, v_p, k_s, v_s, ret_p, ret_s, rwkv_p, rwkv_s, shift_p, shift_s)
```

```python
import functools
import math

import numpy as np
import jax
import jax.numpy as jnp
from jax import lax
from jax.experimental import pallas as pl
from jax.experimental.pallas import tpu as pltpu

F32 = jnp.float32
BF16 = jnp.bfloat16

H_A, DK_A, DV_A = 4, 128, 128
RET_CHUNK = 128
ROPE_BASE = 10000.0
H_B, DH_B, DV_B = 4, 64, 128
N_BUCKETS, MAX_DISTANCE = 32, 128
H_C, DH_C = 8, 64
R_DECAY, R_A, R_V, R_G = 64, 64, 32, 160
LNX_EPS = 64e-5
EPS = 1e-6
PAGE_SIZE = 128
W_AQ, W_A = H_A * DK_A, H_A * DV_A
W_BQK, W_B = H_B * 2 * DH_B, H_B * DV_B
W_C = H_C * DH_C
RET_COLS = 2 * W_AQ + 2 * W_A
DIFF_COLS = 2 * W_BQK + W_B
RWKV_COLS = 3 * W_C + R_DECAY + R_A + R_G
RWKV_PAD = 2048
LR_COLS = RWKV_PAD - 3 * W_C

Z_RET, Z_RWKV = 0, RET_COLS
Z_GATE = Z_RWKV + RWKV_PAD
NEG = -1e30
RW_CHUNK = 64
VMEM_LIMIT = 56 * 1024 * 1024


def _cp(*sem):
    return pltpu.CompilerParams(dimension_semantics=sem, vmem_limit_bytes=VMEM_LIMIT)


def _silu(x):
    return x * jax.nn.sigmoid(x)


def _split2(x):
    hi = x.astype(BF16)
    lo = (x - hi.astype(F32)).astype(BF16)
    return hi, lo


def _seg_sum(x, ones):
    return jnp.dot(x.astype(BF16), ones, preferred_element_type=F32)


def _seg_sum_wide(x, ones2):
    return jnp.concatenate(
        [_seg_sum(x[:, c * 128:(c + 1) * 128], ones2) for c in range(x.shape[1] // 128)], axis=1)


def _mod_kernel(c_ref, w_ref, b_ref, o_ref):
    s = _silu(c_ref[...])
    o_ref[...] = jnp.dot(s.astype(BF16), w_ref[...].astype(BF16), preferred_element_type=F32) + b_ref[...]


def ada_mod(c_pad, w_stack, b_row, layer):
    R, D = c_pad.shape
    N = w_stack.shape[2]
    tn = 1536
    return pl.pallas_call(
        _mod_kernel,
        grid=(N // tn,),
        in_specs=[pl.BlockSpec((R, D), lambda n: (0, 0)),
                  pl.BlockSpec((None, D, tn), lambda n: (layer, 0, n)),
                  pl.BlockSpec((1, tn), lambda n: (0, n))],
        out_specs=pl.BlockSpec((R, tn), lambda n: (0, n)),
        out_shape=jax.ShapeDtypeStruct((R, N), F32),
        compiler_params=_cp("parallel"),
        name="ada_mod",
    )(c_pad, w_stack, b_row)


def _inproj_kernel(x_ref, g_ref, sc_ref, sh_ref, w_ref, z_ref, h_scr):
    @pl.when(pl.program_id(1) == 0)
    def _():
        x = x_ref[...]
        y = x * lax.rsqrt(jnp.mean(x * x, axis=-1, keepdims=True) + EPS) * g_ref[...]
        h_scr[...] = (y * (1.0 + sc_ref[...]) + sh_ref[...]).astype(BF16)

    z_ref[...] = lax.dot_general(h_scr[...], w_ref[...], (((1,), (1,)), ((), ())),
                                 preferred_element_type=F32)


def in_proj(x2, gain_row, sc, sh, w_bf, layer, tm, tn):
    T, D = x2.shape
    N = w_bf.shape[1]
    G, R, _ = sc.shape
    tiles_per_group = (T // tm) // G
    mod_spec = pl.BlockSpec((None, R, D), lambda m, n: (m // tiles_per_group, 0, 0))
    return pl.pallas_call(
        _inproj_kernel,
        grid=(T // tm, N // tn),
        in_specs=[pl.BlockSpec((tm, D), lambda m, n: (m, 0)),
                  pl.BlockSpec((1, D), lambda m, n: (0, 0)),
                  mod_spec, mod_spec,
                  pl.BlockSpec((None, tn, D), lambda m, n: (layer, n, 0))],
        out_specs=pl.BlockSpec((tm, tn), lambda m, n: (m, n)),
        out_shape=jax.ShapeDtypeStruct((T, N), F32),
        scratch_shapes=[pltpu.VMEM((tm, D), BF16)],
        compiler_params=_cp("parallel", "arbitrary"),
        name="in_proj",
    )(x2, gain_row, sc, sh, w_bf)


def _rope(x, cos, sin_signed):
    return x * cos + pltpu.roll(x, DK_A // 2, 1) * sin_signed


def _ret_kernel(q_ref, k_ref, v_ref, g_ref, cos_ref, sin_ref, inner_ref, cross_ref, tail_ref, cd_ref,
                ya_ref, so_ref, s_scr, *, n_sub):
    lt = pl.program_id(1)

    @pl.when(lt == 0)
    def _():
        s_scr[...] = jnp.zeros_like(s_scr)

    C = RET_CHUNK
    heads = range(H_A)
    cols = [slice(h * DK_A, (h + 1) * DK_A) for h in heads]
    dot = lambda x, y: jnp.dot(x, y, preferred_element_type=F32)
    for c in range(n_sub):
        rows = slice(c * C, (c + 1) * C)
        cos = cos_ref[rows, :]
        sin = sin_ref[rows, :]
        qb, kb, vb, kt = [], [], [], []
        for h in heads:
            q = _rope(q_ref[rows, cols[h]], cos, sin)
            k = _rope(k_ref[rows, cols[h]], cos, sin) * (DK_A ** -0.5)
            qb.append(q.astype(BF16))
            kb.append(k.astype(BF16))
            vb.append(v_ref[rows, cols[h]].astype(BF16))
            kt.append((k * tail_ref[h]).T.astype(BF16))
        S = [s_scr[h] for h in heads]
        sc = [lax.dot_general(qb[h], kb[h], (((1,), (1,)), ((), ())), preferred_element_type=F32) * inner_ref[h]
              for h in heads]
        qs = [dot(qb[h], S[h].astype(BF16)) * cross_ref[h] for h in heads]
        kv = [dot(kt[h], vb[h]) for h in heads]
        o = [dot(sc[h].astype(BF16), vb[h]) + qs[h] for h in heads]
        for h in heads:
            s_scr[h] = S[h] * cd_ref[h] + kv[h]
            on = o[h] * lax.rsqrt(jnp.mean(o[h] * o[h], axis=-1, keepdims=True) + EPS)
            ya_ref[rows, cols[h]] = (on * _silu(g_ref[rows, cols[h]])).astype(ya_ref.dtype)

    @pl.when(lt == pl.num_programs(1) - 1)
    def _():
        so_ref[...] = s_scr[...]


def _ret_tables(L):
    C = math.gcd(L, RET_CHUNK)
    log_g = np.log1p(-np.exp2(-5.0 - np.arange(H_A, dtype=np.float32))).astype(np.float32)
    i = np.arange(C, dtype=np.float32)
    dist = i[:, None] - i[None, :]
    causal = dist >= 0
    inner = np.where(causal[None], np.exp(np.where(causal, dist, 0.0)[None] * log_g[:, None, None]), 0.0)
    cross = np.exp((i[None, :] + 1.0) * log_g[:, None])
    tail = np.exp((C - 1.0 - i)[None, :] * log_g[:, None])
    chunk = np.exp(C * log_g)
    bc = lambda t: np.broadcast_to(t[:, :, None], (H_A, C, 128)).astype(np.float32)
    cd = np.broadcast_to(chunk[:, None, None], (H_A, 1, 128)).astype(np.float32)
    return inner.astype(np.float32), bc(cross), bc(tail), cd


def _rope_tables(pos):
    half = DK_A // 2
    inv = ROPE_BASE ** (-jnp.arange(half, dtype=F32) / half)
    ang = pos.astype(F32)[:, None] * inv[None, :]
    cos, sin = jnp.cos(ang), jnp.sin(ang)
    return jnp.concatenate([cos, cos], axis=-1), jnp.concatenate([-sin, sin], axis=-1)


def retention_prompt(z3, cos_t, sin_t, tb):
    B, L, ZC = z3.shape
    inner, cross, tail, cd = _ret_tables(L)
    zspec = lambda idx: pl.BlockSpec((None, tb, W_AQ), lambda b, l: (b, l, idx))
    full3 = lambda s: pl.BlockSpec(s, lambda b, l: (0, 0, 0))
    return pl.pallas_call(
        functools.partial(_ret_kernel, n_sub=tb // RET_CHUNK),
        grid=(B, L // tb),
        in_specs=[zspec(0), zspec(1), zspec(2), zspec(3),
                  pl.BlockSpec((tb, 128), lambda b, l: (l, 0)),
                  pl.BlockSpec((tb, 128), lambda b, l: (l, 0)),
                  full3(inner.shape), full3(cross.shape), full3(tail.shape), full3(cd.shape)],
        out_specs=[pl.BlockSpec((None, tb, W_A), lambda b, l: (b, l, 0)),
                   pl.BlockSpec((None, H_A, DK_A, DV_A), lambda b, l: (b, 0, 0, 0))],
        out_shape=[jax.ShapeDtypeStruct((B, L, W_A), BF16),
                   jax.ShapeDtypeStruct((B, H_A, DK_A, DV_A), F32)],
        scratch_shapes=[pltpu.VMEM((H_A, DK_A, DV_A), F32)],
        compiler_params=_cp("parallel", "arbitrary"),
        name="retention_prompt",
    )(z3, z3, z3, z3, cos_t, sin_t, inner, cross, tail, cd)


def _ret_step_kernel(q_ref, k_ref, v_ref, g_ref, cos_ref, sin_ref, gam_ref, s_ref, stack_ref, ya_ref, so_ref,
                     *, bb):
    del stack_ref
    cos, sin = cos_ref[...], sin_ref[...]
    row = lax.broadcasted_iota(jnp.int32, (bb, 128), 0)
    for h in range(H_A):
        cols = slice(h * DK_A, (h + 1) * DK_A)
        gam = gam_ref[h]
        q = _rope(q_ref[:, cols], cos, sin)
        k = _rope(k_ref[:, cols], cos, sin) * (DK_A ** -0.5)
        v = v_ref[:, cols]
        qb = q.astype(BF16)
        qk = jnp.sum(qb.astype(F32) * k.astype(BF16).astype(F32), axis=-1, keepdims=True)
        qs = jnp.zeros((bb, 128), F32)
        for b in range(bb):
            S = s_ref[b, h]
            onehot = row == b
            qs = jnp.where(onehot, jnp.dot(qb, S.astype(BF16), preferred_element_type=F32), qs)
            kb_t = jnp.where(onehot, k, 0.0).T.astype(BF16)
            so_ref[b, h] = S * gam + jnp.dot(kb_t, v.astype(BF16), preferred_element_type=F32)
        o = qk * v.astype(BF16).astype(F32) + qs * gam
        on = o * lax.rsqrt(jnp.mean(o * o, axis=-1, keepdims=True) + EPS)
        ya_ref[:, cols] = on * _silu(g_ref[:, cols])


def retention_step(z2, state, layer, pos, new_state):
    B = z2.shape[0]
    bb = 8
    cos_t, sin_t = _rope_tables(jnp.full((1,), pos))
    log_g = np.log1p(-np.exp2(-5.0 - np.arange(H_A, dtype=np.float32))).astype(np.float32)
    gam = np.broadcast_to(np.exp(log_g)[:, None, None], (H_A, 1, 128)).astype(np.float32)
    zspec = lambda idx: pl.BlockSpec((bb, W_AQ), lambda i: (i, idx))
    return pl.pallas_call(
        functools.partial(_ret_step_kernel, bb=bb),
        grid=(B // bb,),
        in_specs=[zspec(0), zspec(1), zspec(2), zspec(3),
                  pl.BlockSpec((1, 128), lambda i: (0, 0)),
                  pl.BlockSpec((1, 128), lambda i: (0, 0)),
                  pl.BlockSpec((H_A, 1, 128), lambda i: (0, 0, 0)),
                  pl.BlockSpec((None, bb, H_A, DK_A, DV_A), lambda i: (layer, i, 0, 0, 0)),
                  pl.BlockSpec(memory_space=pl.ANY)],
        out_specs=[pl.BlockSpec((bb, W_A), lambda i: (i, 0)),
                   pl.BlockSpec((None, bb, H_A, DK_A, DV_A), lambda i: (layer, i, 0, 0, 0))],
        out_shape=[jax.ShapeDtypeStruct((B, W_A), F32),
                   jax.ShapeDtypeStruct(new_state.shape, F32)],
        input_output_aliases={8: 1},
        compiler_params=_cp("parallel"),
        name="retention_step",
    )(z2, z2, z2, z2, cos_t, sin_t, gam, state, new_state)


def _qknorm_kernel(q_ref, k_ref, v_ref, gq_ref, gk_ref, ones_ref, k_stack_ref, v_stack_ref,
                   qb_ref, k32_ref, kb_ref, v32_ref, vb_ref, *, q_scale):
    del k_stack_ref, v_stack_ref
    ones2 = ones_ref[...]
    q, k, v = q_ref[...], k_ref[...], v_ref[...]
    tm = q.shape[0]
    qn = q * lax.rsqrt(_seg_sum_wide(q * q, ones2) * (1.0 / DH_B) + EPS) * gq_ref[...] * q_scale
    kn = k * lax.rsqrt(_seg_sum_wide(k * k, ones2) * (1.0 / DH_B) + EPS) * gk_ref[...]
    qb_ref[...] = qn.astype(BF16)
    kb_ref[...] = kn.astype(BF16)
    vb_ref[...] = v.astype(BF16)
    for h in range(H_B):
        head_rows = pl.ds(h, tm, stride=H_B)
        k32_ref[head_rows, :] = kn[:, h * DV_B:(h + 1) * DV_B]
        v32_ref[head_rows, :] = v[:, h * DV_B:(h + 1) * DV_B]


def _block_ones(group):
    i = np.arange(128)
    return jnp.asarray((i[:, None] // group == i[None, :] // group).astype(np.float32), BF16)


def qk_norm(z2, gq_row, gk_row, tm, q_scale, layer, k_stack, v_stack):
    T = z2.shape[0]
    base = (Z_GATE + 3 * 1024) // W_BQK
    zspec = lambda idx: pl.BlockSpec((tm, W_BQK), lambda m: (m, base + idx))
    row = pl.BlockSpec((1, W_BQK), lambda m: (0, 0))
    out = pl.BlockSpec((tm, W_BQK), lambda m: (m, 0))
    out_rows = pl.BlockSpec((None, tm * H_B, DV_B), lambda m: (layer, m, 0))
    stack = pl.BlockSpec(memory_space=pl.ANY)
    return pl.pallas_call(
        functools.partial(_qknorm_kernel, q_scale=q_scale),
        grid=(T // tm,),
        in_specs=[zspec(0), zspec(1), zspec(2), row, row, pl.BlockSpec((128, 128), lambda m: (0, 0)),
                  stack, stack],
        out_specs=[out, out_rows, out, out_rows, out],
        out_shape=[jax.ShapeDtypeStruct((T, W_BQK), BF16), jax.ShapeDtypeStruct(k_stack.shape, F32),
                   jax.ShapeDtypeStruct((T, W_BQK), BF16), jax.ShapeDtypeStruct(v_stack.shape, F32),
                   jax.ShapeDtypeStruct((T, W_B), BF16)],
        input_output_aliases={6: 1, 7: 3},
        compiler_params=_cp("parallel"),
        name="qk_norm",
    )(z2, z2, z2, gq_row, gk_row, _block_ones(DH_B), k_stack, v_stack)


def _bucket_np(n):
    max_exact = N_BUCKETS // 2
    nf = np.maximum(n, 1).astype(np.float32)
    large = max_exact + (np.log(nf / np.float32(max_exact)) / np.float32(math.log(MAX_DISTANCE / max_exact))
                         * np.float32(N_BUCKETS - max_exact)).astype(np.int32)
    large = np.minimum(large, N_BUCKETS - 1)
    return np.where(n < max_exact, n, large).astype(np.int32)


def _bias_kernel(tab_ref, bkt_ref, o_ref, *, log2_far_shift):
    h = pl.program_id(0)
    bk = bkt_ref[...]
    acc = jnp.zeros(bk.shape, F32)
    for b in range(N_BUCKETS):
        acc = jnp.where(bk == b, tab_ref[b, h], acc)
    if log2_far_shift:
        acc = (acc - tab_ref[N_BUCKETS - 1, h]) * math.log2(math.e)
    o_ref[...] = jnp.where(bk < 0, NEG, acc)


def bias_tiles(rel_bias, buckets, log2_far_shift):
    R, C = buckets.shape
    tr = min(R, 512)
    return pl.pallas_call(
        functools.partial(_bias_kernel, log2_far_shift=log2_far_shift),
        grid=(H_B, R // tr),
        in_specs=[pl.BlockSpec(memory_space=pltpu.SMEM),
                  pl.BlockSpec((tr, C), lambda h, r: (r, 0))],
        out_specs=pl.BlockSpec((None, tr, C), lambda h, r: (h, r, 0)),
        out_shape=jax.ShapeDtypeStruct((H_B, R, C), F32),
        compiler_params=_cp("parallel", "parallel"),
        name="bias_tiles",
    )(rel_bias, jnp.asarray(buckets))


def _prompt_buckets(t):
    r = np.arange(t)[:, None]
    c = np.arange(t)[None, :]
    diag = np.where(c <= r, _bucket_np(np.maximum(r - c, 0)), -1)
    off1 = _bucket_np(t + r - c)
    assert t >= MAX_DISTANCE
    return np.concatenate([diag, off1], axis=0).astype(np.int32)


def _lambda_full(lam_ref, lam_init):
    lv = lam_ref[...]
    s1 = jnp.sum(lv[0:1] * lv[1:2], axis=-1, keepdims=True)
    s2 = jnp.sum(lv[2:3] * lv[3:4], axis=-1, keepdims=True)
    return jnp.exp(s1) - jnp.exp(s2) + lam_init


def _flash_kernel(q_ref, k_ref, v_ref, bias_ref, lam_ref, gain_ref, o_ref, qq_scr, m_scr, acc_scr, s_scr,
                  *, lam_init):
    i = pl.program_id(2)
    t = q_ref.shape[0]

    q = q_ref[...]
    first = lax.broadcasted_iota(jnp.int32, q.shape, 1) < DH_B
    qq_scr[:t] = jnp.where(first, q, jnp.zeros_like(q))
    qq_scr[t:] = jnp.where(first, jnp.zeros_like(q), q)
    m_scr[...] = jnp.full(m_scr.shape, NEG, F32)
    acc_scr[...] = jnp.zeros_like(acc_scr)

    def raw_scores(comp, j):
        krows = pl.ds(pl.multiple_of(j * t, t), t)
        return lax.dot_general(qq_scr[comp * t:(comp + 1) * t], k_ref[krows, :], (((1,), (1,)), ((), ())),
                               preferred_element_type=F32)

    def accumulate(comp, s, v1):
        rows = slice(comp * t, (comp + 1) * t)
        m_old = m_scr[rows]
        m_new = jnp.maximum(m_old, jnp.max(s, axis=-1, keepdims=True))
        alpha = jnp.exp2(m_old - m_new)
        p = jnp.concatenate([jnp.exp2(s[:, c * 128:(c + 1) * 128] - m_new) for c in range(t // 128)],
                            axis=1).astype(BF16)
        acc_scr[rows] = (jnp.concatenate([alpha, alpha], axis=1) * acc_scr[rows]
                         + jnp.dot(p, v1, preferred_element_type=F32))
        m_scr[rows] = m_new

    def kv_block(j, tile, has_next):
        krows = pl.ds(pl.multiple_of(j * t, t), t)
        v1 = jnp.concatenate([v_ref[krows, :], jnp.ones((t, DV_B), BF16)], axis=1)
        with_bias = (lambda s: s) if tile is None else (lambda s: s + bias_ref[tile * t:(tile + 1) * t, :])
        s_first = s_scr[...]
        s_second = raw_scores(1, j)
        accumulate(0, with_bias(s_first), v1)
        if has_next:
            s_scr[...] = raw_scores(0, j + 1)
        accumulate(1, with_bias(s_second), v1)

    s_scr[...] = raw_scores(0, 0)

    def far_pair(jj, carry):
        kv_block(2 * jj, None, True)
        kv_block(2 * jj + 1, None, True)
        return carry

    n_far = jnp.maximum(i - 1, 0)
    lax.fori_loop(0, n_far // 2, far_pair, 0)

    @pl.when(jnp.logical_and(i >= 1, n_far % 2 == 1))
    def _():
        kv_block(i - 2, None, True)
        kv_block(i - 1, 1, True)
        kv_block(i, 0, False)

    @pl.when(jnp.logical_and(i >= 1, n_far % 2 == 0))
    def _():
        kv_block(i - 1, 1, True)
        kv_block(i, 0, False)

    @pl.when(i == 0)
    def _():
        kv_block(i, 0, False)

    lam = _lambda_full(lam_ref, lam_init)
    acc = acc_scr[...]
    ob = acc[:t, :DV_B] / acc[:t, DV_B:] - lam * (acc[t:, :DV_B] / acc[t:, DV_B:])
    on = ob * lax.rsqrt(jnp.mean(ob * ob, axis=-1, keepdims=True) + EPS) * gain_ref[...]
    o_ref[...] = (on * (1.0 - lam_init)).astype(o_ref.dtype)


def diff_attention_prompt(qb, kb, vb, bias, lam_rows, gain_row, lam_init, t):
    B, L, _ = qb.shape
    n = L // t
    kv_spec = pl.BlockSpec((None, L, DV_B), lambda b, h, i: (b, 0, h))
    return pl.pallas_call(
        functools.partial(_flash_kernel, lam_init=lam_init),
        grid=(B, H_B, n),
        in_specs=[pl.BlockSpec((None, t, DV_B), lambda b, h, i: (b, i, h)),
                  kv_spec, kv_spec,
                  pl.BlockSpec((None, 2 * t, t), lambda b, h, i: (h, 0, 0)),
                  pl.BlockSpec((4, DH_B), lambda b, h, i: (0, 0)),
                  pl.BlockSpec((1, DV_B), lambda b, h, i: (0, 0))],
        out_specs=pl.BlockSpec((None, t, DV_B), lambda b, h, i: (b, i, h)),
        out_shape=jax.ShapeDtypeStruct((B, L, W_B), BF16),
        scratch_shapes=[pltpu.VMEM((2 * t, DV_B), BF16), pltpu.VMEM((2 * t, 128), F32),
                        pltpu.VMEM((2 * t, 2 * DV_B), F32), pltpu.VMEM((t, t), F32)],
        compiler_params=_cp("parallel", "parallel", "arbitrary"),
        name="diff_attention_prompt",
    )(qb, kb, vb, bias, lam_rows, gain_row)


def _decode_kernel(pt_ref, q_ref, kn_ref, vn_ref, bias_ref, lam_ref, gain_ref, *rest, n_pages, lam_init):
    k_refs, v_refs, o_ref = rest[:n_pages], rest[n_pages:2 * n_pages], rest[2 * n_pages]
    lam = _lambda_full(lam_ref, lam_init)
    gain = gain_ref[...]
    r8 = lax.broadcasted_iota(jnp.int32, (8, 2 * DH_B), 0)
    l8 = lax.broadcasted_iota(jnp.int32, (8, 2 * DH_B), 1)
    comp_rows = (l8 // DH_B) == r8
    row0 = lax.broadcasted_iota(jnp.int32, (PAGE_SIZE, DV_B), 0) == 0
    nt = lambda x, y: lax.dot_general(x, y, (((1,), (1,)), ((), ())), preferred_element_type=F32)
    heads = range(H_B)
    cols = [slice(h * DV_B, (h + 1) * DV_B) for h in heads]
    head_rows = [pl.ds(h, PAGE_SIZE, stride=H_B) for h in heads]
    ss = []
    for h in heads:
        qrows = jnp.where(comp_rows, q_ref[:, cols[h]].astype(F32), 0.0).astype(BF16)
        ks = [r[head_rows[h], :].astype(BF16) for r in k_refs]
        ks.append(jnp.where(row0, kn_ref[h:h + 1, :], 0.0).astype(BF16))
        ss.append([nt(qrows, kp) + bias_ref[h, p] for p, kp in enumerate(ks)])
    ps, ls = [], []
    for h in heads:
        m = functools.reduce(jnp.maximum, [jnp.max(s, axis=-1, keepdims=True) for s in ss[h]])
        p = [jnp.exp(s - m) for s in ss[h]]
        ls.append(functools.reduce(jnp.add, [jnp.sum(x, axis=-1, keepdims=True) for x in p]))
        ps.append([x.astype(BF16) for x in p])
    for h in heads:
        vs = [r[head_rows[h], :].astype(BF16) for r in v_refs]
        vs.append(jnp.where(row0, vn_ref[h:h + 1, :], 0.0).astype(BF16))
        acc = functools.reduce(jnp.add, [jnp.dot(p, vp, preferred_element_type=F32)
                                         for p, vp in zip(ps[h], vs)])
        outn = acc / ls[h]
        oh = outn[0:1] - lam * outn[1:2]
        on = oh * lax.rsqrt(jnp.mean(oh * oh, axis=-1, keepdims=True) + EPS) * gain
        o_ref[:, cols[h]] = on * (1.0 - lam_init)


def diff_attention_decode(qb, k32, v32, cache_k, cache_v, pt_flat, bias_dec, lam_rows, gain_row, layer,
                          lam_init, n_pages):
    B = qb.shape[0]
    rowspec = pl.BlockSpec((None, 1, W_BQK), lambda b, pt: (b, 0, 0))
    headspec = pl.BlockSpec((None, H_B, DV_B), lambda b, pt: (b, 0, 0))
    const = lambda s: pl.BlockSpec(s, lambda b, pt: tuple(0 for _ in s))
    page_spec = lambda p: pl.BlockSpec((None, None, PAGE_SIZE * H_B, DV_B),
                                       lambda b, pt, p=p: (layer, pt[b * n_pages + p], 0, 0))
    grid_spec = pltpu.PrefetchScalarGridSpec(
        num_scalar_prefetch=1,
        grid=(B,),
        in_specs=[rowspec, headspec, headspec, const(bias_dec.shape), const((4, DH_B)), const((1, DV_B))]
                 + [page_spec(p) for p in range(n_pages)] * 2,
        out_specs=pl.BlockSpec((None, 1, W_B), lambda b, pt: (b, 0, 0)),
    )
    return pl.pallas_call(
        functools.partial(_decode_kernel, n_pages=n_pages, lam_init=lam_init),
        grid_spec=grid_spec,
        out_shape=jax.ShapeDtypeStruct((B, 1, W_B), F32),
        compiler_params=_cp("arbitrary"),
        name="diff_attention_decode",
    )(pt_flat, qb, k32, v32, bias_dec, lam_rows, gain_row,
      *([cache_k] * n_pages), *([cache_v] * n_pages))


def _rwkv_prep_kernel(*refs, carry_shift, gate_v):
    it = iter(refs)
    z_ref, prev_ref, mu_ref, w0_ref, a0_ref, kk_ref, ka_ref, wlr_ref, ones_ref = (next(it) for _ in range(9))
    if gate_v:
        v0_ref, wvd_ref, wvu_ref, vf_ref = (next(it) for _ in range(4))
    r_ref, lw_ref, k_ref, v_ref, a_ref, b_ref, gg_ref = (next(it) for _ in range(7))
    z = z_ref[...]
    if carry_shift:
        carry_scr = next(it)

        @pl.when(pl.program_id(1) == 0)
        def _():
            carry_scr[...] = prev_ref[...]

        first = lax.broadcasted_iota(jnp.int32, z.shape, 0) == 0
        zprev = jnp.where(first, carry_scr[...], pltpu.roll(z, 1, 0))
        carry_scr[...] = z_ref[z.shape[0] - 1:z.shape[0], :]
    else:
        zprev = prev_ref[...]
    zs = z + (zprev - z) * mu_ref[...]
    rc, kc, vc = zs[:, :W_C], zs[:, W_C:2 * W_C], zs[:, 2 * W_C:3 * W_C]
    lr = zs[:, 3 * W_C:]
    col = lax.broadcasted_iota(jnp.int32, lr.shape, 1)
    act = jnp.where(col < R_DECAY, jnp.tanh(lr), jnp.where(col < R_DECAY + R_A, lr, jax.nn.sigmoid(lr)))
    up = jnp.dot(act.astype(BF16), wlr_ref[...], preferred_element_type=F32)
    y = -(w0_ref[...] + up[:, :W_C])
    softplus = jnp.maximum(y, 0.0) + jnp.log(1.0 + jnp.exp(-jnp.abs(y)))
    lw_ref[...] = -jnp.exp(-softplus - 0.5)
    if gate_v:
        down = jnp.dot(vc.astype(BF16), wvd_ref[...], preferred_element_type=F32)
        vgate = jax.nn.sigmoid(v0_ref[...] + jnp.dot(down.astype(BF16), wvu_ref[...], preferred_element_type=F32))
        vc = vc + (vf_ref[...].astype(F32) - vc) * vgate
    a = jax.nn.sigmoid(a0_ref[...] + up[:, W_C:2 * W_C])
    gg_ref[...] = up[:, 2 * W_C:].astype(gg_ref.dtype)
    kk = kc * kk_ref[...]
    norm = jnp.sqrt(_seg_sum_wide(kk * kk, ones_ref[...]))
    kk = kk / jnp.maximum(norm, 1e-12)
    r_ref[...] = rc.astype(r_ref.dtype)
    k_ref[...] = (kc * (1.0 + (a - 1.0) * ka_ref[...])).astype(k_ref.dtype)
    v_ref[...] = vc.astype(v_ref.dtype)
    a_ref[...] = (-kk).astype(a_ref.dtype)
    b_ref[...] = (kk * a).astype(b_ref.dtype)


def rwkv_prep(z3, prev, mu_row, w0, a0, k_k, k_a, w_lr, vgate, tm, carry_shift):
    G, Lg, _ = z3.shape
    blk = lambda w, idx: pl.BlockSpec((None, tm, w), lambda g, l: (g, l, idx))
    row = lambda w: pl.BlockSpec((1, w), lambda g, l: (0, 0))
    prev_spec = (pl.BlockSpec((None, 1, RWKV_PAD), lambda g, l: (g, 0, 0)) if carry_shift
                 else blk(RWKV_PAD, 0))
    in_specs = [blk(RWKV_PAD, Z_RWKV // RWKV_PAD), prev_spec, row(RWKV_PAD), row(W_C), row(W_C), row(W_C),
                row(W_C), pl.BlockSpec(w_lr.shape, lambda g, l: (0, 0)),
                pl.BlockSpec((128, 128), lambda g, l: (0, 0))]
    args = [z3, prev, mu_row, w0, a0, k_k, k_a, w_lr, _block_ones(DH_C)]
    if vgate is not None:
        v0, wvd, wvu, vfirst = vgate
        in_specs += [row(W_C), pl.BlockSpec(wvd.shape, lambda g, l: (0, 0)),
                     pl.BlockSpec(wvu.shape, lambda g, l: (0, 0)), blk(W_C, 0)]
        args += [v0, wvd, wvu, vfirst]
    return pl.pallas_call(
        functools.partial(_rwkv_prep_kernel, carry_shift=carry_shift, gate_v=vgate is not None),
        grid=(G, Lg // tm),
        in_specs=in_specs,
        out_specs=[blk(W_C, 0)] * 7,
        out_shape=[jax.ShapeDtypeStruct((G, Lg, W_C), F32 if i == 1 else BF16) for i in range(7)],
        scratch_shapes=[pltpu.VMEM((1, RWKV_PAD), F32)] if carry_shift else [],
        compiler_params=_cp("parallel", "arbitrary"),
        name="rwkv_prep",
    )(*args)


def _rwkv_chunk_kernel(r_ref, lw_ref, k_ref, v_ref, a_ref, b_ref, tri_ref, msl_ref, mli_ref, lvl_ref,
                       y_ref, so_ref, s_scr, *, n_chunks, nb):
    C = RW_CHUNK

    @pl.when(pl.program_id(1) == 0)
    def _():
        s_scr[...] = jnp.zeros_like(s_scr)

    head0 = lax.broadcasted_iota(jnp.int32, (C, 128), 1) < DH_C
    tri3, msl, mli = tri_ref[...], msl_ref[...], mli_ref[...]
    ri = lax.broadcasted_iota(jnp.int32, (128, 128), 0)
    ci = lax.broadcasted_iota(jnp.int32, (128, 128), 1)
    eye = (ri == ci).astype(F32)
    n_levels = lvl_ref.shape[0]
    chains = [(bi, p) for bi in range(nb) for p in range(H_C // 2)]

    def stack(x):
        return jnp.concatenate([jnp.where(head0, x, 0.0), jnp.where(head0, 0.0, x)], axis=0)

    def nt(x, y):
        return lax.dot_general(x, y, (((1,), (1,)), ((), ())), preferred_element_type=F32)

    def dot(x, y):
        return jnp.dot(x, y, preferred_element_type=F32)

    def chunk(ci_, carry):
        rows = pl.ds(pl.multiple_of(ci_ * C, C), C)
        ld = lambda ref, bi, p: ref[bi, rows, p * 128:(p + 1) * 128].astype(F32)
        lams = []
        for bi, p in chains:
            lw = ld(lw_ref, bi, p)
            hi = lw.astype(BF16)
            mid = (lw - hi.astype(F32)).astype(BF16)
            lo = (lw - hi.astype(F32) - mid.astype(F32)).astype(BF16)
            lams.append(dot(tri3, jnp.concatenate([hi, mid, lo], axis=0)))
        AR, BK, Vs, KB, dec = [], [], [], [], []
        for (bi, p), lam in zip(chains, lams):
            r, lw, k = ld(r_ref, bi, p), ld(lw_ref, bi, p), ld(k_ref, bi, p)
            v, a, b = ld(v_ref, bi, p), ld(a_ref, bi, p), ld(b_ref, bi, p)
            lam_c = lam[C - 1:C, :]
            e_neg = jnp.exp(-lam)
            e_tail = jnp.exp(lam_c - lam)
            AR.append(jnp.concatenate([stack(a * jnp.exp(lam - lw)), stack(r * jnp.exp(lam))],
                                      axis=0).astype(BF16))
            BK.append(jnp.concatenate([stack(b * e_neg), stack(k * e_neg)], axis=0).astype(BF16))
            KB.append(jnp.concatenate([stack(k * e_tail), stack(b * e_tail)], axis=0).astype(BF16))
            Vs.append(stack(v))
            dec.append(jnp.exp(lam_c))
        G = [nt(x, y) for x, y in zip(AR, BK)]
        I0 = [nt(x, s_scr[bi, p].astype(BF16)) for x, (bi, p) in zip(AR, chains)]
        rhs = [i0[:128] + dot((g[:128, 128:] * msl).astype(BF16), v.astype(BF16))
               for g, i0, v in zip(G, I0, Vs)]
        Nh, Nl, D = [], [], []
        for g in G:
            h, l = _split2(g[:128, :128] * msl)
            Nh.append(h)
            Nl.append(l)
            D.append(eye + (h * lvl_ref[0]).astype(F32) + (l * lvl_ref[0]).astype(F32))
        for lv in range(1, n_levels):
            m = lvl_ref[lv]
            Db = [d.astype(BF16) for d in D]
            X = [dot(h * m, db) for h, db in zip(Nh, Db)]
            D = [d + dot(db, x.astype(BF16)) for d, db, x in zip(D, Db, X)]
        Us = [dot(d.astype(BF16), x.astype(BF16)) for d, x in zip(D, rhs)]
        for (bi, p), g, i0, u, v in zip(chains, G, I0, Us, Vs):
            ys = i0[128:] + dot(jnp.concatenate([g[128:, :128] * mli, g[128:, 128:] * mli], axis=1).astype(BF16),
                                jnp.concatenate([u, v], axis=0).astype(BF16))
            y_ref[bi, rows, p * 128:(p + 1) * 128] = ys[:C] + ys[C:]
        for (bi, p), u, v, kb, d in zip(chains, Us, Vs, KB, dec):
            vu = jnp.concatenate([v, u], axis=0)
            s_scr[bi, p] = s_scr[bi, p] * d + dot(vu.T.astype(BF16), kb)
        return carry

    lax.fori_loop(0, n_chunks, chunk, 0)

    @pl.when(pl.program_id(1) == pl.num_programs(1) - 1)
    def _():
        so_ref[...] = s_scr[...]


def _rwkv_chunk_tables():
    C = RW_CHUNK
    t = np.arange(C)
    tri = (t[:, None] >= t[None, :]).astype(np.float32)
    tri3 = np.concatenate([tri, tri, tri], axis=1)
    i = np.arange(2 * C)
    same = (i[:, None] // C) == (i[None, :] // C)
    msl = (same & (i[:, None] > i[None, :])).astype(np.float32)
    mli = (same & (i[:, None] >= i[None, :])).astype(np.float32)
    levels = []
    n = 1
    while n < C:
        levels.append(same & ((i[:, None] // (2 * n)) == (i[None, :] // (2 * n)))
                      & ((i[:, None] // n) % 2 == 1) & ((i[None, :] // n) % 2 == 0))
        n *= 2
    return jnp.asarray(tri3, BF16), msl, mli, jnp.asarray(np.stack(levels).astype(np.float32), BF16)


def rwkv_chunk_scan(r, lw, k, v, a, b, tb, nb):
    B, L, _ = r.shape
    tri3, msl, mli, lvl = _rwkv_chunk_tables()
    blk = pl.BlockSpec((nb, tb, W_C), lambda bi, l: (bi, l, 0))
    c2 = lambda s: pl.BlockSpec(s, lambda bi, l: tuple(0 for _ in s))
    return pl.pallas_call(
        functools.partial(_rwkv_chunk_kernel, n_chunks=tb // RW_CHUNK, nb=nb),
        grid=(B // nb, L // tb),
        in_specs=[blk] * 6 + [c2(tri3.shape), c2(msl.shape), c2(mli.shape), c2(lvl.shape)],
        out_specs=[blk, pl.BlockSpec((nb, H_C // 2, 128, 128), lambda bi, l: (bi, 0, 0, 0))],
        out_shape=[jax.ShapeDtypeStruct((B, L, W_C), F32),
                   jax.ShapeDtypeStruct((B, H_C // 2, 128, 128), F32)],
        scratch_shapes=[pltpu.VMEM((nb, H_C // 2, 128, 128), F32)],
        compiler_params=_cp("parallel", "arbitrary"),
        name="rwkv_chunk_scan",
    )(r, lw, k, v, a, b, tri3, msl, mli, lvl)


def _rwkv_step_kernel(s_ref, w_ref, a_ref, b_ref, k_ref, r_ref, v_ref, stack_ref, y_ref, so_ref, vt_scr, y_scr):
    del stack_ref
    tr = lambda ref: ref[...].astype(F32).T
    wT, aT, bT, kT, rT = jnp.exp(tr(w_ref)), tr(a_ref), tr(b_ref), tr(k_ref), tr(r_ref)
    vt_scr[...] = tr(v_ref)
    for hh in range(2):
        ch = slice(hh * DH_C, (hh + 1) * DH_C)
        w, a, b, k, r = wT[ch], aT[ch], bT[ch], kT[ch], rT[ch]

        def value_row(i, carry):
            row = pl.ds(hh * DH_C + i, 1)
            S = s_ref[hh, i]
            sa = jnp.sum(S * a, axis=0, keepdims=True)
            Sn = S * w + sa * b + vt_scr[row, :] * k
            so_ref[hh, i] = Sn
            y_scr[row, :] = jnp.sum(Sn * r, axis=0, keepdims=True)
            return carry

        lax.fori_loop(0, DH_C, value_row, 0)
    y_ref[...] = y_scr[...].T


def rwkv_step(state_t, layer, lw, a, b, k, r, v, new_state_t):
    B = lw.shape[0]
    vec = pl.BlockSpec((B, 2 * DH_C), lambda p: (0, p))
    sspec = pl.BlockSpec((None, 2, DH_C, DH_C, B), lambda p: (layer, p, 0, 0, 0))
    return pl.pallas_call(
        _rwkv_step_kernel,
        grid=(H_C // 2,),
        in_specs=[sspec] + [vec] * 6 + [pl.BlockSpec(memory_space=pl.ANY)],
        out_specs=[vec, sspec],
        out_shape=[jax.ShapeDtypeStruct((B, W_C), F32),
                   jax.ShapeDtypeStruct(new_state_t.shape, F32)],
        scratch_shapes=[pltpu.VMEM((2 * DH_C, B), F32), pltpu.VMEM((2 * DH_C, B), F32)],
        input_output_aliases={7: 1},
        compiler_params=_cp("parallel"),
        name="rwkv_step",
    )(state_t, lw, a, b, k, r, v, new_state_t)


def _merge_kernel(x_ref, ya_ref, yb_ref, y_ref, r_ref, k_ref, v_ref, gg_ref, za_ref, zb_ref, zc_ref, gt_ref,
                  lng_ref, lnb_ref, rk_ref, ones_ref, wa_ref, wb_ref, wc_ref, wo_ref, o_ref):
    ones2 = ones_ref[...]
    y = y_ref[...]
    mu = _seg_sum_wide(y, ones2) * (1.0 / DH_C)
    d = y - mu
    var = _seg_sum_wide(d * d, ones2) * (1.0 / DH_C)
    ycn = d * lax.rsqrt(var + LNX_EPS) * lng_ref[...] + lnb_ref[...]
    f32 = lambda ref: ref[...].astype(F32)
    bonus = _seg_sum_wide(f32(r_ref) * f32(k_ref) * rk_ref[...], ones2) * f32(v_ref)
    yc = (ycn + bonus) * f32(gg_ref)
    proj = lambda t, w: jnp.dot(t.astype(BF16), w[...], preferred_element_type=F32)
    merged = (jax.nn.sigmoid(za_ref[...]) * proj(ya_ref[...], wa_ref)
              + jax.nn.sigmoid(zb_ref[...]) * proj(yb_ref[...], wb_ref)
              + jax.nn.sigmoid(zc_ref[...]) * proj(yc, wc_ref))
    o_ref[...] = x_ref[...] + gt_ref[...] * proj(merged, wo_ref)


def merge_out(x2, ya, yb, y, r, k, v, gg, z2, gt, lng, lnb, rk, wa, wb, wc, wo, tm):
    T, D = x2.shape
    G, R, _ = gt.shape
    tiles_per_group = (T // tm) // G
    tok = lambda w: pl.BlockSpec((tm, w), lambda m: (m, 0))
    gate = lambda idx: pl.BlockSpec((tm, D), lambda m: (m, Z_GATE // D + idx))
    row = lambda w: pl.BlockSpec((1, w), lambda m: (0, 0))
    full = lambda a: pl.BlockSpec(a.shape, lambda m: (0, 0))
    return pl.pallas_call(
        _merge_kernel,
        grid=(T // tm,),
        in_specs=[tok(D)] + [tok(W_C)] * 7 + [gate(0), gate(1), gate(2),
                  pl.BlockSpec((None, R, D), lambda m: (m // tiles_per_group, 0, 0)),
                  row(W_C), row(W_C), row(W_C), pl.BlockSpec((128, 128), lambda m: (0, 0)),
                  full(wa), full(wb), full(wc), full(wo)],
        out_specs=tok(D),
        out_shape=jax.ShapeDtypeStruct((T, D), F32),
        compiler_params=_cp("parallel"),
        name="merge_out",
    )(x2, ya, yb, y, r, k, v, gg, z2, z2, z2, gt, lng, lnb, rk, _block_ones(DH_C), wa, wb, wc, wo)


def _ffn_kernel(x_ref, g_ref, sc_ref, sh_ref, gt_ref, w1_ref, w2_ref, o_ref, h_scr, acc_scr):
    f = pl.program_id(1)

    @pl.when(f == 0)
    def _():
        x = x_ref[...]
        y = x * lax.rsqrt(jnp.mean(x * x, axis=-1, keepdims=True) + EPS) * g_ref[...]
        h_scr[...] = (y * (1.0 + sc_ref[...]) + sh_ref[...]).astype(BF16)
        acc_scr[...] = jnp.zeros_like(acc_scr)

    u = jnp.maximum(jnp.dot(h_scr[...], w1_ref[...], preferred_element_type=F32), 0.0)
    acc_scr[...] += jnp.dot((u * u).astype(BF16), w2_ref[...], preferred_element_type=F32)

    @pl.when(f == pl.num_programs(1) - 1)
    def _():
        o_ref[...] = x_ref[...] + gt_ref[...] * acc_scr[...]


def ffn(x2, gain_row, sc, sh, gt, w1, w2, tm, tf):
    T, D = x2.shape
    F = w1.shape[1]
    G, R, _ = sc.shape
    tiles_per_group = (T // tm) // G
    mod_spec = pl.BlockSpec((None, R, D), lambda m, f: (m // tiles_per_group, 0, 0))
    return pl.pallas_call(
        _ffn_kernel,
        grid=(T // tm, F // tf),
        in_specs=[pl.BlockSpec((tm, D), lambda m, f: (m, 0)),
                  pl.BlockSpec((1, D), lambda m, f: (0, 0)),
                  mod_spec, mod_spec, mod_spec,
                  pl.BlockSpec((D, tf), lambda m, f: (0, f)),
                  pl.BlockSpec((tf, D), lambda m, f: (f, 0))],
        out_specs=pl.BlockSpec((tm, D), lambda m, f: (m, 0)),
        out_shape=jax.ShapeDtypeStruct((T, D), F32),
        scratch_shapes=[pltpu.VMEM((tm, D), BF16), pltpu.VMEM((tm, D), F32)],
        compiler_params=_cp("parallel", "arbitrary"),
        name="ffn",
    )(x2, gain_row, sc, sh, gt, w1, w2)


def prep_w_in(w_in):
    depth, D, _ = w_in.shape
    wt = jnp.transpose(w_in, (0, 2, 1))
    o_diff, o_rwkv, o_gate = RET_COLS, RET_COLS + DIFF_COLS, RET_COLS + DIFF_COLS + RWKV_COLS
    return jnp.concatenate([
        wt[:, :RET_COLS], wt[:, o_rwkv:o_gate], jnp.zeros((depth, RWKV_PAD - RWKV_COLS, D), F32),
        wt[:, o_gate:], wt[:, o_diff:o_rwkv]], axis=1).astype(BF16)


def _prep_layer(p, l, w_in_p):
    place = lambda w, slot: jnp.pad(w, ((0, 0), (slot * W_C, (2 - slot) * W_C)))
    w_lr = jnp.concatenate([place(p['w_decay_up'][l], 0), place(p['w_a_up'][l], 1), place(p['w_g_up'][l], 2),
                            jnp.zeros((LR_COLS - R_DECAY - R_A - R_G, 3 * W_C), F32)], axis=0)
    row = lambda t: t.reshape(1, -1)
    lp = dict(
        layer=l, w_ada=p['w_ada'], b_ada=row(p['b_ada'][l]),
        norm1=row(p['norm1'][l]), norm2=row(p['norm2'][l]), w_in=w_in_p,
        gq=row(jnp.tile(p['qk_norm_q'][l], 2 * H_B)), gk=row(jnp.tile(p['qk_norm_k'][l], 2 * H_B)),
        lam_rows=jnp.stack([p['lambda_q1'][l], p['lambda_k1'][l], p['lambda_q2'][l], p['lambda_k2'][l]]),
        subln=row(p['subln_diff'][l]),
        mu=row(jnp.pad(p['mu_shift'][l], (0, RWKV_PAD - RWKV_COLS))),
        w0=row(p['w0'][l]), a0=row(p['a0'][l]), k_k=row(p['k_k'][l]), k_a=row(p['k_a'][l]),
        w_lr=w_lr.astype(BF16), r_k=row(p['r_k'][l]), lnx_g=row(p['lnx_g'][l]), lnx_b=row(p['lnx_b'][l]),
        w_up_a=p['w_up_a'][l].astype(BF16), w_up_b=p['w_up_b'][l].astype(BF16),
        w_up_c=p['w_up_c'][l].astype(BF16), w_out=p['w_out'][l].astype(BF16),
        w_ff1=p['w_ff1'][l].astype(BF16), w_ff2=p['w_ff2'][l].astype(BF16),
        lam_init=0.8 - 0.6 * math.exp(-0.3 * l),
    )
    if l > 0:
        lp['v0'] = row(p['v0'][l - 1])
        lp['w_v_down'] = jnp.pad(p['w_v_down'][l - 1], ((0, 0), (0, 128 - R_V))).astype(BF16)
        lp['w_v_up'] = jnp.pad(p['w_v_up'][l - 1], ((0, 128 - R_V), (0, 0))).astype(BF16)
    return lp


def _modulation(c, lp, per_token):
    B, D = c.shape
    rows = -(-B // 16) * 16
    mod = ada_mod(jnp.pad(c, ((0, rows - B), (0, 0))), lp['w_ada'], lp['b_ada'], lp['layer'])[:B]
    parts = [mod[:, i * D:(i + 1) * D] for i in range(6)]
    shape = (1, B, D) if per_token else (B, 1, D)
    return [t.reshape(shape) for t in parts]


def _trunk_prompt(x, c, layers, bias_p):
    B, L, D = x.shape
    T = B * L
    tm = min(1024, L)
    t_attn = min(512, L)
    x2 = x.reshape(T, D)
    cos_t, sin_t = _rope_tables(jnp.arange(L))
    ret_out, rwkv_out, shift_out = [], [], []
    k_st = jnp.zeros((len(layers), T * H_B, DV_B), F32)
    v_st = jnp.zeros((len(layers), T * H_B, DV_B), F32)
    v_first = None
    for l, lp in enumerate(layers):
        sh1, sc1, gt1, sh2, sc2, gt2 = _modulation(c, lp, per_token=False)
        z2 = in_proj(x2, lp['norm1'], sc1, sh1, lp['w_in'], l, min(1024, L), 2176)
        ZC = z2.shape[1]
        z3 = z2.reshape(B, L, ZC)
        ya, s_ret = retention_prompt(z3, cos_t, sin_t, min(256, L))
        qb, k_st, kb, v_st, vb = qk_norm(z2, lp['gq'], lp['gk'], tm, DH_B ** -0.5 * math.log2(math.e),
                                         l, k_st, v_st)
        r3 = lambda t: t.reshape(B, L, W_B)
        yb = diff_attention_prompt(r3(qb), r3(kb), r3(vb), bias_p, lp['lam_rows'], lp['subln'],
                                   lp['lam_init'], t_attn)
        vgate = None if l == 0 else (lp['v0'], lp['w_v_down'], lp['w_v_up'], v_first)
        shift0 = jnp.zeros((B, 1, RWKV_PAD), F32)
        r, lw, k, v, a, b, gg = rwkv_prep(z3, shift0, lp['mu'], lp['w0'], lp['a0'], lp['k_k'], lp['k_a'],
                                          lp['w_lr'], vgate, min(256, L), carry_shift=True)
        if l == 0:
            v_first = v
        y, s_pair = rwkv_chunk_scan(r, lw, k, v, a, b, min(256, L), B)
        shift_out.append(z3[:, L - 1:, Z_RWKV:Z_RWKV + RWKV_COLS])
        f2 = lambda t: t.reshape(T, -1)
        x2 = merge_out(x2, f2(ya), f2(yb), f2(y), f2(r), f2(k), f2(v), f2(gg), z2, gt1, lp['lnx_g'],
                       lp['lnx_b'], lp['r_k'], lp['w_up_a'], lp['w_up_b'], lp['w_up_c'], lp['w_out'],
                       min(512, L))
        x2 = ffn(x2, lp['norm2'], sc2, sh2, gt2, lp['w_ff1'], lp['w_ff2'], tm, 1024)
        ret_out.append(s_ret)
        sp = s_pair.reshape(B, H_C // 2, 2, DH_C, 2, DH_C)
        rwkv_out.append(jnp.stack([sp[:, :, 0, :, 0, :], sp[:, :, 1, :, 1, :]], axis=2)
                        .reshape(B, H_C, DH_C, DH_C))
    kv_shape = (len(layers), B, L, H_B, DV_B)
    return (x2.reshape(B, L, D), k_st.reshape(kv_shape), v_st.reshape(kv_shape), jnp.stack(ret_out),
            jnp.stack(rwkv_out), jnp.stack(shift_out))


def _trunk_decode(x, c, layers, bias_d, state_ret, state_rwkv, state_shift, cache_k, cache_v, page_table):
    B, _, D = x.shape
    n_pages = page_table.shape[1]
    past = n_pages * PAGE_SIZE
    x2 = x.reshape(B, D)
    pt_flat = page_table.reshape(-1)
    ck = cache_k.reshape(cache_k.shape[0], cache_k.shape[1], PAGE_SIZE * H_B, 2 * DH_B)
    cv = cache_v.reshape(cache_v.shape[0], cache_v.shape[1], PAGE_SIZE * H_B, DV_B)
    state_t = jnp.transpose(state_rwkv, (0, 2, 3, 4, 1))
    shift_out = []
    ret_st = jnp.zeros(state_ret.shape, F32)
    rwkv_st = jnp.zeros(state_t.shape, F32)
    k_st = jnp.zeros((len(layers), B * H_B, DV_B), F32)
    v_st = jnp.zeros((len(layers), B * H_B, DV_B), F32)
    v_first = None
    for l, lp in enumerate(layers):
        sh1, sc1, gt1, sh2, sc2, gt2 = _modulation(c, lp, per_token=True)
        z2 = in_proj(x2, lp['norm1'], sc1, sh1, lp['w_in'], l, B, 512)
        ZC = z2.shape[1]
        ya, ret_st = retention_step(z2, state_ret, l, past, ret_st)
        qb, k_st, kb, v_st, vb = qk_norm(z2, lp['gq'], lp['gk'], B, DH_B ** -0.5, l, k_st, v_st)
        h3 = lambda t: t[l].reshape(B, H_B, DV_B)
        yb = diff_attention_decode(qb.reshape(B, 1, W_BQK), h3(k_st), h3(v_st), ck, cv, pt_flat, bias_d,
                                   lp['lam_rows'], lp['subln'], l, lp['lam_init'], n_pages).reshape(B, W_B)
        vgate = None if l == 0 else (lp['v0'], lp['w_v_down'], lp['w_v_up'], v_first)
        prev = jnp.pad(state_shift[l].reshape(1, B, RWKV_COLS), ((0, 0), (0, 0), (0, RWKV_PAD - RWKV_COLS)))
        r, lw, k, v, a, b, gg = rwkv_prep(z2.reshape(1, B, ZC), prev, lp['mu'], lp['w0'], lp['a0'], lp['k_k'],
                                          lp['k_a'], lp['w_lr'], vgate, B, carry_shift=False)
        if l == 0:
            v_first = v
        f2 = lambda t: t.reshape(B, -1)
        y_dec, rwkv_st = rwkv_step(state_t, l, f2(lw), f2(a), f2(b), f2(k), f2(r), f2(v), rwkv_st)
        shift_out.append(z2[:, Z_RWKV:Z_RWKV + RWKV_COLS].reshape(B, 1, RWKV_COLS))
        x2 = merge_out(x2, ya, yb, y_dec, f2(r), f2(k), f2(v), f2(gg), z2, gt1, lp['lnx_g'], lp['lnx_b'],
                       lp['r_k'], lp['w_up_a'], lp['w_up_b'], lp['w_up_c'], lp['w_out'], B)
        x2 = ffn(x2, lp['norm2'], sc2, sh2, gt2, lp['w_ff1'], lp['w_ff2'], B, 1024)
    rwkv_state = jnp.transpose(rwkv_st, (0, 4, 1, 2, 3))
    kv_shape = (len(layers), B, 1, H_B, DV_B)
    return (x2.reshape(B, 1, D), k_st.reshape(kv_shape), v_st.reshape(kv_shape), ret_st,
            rwkv_state, jnp.stack(shift_out))


def _decode_buckets(n_pages):
    past = n_pages * PAGE_SIZE
    key = np.arange((n_pages + 1) * PAGE_SIZE)
    bk = np.where(key <= past, _bucket_np(np.maximum(past - key, 0)), -1)
    return np.broadcast_to(bk.reshape(n_pages + 1, 1, PAGE_SIZE), (n_pages + 1, 8, PAGE_SIZE)).astype(np.int32)


def kernel(x_prompt, x_sample, c_prompt, c_sample, cache_k_diff, cache_v_diff, page_table, state_ret, state_rwkv, state_shift, rel_bias, w_ada, b_ada, norm1, norm2, w_in, qk_norm_q, qk_norm_k, lambda_q1, lambda_k1, lambda_q2, lambda_k2, subln_diff, mu_shift, w0, w_decay_up, a0, w_a_up, w_g_up, v0, w_v_down, w_v_up, k_k, k_a, r_k, lnx_g, lnx_b, w_up_a, w_up_b, w_up_c, w_out, w_ff1, w_ff2):
    p = dict(w_ada=w_ada, b_ada=b_ada, norm1=norm1, norm2=norm2, w_in=w_in,
             qk_norm_q=qk_norm_q, qk_norm_k=qk_norm_k, lambda_q1=lambda_q1, lambda_k1=lambda_k1,
             lambda_q2=lambda_q2, lambda_k2=lambda_k2, subln_diff=subln_diff, mu_shift=mu_shift,
             w0=w0, w_decay_up=w_decay_up, a0=a0, w_a_up=w_a_up, w_g_up=w_g_up, v0=v0,
             w_v_down=w_v_down, w_v_up=w_v_up, k_k=k_k, k_a=k_a, r_k=r_k, lnx_g=lnx_g, lnx_b=lnx_b,
             w_up_a=w_up_a, w_up_b=w_up_b, w_up_c=w_up_c, w_out=w_out, w_ff1=w_ff1, w_ff2=w_ff2)
    depth = w_in.shape[0]
    w_in_p = prep_w_in(w_in)
    layers = [_prep_layer(p, l, w_in_p) for l in range(depth)]
    L = x_prompt.shape[1]
    n_pages = page_table.shape[1]
    t_attn = min(512, L)
    bias_p = bias_tiles(rel_bias, _prompt_buckets(t_attn), True)
    bd = bias_tiles(rel_bias, _decode_buckets(n_pages).reshape((n_pages + 1) * 8, PAGE_SIZE), False)
    bias_d = bd.reshape(H_B, n_pages + 1, 8, PAGE_SIZE)

    y_p, k_p, v_p, ret_p, rwkv_p, shift_p = _trunk_prompt(x_prompt, c_prompt, layers, bias_p)
    y_s, k_s, v_s, ret_s, rwkv_s, shift_s = _trunk_decode(
        x_sample, c_sample, layers, bias_d, state_ret, state_rwkv, state_shift,
        cache_k_diff, cache_v_diff, page_table)
    return (y_p, y_s, k_p, v_p, k_s, v_s, ret_p, ret_s, rwkv_p, rwkv_s, shift_p, shift_s)
```

```python
import functools
import math

import numpy as np
import jax
import jax.numpy as jnp
from jax import lax
from jax.experimental import pallas as pl
from jax.experimental.pallas import tpu as pltpu

F32 = jnp.float32
BF16 = jnp.bfloat16

H_A, DK_A, DV_A = 4, 128, 128
RET_CHUNK = 128
ROPE_BASE = 10000.0
H_B, DH_B, DV_B = 4, 64, 128
N_BUCKETS, MAX_DISTANCE = 32, 128
H_C, DH_C = 8, 64
R_DECAY, R_A, R_V, R_G = 64, 64, 32, 160
LNX_EPS = 64e-5
EPS = 1e-6
PAGE_SIZE = 128
W_AQ, W_A = H_A * DK_A, H_A * DV_A
W_BQK, W_B = H_B * 2 * DH_B, H_B * DV_B
W_C = H_C * DH_C
RET_COLS = 2 * W_AQ + 2 * W_A
DIFF_COLS = 2 * W_BQK + W_B
RWKV_COLS = 3 * W_C + R_DECAY + R_A + R_G
RWKV_PAD = 2048
LR_COLS = RWKV_PAD - 3 * W_C

Z_RET, Z_RWKV = 0, RET_COLS
Z_GATE = Z_RWKV + RWKV_PAD
NEG = -1e30
RW_CHUNK = 64
VMEM_LIMIT = 56 * 1024 * 1024


def _cp(*sem):
    return pltpu.CompilerParams(dimension_semantics=sem, vmem_limit_bytes=VMEM_LIMIT)


def _silu(x):
    return x * jax.nn.sigmoid(x)


def _split2(x):
    hi = x.astype(BF16)
    lo = (x - hi.astype(F32)).astype(BF16)
    return hi, lo


def _seg_sum(x, ones):
    return jnp.dot(x.astype(BF16), ones, preferred_element_type=F32)


def _seg_sum_wide(x, ones2):
    return jnp.concatenate(
        [_seg_sum(x[:, c * 128:(c + 1) * 128], ones2) for c in range(x.shape[1] // 128)], axis=1)


def _mod_kernel(c_ref, w_ref, b_ref, o_ref):
    s = _silu(c_ref[...])
    o_ref[...] = jnp.dot(s.astype(BF16), w_ref[...].astype(BF16), preferred_element_type=F32) + b_ref[...]


def ada_mod(c_pad, w_stack, b_row, layer):
    R, D = c_pad.shape
    N = w_stack.shape[2]
    tn = 1536
    return pl.pallas_call(
        _mod_kernel,
        grid=(N // tn,),
        in_specs=[pl.BlockSpec((R, D), lambda n: (0, 0)),
                  pl.BlockSpec((None, D, tn), lambda n: (layer, 0, n)),
                  pl.BlockSpec((1, tn), lambda n: (0, n))],
        out_specs=pl.BlockSpec((R, tn), lambda n: (0, n)),
        out_shape=jax.ShapeDtypeStruct((R, N), F32),
        compiler_params=_cp("parallel"),
        name="ada_mod",
    )(c_pad, w_stack, b_row)


def _inproj_kernel(x_ref, g_ref, sc_ref, sh_ref, w_ref, z_ref, h_scr):
    @pl.when(pl.program_id(1) == 0)
    def _():
        x = x_ref[...]
        y = x * lax.rsqrt(jnp.mean(x * x, axis=-1, keepdims=True) + EPS) * g_ref[...]
        h_scr[...] = (y * (1.0 + sc_ref[...]) + sh_ref[...]).astype(BF16)

    z_ref[...] = lax.dot_general(h_scr[...], w_ref[...], (((1,), (1,)), ((), ())),
                                 preferred_element_type=F32)


def in_proj(x2, gain_row, sc, sh, w_bf, layer, tm, tn):
    T, D = x2.shape
    N = w_bf.shape[1]
    G, R, _ = sc.shape
    tiles_per_group = (T // tm) // G
    mod_spec = pl.BlockSpec((None, R, D), lambda m, n: (m // tiles_per_group, 0, 0))
    return pl.pallas_call(
        _inproj_kernel,
        grid=(T // tm, N // tn),
        in_specs=[pl.BlockSpec((tm, D), lambda m, n: (m, 0)),
                  pl.BlockSpec((1, D), lambda m, n: (0, 0)),
                  mod_spec, mod_spec,
                  pl.BlockSpec((None, tn, D), lambda m, n: (layer, n, 0))],
        out_specs=pl.BlockSpec((tm, tn), lambda m, n: (m, n)),
        out_shape=jax.ShapeDtypeStruct((T, N), F32),
        scratch_shapes=[pltpu.VMEM((tm, D), BF16)],
        compiler_params=_cp("parallel", "arbitrary"),
        name="in_proj",
    )(x2, gain_row, sc, sh, w_bf)


def _rope(x, cos, sin_signed):
    return x * cos + pltpu.roll(x, DK_A // 2, 1) * sin_signed


def _ret_kernel(q_ref, k_ref, v_ref, g_ref, cos_ref, sin_ref, inner_ref, cross_ref, tail_ref, cd_ref,
                ya_ref, so_ref, s_scr, *, n_sub):
    lt = pl.program_id(1)

    @pl.when(lt == 0)
    def _():
        s_scr[...] = jnp.zeros_like(s_scr)

    C = RET_CHUNK
    heads = range(H_A)
    cols = [slice(h * DK_A, (h + 1) * DK_A) for h in heads]
    dot = lambda x, y: jnp.dot(x, y, preferred_element_type=F32)
    for c in range(n_sub):
        rows = slice(c * C, (c + 1) * C)
        cos = cos_ref[rows, :]
        sin = sin_ref[rows, :]
        qb, kb, vb, kt = [], [], [], []
        for h in heads:
            q = _rope(q_ref[rows, cols[h]], cos, sin)
            k = _rope(k_ref[rows, cols[h]], cos, sin) * (DK_A ** -0.5)
            qb.append(q.astype(BF16))
            kb.append(k.astype(BF16))
            vb.append(v_ref[rows, cols[h]].astype(BF16))
            kt.append((k * tail_ref[h]).T.astype(BF16))
        S = [s_scr[h] for h in heads]
        sc = [lax.dot_general(qb[h], kb[h], (((1,), (1,)), ((), ())), preferred_element_type=F32) * inner_ref[h]
              for h in heads]
        qs = [dot(qb[h], S[h].astype(BF16)) * cross_ref[h] for h in heads]
        kv = [dot(kt[h], vb[h]) for h in heads]
        o = [dot(sc[h].astype(BF16), vb[h]) + qs[h] for h in heads]
        for h in heads:
            s_scr[h] = S[h] * cd_ref[h] + kv[h]
            on = o[h] * lax.rsqrt(jnp.mean(o[h] * o[h], axis=-1, keepdims=True) + EPS)
            ya_ref[rows, cols[h]] = (on * _silu(g_ref[rows, cols[h]])).astype(ya_ref.dtype)

    @pl.when(lt == pl.num_programs(1) - 1)
    def _():
        so_ref[...] = s_scr[...]


def _ret_tables(L):
    C = math.gcd(L, RET_CHUNK)
    log_g = np.log1p(-np.exp2(-5.0 - np.arange(H_A, dtype=np.float32))).astype(np.float32)
    i = np.arange(C, dtype=np.float32)
    dist = i[:, None] - i[None, :]
    causal = dist >= 0
    inner = np.where(causal[None], np.exp(np.where(causal, dist, 0.0)[None] * log_g[:, None, None]), 0.0)
    cross = np.exp((i[None, :] + 1.0) * log_g[:, None])
    tail = np.exp((C - 1.0 - i)[None, :] * log_g[:, None])
    chunk = np.exp(C * log_g)
    bc = lambda t: np.broadcast_to(t[:, :, None], (H_A, C, 128)).astype(np.float32)
    cd = np.broadcast_to(chunk[:, None, None], (H_A, 1, 128)).astype(np.float32)
    return inner.astype(np.float32), bc(cross), bc(tail), cd


def _rope_tables(pos):
    half = DK_A // 2
    inv = ROPE_BASE ** (-jnp.arange(half, dtype=F32) / half)
    ang = pos.astype(F32)[:, None] * inv[None, :]
    cos, sin = jnp.cos(ang), jnp.sin(ang)
    return jnp.concatenate([cos, cos], axis=-1), jnp.concatenate([-sin, sin], axis=-1)


def retention_prompt(z3, cos_t, sin_t, tb):
    B, L, ZC = z3.shape
    inner, cross, tail, cd = _ret_tables(L)
    zspec = lambda idx: pl.BlockSpec((None, tb, W_AQ), lambda b, l: (b, l, idx))
    full3 = lambda s: pl.BlockSpec(s, lambda b, l: (0, 0, 0))
    return pl.pallas_call(
        functools.partial(_ret_kernel, n_sub=tb // RET_CHUNK),
        grid=(B, L // tb),
        in_specs=[zspec(0), zspec(1), zspec(2), zspec(3),
                  pl.BlockSpec((tb, 128), lambda b, l: (l, 0)),
                  pl.BlockSpec((tb, 128), lambda b, l: (l, 0)),
                  full3(inner.shape), full3(cross.shape), full3(tail.shape), full3(cd.shape)],
        out_specs=[pl.BlockSpec((None, tb, W_A), lambda b, l: (b, l, 0)),
                   pl.BlockSpec((None, H_A, DK_A, DV_A), lambda b, l: (b, 0, 0, 0))],
        out_shape=[jax.ShapeDtypeStruct((B, L, W_A), BF16),
                   jax.ShapeDtypeStruct((B, H_A, DK_A, DV_A), F32)],
        scratch_shapes=[pltpu.VMEM((H_A, DK_A, DV_A), F32)],
        compiler_params=_cp("parallel", "arbitrary"),
        name="retention_prompt",
    )(z3, z3, z3, z3, cos_t, sin_t, inner, cross, tail, cd)


def _ret_step_kernel(q_ref, k_ref, v_ref, g_ref, cos_ref, sin_ref, gam_ref, s_ref, stack_ref, ya_ref, so_ref,
                     *, bb):
    del stack_ref
    cos, sin = cos_ref[...], sin_ref[...]
    row = lax.broadcasted_iota(jnp.int32, (bb, 128), 0)
    for h in range(H_A):
        cols = slice(h * DK_A, (h + 1) * DK_A)
        gam = gam_ref[h]
        q = _rope(q_ref[:, cols], cos, sin)
        k = _rope(k_ref[:, cols], cos, sin) * (DK_A ** -0.5)
        v = v_ref[:, cols]
        qb = q.astype(BF16)
        qk = jnp.sum(qb.astype(F32) * k.astype(BF16).astype(F32), axis=-1, keepdims=True)
        qs = jnp.zeros((bb, 128), F32)
        for b in range(bb):
            S = s_ref[b, h]
            onehot = row == b
            qs = jnp.where(onehot, jnp.dot(qb, S.astype(BF16), preferred_element_type=F32), qs)
            kb_t = jnp.where(onehot, k, 0.0).T.astype(BF16)
            so_ref[b, h] = S * gam + jnp.dot(kb_t, v.astype(BF16), preferred_element_type=F32)
        o = qk * v.astype(BF16).astype(F32) + qs * gam
        on = o * lax.rsqrt(jnp.mean(o * o, axis=-1, keepdims=True) + EPS)
        ya_ref[:, cols] = on * _silu(g_ref[:, cols])


def retention_step(z2, state, layer, pos, new_state):
    B = z2.shape[0]
    bb = 8
    cos_t, sin_t = _rope_tables(jnp.full((1,), pos))
    log_g = np.log1p(-np.exp2(-5.0 - np.arange(H_A, dtype=np.float32))).astype(np.float32)
    gam = np.broadcast_to(np.exp(log_g)[:, None, None], (H_A, 1, 128)).astype(np.float32)
    zspec = lambda idx: pl.BlockSpec((bb, W_AQ), lambda i: (i, idx))
    return pl.pallas_call(
        functools.partial(_ret_step_kernel, bb=bb),
        grid=(B // bb,),
        in_specs=[zspec(0), zspec(1), zspec(2), zspec(3),
                  pl.BlockSpec((1, 128), lambda i: (0, 0)),
                  pl.BlockSpec((1, 128), lambda i: (0, 0)),
                  pl.BlockSpec((H_A, 1, 128), lambda i: (0, 0, 0)),
                  pl.BlockSpec((None, bb, H_A, DK_A, DV_A), lambda i: (layer, i, 0, 0, 0)),
                  pl.BlockSpec(memory_space=pl.ANY)],
        out_specs=[pl.BlockSpec((bb, W_A), lambda i: (i, 0)),
                   pl.BlockSpec((None, bb, H_A, DK_A, DV_A), lambda i: (layer, i, 0, 0, 0))],
        out_shape=[jax.ShapeDtypeStruct((B, W_A), F32),
                   jax.ShapeDtypeStruct(new_state.shape, F32)],
        input_output_aliases={8: 1},
        compiler_params=_cp("parallel"),
        name="retention_step",
    )(z2, z2, z2, z2, cos_t, sin_t, gam, state, new_state)


def _qknorm_kernel(q_ref, k_ref, v_ref, gq_ref, gk_ref, ones_ref, k_stack_ref, v_stack_ref,
                   qb_ref, k32_ref, kb_ref, v32_ref, vb_ref, *, q_scale):
    del k_stack_ref, v_stack_ref
    ones2 = ones_ref[...]
    q, k, v = q_ref[...], k_ref[...], v_ref[...]
    tm = q.shape[0]
    qn = q * lax.rsqrt(_seg_sum_wide(q * q, ones2) * (1.0 / DH_B) + EPS) * gq_ref[...] * q_scale
    kn = k * lax.rsqrt(_seg_sum_wide(k * k, ones2) * (1.0 / DH_B) + EPS) * gk_ref[...]
    qb_ref[...] = qn.astype(BF16)
    kb_ref[...] = kn.astype(BF16)
    vb_ref[...] = v.astype(BF16)
    for h in range(H_B):
        head_rows = pl.ds(h, tm, stride=H_B)
        k32_ref[head_rows, :] = kn[:, h * DV_B:(h + 1) * DV_B]
        v32_ref[head_rows, :] = v[:, h * DV_B:(h + 1) * DV_B]


def _block_ones(group):
    i = np.arange(128)
    return jnp.asarray((i[:, None] // group == i[None, :] // group).astype(np.float32), BF16)


def qk_norm(z2, gq_row, gk_row, tm, q_scale, layer, k_stack, v_stack):
    T = z2.shape[0]
    base = (Z_GATE + 3 * 1024) // W_BQK
    zspec = lambda idx: pl.BlockSpec((tm, W_BQK), lambda m: (m, base + idx))
    row = pl.BlockSpec((1, W_BQK), lambda m: (0, 0))
    out = pl.BlockSpec((tm, W_BQK), lambda m: (m, 0))
    out_rows = pl.BlockSpec((None, tm * H_B, DV_B), lambda m: (layer, m, 0))
    stack = pl.BlockSpec(memory_space=pl.ANY)
    return pl.pallas_call(
        functools.partial(_qknorm_kernel, q_scale=q_scale),
        grid=(T // tm,),
        in_specs=[zspec(0), zspec(1), zspec(2), row, row, pl.BlockSpec((128, 128), lambda m: (0, 0)),
                  stack, stack],
        out_specs=[out, out_rows, out, out_rows, out],
        out_shape=[jax.ShapeDtypeStruct((T, W_BQK), BF16), jax.ShapeDtypeStruct(k_stack.shape, F32),
                   jax.ShapeDtypeStruct((T, W_BQK), BF16), jax.ShapeDtypeStruct(v_stack.shape, F32),
                   jax.ShapeDtypeStruct((T, W_B), BF16)],
        input_output_aliases={6: 1, 7: 3},
        compiler_params=_cp("parallel"),
        name="qk_norm",
    )(z2, z2, z2, gq_row, gk_row, _block_ones(DH_B), k_stack, v_stack)


def _bucket_np(n):
    max_exact = N_BUCKETS // 2
    nf = np.maximum(n, 1).astype(np.float32)
    large = max_exact + (np.log(nf / np.float32(max_exact)) / np.float32(math.log(MAX_DISTANCE / max_exact))
                         * np.float32(N_BUCKETS - max_exact)).astype(np.int32)
    large = np.minimum(large, N_BUCKETS - 1)
    return np.where(n < max_exact, n, large).astype(np.int32)


def _bias_kernel(tab_ref, bkt_ref, o_ref, *, log2_far_shift):
    h = pl.program_id(0)
    bk = bkt_ref[...]
    acc = jnp.zeros(bk.shape, F32)
    for b in range(N_BUCKETS):
        acc = jnp.where(bk == b, tab_ref[b, h], acc)
    if log2_far_shift:
        acc = (acc - tab_ref[N_BUCKETS - 1, h]) * math.log2(math.e)
    o_ref[...] = jnp.where(bk < 0, NEG, acc)


def bias_tiles(rel_bias, buckets, log2_far_shift):
    R, C = buckets.shape
    tr = min(R, 512)
    return pl.pallas_call(
        functools.partial(_bias_kernel, log2_far_shift=log2_far_shift),
        grid=(H_B, R // tr),
        in_specs=[pl.BlockSpec(memory_space=pltpu.SMEM),
                  pl.BlockSpec((tr, C), lambda h, r: (r, 0))],
        out_specs=pl.BlockSpec((None, tr, C), lambda h, r: (h, r, 0)),
        out_shape=jax.ShapeDtypeStruct((H_B, R, C), F32),
        compiler_params=_cp("parallel", "parallel"),
        name="bias_tiles",
    )(rel_bias, jnp.asarray(buckets))


def _prompt_buckets(t):
    r = np.arange(t)[:, None]
    c = np.arange(t)[None, :]
    diag = np.where(c <= r, _bucket_np(np.maximum(r - c, 0)), -1)
    off1 = _bucket_np(t + r - c)
    assert t >= MAX_DISTANCE
    return np.concatenate([diag, off1], axis=0).astype(np.int32)


def _lambda_full(lam_ref, lam_init):
    lv = lam_ref[...]
    s1 = jnp.sum(lv[0:1] * lv[1:2], axis=-1, keepdims=True)
    s2 = jnp.sum(lv[2:3] * lv[3:4], axis=-1, keepdims=True)
    return jnp.exp(s1) - jnp.exp(s2) + lam_init


def _flash_kernel(q_ref, k_ref, v_ref, bias_ref, lam_ref, gain_ref, o_ref, qq_scr, m_scr, acc_scr, s_scr,
                  *, lam_init):
    i = pl.program_id(2)
    t = q_ref.shape[0]

    q = q_ref[...]
    first = lax.broadcasted_iota(jnp.int32, q.shape, 1) < DH_B
    qq_scr[:t] = jnp.where(first, q, jnp.zeros_like(q))
    qq_scr[t:] = jnp.where(first, jnp.zeros_like(q), q)
    m_scr[...] = jnp.full(m_scr.shape, NEG, F32)
    acc_scr[...] = jnp.zeros_like(acc_scr)

    def raw_scores(comp, j):
        krows = pl.ds(pl.multiple_of(j * t, t), t)
        return lax.dot_general(qq_scr[comp * t:(comp + 1) * t], k_ref[krows, :], (((1,), (1,)), ((), ())),
                               preferred_element_type=F32)

    def accumulate(comp, s, v1):
        rows = slice(comp * t, (comp + 1) * t)
        m_old = m_scr[rows]
        m_new = jnp.maximum(m_old, jnp.max(s, axis=-1, keepdims=True))
        alpha = jnp.exp2(m_old - m_new)
        p = jnp.concatenate([jnp.exp2(s[:, c * 128:(c + 1) * 128] - m_new) for c in range(t // 128)],
                            axis=1).astype(BF16)
        acc_scr[rows] = (jnp.concatenate([alpha, alpha], axis=1) * acc_scr[rows]
                         + jnp.dot(p, v1, preferred_element_type=F32))
        m_scr[rows] = m_new

    def kv_block(j, tile, has_next):
        krows = pl.ds(pl.multiple_of(j * t, t), t)
        v1 = jnp.concatenate([v_ref[krows, :], jnp.ones((t, DV_B), BF16)], axis=1)
        with_bias = (lambda s: s) if tile is None else (lambda s: s + bias_ref[tile * t:(tile + 1) * t, :])
        s_first = s_scr[...]
        s_second = raw_scores(1, j)
        accumulate(0, with_bias(s_first), v1)
        if has_next:
            s_scr[...] = raw_scores(0, j + 1)
        accumulate(1, with_bias(s_second), v1)

    s_scr[...] = raw_scores(0, 0)

    def far_pair(jj, carry):
        kv_block(2 * jj, None, True)
        kv_block(2 * jj + 1, None, True)
        return carry

    n_far = jnp.maximum(i - 1, 0)
    lax.fori_loop(0, n_far // 2, far_pair, 0)

    @pl.when(jnp.logical_and(i >= 1, n_far % 2 == 1))
    def _():
        kv_block(i - 2, None, True)
        kv_block(i - 1, 1, True)
        kv_block(i, 0, False)

    @pl.when(jnp.logical_and(i >= 1, n_far % 2 == 0))
    def _():
        kv_block(i - 1, 1, True)
        kv_block(i, 0, False)

    @pl.when(i == 0)
    def _():
        kv_block(i, 0, False)

    lam = _lambda_full(lam_ref, lam_init)
    acc = acc_scr[...]
    ob = acc[:t, :DV_B] / acc[:t, DV_B:] - lam * (acc[t:, :DV_B] / acc[t:, DV_B:])
    on = ob * lax.rsqrt(jnp.mean(ob * ob, axis=-1, keepdims=True) + EPS) * gain_ref[...]
    o_ref[...] = (on * (1.0 - lam_init)).astype(o_ref.dtype)


def diff_attention_prompt(qb, kb, vb, bias, lam_rows, gain_row, lam_init, t):
    B, L, _ = qb.shape
    n = L // t
    kv_spec = pl.BlockSpec((None, L, DV_B), lambda b, h, i: (b, 0, h))
    return pl.pallas_call(
        functools.partial(_flash_kernel, lam_init=lam_init),
        grid=(B, H_B, n),
        in_specs=[pl.BlockSpec((None, t, DV_B), lambda b, h, i: (b, i, h)),
                  kv_spec, kv_spec,
                  pl.BlockSpec((None, 2 * t, t), lambda b, h, i: (h, 0, 0)),
                  pl.BlockSpec((4, DH_B), lambda b, h, i: (0, 0)),
                  pl.BlockSpec((1, DV_B), lambda b, h, i: (0, 0))],
        out_specs=pl.BlockSpec((None, t, DV_B), lambda b, h, i: (b, i, h)),
        out_shape=jax.ShapeDtypeStruct((B, L, W_B), BF16),
        scratch_shapes=[pltpu.VMEM((2 * t, DV_B), BF16), pltpu.VMEM((2 * t, 128), F32),
                        pltpu.VMEM((2 * t, 2 * DV_B), F32), pltpu.VMEM((t, t), F32)],
        compiler_params=_cp("parallel", "parallel", "arbitrary"),
        name="diff_attention_prompt",
    )(qb, kb, vb, bias, lam_rows, gain_row)


def _decode_kernel(pt_ref, q_ref, kn_ref, vn_ref, bias_ref, lam_ref, gain_ref, *rest, n_pages, lam_init, spb):
    n = spb * n_pages
    k_all, v_all, o_ref = rest[:n], rest[n:2 * n], rest[2 * n]
    lam = _lambda_full(lam_ref, lam_init)
    gain = gain_ref[...]
    r8 = lax.broadcasted_iota(jnp.int32, (8, 2 * DH_B), 0)
    l8 = lax.broadcasted_iota(jnp.int32, (8, 2 * DH_B), 1)
    comp_rows = (l8 // DH_B) == r8
    row0 = lax.broadcasted_iota(jnp.int32, (PAGE_SIZE, DV_B), 0) == 0
    nt = lambda x, y: lax.dot_general(x, y, (((1,), (1,)), ((), ())), preferred_element_type=F32)
    units = [(s, h) for s in range(spb) for h in range(H_B)]
    cols = [slice(h * DV_B, (h + 1) * DV_B) for h in range(H_B)]
    head_rows = [pl.ds(h, PAGE_SIZE, stride=H_B) for h in range(H_B)]
    pages = lambda refs, s: refs[s * n_pages:(s + 1) * n_pages]
    ss = []
    for s, h in units:
        qrows = jnp.where(comp_rows, q_ref[s, :, cols[h]].astype(F32), 0.0).astype(BF16)
        ks = [r[head_rows[h], :].astype(BF16) for r in pages(k_all, s)]
        ks.append(jnp.where(row0, kn_ref[s, h:h + 1, :], 0.0).astype(BF16))
        ss.append([nt(qrows, kp) + bias_ref[h, p] for p, kp in enumerate(ks)])
    ps, ls = [], []
    for sc in ss:
        m = functools.reduce(jnp.maximum, [jnp.max(x, axis=-1, keepdims=True) for x in sc])
        p = [jnp.exp(x - m) for x in sc]
        ls.append(functools.reduce(jnp.add, [jnp.sum(x, axis=-1, keepdims=True) for x in p]))
        ps.append([x.astype(BF16) for x in p])
    for (s, h), p_u, l_u in zip(units, ps, ls):
        vs = [r[head_rows[h], :].astype(BF16) for r in pages(v_all, s)]
        vs.append(jnp.where(row0, vn_ref[s, h:h + 1, :], 0.0).astype(BF16))
        acc = functools.reduce(jnp.add, [jnp.dot(p, vp, preferred_element_type=F32)
                                         for p, vp in zip(p_u, vs)])
        outn = acc / l_u
        oh = outn[0:1] - lam * outn[1:2]
        on = oh * lax.rsqrt(jnp.mean(oh * oh, axis=-1, keepdims=True) + EPS) * gain
        o_ref[s, :, cols[h]] = on * (1.0 - lam_init)


def diff_attention_decode(qb, k32, v32, cache_k, cache_v, pt_flat, bias_dec, lam_rows, gain_row, layer,
                          lam_init, n_pages):
    B = qb.shape[0]
    spb = 2 if B % 2 == 0 else 1
    rowspec = pl.BlockSpec((spb, 1, W_BQK), lambda g, pt: (g, 0, 0))
    headspec = pl.BlockSpec((spb, H_B, DV_B), lambda g, pt: (g, 0, 0))
    const = lambda s: pl.BlockSpec(s, lambda g, pt: tuple(0 for _ in s))
    page_spec = lambda s, p: pl.BlockSpec(
        (None, None, PAGE_SIZE * H_B, DV_B),
        lambda g, pt, s=s, p=p: (layer, pt[(g * spb + s) * n_pages + p], 0, 0))
    page_specs = [page_spec(s, p) for s in range(spb) for p in range(n_pages)]
    grid_spec = pltpu.PrefetchScalarGridSpec(
        num_scalar_prefetch=1,
        grid=(B // spb,),
        in_specs=[rowspec, headspec, headspec, const(bias_dec.shape), const((4, DH_B)), const((1, DV_B))]
                 + page_specs * 2,
        out_specs=pl.BlockSpec((spb, 1, W_B), lambda g, pt: (g, 0, 0)),
    )
    return pl.pallas_call(
        functools.partial(_decode_kernel, n_pages=n_pages, lam_init=lam_init, spb=spb),
        grid_spec=grid_spec,
        out_shape=jax.ShapeDtypeStruct((B, 1, W_B), F32),
        compiler_params=_cp("arbitrary"),
        name="diff_attention_decode",
    )(pt_flat, qb, k32, v32, bias_dec, lam_rows, gain_row,
      *([cache_k] * (spb * n_pages)), *([cache_v] * (spb * n_pages)))


def _rwkv_prep_kernel(*refs, carry_shift, gate_v):
    it = iter(refs)
    z_ref, prev_ref, mu_ref, w0_ref, a0_ref, kk_ref, ka_ref, wlr_ref, ones_ref = (next(it) for _ in range(9))
    if gate_v:
        v0_ref, wvd_ref, wvu_ref, vf_ref = (next(it) for _ in range(4))
    r_ref, lw_ref, k_ref, v_ref, a_ref, b_ref, gg_ref = (next(it) for _ in range(7))
    z = z_ref[...]
    if carry_shift:
        carry_scr = next(it)

        @pl.when(pl.program_id(1) == 0)
        def _():
            carry_scr[...] = prev_ref[...]

        first = lax.broadcasted_iota(jnp.int32, z.shape, 0) == 0
        zprev = jnp.where(first, carry_scr[...], pltpu.roll(z, 1, 0))
        carry_scr[...] = z_ref[z.shape[0] - 1:z.shape[0], :]
    else:
        zprev = prev_ref[...]
    zs = z + (zprev - z) * mu_ref[...]
    rc, kc, vc = zs[:, :W_C], zs[:, W_C:2 * W_C], zs[:, 2 * W_C:3 * W_C]
    lr = zs[:, 3 * W_C:]
    head = lr[:, :R_DECAY + R_A]
    col = lax.broadcasted_iota(jnp.int32, head.shape, 1)
    act = jnp.concatenate([jnp.where(col < R_DECAY, jnp.tanh(head), head),
                           jax.nn.sigmoid(lr[:, R_DECAY + R_A:])], axis=1)
    up = jnp.dot(act.astype(BF16), wlr_ref[...], preferred_element_type=F32)
    y = -(w0_ref[...] + up[:, :W_C])
    softplus = jnp.maximum(y, 0.0) + jnp.log(1.0 + jnp.exp(-jnp.abs(y)))
    lw_ref[...] = -jnp.exp(-softplus - 0.5)
    if gate_v:
        down = jnp.dot(vc.astype(BF16), wvd_ref[...], preferred_element_type=F32)
        vgate = jax.nn.sigmoid(v0_ref[...] + jnp.dot(down.astype(BF16), wvu_ref[...], preferred_element_type=F32))
        vc = vc + (vf_ref[...].astype(F32) - vc) * vgate
    a = jax.nn.sigmoid(a0_ref[...] + up[:, W_C:2 * W_C])
    gg_ref[...] = up[:, 2 * W_C:].astype(gg_ref.dtype)
    kk = kc * kk_ref[...]
    norm = jnp.sqrt(_seg_sum_wide(kk * kk, ones_ref[...]))
    kk = kk / jnp.maximum(norm, 1e-12)
    r_ref[...] = rc.astype(r_ref.dtype)
    k_ref[...] = (kc * (1.0 + (a - 1.0) * ka_ref[...])).astype(k_ref.dtype)
    v_ref[...] = vc.astype(v_ref.dtype)
    a_ref[...] = (-kk).astype(a_ref.dtype)
    b_ref[...] = (kk * a).astype(b_ref.dtype)


def rwkv_prep(z3, prev, mu_row, w0, a0, k_k, k_a, w_lr, vgate, tm, carry_shift):
    G, Lg, _ = z3.shape
    blk = lambda w, idx: pl.BlockSpec((None, tm, w), lambda g, l: (g, l, idx))
    row = lambda w: pl.BlockSpec((1, w), lambda g, l: (0, 0))
    prev_spec = (pl.BlockSpec((None, 1, RWKV_PAD), lambda g, l: (g, 0, 0)) if carry_shift
                 else blk(RWKV_PAD, 0))
    in_specs = [blk(RWKV_PAD, Z_RWKV // RWKV_PAD), prev_spec, row(RWKV_PAD), row(W_C), row(W_C), row(W_C),
                row(W_C), pl.BlockSpec(w_lr.shape, lambda g, l: (0, 0)),
                pl.BlockSpec((128, 128), lambda g, l: (0, 0))]
    args = [z3, prev, mu_row, w0, a0, k_k, k_a, w_lr, _block_ones(DH_C)]
    if vgate is not None:
        v0, wvd, wvu, vfirst = vgate
        in_specs += [row(W_C), pl.BlockSpec(wvd.shape, lambda g, l: (0, 0)),
                     pl.BlockSpec(wvu.shape, lambda g, l: (0, 0)), blk(W_C, 0)]
        args += [v0, wvd, wvu, vfirst]
    return pl.pallas_call(
        functools.partial(_rwkv_prep_kernel, carry_shift=carry_shift, gate_v=vgate is not None),
        grid=(G, Lg // tm),
        in_specs=in_specs,
        out_specs=[blk(W_C, 0)] * 7,
        out_shape=[jax.ShapeDtypeStruct((G, Lg, W_C), F32 if i == 1 else BF16) for i in range(7)],
        scratch_shapes=[pltpu.VMEM((1, RWKV_PAD), F32)] if carry_shift else [],
        compiler_params=_cp("parallel", "arbitrary"),
        name="rwkv_prep",
    )(*args)


def _rwkv_chunk_kernel(r_ref, lw_ref, k_ref, v_ref, a_ref, b_ref, tri_ref, msl_ref, mli_ref, lvl_ref,
                       y_ref, so_ref, s_scr, *, n_chunks, nb):
    C = RW_CHUNK

    @pl.when(pl.program_id(1) == 0)
    def _():
        s_scr[...] = jnp.zeros_like(s_scr)

    head0 = lax.broadcasted_iota(jnp.int32, (C, 128), 1) < DH_C
    tri3, msl, mli = tri_ref[...], msl_ref[...], mli_ref[...]
    ri = lax.broadcasted_iota(jnp.int32, (128, 128), 0)
    ci = lax.broadcasted_iota(jnp.int32, (128, 128), 1)
    eye = (ri == ci).astype(F32)
    n_levels = lvl_ref.shape[0]
    chains = [(bi, p) for bi in range(nb) for p in range(H_C // 2)]

    def stack(x):
        return jnp.concatenate([jnp.where(head0, x, 0.0), jnp.where(head0, 0.0, x)], axis=0)

    def nt(x, y):
        return lax.dot_general(x, y, (((1,), (1,)), ((), ())), preferred_element_type=F32)

    def dot(x, y):
        return jnp.dot(x, y, preferred_element_type=F32)

    def chunk(ci_, carry):
        rows = pl.ds(pl.multiple_of(ci_ * C, C), C)
        ld = lambda ref, bi, p: ref[bi, rows, p * 128:(p + 1) * 128].astype(F32)
        lams = []
        for bi, p in chains:
            lw = ld(lw_ref, bi, p)
            hi = lw.astype(BF16)
            mid = (lw - hi.astype(F32)).astype(BF16)
            lo = (lw - hi.astype(F32) - mid.astype(F32)).astype(BF16)
            lams.append(dot(tri3, jnp.concatenate([hi, mid, lo], axis=0)))
        AR, BK, Vs, KB, dec = [], [], [], [], []
        for (bi, p), lam in zip(chains, lams):
            r, lw, k = ld(r_ref, bi, p), ld(lw_ref, bi, p), ld(k_ref, bi, p)
            v, a, b = ld(v_ref, bi, p), ld(a_ref, bi, p), ld(b_ref, bi, p)
            lam_c = lam[C - 1:C, :]
            e_neg = jnp.exp(-lam)
            e_tail = jnp.exp(lam_c - lam)
            AR.append(jnp.concatenate([stack(a * jnp.exp(lam - lw)), stack(r * jnp.exp(lam))],
                                      axis=0).astype(BF16))
            BK.append(jnp.concatenate([stack(b * e_neg), stack(k * e_neg)], axis=0).astype(BF16))
            KB.append(jnp.concatenate([stack(k * e_tail), stack(b * e_tail)], axis=0).astype(BF16))
            Vs.append(stack(v))
            dec.append(jnp.exp(lam_c))
        G = [nt(x, y) for x, y in zip(AR, BK)]
        I0 = [nt(x, s_scr[bi, p].astype(BF16)) for x, (bi, p) in zip(AR, chains)]
        rhs = [i0[:128] + dot((g[:128, 128:] * msl).astype(BF16), v.astype(BF16))
               for g, i0, v in zip(G, I0, Vs)]
        Nh, Nl, D = [], [], []
        for g in G:
            h, l = _split2(g[:128, :128] * msl)
            Nh.append(h)
            Nl.append(l)
            D.append(eye + (h * lvl_ref[0]).astype(F32) + (l * lvl_ref[0]).astype(F32))
        for lv in range(1, n_levels):
            m = lvl_ref[lv]
            Db = [d.astype(BF16) for d in D]
            X = [dot(h * m, db) for h, db in zip(Nh, Db)]
            D = [d + dot(db, x.astype(BF16)) for d, db, x in zip(D, Db, X)]
        Us = [dot(d.astype(BF16), x.astype(BF16)) for d, x in zip(D, rhs)]
        for (bi, p), g, i0, u, v in zip(chains, G, I0, Us, Vs):
            ys = i0[128:] + dot(jnp.concatenate([g[128:, :128] * mli, g[128:, 128:] * mli], axis=1).astype(BF16),
                                jnp.concatenate([u, v], axis=0).astype(BF16))
            y_ref[bi, rows, p * 128:(p + 1) * 128] = ys[:C] + ys[C:]
        for (bi, p), u, v, kb, d in zip(chains, Us, Vs, KB, dec):
            vu = jnp.concatenate([v, u], axis=0)
            s_scr[bi, p] = s_scr[bi, p] * d + dot(vu.T.astype(BF16), kb)
        return carry

    lax.fori_loop(0, n_chunks, chunk, 0)

    @pl.when(pl.program_id(1) == pl.num_programs(1) - 1)
    def _():
        so_ref[...] = s_scr[...]


def _rwkv_chunk_tables():
    C = RW_CHUNK
    t = np.arange(C)
    tri = (t[:, None] >= t[None, :]).astype(np.float32)
    tri3 = np.concatenate([tri, tri, tri], axis=1)
    i = np.arange(2 * C)
    same = (i[:, None] // C) == (i[None, :] // C)
    msl = (same & (i[:, None] > i[None, :])).astype(np.float32)
    mli = (same & (i[:, None] >= i[None, :])).astype(np.float32)
    levels = []
    n = 1
    while n < C:
        levels.append(same & ((i[:, None] // (2 * n)) == (i[None, :] // (2 * n)))
                      & ((i[:, None] // n) % 2 == 1) & ((i[None, :] // n) % 2 == 0))
        n *= 2
    return jnp.asarray(tri3, BF16), msl, mli, jnp.asarray(np.stack(levels).astype(np.float32), BF16)


def rwkv_chunk_scan(r, lw, k, v, a, b, tb, nb):
    B, L, _ = r.shape
    tri3, msl, mli, lvl = _rwkv_chunk_tables()
    blk = pl.BlockSpec((nb, tb, W_C), lambda bi, l: (bi, l, 0))
    c2 = lambda s: pl.BlockSpec(s, lambda bi, l: tuple(0 for _ in s))
    return pl.pallas_call(
        functools.partial(_rwkv_chunk_kernel, n_chunks=tb // RW_CHUNK, nb=nb),
        grid=(B // nb, L // tb),
        in_specs=[blk] * 6 + [c2(tri3.shape), c2(msl.shape), c2(mli.shape), c2(lvl.shape)],
        out_specs=[blk, pl.BlockSpec((nb, H_C // 2, 128, 128), lambda bi, l: (bi, 0, 0, 0))],
        out_shape=[jax.ShapeDtypeStruct((B, L, W_C), F32),
                   jax.ShapeDtypeStruct((B, H_C // 2, 128, 128), F32)],
        scratch_shapes=[pltpu.VMEM((nb, H_C // 2, 128, 128), F32)],
        compiler_params=_cp("parallel", "arbitrary"),
        name="rwkv_chunk_scan",
    )(r, lw, k, v, a, b, tri3, msl, mli, lvl)


def _rwkv_step_kernel(s_ref, w_ref, a_ref, b_ref, k_ref, r_ref, v_ref, stack_ref, y_ref, so_ref, vt_scr, y_scr):
    del stack_ref
    tr = lambda ref: ref[...].astype(F32).T
    wT, aT, bT, kT, rT = jnp.exp(tr(w_ref)), tr(a_ref), tr(b_ref), tr(k_ref), tr(r_ref)
    vt_scr[...] = tr(v_ref)
    for hh in range(2):
        ch = slice(hh * DH_C, (hh + 1) * DH_C)
        w, a, b, k, r = wT[ch], aT[ch], bT[ch], kT[ch], rT[ch]

        def value_row(i, carry):
            row = pl.ds(hh * DH_C + i, 1)
            S = s_ref[hh, i]
            sa = jnp.sum(S * a, axis=0, keepdims=True)
            Sn = S * w + sa * b + vt_scr[row, :] * k
            so_ref[hh, i] = Sn
            y_scr[row, :] = jnp.sum(Sn * r, axis=0, keepdims=True)
            return carry

        lax.fori_loop(0, DH_C, value_row, 0)
    y_ref[...] = y_scr[...].T


def rwkv_step(state_t, layer, lw, a, b, k, r, v, new_state_t):
    B = lw.shape[0]
    vec = pl.BlockSpec((B, 2 * DH_C), lambda p: (0, p))
    sspec = pl.BlockSpec((None, 2, DH_C, DH_C, B), lambda p: (layer, p, 0, 0, 0))
    return pl.pallas_call(
        _rwkv_step_kernel,
        grid=(H_C // 2,),
        in_specs=[sspec] + [vec] * 6 + [pl.BlockSpec(memory_space=pl.ANY)],
        out_specs=[vec, sspec],
        out_shape=[jax.ShapeDtypeStruct((B, W_C), F32),
                   jax.ShapeDtypeStruct(new_state_t.shape, F32)],
        scratch_shapes=[pltpu.VMEM((2 * DH_C, B), F32), pltpu.VMEM((2 * DH_C, B), F32)],
        input_output_aliases={7: 1},
        compiler_params=_cp("parallel"),
        name="rwkv_step",
    )(state_t, lw, a, b, k, r, v, new_state_t)


def _merge_kernel(x_ref, ya_ref, yb_ref, y_ref, r_ref, k_ref, v_ref, gg_ref, za_ref, zb_ref, zc_ref, gt_ref,
                  lng_ref, lnb_ref, rk_ref, ones_ref, wa_ref, wb_ref, wc_ref, wo_ref, o_ref):
    ones2 = ones_ref[...]
    y = y_ref[...]
    mu = _seg_sum_wide(y, ones2) * (1.0 / DH_C)
    d = y - mu
    var = _seg_sum_wide(d * d, ones2) * (1.0 / DH_C)
    ycn = d * lax.rsqrt(var + LNX_EPS) * lng_ref[...] + lnb_ref[...]
    f32 = lambda ref: ref[...].astype(F32)
    bonus = _seg_sum_wide(f32(r_ref) * f32(k_ref) * rk_ref[...], ones2) * f32(v_ref)
    yc = (ycn + bonus) * f32(gg_ref)
    proj = lambda t, w: jnp.dot(t.astype(BF16), w[...], preferred_element_type=F32)
    merged = (jax.nn.sigmoid(za_ref[...]) * proj(ya_ref[...], wa_ref)
              + jax.nn.sigmoid(zb_ref[...]) * proj(yb_ref[...], wb_ref)
              + jax.nn.sigmoid(zc_ref[...]) * proj(yc, wc_ref))
    o_ref[...] = x_ref[...] + gt_ref[...] * proj(merged, wo_ref)


def merge_out(x2, ya, yb, y, r, k, v, gg, z2, gt, lng, lnb, rk, wa, wb, wc, wo, tm):
    T, D = x2.shape
    G, R, _ = gt.shape
    tiles_per_group = (T // tm) // G
    tok = lambda w: pl.BlockSpec((tm, w), lambda m: (m, 0))
    gate = lambda idx: pl.BlockSpec((tm, D), lambda m: (m, Z_GATE // D + idx))
    row = lambda w: pl.BlockSpec((1, w), lambda m: (0, 0))
    full = lambda a: pl.BlockSpec(a.shape, lambda m: (0, 0))
    return pl.pallas_call(
        _merge_kernel,
        grid=(T // tm,),
        in_specs=[tok(D)] + [tok(W_C)] * 7 + [gate(0), gate(1), gate(2),
                  pl.BlockSpec((None, R, D), lambda m: (m // tiles_per_group, 0, 0)),
                  row(W_C), row(W_C), row(W_C), pl.BlockSpec((128, 128), lambda m: (0, 0)),
                  full(wa), full(wb), full(wc), full(wo)],
        out_specs=tok(D),
        out_shape=jax.ShapeDtypeStruct((T, D), F32),
        compiler_params=_cp("parallel"),
        name="merge_out",
    )(x2, ya, yb, y, r, k, v, gg, z2, z2, z2, gt, lng, lnb, rk, _block_ones(DH_C), wa, wb, wc, wo)


def _ffn_kernel(x_ref, g_ref, sc_ref, sh_ref, gt_ref, w1_ref, w2_ref, o_ref, h_scr, acc_scr):
    f = pl.program_id(1)

    @pl.when(f == 0)
    def _():
        x = x_ref[...]
        y = x * lax.rsqrt(jnp.mean(x * x, axis=-1, keepdims=True) + EPS) * g_ref[...]
        h_scr[...] = (y * (1.0 + sc_ref[...]) + sh_ref[...]).astype(BF16)
        acc_scr[...] = jnp.zeros_like(acc_scr)

    u = jnp.maximum(jnp.dot(h_scr[...], w1_ref[...], preferred_element_type=F32), 0.0)
    acc_scr[...] += jnp.dot((u * u).astype(BF16), w2_ref[...], preferred_element_type=F32)

    @pl.when(f == pl.num_programs(1) - 1)
    def _():
        o_ref[...] = x_ref[...] + gt_ref[...] * acc_scr[...]


def ffn(x2, gain_row, sc, sh, gt, w1, w2, tm, tf):
    T, D = x2.shape
    F = w1.shape[1]
    G, R, _ = sc.shape
    tiles_per_group = (T // tm) // G
    mod_spec = pl.BlockSpec((None, R, D), lambda m, f: (m // tiles_per_group, 0, 0))
    return pl.pallas_call(
        _ffn_kernel,
        grid=(T // tm, F // tf),
        in_specs=[pl.BlockSpec((tm, D), lambda m, f: (m, 0)),
                  pl.BlockSpec((1, D), lambda m, f: (0, 0)),
                  mod_spec, mod_spec, mod_spec,
                  pl.BlockSpec((D, tf), lambda m, f: (0, f)),
                  pl.BlockSpec((tf, D), lambda m, f: (f, 0))],
        out_specs=pl.BlockSpec((tm, D), lambda m, f: (m, 0)),
        out_shape=jax.ShapeDtypeStruct((T, D), F32),
        scratch_shapes=[pltpu.VMEM((tm, D), BF16), pltpu.VMEM((tm, D), F32)],
        compiler_params=_cp("parallel", "arbitrary"),
        name="ffn",
    )(x2, gain_row, sc, sh, gt, w1, w2)


def prep_w_in(w_in):
    depth, D, _ = w_in.shape
    wt = jnp.transpose(w_in, (0, 2, 1))
    o_diff, o_rwkv, o_gate = RET_COLS, RET_COLS + DIFF_COLS, RET_COLS + DIFF_COLS + RWKV_COLS
    return jnp.concatenate([
        wt[:, :RET_COLS], wt[:, o_rwkv:o_gate], jnp.zeros((depth, RWKV_PAD - RWKV_COLS, D), F32),
        wt[:, o_gate:], wt[:, o_diff:o_rwkv]], axis=1).astype(BF16)


def _prep_layer(p, l, w_in_p):
    place = lambda w, slot: jnp.pad(w, ((0, 0), (slot * W_C, (2 - slot) * W_C)))
    w_lr = jnp.concatenate([place(p['w_decay_up'][l], 0), place(p['w_a_up'][l], 1), place(p['w_g_up'][l], 2),
                            jnp.zeros((LR_COLS - R_DECAY - R_A - R_G, 3 * W_C), F32)], axis=0)
    row = lambda t: t.reshape(1, -1)
    lp = dict(
        layer=l, w_ada=p['w_ada'], b_ada=row(p['b_ada'][l]),
        norm1=row(p['norm1'][l]), norm2=row(p['norm2'][l]), w_in=w_in_p,
        gq=row(jnp.tile(p['qk_norm_q'][l], 2 * H_B)), gk=row(jnp.tile(p['qk_norm_k'][l], 2 * H_B)),
        lam_rows=jnp.stack([p['lambda_q1'][l], p['lambda_k1'][l], p['lambda_q2'][l], p['lambda_k2'][l]]),
        subln=row(p['subln_diff'][l]),
        mu=row(jnp.pad(p['mu_shift'][l], (0, RWKV_PAD - RWKV_COLS))),
        w0=row(p['w0'][l]), a0=row(p['a0'][l]), k_k=row(p['k_k'][l]), k_a=row(p['k_a'][l]),
        w_lr=w_lr.astype(BF16), r_k=row(p['r_k'][l]), lnx_g=row(p['lnx_g'][l]), lnx_b=row(p['lnx_b'][l]),
        w_up_a=p['w_up_a'][l].astype(BF16), w_up_b=p['w_up_b'][l].astype(BF16),
        w_up_c=p['w_up_c'][l].astype(BF16), w_out=p['w_out'][l].astype(BF16),
        w_ff1=p['w_ff1'][l].astype(BF16), w_ff2=p['w_ff2'][l].astype(BF16),
        lam_init=0.8 - 0.6 * math.exp(-0.3 * l),
    )
    if l > 0:
        lp['v0'] = row(p['v0'][l - 1])
        lp['w_v_down'] = jnp.pad(p['w_v_down'][l - 1], ((0, 0), (0, 128 - R_V))).astype(BF16)
        lp['w_v_up'] = jnp.pad(p['w_v_up'][l - 1], ((0, 128 - R_V), (0, 0))).astype(BF16)
    return lp


def _modulation(c, lp, per_token):
    B, D = c.shape
    start = 0 if not per_token else lp['mod'].shape[0] - B
    mod = lp['mod'][start:start + B]
    parts = [mod[:, i * D:(i + 1) * D] for i in range(6)]
    shape = (1, B, D) if per_token else (B, 1, D)
    return [t.reshape(shape) for t in parts]


def _trunk_prompt(x, c, layers, bias_p):
    B, L, D = x.shape
    T = B * L
    tm = min(1024, L)
    t_attn = min(512, L)
    x2 = x.reshape(T, D)
    cos_t, sin_t = _rope_tables(jnp.arange(L))
    ret_out, rwkv_out, shift_out = [], [], []
    k_st = jnp.zeros((len(layers), T * H_B, DV_B), F32)
    v_st = jnp.zeros((len(layers), T * H_B, DV_B), F32)
    v_first = None
    for l, lp in enumerate(layers):
        sh1, sc1, gt1, sh2, sc2, gt2 = _modulation(c, lp, per_token=False)
        z2 = in_proj(x2, lp['norm1'], sc1, sh1, lp['w_in'], l, min(1024, L), 2176)
        ZC = z2.shape[1]
        z3 = z2.reshape(B, L, ZC)
        ya, s_ret = retention_prompt(z3, cos_t, sin_t, min(512, L))
        qb, k_st, kb, v_st, vb = qk_norm(z2, lp['gq'], lp['gk'], tm, DH_B ** -0.5 * math.log2(math.e),
                                         l, k_st, v_st)
        r3 = lambda t: t.reshape(B, L, W_B)
        yb = diff_attention_prompt(r3(qb), r3(kb), r3(vb), bias_p, lp['lam_rows'], lp['subln'],
                                   lp['lam_init'], t_attn)
        vgate = None if l == 0 else (lp['v0'], lp['w_v_down'], lp['w_v_up'], v_first)
        shift0 = jnp.zeros((B, 1, RWKV_PAD), F32)
        r, lw, k, v, a, b, gg = rwkv_prep(z3, shift0, lp['mu'], lp['w0'], lp['a0'], lp['k_k'], lp['k_a'],
                                          lp['w_lr'], vgate, min(256, L), carry_shift=True)
        if l == 0:
            v_first = v
        y, s_pair = rwkv_chunk_scan(r, lw, k, v, a, b, min(256, L), B)
        shift_out.append(z3[:, L - 1:, Z_RWKV:Z_RWKV + RWKV_COLS])
        f2 = lambda t: t.reshape(T, -1)
        x2 = merge_out(x2, f2(ya), f2(yb), f2(y), f2(r), f2(k), f2(v), f2(gg), z2, gt1, lp['lnx_g'],
                       lp['lnx_b'], lp['r_k'], lp['w_up_a'], lp['w_up_b'], lp['w_up_c'], lp['w_out'],
                       min(512, L))
        x2 = ffn(x2, lp['norm2'], sc2, sh2, gt2, lp['w_ff1'], lp['w_ff2'], tm, 1024)
        ret_out.append(s_ret)
        sp = s_pair.reshape(B, H_C // 2, 2, DH_C, 2, DH_C)
        rwkv_out.append(jnp.stack([sp[:, :, 0, :, 0, :], sp[:, :, 1, :, 1, :]], axis=2)
                        .reshape(B, H_C, DH_C, DH_C))
    kv_shape = (len(layers), B, L, H_B, DV_B)
    return (x2.reshape(B, L, D), k_st.reshape(kv_shape), v_st.reshape(kv_shape), jnp.stack(ret_out),
            jnp.stack(rwkv_out), jnp.stack(shift_out))


def _trunk_decode(x, c, layers, bias_d, state_ret, state_rwkv, state_shift, cache_k, cache_v, page_table):
    B, _, D = x.shape
    n_pages = page_table.shape[1]
    past = n_pages * PAGE_SIZE
    x2 = x.reshape(B, D)
    pt_flat = page_table.reshape(-1)
    ck = cache_k.reshape(cache_k.shape[0], cache_k.shape[1], PAGE_SIZE * H_B, 2 * DH_B)
    cv = cache_v.reshape(cache_v.shape[0], cache_v.shape[1], PAGE_SIZE * H_B, DV_B)
    state_t = jnp.transpose(state_rwkv, (0, 2, 3, 4, 1))
    shift_out = []
    ret_st = jnp.zeros(state_ret.shape, F32)
    rwkv_st = jnp.zeros(state_t.shape, F32)
    k_st = jnp.zeros((len(layers), B * H_B, DV_B), F32)
    v_st = jnp.zeros((len(layers), B * H_B, DV_B), F32)
    v_first = None
    for l, lp in enumerate(layers):
        sh1, sc1, gt1, sh2, sc2, gt2 = _modulation(c, lp, per_token=True)
        z2 = in_proj(x2, lp['norm1'], sc1, sh1, lp['w_in'], l, B, 512)
        ZC = z2.shape[1]
        ya, ret_st = retention_step(z2, state_ret, l, past, ret_st)
        qb, k_st, kb, v_st, vb = qk_norm(z2, lp['gq'], lp['gk'], B, DH_B ** -0.5, l, k_st, v_st)
        h3 = lambda t: t[l].reshape(B, H_B, DV_B)
        yb = diff_attention_decode(qb.reshape(B, 1, W_BQK), h3(k_st), h3(v_st), ck, cv, pt_flat, bias_d,
                                   lp['lam_rows'], lp['subln'], l, lp['lam_init'], n_pages).reshape(B, W_B)
        vgate = None if l == 0 else (lp['v0'], lp['w_v_down'], lp['w_v_up'], v_first)
        prev = jnp.pad(state_shift[l].reshape(1, B, RWKV_COLS), ((0, 0), (0, 0), (0, RWKV_PAD - RWKV_COLS)))
        r, lw, k, v, a, b, gg = rwkv_prep(z2.reshape(1, B, ZC), prev, lp['mu'], lp['w0'], lp['a0'], lp['k_k'],
                                          lp['k_a'], lp['w_lr'], vgate, B, carry_shift=False)
        if l == 0:
            v_first = v
        f2 = lambda t: t.reshape(B, -1)
        y_dec, rwkv_st = rwkv_step(state_t, l, f2(lw), f2(a), f2(b), f2(k), f2(r), f2(v), rwkv_st)
        shift_out.append(z2[:, Z_RWKV:Z_RWKV + RWKV_COLS].reshape(B, 1, RWKV_COLS))
        x2 = merge_out(x2, ya, yb, y_dec, f2(r), f2(k), f2(v), f2(gg), z2, gt1, lp['lnx_g'], lp['lnx_b'],
                       lp['r_k'], lp['w_up_a'], lp['w_up_b'], lp['w_up_c'], lp['w_out'], B)
        x2 = ffn(x2, lp['norm2'], sc2, sh2, gt2, lp['w_ff1'], lp['w_ff2'], B, 1024)
    rwkv_state = jnp.transpose(rwkv_st, (0, 4, 1, 2, 3))
    kv_shape = (len(layers), B, 1, H_B, DV_B)
    return (x2.reshape(B, 1, D), k_st.reshape(kv_shape), v_st.reshape(kv_shape), ret_st,
            rwkv_state, jnp.stack(shift_out))


def _decode_buckets(n_pages):
    past = n_pages * PAGE_SIZE
    key = np.arange((n_pages + 1) * PAGE_SIZE)
    bk = np.where(key <= past, _bucket_np(np.maximum(past - key, 0)), -1)
    return np.broadcast_to(bk.reshape(n_pages + 1, 1, PAGE_SIZE), (n_pages + 1, 8, PAGE_SIZE)).astype(np.int32)


def kernel(x_prompt, x_sample, c_prompt, c_sample, cache_k_diff, cache_v_diff, page_table, state_ret, state_rwkv, state_shift, rel_bias, w_ada, b_ada, norm1, norm2, w_in, qk_norm_q, qk_norm_k, lambda_q1, lambda_k1, lambda_q2, lambda_k2, subln_diff, mu_shift, w0, w_decay_up, a0, w_a_up, w_g_up, v0, w_v_down, w_v_up, k_k, k_a, r_k, lnx_g, lnx_b, w_up_a, w_up_b, w_up_c, w_out, w_ff1, w_ff2):
    p = dict(w_ada=w_ada, b_ada=b_ada, norm1=norm1, norm2=norm2, w_in=w_in,
             qk_norm_q=qk_norm_q, qk_norm_k=qk_norm_k, lambda_q1=lambda_q1, lambda_k1=lambda_k1,
             lambda_q2=lambda_q2, lambda_k2=lambda_k2, subln_diff=subln_diff, mu_shift=mu_shift,
             w0=w0, w_decay_up=w_decay_up, a0=a0, w_a_up=w_a_up, w_g_up=w_g_up, v0=v0,
             w_v_down=w_v_down, w_v_up=w_v_up, k_k=k_k, k_a=k_a, r_k=r_k, lnx_g=lnx_g, lnx_b=lnx_b,
             w_up_a=w_up_a, w_up_b=w_up_b, w_up_c=w_up_c, w_out=w_out, w_ff1=w_ff1, w_ff2=w_ff2)
    depth = w_in.shape[0]
    w_in_p = prep_w_in(w_in)
    layers = [_prep_layer(p, l, w_in_p) for l in range(depth)]
    bp = c_prompt.shape[0]
    c_all = jnp.concatenate([jnp.pad(c_prompt, ((0, -bp % 16), (0, 0))), c_sample], axis=0)
    for lp in layers:
        lp['mod'] = ada_mod(c_all, lp['w_ada'], lp['b_ada'], lp['layer'])
    L = x_prompt.shape[1]
    n_pages = page_table.shape[1]
    t_attn = min(512, L)
    bias_p = bias_tiles(rel_bias, _prompt_buckets(t_attn), True)
    bd = bias_tiles(rel_bias, _decode_buckets(n_pages).reshape((n_pages + 1) * 8, PAGE_SIZE), False)
    bias_d = bd.reshape(H_B, n_pages + 1, 8, PAGE_SIZE)

    y_p, k_p, v_p, ret_p, rwkv_p, shift_p = _trunk_prompt(x_prompt, c_prompt, layers, bias_p)
    y_s, k_s, v_s, ret_s, rwkv_s, shift_s = _trunk_decode(
        x_sample, c_sample, layers, bias_d, state_ret, state_rwkv, state_shift,
        cache_k_diff, cache_v_diff, page_table)
    return (y_p, y_s, k_p, v_p, k_s, v_s, ret_p, ret_s, rwkv_p, rwkv_s, shift_p, shift_s)
```

```python
import functools
import math

import numpy as np
import jax
import jax.numpy as jnp
from jax import lax
from jax.experimental import pallas as pl
from jax.experimental.pallas import tpu as pltpu

F32 = jnp.float32
BF16 = jnp.bfloat16

H_A, DK_A, DV_A = 4, 128, 128
RET_CHUNK = 128
ROPE_BASE = 10000.0
H_B, DH_B, DV_B = 4, 64, 128
N_BUCKETS, MAX_DISTANCE = 32, 128
H_C, DH_C = 8, 64
R_DECAY, R_A, R_V, R_G = 64, 64, 32, 160
LNX_EPS = 64e-5
EPS = 1e-6
PAGE_SIZE = 128
W_AQ, W_A = H_A * DK_A, H_A * DV_A
W_BQK, W_B = H_B * 2 * DH_B, H_B * DV_B
W_C = H_C * DH_C
RET_COLS = 2 * W_AQ + 2 * W_A
DIFF_COLS = 2 * W_BQK + W_B
RWKV_COLS = 3 * W_C + R_DECAY + R_A + R_G
RWKV_PAD = 2048
LR_COLS = RWKV_PAD - 3 * W_C

Z_RET, Z_RWKV = 0, RET_COLS
Z_GATE = Z_RWKV + RWKV_PAD
NEG = -1e30
RW_CHUNK = 64
VMEM_LIMIT = 56 * 1024 * 1024


def _cp(*sem):
    return pltpu.CompilerParams(dimension_semantics=sem, vmem_limit_bytes=VMEM_LIMIT)


def _silu(x):
    return x * jax.nn.sigmoid(x)


def _split2(x):
    hi = x.astype(BF16)
    lo = (x - hi.astype(F32)).astype(BF16)
    return hi, lo


def _seg_sum(x, ones):
    return jnp.dot(x.astype(BF16), ones, preferred_element_type=F32)


def _seg_sum_wide(x, ones2):
    return jnp.concatenate(
        [_seg_sum(x[:, c * 128:(c + 1) * 128], ones2) for c in range(x.shape[1] // 128)], axis=1)


def _mod_kernel(c_ref, w_ref, b_ref, o_ref):
    s = _silu(c_ref[...])
    o_ref[...] = jnp.dot(s.astype(BF16), w_ref[...].astype(BF16), preferred_element_type=F32) + b_ref[...]


def ada_mod(c_pad, w_stack, b_row, layer):
    R, D = c_pad.shape
    N = w_stack.shape[2]
    tn = 1536
    return pl.pallas_call(
        _mod_kernel,
        grid=(N // tn,),
        in_specs=[pl.BlockSpec((R, D), lambda n: (0, 0)),
                  pl.BlockSpec((None, D, tn), lambda n: (layer, 0, n)),
                  pl.BlockSpec((1, tn), lambda n: (0, n))],
        out_specs=pl.BlockSpec((R, tn), lambda n: (0, n)),
        out_shape=jax.ShapeDtypeStruct((R, N), F32),
        compiler_params=_cp("parallel"),
        name="ada_mod",
    )(c_pad, w_stack, b_row)


def _inproj_kernel(x_ref, g_ref, sc_ref, sh_ref, w_ref, z_ref, h_scr):
    @pl.when(pl.program_id(1) == 0)
    def _():
        x = x_ref[...]
        y = x * lax.rsqrt(jnp.mean(x * x, axis=-1, keepdims=True) + EPS) * g_ref[...]
        h_scr[...] = (y * (1.0 + sc_ref[...]) + sh_ref[...]).astype(BF16)

    z_ref[...] = lax.dot_general(h_scr[...], w_ref[...], (((1,), (1,)), ((), ())),
                                 preferred_element_type=F32)


def in_proj(x2, gain_row, sc, sh, w_bf, layer, tm, tn):
    T, D = x2.shape
    N = w_bf.shape[1]
    G, R, _ = sc.shape
    tiles_per_group = (T // tm) // G
    mod_spec = pl.BlockSpec((None, R, D), lambda m, n: (m // tiles_per_group, 0, 0))
    return pl.pallas_call(
        _inproj_kernel,
        grid=(T // tm, N // tn),
        in_specs=[pl.BlockSpec((tm, D), lambda m, n: (m, 0)),
                  pl.BlockSpec((1, D), lambda m, n: (0, 0)),
                  mod_spec, mod_spec,
                  pl.BlockSpec((None, tn, D), lambda m, n: (layer, n, 0))],
        out_specs=pl.BlockSpec((tm, tn), lambda m, n: (m, n)),
        out_shape=jax.ShapeDtypeStruct((T, N), F32),
        scratch_shapes=[pltpu.VMEM((tm, D), BF16)],
        compiler_params=_cp("parallel", "arbitrary"),
        name="in_proj",
    )(x2, gain_row, sc, sh, w_bf)


def _rope(x, cos, sin_signed):
    return x * cos + pltpu.roll(x, DK_A // 2, 1) * sin_signed


def _ret_kernel(q_ref, k_ref, v_ref, g_ref, cos_ref, sin_ref, inner_ref, cross_ref, tail_ref, cd_ref,
                ya_ref, so_ref, s_scr, *, n_sub):
    lt = pl.program_id(1)

    @pl.when(lt == 0)
    def _():
        s_scr[...] = jnp.zeros_like(s_scr)

    C = RET_CHUNK
    heads = range(H_A)
    cols = [slice(h * DK_A, (h + 1) * DK_A) for h in heads]
    dot = lambda x, y: jnp.dot(x, y, preferred_element_type=F32)
    for c in range(n_sub):
        rows = slice(c * C, (c + 1) * C)
        cos = cos_ref[rows, :]
        sin = sin_ref[rows, :]
        qb, kb, vb, kt = [], [], [], []
        for h in heads:
            q = _rope(q_ref[rows, cols[h]], cos, sin)
            k = _rope(k_ref[rows, cols[h]], cos, sin) * (DK_A ** -0.5)
            qb.append(q.astype(BF16))
            kb.append(k.astype(BF16))
            vb.append(v_ref[rows, cols[h]].astype(BF16))
            kt.append((k * tail_ref[h]).T.astype(BF16))
        S = [s_scr[h] for h in heads]
        sc = [lax.dot_general(qb[h], kb[h], (((1,), (1,)), ((), ())), preferred_element_type=F32) * inner_ref[h]
              for h in heads]
        qs = [dot(qb[h], S[h].astype(BF16)) * cross_ref[h] for h in heads]
        kv = [dot(kt[h], vb[h]) for h in heads]
        o = [dot(sc[h].astype(BF16), vb[h]) + qs[h] for h in heads]
        for h in heads:
            s_scr[h] = S[h] * cd_ref[h] + kv[h]
            on = o[h] * lax.rsqrt(jnp.mean(o[h] * o[h], axis=-1, keepdims=True) + EPS)
            ya_ref[rows, cols[h]] = (on * _silu(g_ref[rows, cols[h]])).astype(ya_ref.dtype)

    @pl.when(lt == pl.num_programs(1) - 1)
    def _():
        so_ref[...] = s_scr[...]


def _ret_tables(L):
    C = math.gcd(L, RET_CHUNK)
    log_g = np.log1p(-np.exp2(-5.0 - np.arange(H_A, dtype=np.float32))).astype(np.float32)
    i = np.arange(C, dtype=np.float32)
    dist = i[:, None] - i[None, :]
    causal = dist >= 0
    inner = np.where(causal[None], np.exp(np.where(causal, dist, 0.0)[None] * log_g[:, None, None]), 0.0)
    cross = np.exp((i[None, :] + 1.0) * log_g[:, None])
    tail = np.exp((C - 1.0 - i)[None, :] * log_g[:, None])
    chunk = np.exp(C * log_g)
    bc = lambda t: np.broadcast_to(t[:, :, None], (H_A, C, 128)).astype(np.float32)
    cd = np.broadcast_to(chunk[:, None, None], (H_A, 1, 128)).astype(np.float32)
    return inner.astype(np.float32), bc(cross), bc(tail), cd


def _rope_tables(pos):
    half = DK_A // 2
    inv = ROPE_BASE ** (-jnp.arange(half, dtype=F32) / half)
    ang = pos.astype(F32)[:, None] * inv[None, :]
    cos, sin = jnp.cos(ang), jnp.sin(ang)
    return jnp.concatenate([cos, cos], axis=-1), jnp.concatenate([-sin, sin], axis=-1)


def retention_prompt(z3, cos_t, sin_t, tb):
    B, L, ZC = z3.shape
    inner, cross, tail, cd = _ret_tables(L)
    zspec = lambda idx: pl.BlockSpec((None, tb, W_AQ), lambda b, l: (b, l, idx))
    full3 = lambda s: pl.BlockSpec(s, lambda b, l: (0, 0, 0))
    return pl.pallas_call(
        functools.partial(_ret_kernel, n_sub=tb // RET_CHUNK),
        grid=(B, L // tb),
        in_specs=[zspec(0), zspec(1), zspec(2), zspec(3),
                  pl.BlockSpec((tb, 128), lambda b, l: (l, 0)),
                  pl.BlockSpec((tb, 128), lambda b, l: (l, 0)),
                  full3(inner.shape), full3(cross.shape), full3(tail.shape), full3(cd.shape)],
        out_specs=[pl.BlockSpec((None, tb, W_A), lambda b, l: (b, l, 0)),
                   pl.BlockSpec((None, H_A, DK_A, DV_A), lambda b, l: (b, 0, 0, 0))],
        out_shape=[jax.ShapeDtypeStruct((B, L, W_A), BF16),
                   jax.ShapeDtypeStruct((B, H_A, DK_A, DV_A), F32)],
        scratch_shapes=[pltpu.VMEM((H_A, DK_A, DV_A), F32)],
        compiler_params=_cp("parallel", "arbitrary"),
        name="retention_prompt",
    )(z3, z3, z3, z3, cos_t, sin_t, inner, cross, tail, cd)


def _ret_step_kernel(q_ref, k_ref, v_ref, g_ref, cos_ref, sin_ref, gam_ref, s_ref, stack_ref, ya_ref, so_ref,
                     *, bb):
    del stack_ref
    cos, sin = cos_ref[...], sin_ref[...]
    row = lax.broadcasted_iota(jnp.int32, (bb, 128), 0)
    for h in range(H_A):
        cols = slice(h * DK_A, (h + 1) * DK_A)
        gam = gam_ref[h]
        q = _rope(q_ref[:, cols], cos, sin)
        k = _rope(k_ref[:, cols], cos, sin) * (DK_A ** -0.5)
        v = v_ref[:, cols]
        qb = q.astype(BF16)
        qk = jnp.sum(qb.astype(F32) * k.astype(BF16).astype(F32), axis=-1, keepdims=True)
        qs = jnp.zeros((bb, 128), F32)
        for b in range(bb):
            S = s_ref[b, h]
            onehot = row == b
            qs = jnp.where(onehot, jnp.dot(qb, S.astype(BF16), preferred_element_type=F32), qs)
            kb_t = jnp.where(onehot, k, 0.0).T.astype(BF16)
            so_ref[b, h] = S * gam + jnp.dot(kb_t, v.astype(BF16), preferred_element_type=F32)
        o = qk * v.astype(BF16).astype(F32) + qs * gam
        on = o * lax.rsqrt(jnp.mean(o * o, axis=-1, keepdims=True) + EPS)
        ya_ref[:, cols] = on * _silu(g_ref[:, cols])


def retention_step(z2, state, layer, pos, new_state):
    B = z2.shape[0]
    bb = 8
    cos_t, sin_t = _rope_tables(jnp.full((1,), pos))
    log_g = np.log1p(-np.exp2(-5.0 - np.arange(H_A, dtype=np.float32))).astype(np.float32)
    gam = np.broadcast_to(np.exp(log_g)[:, None, None], (H_A, 1, 128)).astype(np.float32)
    zspec = lambda idx: pl.BlockSpec((bb, W_AQ), lambda i: (i, idx))
    return pl.pallas_call(
        functools.partial(_ret_step_kernel, bb=bb),
        grid=(B // bb,),
        in_specs=[zspec(0), zspec(1), zspec(2), zspec(3),
                  pl.BlockSpec((1, 128), lambda i: (0, 0)),
                  pl.BlockSpec((1, 128), lambda i: (0, 0)),
                  pl.BlockSpec((H_A, 1, 128), lambda i: (0, 0, 0)),
                  pl.BlockSpec((None, bb, H_A, DK_A, DV_A), lambda i: (layer, i, 0, 0, 0)),
                  pl.BlockSpec(memory_space=pl.ANY)],
        out_specs=[pl.BlockSpec((bb, W_A), lambda i: (i, 0)),
                   pl.BlockSpec((None, bb, H_A, DK_A, DV_A), lambda i: (layer, i, 0, 0, 0))],
        out_shape=[jax.ShapeDtypeStruct((B, W_A), F32),
                   jax.ShapeDtypeStruct(new_state.shape, F32)],
        input_output_aliases={8: 1},
        compiler_params=_cp("parallel"),
        name="retention_step",
    )(z2, z2, z2, z2, cos_t, sin_t, gam, state, new_state)


def _qknorm_kernel(q_ref, k_ref, v_ref, gq_ref, gk_ref, ones_ref, k_stack_ref, v_stack_ref,
                   qb_ref, k32_ref, kb_ref, v32_ref, vb_ref, *, q_scale):
    del k_stack_ref, v_stack_ref
    ones2 = ones_ref[...]
    q, k, v = q_ref[...], k_ref[...], v_ref[...]
    tm = q.shape[0]
    qn = q * lax.rsqrt(_seg_sum_wide(q * q, ones2) * (1.0 / DH_B) + EPS) * gq_ref[...] * q_scale
    kn = k * lax.rsqrt(_seg_sum_wide(k * k, ones2) * (1.0 / DH_B) + EPS) * gk_ref[...]
    qb_ref[...] = qn.astype(BF16)
    kb_ref[...] = kn.astype(BF16)
    vb_ref[...] = v.astype(BF16)
    for h in range(H_B):
        head_rows = pl.ds(h, tm, stride=H_B)
        k32_ref[head_rows, :] = kn[:, h * DV_B:(h + 1) * DV_B]
        v32_ref[head_rows, :] = v[:, h * DV_B:(h + 1) * DV_B]


def _block_ones(group):
    i = np.arange(128)
    return jnp.asarray((i[:, None] // group == i[None, :] // group).astype(np.float32), BF16)


def qk_norm(z2, gq_row, gk_row, tm, q_scale, layer, k_stack, v_stack):
    T = z2.shape[0]
    base = (Z_GATE + 3 * 1024) // W_BQK
    zspec = lambda idx: pl.BlockSpec((tm, W_BQK), lambda m: (m, base + idx))
    row = pl.BlockSpec((1, W_BQK), lambda m: (0, 0))
    out = pl.BlockSpec((tm, W_BQK), lambda m: (m, 0))
    out_rows = pl.BlockSpec((None, tm * H_B, DV_B), lambda m: (layer, m, 0))
    stack = pl.BlockSpec(memory_space=pl.ANY)
    return pl.pallas_call(
        functools.partial(_qknorm_kernel, q_scale=q_scale),
        grid=(T // tm,),
        in_specs=[zspec(0), zspec(1), zspec(2), row, row, pl.BlockSpec((128, 128), lambda m: (0, 0)),
                  stack, stack],
        out_specs=[out, out_rows, out, out_rows, out],
        out_shape=[jax.ShapeDtypeStruct((T, W_BQK), BF16), jax.ShapeDtypeStruct(k_stack.shape, F32),
                   jax.ShapeDtypeStruct((T, W_BQK), BF16), jax.ShapeDtypeStruct(v_stack.shape, F32),
                   jax.ShapeDtypeStruct((T, W_B), BF16)],
        input_output_aliases={6: 1, 7: 3},
        compiler_params=_cp("parallel"),
        name="qk_norm",
    )(z2, z2, z2, gq_row, gk_row, _block_ones(DH_B), k_stack, v_stack)


def _bucket_np(n):
    max_exact = N_BUCKETS // 2
    nf = np.maximum(n, 1).astype(np.float32)
    large = max_exact + (np.log(nf / np.float32(max_exact)) / np.float32(math.log(MAX_DISTANCE / max_exact))
                         * np.float32(N_BUCKETS - max_exact)).astype(np.int32)
    large = np.minimum(large, N_BUCKETS - 1)
    return np.where(n < max_exact, n, large).astype(np.int32)


def _bias_kernel(tab_ref, bkt_ref, o_ref, *, log2_far_shift):
    h = pl.program_id(0)
    bk = bkt_ref[...]
    acc = jnp.zeros(bk.shape, F32)
    for b in range(N_BUCKETS):
        acc = jnp.where(bk == b, tab_ref[b, h], acc)
    if log2_far_shift:
        acc = (acc - tab_ref[N_BUCKETS - 1, h]) * math.log2(math.e)
    o_ref[...] = jnp.where(bk < 0, NEG, acc)


def bias_tiles(rel_bias, buckets, log2_far_shift):
    R, C = buckets.shape
    tr = min(R, 512)
    return pl.pallas_call(
        functools.partial(_bias_kernel, log2_far_shift=log2_far_shift),
        grid=(H_B, R // tr),
        in_specs=[pl.BlockSpec(memory_space=pltpu.SMEM),
                  pl.BlockSpec((tr, C), lambda h, r: (r, 0))],
        out_specs=pl.BlockSpec((None, tr, C), lambda h, r: (h, r, 0)),
        out_shape=jax.ShapeDtypeStruct((H_B, R, C), F32),
        compiler_params=_cp("parallel", "parallel"),
        name="bias_tiles",
    )(rel_bias, jnp.asarray(buckets))


def _prompt_buckets(t):
    r = np.arange(t)[:, None]
    c = np.arange(t)[None, :]
    diag = np.where(c <= r, _bucket_np(np.maximum(r - c, 0)), -1)
    off1 = _bucket_np(t + r - c)
    assert t >= MAX_DISTANCE
    return np.concatenate([diag, off1], axis=0).astype(np.int32)


def _lambda_full(lam_ref, lam_init):
    lv = lam_ref[...]
    s1 = jnp.sum(lv[0:1] * lv[1:2], axis=-1, keepdims=True)
    s2 = jnp.sum(lv[2:3] * lv[3:4], axis=-1, keepdims=True)
    return jnp.exp(s1) - jnp.exp(s2) + lam_init


def _flash_kernel(q_ref, k_ref, v_ref, bias_ref, lam_ref, gain_ref, o_ref, qq_scr, m_scr, acc_scr, s_scr,
                  *, lam_init):
    i = pl.program_id(2)
    t = q_ref.shape[0]

    q = q_ref[...]
    first = lax.broadcasted_iota(jnp.int32, q.shape, 1) < DH_B
    qq_scr[:t] = jnp.where(first, q, jnp.zeros_like(q))
    qq_scr[t:] = jnp.where(first, jnp.zeros_like(q), q)
    m_scr[...] = jnp.full(m_scr.shape, NEG, F32)
    acc_scr[...] = jnp.zeros_like(acc_scr)

    def raw_scores(comp, j):
        krows = pl.ds(pl.multiple_of(j * t, t), t)
        return lax.dot_general(qq_scr[comp * t:(comp + 1) * t], k_ref[krows, :], (((1,), (1,)), ((), ())),
                               preferred_element_type=F32)

    def accumulate(comp, s, v1):
        rows = slice(comp * t, (comp + 1) * t)
        m_old = m_scr[rows]
        m_new = jnp.maximum(m_old, jnp.max(s, axis=-1, keepdims=True))
        alpha = jnp.exp2(m_old - m_new)
        p = jnp.concatenate([jnp.exp2(s[:, c * 128:(c + 1) * 128] - m_new) for c in range(t // 128)],
                            axis=1).astype(BF16)
        acc_scr[rows] = (jnp.concatenate([alpha, alpha], axis=1) * acc_scr[rows]
                         + jnp.dot(p, v1, preferred_element_type=F32))
        m_scr[rows] = m_new

    def kv_block(j, tile, has_next):
        krows = pl.ds(pl.multiple_of(j * t, t), t)
        v1 = jnp.concatenate([v_ref[krows, :], jnp.ones((t, DV_B), BF16)], axis=1)
        with_bias = (lambda s: s) if tile is None else (lambda s: s + bias_ref[tile * t:(tile + 1) * t, :])
        s_first = s_scr[...]
        s_second = raw_scores(1, j)
        accumulate(0, with_bias(s_first), v1)
        if has_next:
            s_scr[...] = raw_scores(0, j + 1)
        accumulate(1, with_bias(s_second), v1)

    s_scr[...] = raw_scores(0, 0)

    def far_pair(jj, carry):
        kv_block(2 * jj, None, True)
        kv_block(2 * jj + 1, None, True)
        return carry

    n_far = jnp.maximum(i - 1, 0)
    lax.fori_loop(0, n_far // 2, far_pair, 0)

    @pl.when(jnp.logical_and(i >= 1, n_far % 2 == 1))
    def _():
        kv_block(i - 2, None, True)
        kv_block(i - 1, 1, True)
        kv_block(i, 0, False)

    @pl.when(jnp.logical_and(i >= 1, n_far % 2 == 0))
    def _():
        kv_block(i - 1, 1, True)
        kv_block(i, 0, False)

    @pl.when(i == 0)
    def _():
        kv_block(i, 0, False)

    lam = _lambda_full(lam_ref, lam_init)
    acc = acc_scr[...]
    ob = acc[:t, :DV_B] / acc[:t, DV_B:] - lam * (acc[t:, :DV_B] / acc[t:, DV_B:])
    on = ob * lax.rsqrt(jnp.mean(ob * ob, axis=-1, keepdims=True) + EPS) * gain_ref[...]
    o_ref[...] = (on * (1.0 - lam_init)).astype(o_ref.dtype)


def diff_attention_prompt(qb, kb, vb, bias, lam_rows, gain_row, lam_init, t):
    B, L, _ = qb.shape
    n = L // t
    kv_spec = pl.BlockSpec((None, L, DV_B), lambda b, h, i: (b, 0, h))
    return pl.pallas_call(
        functools.partial(_flash_kernel, lam_init=lam_init),
        grid=(B, H_B, n),
        in_specs=[pl.BlockSpec((None, t, DV_B), lambda b, h, i: (b, i, h)),
                  kv_spec, kv_spec,
                  pl.BlockSpec((None, 2 * t, t), lambda b, h, i: (h, 0, 0)),
                  pl.BlockSpec((4, DH_B), lambda b, h, i: (0, 0)),
                  pl.BlockSpec((1, DV_B), lambda b, h, i: (0, 0))],
        out_specs=pl.BlockSpec((None, t, DV_B), lambda b, h, i: (b, i, h)),
        out_shape=jax.ShapeDtypeStruct((B, L, W_B), BF16),
        scratch_shapes=[pltpu.VMEM((2 * t, DV_B), BF16), pltpu.VMEM((2 * t, 128), F32),
                        pltpu.VMEM((2 * t, 2 * DV_B), F32), pltpu.VMEM((t, t), F32)],
        compiler_params=_cp("parallel", "parallel", "arbitrary"),
        name="diff_attention_prompt",
    )(qb, kb, vb, bias, lam_rows, gain_row)


def _decode_kernel(pt_ref, q_ref, kn_ref, vn_ref, bias_ref, lam_ref, gain_ref, *rest, n_pages, lam_init, spb):
    n = spb * n_pages
    k_all, v_all, o_ref = rest[:n], rest[n:2 * n], rest[2 * n]
    lam = _lambda_full(lam_ref, lam_init)
    gain = gain_ref[...]
    r8 = lax.broadcasted_iota(jnp.int32, (8, 2 * DH_B), 0)
    l8 = lax.broadcasted_iota(jnp.int32, (8, 2 * DH_B), 1)
    comp_rows = (l8 // DH_B) == r8
    row0 = lax.broadcasted_iota(jnp.int32, (PAGE_SIZE, DV_B), 0) == 0
    nt = lambda x, y: lax.dot_general(x, y, (((1,), (1,)), ((), ())), preferred_element_type=F32)
    units = [(s, h) for s in range(spb) for h in range(H_B)]
    cols = [slice(h * DV_B, (h + 1) * DV_B) for h in range(H_B)]
    head_rows = [pl.ds(h, PAGE_SIZE, stride=H_B) for h in range(H_B)]
    pages = lambda refs, s: refs[s * n_pages:(s + 1) * n_pages]
    ss = []
    for s, h in units:
        qrows = jnp.where(comp_rows, q_ref[s, :, cols[h]].astype(F32), 0.0).astype(BF16)
        ks = [r[head_rows[h], :].astype(BF16) for r in pages(k_all, s)]
        ks.append(jnp.where(row0, kn_ref[s, h:h + 1, :], 0.0).astype(BF16))
        sc = []
        for p in range(0, len(ks) - 1, 2):
            s2 = nt(qrows, jnp.concatenate([ks[p], ks[p + 1]], axis=0))
            sc += [s2[:, :PAGE_SIZE] + bias_ref[h, p], s2[:, PAGE_SIZE:] + bias_ref[h, p + 1]]
        if len(ks) % 2:
            sc.append(nt(qrows, ks[-1]) + bias_ref[h, len(ks) - 1])
        ss.append(sc)
    ps, ls = [], []
    for sc in ss:
        m = functools.reduce(jnp.maximum, [jnp.max(x, axis=-1, keepdims=True) for x in sc])
        p = [jnp.exp(x - m) for x in sc]
        ls.append(functools.reduce(jnp.add, [jnp.sum(x, axis=-1, keepdims=True) for x in p]))
        ps.append([x.astype(BF16) for x in p])
    for (s, h), p_u, l_u in zip(units, ps, ls):
        vs = [r[head_rows[h], :].astype(BF16) for r in pages(v_all, s)]
        vs.append(jnp.where(row0, vn_ref[s, h:h + 1, :], 0.0).astype(BF16))
        parts = [jnp.dot(jnp.concatenate([p_u[p], p_u[p + 1]], axis=1),
                         jnp.concatenate([vs[p], vs[p + 1]], axis=0), preferred_element_type=F32)
                 for p in range(0, len(vs) - 1, 2)]
        if len(vs) % 2:
            parts.append(jnp.dot(p_u[-1], vs[-1], preferred_element_type=F32))
        acc = functools.reduce(jnp.add, parts)
        outn = acc / l_u
        oh = outn[0:1] - lam * outn[1:2]
        on = oh * lax.rsqrt(jnp.mean(oh * oh, axis=-1, keepdims=True) + EPS) * gain
        o_ref[s, :, cols[h]] = on * (1.0 - lam_init)


def diff_attention_decode(qb, k32, v32, cache_k, cache_v, pt_flat, bias_dec, lam_rows, gain_row, layer,
                          lam_init, n_pages):
    B = qb.shape[0]
    spb = 2 if B % 2 == 0 else 1
    rowspec = pl.BlockSpec((spb, 1, W_BQK), lambda g, pt: (g, 0, 0))
    headspec = pl.BlockSpec((spb, H_B, DV_B), lambda g, pt: (g, 0, 0))
    const = lambda s: pl.BlockSpec(s, lambda g, pt: tuple(0 for _ in s))
    page_spec = lambda s, p: pl.BlockSpec(
        (None, None, PAGE_SIZE * H_B, DV_B),
        lambda g, pt, s=s, p=p: (layer, pt[(g * spb + s) * n_pages + p], 0, 0))
    page_specs = [page_spec(s, p) for s in range(spb) for p in range(n_pages)]
    grid_spec = pltpu.PrefetchScalarGridSpec(
        num_scalar_prefetch=1,
        grid=(B // spb,),
        in_specs=[rowspec, headspec, headspec, const(bias_dec.shape), const((4, DH_B)), const((1, DV_B))]
                 + page_specs * 2,
        out_specs=pl.BlockSpec((spb, 1, W_B), lambda g, pt: (g, 0, 0)),
    )
    return pl.pallas_call(
        functools.partial(_decode_kernel, n_pages=n_pages, lam_init=lam_init, spb=spb),
        grid_spec=grid_spec,
        out_shape=jax.ShapeDtypeStruct((B, 1, W_B), F32),
        compiler_params=_cp("arbitrary"),
        name="diff_attention_decode",
    )(pt_flat, qb, k32, v32, bias_dec, lam_rows, gain_row,
      *([cache_k] * (spb * n_pages)), *([cache_v] * (spb * n_pages)))


def _rwkv_prep_kernel(*refs, carry_shift, gate_v):
    it = iter(refs)
    z_ref, prev_ref, mu_ref, w0_ref, a0_ref, kk_ref, ka_ref, wlr_ref, ones_ref = (next(it) for _ in range(9))
    if gate_v:
        v0_ref, wvd_ref, wvu_ref, vf_ref = (next(it) for _ in range(4))
    r_ref, lw_ref, k_ref, v_ref, a_ref, b_ref, gg_ref = (next(it) for _ in range(7))
    z = z_ref[...]
    if carry_shift:
        carry_scr = next(it)

        @pl.when(pl.program_id(1) == 0)
        def _():
            carry_scr[...] = prev_ref[...]

        first = lax.broadcasted_iota(jnp.int32, z.shape, 0) == 0
        zprev = jnp.where(first, carry_scr[...], pltpu.roll(z, 1, 0))
        carry_scr[...] = z_ref[z.shape[0] - 1:z.shape[0], :]
    else:
        zprev = prev_ref[...]
    zs = z + (zprev - z) * mu_ref[...]
    rc, kc, vc = zs[:, :W_C], zs[:, W_C:2 * W_C], zs[:, 2 * W_C:3 * W_C]
    lr = zs[:, 3 * W_C:]
    head = lr[:, :R_DECAY + R_A]
    col = lax.broadcasted_iota(jnp.int32, head.shape, 1)
    act = jnp.concatenate([jnp.where(col < R_DECAY, jnp.tanh(head), head),
                           jax.nn.sigmoid(lr[:, R_DECAY + R_A:])], axis=1)
    up = jnp.dot(act.astype(BF16), wlr_ref[...], preferred_element_type=F32)
    y = -(w0_ref[...] + up[:, :W_C])
    softplus = jnp.maximum(y, 0.0) + jnp.log(1.0 + jnp.exp(-jnp.abs(y)))
    lw_ref[...] = -jnp.exp(-softplus - 0.5)
    if gate_v:
        down = jnp.dot(vc.astype(BF16), wvd_ref[...], preferred_element_type=F32)
        vgate = jax.nn.sigmoid(v0_ref[...] + jnp.dot(down.astype(BF16), wvu_ref[...], preferred_element_type=F32))
        vc = vc + (vf_ref[...].astype(F32) - vc) * vgate
    a = jax.nn.sigmoid(a0_ref[...] + up[:, W_C:2 * W_C])
    gg_ref[...] = up[:, 2 * W_C:].astype(gg_ref.dtype)
    kk = kc * kk_ref[...]
    norm = jnp.sqrt(_seg_sum_wide(kk * kk, ones_ref[...]))
    kk = kk / jnp.maximum(norm, 1e-12)
    r_ref[...] = rc.astype(r_ref.dtype)
    k_ref[...] = (kc * (1.0 + (a - 1.0) * ka_ref[...])).astype(k_ref.dtype)
    v_ref[...] = vc.astype(v_ref.dtype)
    a_ref[...] = (-kk).astype(a_ref.dtype)
    b_ref[...] = (kk * a).astype(b_ref.dtype)


def rwkv_prep(z3, prev, mu_row, w0, a0, k_k, k_a, w_lr, vgate, tm, carry_shift):
    G, Lg, _ = z3.shape
    blk = lambda w, idx: pl.BlockSpec((None, tm, w), lambda g, l: (g, l, idx))
    row = lambda w: pl.BlockSpec((1, w), lambda g, l: (0, 0))
    prev_spec = (pl.BlockSpec((None, 1, RWKV_PAD), lambda g, l: (g, 0, 0)) if carry_shift
                 else blk(RWKV_PAD, 0))
    in_specs = [blk(RWKV_PAD, Z_RWKV // RWKV_PAD), prev_spec, row(RWKV_PAD), row(W_C), row(W_C), row(W_C),
                row(W_C), pl.BlockSpec(w_lr.shape, lambda g, l: (0, 0)),
                pl.BlockSpec((128, 128), lambda g, l: (0, 0))]
    args = [z3, prev, mu_row, w0, a0, k_k, k_a, w_lr, _block_ones(DH_C)]
    if vgate is not None:
        v0, wvd, wvu, vfirst = vgate
        in_specs += [row(W_C), pl.BlockSpec(wvd.shape, lambda g, l: (0, 0)),
                     pl.BlockSpec(wvu.shape, lambda g, l: (0, 0)), blk(W_C, 0)]
        args += [v0, wvd, wvu, vfirst]
    return pl.pallas_call(
        functools.partial(_rwkv_prep_kernel, carry_shift=carry_shift, gate_v=vgate is not None),
        grid=(G, Lg // tm),
        in_specs=in_specs,
        out_specs=[blk(W_C, 0)] * 7,
        out_shape=[jax.ShapeDtypeStruct((G, Lg, W_C), F32 if i == 1 else BF16) for i in range(7)],
        scratch_shapes=[pltpu.VMEM((1, RWKV_PAD), F32)] if carry_shift else [],
        compiler_params=_cp("parallel", "arbitrary"),
        name="rwkv_prep",
    )(*args)


def _rwkv_chunk_kernel(r_ref, lw_ref, k_ref, v_ref, a_ref, b_ref, tri_ref, msl_ref, mli_ref, lvl_ref,
                       y_ref, so_ref, s_scr, *, n_chunks, nb):
    C = RW_CHUNK

    @pl.when(pl.program_id(1) == 0)
    def _():
        s_scr[...] = jnp.zeros_like(s_scr)

    head0 = lax.broadcasted_iota(jnp.int32, (C, 128), 1) < DH_C
    tri3, msl, mli = tri_ref[...], msl_ref[...], mli_ref[...]
    ri = lax.broadcasted_iota(jnp.int32, (128, 128), 0)
    ci = lax.broadcasted_iota(jnp.int32, (128, 128), 1)
    eye = (ri == ci).astype(F32)
    n_levels = lvl_ref.shape[0]
    chains = [(bi, p) for bi in range(nb) for p in range(H_C // 2)]

    def stack(x):
        return jnp.concatenate([jnp.where(head0, x, 0.0), jnp.where(head0, 0.0, x)], axis=0)

    def nt(x, y):
        return lax.dot_general(x, y, (((1,), (1,)), ((), ())), preferred_element_type=F32)

    def dot(x, y):
        return jnp.dot(x, y, preferred_element_type=F32)

    def chunk(ci_, carry):
        rows = pl.ds(pl.multiple_of(ci_ * C, C), C)
        ld = lambda ref, bi, p: ref[bi, rows, p * 128:(p + 1) * 128].astype(F32)
        lams = []
        for bi, p in chains:
            lw = ld(lw_ref, bi, p)
            hi = lw.astype(BF16)
            mid = (lw - hi.astype(F32)).astype(BF16)
            lo = (lw - hi.astype(F32) - mid.astype(F32)).astype(BF16)
            lams.append(dot(tri3, jnp.concatenate([hi, mid, lo], axis=0)))
        AR, BK, Vs, KB, dec = [], [], [], [], []
        for (bi, p), lam in zip(chains, lams):
            r, lw, k = ld(r_ref, bi, p), ld(lw_ref, bi, p), ld(k_ref, bi, p)
            v, a, b = ld(v_ref, bi, p), ld(a_ref, bi, p), ld(b_ref, bi, p)
            lam_c = lam[C - 1:C, :]
            e_neg = jnp.exp(-lam)
            e_tail = jnp.exp(lam_c - lam)
            AR.append(jnp.concatenate([stack(a * jnp.exp(lam - lw)), stack(r * jnp.exp(lam))],
                                      axis=0).astype(BF16))
            BK.append(jnp.concatenate([stack(b * e_neg), stack(k * e_neg)], axis=0).astype(BF16))
            KB.append(jnp.concatenate([stack(k * e_tail), stack(b * e_tail)], axis=0).astype(BF16))
            Vs.append(stack(v))
            dec.append(jnp.exp(lam_c))
        G = [nt(x, y) for x, y in zip(AR, BK)]
        I0 = [nt(x, s_scr[bi, p].astype(BF16)) for x, (bi, p) in zip(AR, chains)]
        rhs = [i0[:128] + dot((g[:128, 128:] * msl).astype(BF16), v.astype(BF16))
               for g, i0, v in zip(G, I0, Vs)]
        Nh, Nl, D = [], [], []
        for g in G:
            h, l = _split2(g[:128, :128] * msl)
            Nh.append(h)
            Nl.append(l)
            D.append(eye + (h * lvl_ref[0]).astype(F32) + (l * lvl_ref[0]).astype(F32))
        for lv in range(1, n_levels):
            m = lvl_ref[lv]
            Db = [d.astype(BF16) for d in D]
            X = [dot(h * m, db) for h, db in zip(Nh, Db)]
            D = [d + dot(db, x.astype(BF16)) for d, db, x in zip(D, Db, X)]
        Us = [dot(d.astype(BF16), x.astype(BF16)) for d, x in zip(D, rhs)]
        for (bi, p), g, i0, u, v in zip(chains, G, I0, Us, Vs):
            ys = i0[128:] + dot(jnp.concatenate([g[128:, :128] * mli, g[128:, 128:] * mli], axis=1).astype(BF16),
                                jnp.concatenate([u, v], axis=0).astype(BF16))
            y_ref[bi, rows, p * 128:(p + 1) * 128] = ys[:C] + ys[C:]
        for (bi, p), u, v, kb, d in zip(chains, Us, Vs, KB, dec):
            vu = jnp.concatenate([v, u], axis=0)
            s_scr[bi, p] = s_scr[bi, p] * d + dot(vu.T.astype(BF16), kb)
        return carry

    lax.fori_loop(0, n_chunks, chunk, 0)

    @pl.when(pl.program_id(1) == pl.num_programs(1) - 1)
    def _():
        so_ref[...] = s_scr[...]


def _rwkv_chunk_tables():
    C = RW_CHUNK
    t = np.arange(C)
    tri = (t[:, None] >= t[None, :]).astype(np.float32)
    tri3 = np.concatenate([tri, tri, tri], axis=1)
    i = np.arange(2 * C)
    same = (i[:, None] // C) == (i[None, :] // C)
    msl = (same & (i[:, None] > i[None, :])).astype(np.float32)
    mli = (same & (i[:, None] >= i[None, :])).astype(np.float32)
    levels = []
    n = 1
    while n < C:
        levels.append(same & ((i[:, None] // (2 * n)) == (i[None, :] // (2 * n)))
                      & ((i[:, None] // n) % 2 == 1) & ((i[None, :] // n) % 2 == 0))
        n *= 2
    return jnp.asarray(tri3, BF16), msl, mli, jnp.asarray(np.stack(levels).astype(np.float32), BF16)


def rwkv_chunk_scan(r, lw, k, v, a, b, tb, nb):
    B, L, _ = r.shape
    tri3, msl, mli, lvl = _rwkv_chunk_tables()
    blk = pl.BlockSpec((nb, tb, W_C), lambda bi, l: (bi, l, 0))
    c2 = lambda s: pl.BlockSpec(s, lambda bi, l: tuple(0 for _ in s))
    return pl.pallas_call(
        functools.partial(_rwkv_chunk_kernel, n_chunks=tb // RW_CHUNK, nb=nb),
        grid=(B // nb, L // tb),
        in_specs=[blk] * 6 + [c2(tri3.shape), c2(msl.shape), c2(mli.shape), c2(lvl.shape)],
        out_specs=[blk, pl.BlockSpec((nb, H_C // 2, 128, 128), lambda bi, l: (bi, 0, 0, 0))],
        out_shape=[jax.ShapeDtypeStruct((B, L, W_C), F32),
                   jax.ShapeDtypeStruct((B, H_C // 2, 128, 128), F32)],
        scratch_shapes=[pltpu.VMEM((nb, H_C // 2, 128, 128), F32)],
        compiler_params=_cp("parallel", "arbitrary"),
        name="rwkv_chunk_scan",
    )(r, lw, k, v, a, b, tri3, msl, mli, lvl)


def _rwkv_step_kernel(s_ref, w_ref, a_ref, b_ref, k_ref, r_ref, v_ref, stack_ref, y_ref, so_ref, vt_scr, y_scr):
    del stack_ref
    tr = lambda ref: ref[...].astype(F32).T
    wT, aT, bT, kT, rT = jnp.exp(tr(w_ref)), tr(a_ref), tr(b_ref), tr(k_ref), tr(r_ref)
    vt_scr[...] = tr(v_ref)
    for hh in range(2):
        ch = slice(hh * DH_C, (hh + 1) * DH_C)
        w, a, b, k, r = wT[ch], aT[ch], bT[ch], kT[ch], rT[ch]

        def value_row(i, carry):
            row = pl.ds(hh * DH_C + i, 1)
            S = s_ref[hh, i]
            sa = jnp.sum(S * a, axis=0, keepdims=True)
            Sn = S * w + sa * b + vt_scr[row, :] * k
            so_ref[hh, i] = Sn
            y_scr[row, :] = jnp.sum(Sn * r, axis=0, keepdims=True)
            return carry

        lax.fori_loop(0, DH_C, value_row, 0)
    y_ref[...] = y_scr[...].T


def rwkv_step(state_t, layer, lw, a, b, k, r, v, new_state_t):
    B = lw.shape[0]
    vec = pl.BlockSpec((B, 2 * DH_C), lambda p: (0, p))
    sspec = pl.BlockSpec((None, 2, DH_C, DH_C, B), lambda p: (layer, p, 0, 0, 0))
    return pl.pallas_call(
        _rwkv_step_kernel,
        grid=(H_C // 2,),
        in_specs=[sspec] + [vec] * 6 + [pl.BlockSpec(memory_space=pl.ANY)],
        out_specs=[vec, sspec],
        out_shape=[jax.ShapeDtypeStruct((B, W_C), F32),
                   jax.ShapeDtypeStruct(new_state_t.shape, F32)],
        scratch_shapes=[pltpu.VMEM((2 * DH_C, B), F32), pltpu.VMEM((2 * DH_C, B), F32)],
        input_output_aliases={7: 1},
        compiler_params=_cp("parallel"),
        name="rwkv_step",
    )(state_t, lw, a, b, k, r, v, new_state_t)


def _merge_kernel(x_ref, ya_ref, yb_ref, y_ref, r_ref, k_ref, v_ref, gg_ref, za_ref, zb_ref, zc_ref, gt_ref,
                  lng_ref, lnb_ref, rk_ref, ones_ref, wa_ref, wb_ref, wc_ref, wo_ref, o_ref):
    ones2 = ones_ref[...]
    y = y_ref[...]
    mu = _seg_sum_wide(y, ones2) * (1.0 / DH_C)
    d = y - mu
    var = _seg_sum_wide(d * d, ones2) * (1.0 / DH_C)
    ycn = d * lax.rsqrt(var + LNX_EPS) * lng_ref[...] + lnb_ref[...]
    f32 = lambda ref: ref[...].astype(F32)
    bonus = _seg_sum_wide(f32(r_ref) * f32(k_ref) * rk_ref[...], ones2) * f32(v_ref)
    yc = (ycn + bonus) * f32(gg_ref)
    proj = lambda t, w: jnp.dot(t.astype(BF16), w[...], preferred_element_type=F32)
    merged = (jax.nn.sigmoid(za_ref[...]) * proj(ya_ref[...], wa_ref)
              + jax.nn.sigmoid(zb_ref[...]) * proj(yb_ref[...], wb_ref)
              + jax.nn.sigmoid(zc_ref[...]) * proj(yc, wc_ref))
    o_ref[...] = x_ref[...] + gt_ref[...] * proj(merged, wo_ref)


def merge_out(x2, ya, yb, y, r, k, v, gg, z2, gt, lng, lnb, rk, wa, wb, wc, wo, tm):
    T, D = x2.shape
    G, R, _ = gt.shape
    tiles_per_group = (T // tm) // G
    tok = lambda w: pl.BlockSpec((tm, w), lambda m: (m, 0))
    gate = lambda idx: pl.BlockSpec((tm, D), lambda m: (m, Z_GATE // D + idx))
    row = lambda w: pl.BlockSpec((1, w), lambda m: (0, 0))
    full = lambda a: pl.BlockSpec(a.shape, lambda m: (0, 0))
    return pl.pallas_call(
        _merge_kernel,
        grid=(T // tm,),
        in_specs=[tok(D)] + [tok(W_C)] * 7 + [gate(0), gate(1), gate(2),
                  pl.BlockSpec((None, R, D), lambda m: (m // tiles_per_group, 0, 0)),
                  row(W_C), row(W_C), row(W_C), pl.BlockSpec((128, 128), lambda m: (0, 0)),
                  full(wa), full(wb), full(wc), full(wo)],
        out_specs=tok(D),
        out_shape=jax.ShapeDtypeStruct((T, D), F32),
        compiler_params=_cp("parallel"),
        name="merge_out",
    )(x2, ya, yb, y, r, k, v, gg, z2, z2, z2, gt, lng, lnb, rk, _block_ones(DH_C), wa, wb, wc, wo)


def _ffn_kernel(x_ref, g_ref, sc_ref, sh_ref, gt_ref, w1_ref, w2_ref, o_ref, h_scr, acc_scr):
    f = pl.program_id(1)

    @pl.when(f == 0)
    def _():
        x = x_ref[...]
        y = x * lax.rsqrt(jnp.mean(x * x, axis=-1, keepdims=True) + EPS) * g_ref[...]
        h_scr[...] = (y * (1.0 + sc_ref[...]) + sh_ref[...]).astype(BF16)
        acc_scr[...] = jnp.zeros_like(acc_scr)

    u = jnp.maximum(jnp.dot(h_scr[...], w1_ref[...], preferred_element_type=F32), 0.0)
    acc_scr[...] += jnp.dot((u * u).astype(BF16), w2_ref[...], preferred_element_type=F32)

    @pl.when(f == pl.num_programs(1) - 1)
    def _():
        o_ref[...] = x_ref[...] + gt_ref[...] * acc_scr[...]


def ffn(x2, gain_row, sc, sh, gt, w1, w2, tm, tf):
    T, D = x2.shape
    F = w1.shape[1]
    G, R, _ = sc.shape
    tiles_per_group = (T // tm) // G
    mod_spec = pl.BlockSpec((None, R, D), lambda m, f: (m // tiles_per_group, 0, 0))
    return pl.pallas_call(
        _ffn_kernel,
        grid=(T // tm, F // tf),
        in_specs=[pl.BlockSpec((tm, D), lambda m, f: (m, 0)),
                  pl.BlockSpec((1, D), lambda m, f: (0, 0)),
                  mod_spec, mod_spec, mod_spec,
                  pl.BlockSpec((D, tf), lambda m, f: (0, f)),
                  pl.BlockSpec((tf, D), lambda m, f: (f, 0))],
        out_specs=pl.BlockSpec((tm, D), lambda m, f: (m, 0)),
        out_shape=jax.ShapeDtypeStruct((T, D), F32),
        scratch_shapes=[pltpu.VMEM((tm, D), BF16), pltpu.VMEM((tm, D), F32)],
        compiler_params=_cp("parallel", "arbitrary"),
        name="ffn",
    )(x2, gain_row, sc, sh, gt, w1, w2)


def prep_w_in(w_in):
    depth, D, _ = w_in.shape
    wt = jnp.transpose(w_in, (0, 2, 1))
    o_diff, o_rwkv, o_gate = RET_COLS, RET_COLS + DIFF_COLS, RET_COLS + DIFF_COLS + RWKV_COLS
    return jnp.concatenate([
        wt[:, :RET_COLS], wt[:, o_rwkv:o_gate], jnp.zeros((depth, RWKV_PAD - RWKV_COLS, D), F32),
        wt[:, o_gate:], wt[:, o_diff:o_rwkv]], axis=1).astype(BF16)


def _prep_layer(p, l, w_in_p):
    place = lambda w, slot: jnp.pad(w, ((0, 0), (slot * W_C, (2 - slot) * W_C)))
    w_lr = jnp.concatenate([place(p['w_decay_up'][l], 0), place(p['w_a_up'][l], 1), place(p['w_g_up'][l], 2),
                            jnp.zeros((LR_COLS - R_DECAY - R_A - R_G, 3 * W_C), F32)], axis=0)
    row = lambda t: t.reshape(1, -1)
    lp = dict(
        layer=l, w_ada=p['w_ada'], b_ada=row(p['b_ada'][l]),
        norm1=row(p['norm1'][l]), norm2=row(p['norm2'][l]), w_in=w_in_p,
        gq=row(jnp.tile(p['qk_norm_q'][l], 2 * H_B)), gk=row(jnp.tile(p['qk_norm_k'][l], 2 * H_B)),
        lam_rows=jnp.stack([p['lambda_q1'][l], p['lambda_k1'][l], p['lambda_q2'][l], p['lambda_k2'][l]]),
        subln=row(p['subln_diff'][l]),
        mu=row(jnp.pad(p['mu_shift'][l], (0, RWKV_PAD - RWKV_COLS))),
        w0=row(p['w0'][l]), a0=row(p['a0'][l]), k_k=row(p['k_k'][l]), k_a=row(p['k_a'][l]),
        w_lr=w_lr.astype(BF16), r_k=row(p['r_k'][l]), lnx_g=row(p['lnx_g'][l]), lnx_b=row(p['lnx_b'][l]),
        w_up_a=p['w_up_a'][l].astype(BF16), w_up_b=p['w_up_b'][l].astype(BF16),
        w_up_c=p['w_up_c'][l].astype(BF16), w_out=p['w_out'][l].astype(BF16),
        w_ff1=p['w_ff1'][l].astype(BF16), w_ff2=p['w_ff2'][l].astype(BF16),
        lam_init=0.8 - 0.6 * math.exp(-0.3 * l),
    )
    if l > 0:
        lp['v0'] = row(p['v0'][l - 1])
        lp['w_v_down'] = jnp.pad(p['w_v_down'][l - 1], ((0, 0), (0, 128 - R_V))).astype(BF16)
        lp['w_v_up'] = jnp.pad(p['w_v_up'][l - 1], ((0, 128 - R_V), (0, 0))).astype(BF16)
    return lp


def _modulation(c, lp, per_token):
    B, D = c.shape
    start = 0 if not per_token else lp['mod'].shape[0] - B
    mod = lp['mod'][start:start + B]
    parts = [mod[:, i * D:(i + 1) * D] for i in range(6)]
    shape = (1, B, D) if per_token else (B, 1, D)
    return [t.reshape(shape) for t in parts]


def _trunk_prompt(x, c, layers, bias_p):
    B, L, D = x.shape
    T = B * L
    tm = min(1024, L)
    t_attn = min(512, L)
    x2 = x.reshape(T, D)
    cos_t, sin_t = _rope_tables(jnp.arange(L))
    ret_out, rwkv_out, shift_out = [], [], []
    k_st = jnp.zeros((len(layers), T * H_B, DV_B), F32)
    v_st = jnp.zeros((len(layers), T * H_B, DV_B), F32)
    v_first = None
    for l, lp in enumerate(layers):
        sh1, sc1, gt1, sh2, sc2, gt2 = _modulation(c, lp, per_token=False)
        z2 = in_proj(x2, lp['norm1'], sc1, sh1, lp['w_in'], l, min(1024, L), 2176)
        ZC = z2.shape[1]
        z3 = z2.reshape(B, L, ZC)
        ya, s_ret = retention_prompt(z3, cos_t, sin_t, min(512, L))
        qb, k_st, kb, v_st, vb = qk_norm(z2, lp['gq'], lp['gk'], tm, DH_B ** -0.5 * math.log2(math.e),
                                         l, k_st, v_st)
        r3 = lambda t: t.reshape(B, L, W_B)
        yb = diff_attention_prompt(r3(qb), r3(kb), r3(vb), bias_p, lp['lam_rows'], lp['subln'],
                                   lp['lam_init'], t_attn)
        vgate = None if l == 0 else (lp['v0'], lp['w_v_down'], lp['w_v_up'], v_first)
        shift0 = jnp.zeros((B, 1, RWKV_PAD), F32)
        r, lw, k, v, a, b, gg = rwkv_prep(z3, shift0, lp['mu'], lp['w0'], lp['a0'], lp['k_k'], lp['k_a'],
                                          lp['w_lr'], vgate, min(512, L), carry_shift=True)
        if l == 0:
            v_first = v
        y, s_pair = rwkv_chunk_scan(r, lw, k, v, a, b, min(256, L), B)
        shift_out.append(z3[:, L - 1:, Z_RWKV:Z_RWKV + RWKV_COLS])
        f2 = lambda t: t.reshape(T, -1)
        x2 = merge_out(x2, f2(ya), f2(yb), f2(y), f2(r), f2(k), f2(v), f2(gg), z2, gt1, lp['lnx_g'],
                       lp['lnx_b'], lp['r_k'], lp['w_up_a'], lp['w_up_b'], lp['w_up_c'], lp['w_out'],
                       min(512, L))
        x2 = ffn(x2, lp['norm2'], sc2, sh2, gt2, lp['w_ff1'], lp['w_ff2'], tm, 1024)
        ret_out.append(s_ret)
        sp = s_pair.reshape(B, H_C // 2, 2, DH_C, 2, DH_C)
        rwkv_out.append(jnp.stack([sp[:, :, 0, :, 0, :], sp[:, :, 1, :, 1, :]], axis=2)
                        .reshape(B, H_C, DH_C, DH_C))
    kv_shape = (len(layers), B, L, H_B, DV_B)
    return (x2.reshape(B, L, D), k_st.reshape(kv_shape), v_st.reshape(kv_shape), jnp.stack(ret_out),
            jnp.stack(rwkv_out), jnp.stack(shift_out))


def _trunk_decode(x, c, layers, bias_d, state_ret, state_rwkv, state_shift, cache_k, cache_v, page_table):
    B, _, D = x.shape
    n_pages = page_table.shape[1]
    past = n_pages * PAGE_SIZE
    x2 = x.reshape(B, D)
    pt_flat = page_table.reshape(-1)
    ck = cache_k.reshape(cache_k.shape[0], cache_k.shape[1], PAGE_SIZE * H_B, 2 * DH_B)
    cv = cache_v.reshape(cache_v.shape[0], cache_v.shape[1], PAGE_SIZE * H_B, DV_B)
    state_t = jnp.transpose(state_rwkv, (0, 2, 3, 4, 1))
    shift_out = []
    ret_st = jnp.zeros(state_ret.shape, F32)
    rwkv_st = jnp.zeros(state_t.shape, F32)
    k_st = jnp.zeros((len(layers), B * H_B, DV_B), F32)
    v_st = jnp.zeros((len(layers), B * H_B, DV_B), F32)
    v_first = None
    for l, lp in enumerate(layers):
        sh1, sc1, gt1, sh2, sc2, gt2 = _modulation(c, lp, per_token=True)
        z2 = in_proj(x2, lp['norm1'], sc1, sh1, lp['w_in'], l, B, 512)
        ZC = z2.shape[1]
        ya, ret_st = retention_step(z2, state_ret, l, past, ret_st)
        qb, k_st, kb, v_st, vb = qk_norm(z2, lp['gq'], lp['gk'], B, DH_B ** -0.5, l, k_st, v_st)
        h3 = lambda t: t[l].reshape(B, H_B, DV_B)
        yb = diff_attention_decode(qb.reshape(B, 1, W_BQK), h3(k_st), h3(v_st), ck, cv, pt_flat, bias_d,
                                   lp['lam_rows'], lp['subln'], l, lp['lam_init'], n_pages).reshape(B, W_B)
        vgate = None if l == 0 else (lp['v0'], lp['w_v_down'], lp['w_v_up'], v_first)
        prev = jnp.pad(state_shift[l].reshape(1, B, RWKV_COLS), ((0, 0), (0, 0), (0, RWKV_PAD - RWKV_COLS)))
        r, lw, k, v, a, b, gg = rwkv_prep(z2.reshape(1, B, ZC), prev, lp['mu'], lp['w0'], lp['a0'], lp['k_k'],
                                          lp['k_a'], lp['w_lr'], vgate, B, carry_shift=False)
        if l == 0:
            v_first = v
        f2 = lambda t: t.reshape(B, -1)
        y_dec, rwkv_st = rwkv_step(state_t, l, f2(lw), f2(a), f2(b), f2(k), f2(r), f2(v), rwkv_st)
        shift_out.append(z2[:, Z_RWKV:Z_RWKV + RWKV_COLS].reshape(B, 1, RWKV_COLS))
        x2 = merge_out(x2, ya, yb, y_dec, f2(r), f2(k), f2(v), f2(gg), z2, gt1, lp['lnx_g'], lp['lnx_b'],
                       lp['r_k'], lp['w_up_a'], lp['w_up_b'], lp['w_up_c'], lp['w_out'], B)
        x2 = ffn(x2, lp['norm2'], sc2, sh2, gt2, lp['w_ff1'], lp['w_ff2'], B, 1024)
    rwkv_state = jnp.transpose(rwkv_st, (0, 4, 1, 2, 3))
    kv_shape = (len(layers), B, 1, H_B, DV_B)
    return (x2.reshape(B, 1, D), k_st.reshape(kv_shape), v_st.reshape(kv_shape), ret_st,
            rwkv_state, jnp.stack(shift_out))


def _decode_buckets(n_pages):
    past = n_pages * PAGE_SIZE
    key = np.arange((n_pages + 1) * PAGE_SIZE)
    bk = np.where(key <= past, _bucket_np(np.maximum(past - key, 0)), -1)
    return np.broadcast_to(bk.reshape(n_pages + 1, 1, PAGE_SIZE), (n_pages + 1, 8, PAGE_SIZE)).astype(np.int32)


def kernel(x_prompt, x_sample, c_prompt, c_sample, cache_k_diff, cache_v_diff, page_table, state_ret, state_rwkv, state_shift, rel_bias, w_ada, b_ada, norm1, norm2, w_in, qk_norm_q, qk_norm_k, lambda_q1, lambda_k1, lambda_q2, lambda_k2, subln_diff, mu_shift, w0, w_decay_up, a0, w_a_up, w_g_up, v0, w_v_down, w_v_up, k_k, k_a, r_k, lnx_g, lnx_b, w_up_a, w_up_b, w_up_c, w_out, w_ff1, w_ff2):
    p = dict(w_ada=w_ada, b_ada=b_ada, norm1=norm1, norm2=norm2, w_in=w_in,
             qk_norm_q=qk_norm_q, qk_norm_k=qk_norm_k, lambda_q1=lambda_q1, lambda_k1=lambda_k1,
             lambda_q2=lambda_q2, lambda_k2=lambda_k2, subln_diff=subln_diff, mu_shift=mu_shift,
             w0=w0, w_decay_up=w_decay_up, a0=a0, w_a_up=w_a_up, w_g_up=w_g_up, v0=v0,
             w_v_down=w_v_down, w_v_up=w_v_up, k_k=k_k, k_a=k_a, r_k=r_k, lnx_g=lnx_g, lnx_b=lnx_b,
             w_up_a=w_up_a, w_up_b=w_up_b, w_up_c=w_up_c, w_out=w_out, w_ff1=w_ff1, w_ff2=w_ff2)
    depth = w_in.shape[0]
    w_in_p = prep_w_in(w_in)
    layers = [_prep_layer(p, l, w_in_p) for l in range(depth)]
    bp = c_prompt.shape[0]
    c_all = jnp.concatenate([jnp.pad(c_prompt, ((0, -bp % 16), (0, 0))), c_sample], axis=0)
    for lp in layers:
        lp['mod'] = ada_mod(c_all, lp['w_ada'], lp['b_ada'], lp['layer'])
    L = x_prompt.shape[1]
    n_pages = page_table.shape[1]
    t_attn = min(512, L)
    bias_p = bias_tiles(rel_bias, _prompt_buckets(t_attn), True)
    bd = bias_tiles(rel_bias, _decode_buckets(n_pages).reshape((n_pages + 1) * 8, PAGE_SIZE), False)
    bias_d = bd.reshape(H_B, n_pages + 1, 8, PAGE_SIZE)

    y_p, k_p, v_p, ret_p, rwkv_p, shift_p = _trunk_prompt(x_prompt, c_prompt, layers, bias_p)
    y_s, k_s, v_s, ret_s, rwkv_s, shift_s = _trunk_decode(
        x_sample, c_sample, layers, bias_d, state_ret, state_rwkv, state_shift,
        cache_k_diff, cache_v_diff, page_table)
    return (y_p, y_s, k_p, v_p, k_s, v_s, ret_p, ret_s, rwkv_p, rwkv_s, shift_p, shift_s)
```

```python
import functools
import math

import numpy as np
import jax
import jax.numpy as jnp
from jax import lax
from jax.experimental import pallas as pl
from jax.experimental.pallas import tpu as pltpu

F32 = jnp.float32
BF16 = jnp.bfloat16

H_A, DK_A, DV_A = 4, 128, 128
RET_CHUNK = 128
ROPE_BASE = 10000.0
H_B, DH_B, DV_B = 4, 64, 128
N_BUCKETS, MAX_DISTANCE = 32, 128
H_C, DH_C = 8, 64
R_DECAY, R_A, R_V, R_G = 64, 64, 32, 160
LNX_EPS = 64e-5
EPS = 1e-6
PAGE_SIZE = 128
W_AQ, W_A = H_A * DK_A, H_A * DV_A
W_BQK, W_B = H_B * 2 * DH_B, H_B * DV_B
W_C = H_C * DH_C
RET_COLS = 2 * W_AQ + 2 * W_A
DIFF_COLS = 2 * W_BQK + W_B
RWKV_COLS = 3 * W_C + R_DECAY + R_A + R_G
RWKV_PAD = 2048
LR_COLS = RWKV_PAD - 3 * W_C

Z_RET, Z_RWKV = 0, RET_COLS
Z_GATE = Z_RWKV + RWKV_PAD
NEG = -1e30
RW_CHUNK = 64
V7X_VMEM_BYTES = 64 * 1024 * 1024
VMEM_LIMIT = V7X_VMEM_BYTES * 7 // 8


def _cp(*sem):
    return pltpu.CompilerParams(dimension_semantics=sem, vmem_limit_bytes=VMEM_LIMIT)


def _prompt_tiles(L):
    cap = lambda n: min(n, L)
    return dict(in_proj_rows=cap(1024), in_proj_cols=2176, retention=cap(512), qk_norm=cap(1024),
                attention=cap(512), rwkv_prep=cap(512), rwkv_scan=cap(256), merge=cap(512),
                ffn_rows=cap(1024), ffn_hidden=1024)


def _silu(x):
    return x * jax.nn.sigmoid(x)


def _split2(x):
    hi = x.astype(BF16)
    lo = (x - hi.astype(F32)).astype(BF16)
    return hi, lo


def _seg_sum(x, ones):
    return jnp.dot(x.astype(BF16), ones, preferred_element_type=F32)


def _seg_sum_wide(x, ones2):
    return jnp.concatenate(
        [_seg_sum(x[:, c * 128:(c + 1) * 128], ones2) for c in range(x.shape[1] // 128)], axis=1)


def _mod_kernel(c_ref, w_ref, b_ref, o_ref):
    s = _silu(c_ref[...])
    o_ref[...] = jnp.dot(s.astype(BF16), w_ref[...].astype(BF16), preferred_element_type=F32) + b_ref[...]


def ada_mod(c_pad, w_stack, b_row, layer):
    R, D = c_pad.shape
    N = w_stack.shape[2]
    tn = 1536
    return pl.pallas_call(
        _mod_kernel,
        grid=(N // tn,),
        in_specs=[pl.BlockSpec((R, D), lambda n: (0, 0)),
                  pl.BlockSpec((None, D, tn), lambda n: (layer, 0, n)),
                  pl.BlockSpec((1, tn), lambda n: (0, n))],
        out_specs=pl.BlockSpec((R, tn), lambda n: (0, n)),
        out_shape=jax.ShapeDtypeStruct((R, N), F32),
        compiler_params=_cp("parallel"),
        name="ada_mod",
    )(c_pad, w_stack, b_row)


def _inproj_kernel(x_ref, g_ref, sc_ref, sh_ref, w_ref, z_ref, h_scr):
    @pl.when(pl.program_id(1) == 0)
    def _():
        x = x_ref[...]
        y = x * lax.rsqrt(jnp.mean(x * x, axis=-1, keepdims=True) + EPS) * g_ref[...]
        h_scr[...] = (y * (1.0 + sc_ref[...]) + sh_ref[...]).astype(BF16)

    z_ref[...] = lax.dot_general(h_scr[...], w_ref[...], (((1,), (1,)), ((), ())),
                                 preferred_element_type=F32)


def in_proj(x2, gain_row, sc, sh, w_bf, layer, tm, tn):
    T, D = x2.shape
    N = w_bf.shape[1]
    G, R, _ = sc.shape
    tiles_per_group = (T // tm) // G
    mod_spec = pl.BlockSpec((None, R, D), lambda m, n: (m // tiles_per_group, 0, 0))
    return pl.pallas_call(
        _inproj_kernel,
        grid=(T // tm, N // tn),
        in_specs=[pl.BlockSpec((tm, D), lambda m, n: (m, 0)),
                  pl.BlockSpec((1, D), lambda m, n: (0, 0)),
                  mod_spec, mod_spec,
                  pl.BlockSpec((None, tn, D), lambda m, n: (layer, n, 0))],
        out_specs=pl.BlockSpec((tm, tn), lambda m, n: (m, n)),
        out_shape=jax.ShapeDtypeStruct((T, N), F32),
        scratch_shapes=[pltpu.VMEM((tm, D), BF16)],
        compiler_params=_cp("parallel", "arbitrary"),
        name="in_proj",
    )(x2, gain_row, sc, sh, w_bf)


def _rope(x, cos, sin_signed):
    return x * cos + pltpu.roll(x, DK_A // 2, 1) * sin_signed


def _ret_kernel(q_ref, k_ref, v_ref, g_ref, cos_ref, sin_ref, inner_ref, cross_ref, tail_ref, cd_ref,
                ya_ref, so_ref, s_scr, *, n_sub):
    lt = pl.program_id(1)

    @pl.when(lt == 0)
    def _():
        s_scr[...] = jnp.zeros_like(s_scr)

    C = RET_CHUNK
    heads = range(H_A)
    cols = [slice(h * DK_A, (h + 1) * DK_A) for h in heads]
    dot = lambda x, y: jnp.dot(x, y, preferred_element_type=F32)
    for c in range(n_sub):
        rows = slice(c * C, (c + 1) * C)
        cos = cos_ref[rows, :]
        sin = sin_ref[rows, :]
        qb, kb, vb, kt = [], [], [], []
        for h in heads:
            q = _rope(q_ref[rows, cols[h]], cos, sin)
            k = _rope(k_ref[rows, cols[h]], cos, sin) * (DK_A ** -0.5)
            qb.append(q.astype(BF16))
            kb.append(k.astype(BF16))
            vb.append(v_ref[rows, cols[h]].astype(BF16))
            kt.append((k * tail_ref[h]).T.astype(BF16))
        S = [s_scr[h] for h in heads]
        sc = [lax.dot_general(qb[h], kb[h], (((1,), (1,)), ((), ())), preferred_element_type=F32) * inner_ref[h]
              for h in heads]
        qs = [dot(qb[h], S[h].astype(BF16)) * cross_ref[h] for h in heads]
        kv = [dot(kt[h], vb[h]) for h in heads]
        o = [dot(sc[h].astype(BF16), vb[h]) + qs[h] for h in heads]
        for h in heads:
            s_scr[h] = S[h] * cd_ref[h] + kv[h]
            on = o[h] * lax.rsqrt(jnp.mean(o[h] * o[h], axis=-1, keepdims=True) + EPS)
            ya_ref[rows, cols[h]] = (on * _silu(g_ref[rows, cols[h]])).astype(ya_ref.dtype)

    @pl.when(lt == pl.num_programs(1) - 1)
    def _():
        so_ref[...] = s_scr[...]


def _ret_tables(L):
    C = math.gcd(L, RET_CHUNK)
    log_g = np.log1p(-np.exp2(-5.0 - np.arange(H_A, dtype=np.float32))).astype(np.float32)
    i = np.arange(C, dtype=np.float32)
    dist = i[:, None] - i[None, :]
    causal = dist >= 0
    inner = np.where(causal[None], np.exp(np.where(causal, dist, 0.0)[None] * log_g[:, None, None]), 0.0)
    cross = np.exp((i[None, :] + 1.0) * log_g[:, None])
    tail = np.exp((C - 1.0 - i)[None, :] * log_g[:, None])
    chunk = np.exp(C * log_g)
    bc = lambda t: np.broadcast_to(t[:, :, None], (H_A, C, 128)).astype(np.float32)
    cd = np.broadcast_to(chunk[:, None, None], (H_A, 1, 128)).astype(np.float32)
    return inner.astype(np.float32), bc(cross), bc(tail), cd


def _rope_tables(pos):
    half = DK_A // 2
    inv = ROPE_BASE ** (-jnp.arange(half, dtype=F32) / half)
    ang = pos.astype(F32)[:, None] * inv[None, :]
    cos, sin = jnp.cos(ang), jnp.sin(ang)
    return jnp.concatenate([cos, cos], axis=-1), jnp.concatenate([-sin, sin], axis=-1)


def retention_prompt(z3, cos_t, sin_t, tb):
    B, L, ZC = z3.shape
    inner, cross, tail, cd = _ret_tables(L)
    zspec = lambda idx: pl.BlockSpec((None, tb, W_AQ), lambda b, l: (b, l, idx))
    full3 = lambda s: pl.BlockSpec(s, lambda b, l: (0, 0, 0))
    return pl.pallas_call(
        functools.partial(_ret_kernel, n_sub=tb // RET_CHUNK),
        grid=(B, L // tb),
        in_specs=[zspec(0), zspec(1), zspec(2), zspec(3),
                  pl.BlockSpec((tb, 128), lambda b, l: (l, 0)),
                  pl.BlockSpec((tb, 128), lambda b, l: (l, 0)),
                  full3(inner.shape), full3(cross.shape), full3(tail.shape), full3(cd.shape)],
        out_specs=[pl.BlockSpec((None, tb, W_A), lambda b, l: (b, l, 0)),
                   pl.BlockSpec((None, H_A, DK_A, DV_A), lambda b, l: (b, 0, 0, 0))],
        out_shape=[jax.ShapeDtypeStruct((B, L, W_A), BF16),
                   jax.ShapeDtypeStruct((B, H_A, DK_A, DV_A), F32)],
        scratch_shapes=[pltpu.VMEM((H_A, DK_A, DV_A), F32)],
        compiler_params=_cp("parallel", "arbitrary"),
        name="retention_prompt",
    )(z3, z3, z3, z3, cos_t, sin_t, inner, cross, tail, cd)


def _ret_step_kernel(q_ref, k_ref, v_ref, g_ref, cos_ref, sin_ref, gam_ref, s_ref, stack_ref, ya_ref, so_ref,
                     *, bb):
    del stack_ref
    cos, sin = cos_ref[...], sin_ref[...]
    row = lax.broadcasted_iota(jnp.int32, (bb, 128), 0)
    for h in range(H_A):
        cols = slice(h * DK_A, (h + 1) * DK_A)
        gam = gam_ref[h]
        q = _rope(q_ref[:, cols], cos, sin)
        k = _rope(k_ref[:, cols], cos, sin) * (DK_A ** -0.5)
        v = v_ref[:, cols]
        qb = q.astype(BF16)
        qk = jnp.sum(qb.astype(F32) * k.astype(BF16).astype(F32), axis=-1, keepdims=True)
        qs = jnp.zeros((bb, 128), F32)
        for b in range(bb):
            S = s_ref[b, h]
            onehot = row == b
            qs = jnp.where(onehot, jnp.dot(qb, S.astype(BF16), preferred_element_type=F32), qs)
            kb_t = jnp.where(onehot, k, 0.0).T.astype(BF16)
            so_ref[b, h] = S * gam + jnp.dot(kb_t, v.astype(BF16), preferred_element_type=F32)
        o = qk * v.astype(BF16).astype(F32) + qs * gam
        on = o * lax.rsqrt(jnp.mean(o * o, axis=-1, keepdims=True) + EPS)
        ya_ref[:, cols] = on * _silu(g_ref[:, cols])


def retention_step(z2, state, layer, pos, new_state):
    B = z2.shape[0]
    bb = 8
    cos_t, sin_t = _rope_tables(jnp.full((1,), pos))
    log_g = np.log1p(-np.exp2(-5.0 - np.arange(H_A, dtype=np.float32))).astype(np.float32)
    gam = np.broadcast_to(np.exp(log_g)[:, None, None], (H_A, 1, 128)).astype(np.float32)
    zspec = lambda idx: pl.BlockSpec((bb, W_AQ), lambda i: (i, idx))
    return pl.pallas_call(
        functools.partial(_ret_step_kernel, bb=bb),
        grid=(B // bb,),
        in_specs=[zspec(0), zspec(1), zspec(2), zspec(3),
                  pl.BlockSpec((1, 128), lambda i: (0, 0)),
                  pl.BlockSpec((1, 128), lambda i: (0, 0)),
                  pl.BlockSpec((H_A, 1, 128), lambda i: (0, 0, 0)),
                  pl.BlockSpec((None, bb, H_A, DK_A, DV_A), lambda i: (layer, i, 0, 0, 0)),
                  pl.BlockSpec(memory_space=pl.ANY)],
        out_specs=[pl.BlockSpec((bb, W_A), lambda i: (i, 0)),
                   pl.BlockSpec((None, bb, H_A, DK_A, DV_A), lambda i: (layer, i, 0, 0, 0))],
        out_shape=[jax.ShapeDtypeStruct((B, W_A), F32),
                   jax.ShapeDtypeStruct(new_state.shape, F32)],
        input_output_aliases={8: 1},
        compiler_params=_cp("parallel"),
        name="retention_step",
    )(z2, z2, z2, z2, cos_t, sin_t, gam, state, new_state)


def _qknorm_kernel(q_ref, k_ref, v_ref, gq_ref, gk_ref, ones_ref, k_stack_ref, v_stack_ref,
                   qb_ref, k32_ref, kb_ref, v32_ref, vb_ref, *, q_scale):
    del k_stack_ref, v_stack_ref
    ones2 = ones_ref[...]
    q, k, v = q_ref[...], k_ref[...], v_ref[...]
    tm = q.shape[0]
    qn = q * lax.rsqrt(_seg_sum_wide(q * q, ones2) * (1.0 / DH_B) + EPS) * gq_ref[...] * q_scale
    kn = k * lax.rsqrt(_seg_sum_wide(k * k, ones2) * (1.0 / DH_B) + EPS) * gk_ref[...]
    qb_ref[...] = qn.astype(BF16)
    kb_ref[...] = kn.astype(BF16)
    vb_ref[...] = v.astype(BF16)
    for h in range(H_B):
        head_rows = pl.ds(h, tm, stride=H_B)
        k32_ref[head_rows, :] = kn[:, h * DV_B:(h + 1) * DV_B]
        v32_ref[head_rows, :] = v[:, h * DV_B:(h + 1) * DV_B]


def _block_ones(group):
    i = np.arange(128)
    return jnp.asarray((i[:, None] // group == i[None, :] // group).astype(np.float32), BF16)


def qk_norm(z2, gq_row, gk_row, tm, q_scale, layer, k_stack, v_stack):
    T = z2.shape[0]
    base = (Z_GATE + 3 * 1024) // W_BQK
    zspec = lambda idx: pl.BlockSpec((tm, W_BQK), lambda m: (m, base + idx))
    row = pl.BlockSpec((1, W_BQK), lambda m: (0, 0))
    out = pl.BlockSpec((tm, W_BQK), lambda m: (m, 0))
    out_rows = pl.BlockSpec((None, tm * H_B, DV_B), lambda m: (layer, m, 0))
    stack = pl.BlockSpec(memory_space=pl.ANY)
    return pl.pallas_call(
        functools.partial(_qknorm_kernel, q_scale=q_scale),
        grid=(T // tm,),
        in_specs=[zspec(0), zspec(1), zspec(2), row, row, pl.BlockSpec((128, 128), lambda m: (0, 0)),
                  stack, stack],
        out_specs=[out, out_rows, out, out_rows, out],
        out_shape=[jax.ShapeDtypeStruct((T, W_BQK), BF16), jax.ShapeDtypeStruct(k_stack.shape, F32),
                   jax.ShapeDtypeStruct((T, W_BQK), BF16), jax.ShapeDtypeStruct(v_stack.shape, F32),
                   jax.ShapeDtypeStruct((T, W_B), BF16)],
        input_output_aliases={6: 1, 7: 3},
        compiler_params=_cp("parallel"),
        name="qk_norm",
    )(z2, z2, z2, gq_row, gk_row, _block_ones(DH_B), k_stack, v_stack)


def _bucket_np(n):
    max_exact = N_BUCKETS // 2
    nf = np.maximum(n, 1).astype(np.float32)
    large = max_exact + (np.log(nf / np.float32(max_exact)) / np.float32(math.log(MAX_DISTANCE / max_exact))
                         * np.float32(N_BUCKETS - max_exact)).astype(np.int32)
    large = np.minimum(large, N_BUCKETS - 1)
    return np.where(n < max_exact, n, large).astype(np.int32)


def _bias_kernel(tab_ref, bkt_ref, o_ref, *, log2_far_shift):
    h = pl.program_id(0)
    bk = bkt_ref[...]
    acc = jnp.zeros(bk.shape, F32)
    for b in range(N_BUCKETS):
        acc = jnp.where(bk == b, tab_ref[b, h], acc)
    if log2_far_shift:
        acc = (acc - tab_ref[N_BUCKETS - 1, h]) * math.log2(math.e)
    o_ref[...] = jnp.where(bk < 0, NEG, acc)


def bias_tiles(rel_bias, buckets, log2_far_shift):
    R, C = buckets.shape
    tr = min(R, 512)
    return pl.pallas_call(
        functools.partial(_bias_kernel, log2_far_shift=log2_far_shift),
        grid=(H_B, R // tr),
        in_specs=[pl.BlockSpec(memory_space=pltpu.SMEM),
                  pl.BlockSpec((tr, C), lambda h, r: (r, 0))],
        out_specs=pl.BlockSpec((None, tr, C), lambda h, r: (h, r, 0)),
        out_shape=jax.ShapeDtypeStruct((H_B, R, C), F32),
        compiler_params=_cp("parallel", "parallel"),
        name="bias_tiles",
    )(rel_bias, jnp.asarray(buckets))


def _prompt_buckets(t):
    r = np.arange(t)[:, None]
    c = np.arange(t)[None, :]
    diag = np.where(c <= r, _bucket_np(np.maximum(r - c, 0)), -1)
    off1 = _bucket_np(t + r - c)
    assert t >= MAX_DISTANCE
    return np.concatenate([diag, off1], axis=0).astype(np.int32)


def _lambda_full(lam_ref, lam_init):
    lv = lam_ref[...]
    s1 = jnp.sum(lv[0:1] * lv[1:2], axis=-1, keepdims=True)
    s2 = jnp.sum(lv[2:3] * lv[3:4], axis=-1, keepdims=True)
    return jnp.exp(s1) - jnp.exp(s2) + lam_init


def _flash_kernel(q_ref, k_ref, v_ref, bias_ref, lam_ref, gain_ref, o_ref, qq_scr, m_scr, acc_scr, s_scr,
                  *, lam_init):
    i = pl.program_id(2)
    t = q_ref.shape[0]

    q = q_ref[...]
    first = lax.broadcasted_iota(jnp.int32, q.shape, 1) < DH_B
    qq_scr[:t] = jnp.where(first, q, jnp.zeros_like(q))
    qq_scr[t:] = jnp.where(first, jnp.zeros_like(q), q)
    m_scr[...] = jnp.full(m_scr.shape, NEG, F32)
    acc_scr[...] = jnp.zeros_like(acc_scr)

    def raw_scores(comp, j):
        krows = pl.ds(pl.multiple_of(j * t, t), t)
        return lax.dot_general(qq_scr[comp * t:(comp + 1) * t], k_ref[krows, :], (((1,), (1,)), ((), ())),
                               preferred_element_type=F32)

    def accumulate(comp, s, v1):
        rows = slice(comp * t, (comp + 1) * t)
        m_old = m_scr[rows]
        m_new = jnp.maximum(m_old, jnp.max(s, axis=-1, keepdims=True))
        alpha = jnp.exp2(m_old - m_new)
        p = jnp.concatenate([jnp.exp2(s[:, c * 128:(c + 1) * 128] - m_new) for c in range(t // 128)],
                            axis=1).astype(BF16)
        acc_scr[rows] = (jnp.concatenate([alpha, alpha], axis=1) * acc_scr[rows]
                         + jnp.dot(p, v1, preferred_element_type=F32))
        m_scr[rows] = m_new

    def kv_block(j, tile, has_next):
        krows = pl.ds(pl.multiple_of(j * t, t), t)
        v1 = jnp.concatenate([v_ref[krows, :], jnp.ones((t, DV_B), BF16)], axis=1)
        with_bias = (lambda s: s) if tile is None else (lambda s: s + bias_ref[tile * t:(tile + 1) * t, :])
        s_first = s_scr[...]
        s_second = raw_scores(1, j)
        accumulate(0, with_bias(s_first), v1)
        if has_next:
            s_scr[...] = raw_scores(0, j + 1)
        accumulate(1, with_bias(s_second), v1)

    s_scr[...] = raw_scores(0, 0)

    def far_pair(jj, carry):
        kv_block(2 * jj, None, True)
        kv_block(2 * jj + 1, None, True)
        return carry

    n_far = jnp.maximum(i - 1, 0)
    lax.fori_loop(0, n_far // 2, far_pair, 0)

    @pl.when(jnp.logical_and(i >= 1, n_far % 2 == 1))
    def _():
        kv_block(i - 2, None, True)
        kv_block(i - 1, 1, True)
        kv_block(i, 0, False)

    @pl.when(jnp.logical_and(i >= 1, n_far % 2 == 0))
    def _():
        kv_block(i - 1, 1, True)
        kv_block(i, 0, False)

    @pl.when(i == 0)
    def _():
        kv_block(i, 0, False)

    lam = _lambda_full(lam_ref, lam_init)
    acc = acc_scr[...]
    ob = acc[:t, :DV_B] / acc[:t, DV_B:] - lam * (acc[t:, :DV_B] / acc[t:, DV_B:])
    on = ob * lax.rsqrt(jnp.mean(ob * ob, axis=-1, keepdims=True) + EPS) * gain_ref[...]
    o_ref[...] = (on * (1.0 - lam_init)).astype(o_ref.dtype)


def diff_attention_prompt(qb, kb, vb, bias, lam_rows, gain_row, lam_init, t):
    B, L, _ = qb.shape
    n = L // t
    kv_spec = pl.BlockSpec((None, L, DV_B), lambda b, h, i: (b, 0, h))
    return pl.pallas_call(
        functools.partial(_flash_kernel, lam_init=lam_init),
        grid=(B, H_B, n),
        in_specs=[pl.BlockSpec((None, t, DV_B), lambda b, h, i: (b, i, h)),
                  kv_spec, kv_spec,
                  pl.BlockSpec((None, 2 * t, t), lambda b, h, i: (h, 0, 0)),
                  pl.BlockSpec((4, DH_B), lambda b, h, i: (0, 0)),
                  pl.BlockSpec((1, DV_B), lambda b, h, i: (0, 0))],
        out_specs=pl.BlockSpec((None, t, DV_B), lambda b, h, i: (b, i, h)),
        out_shape=jax.ShapeDtypeStruct((B, L, W_B), BF16),
        scratch_shapes=[pltpu.VMEM((2 * t, DV_B), BF16), pltpu.VMEM((2 * t, 128), F32),
                        pltpu.VMEM((2 * t, 2 * DV_B), F32), pltpu.VMEM((t, t), F32)],
        compiler_params=_cp("parallel", "parallel", "arbitrary"),
        name="diff_attention_prompt",
    )(qb, kb, vb, bias, lam_rows, gain_row)


def _decode_kernel(pt_ref, q_ref, kn_ref, vn_ref, bias_ref, lam_ref, gain_ref, *rest, n_pages, lam_init, spb):
    n = spb * n_pages
    k_all, v_all, o_ref = rest[:n], rest[n:2 * n], rest[2 * n]
    lam = _lambda_full(lam_ref, lam_init)
    gain = gain_ref[...]
    r8 = lax.broadcasted_iota(jnp.int32, (8, 2 * DH_B), 0)
    l8 = lax.broadcasted_iota(jnp.int32, (8, 2 * DH_B), 1)
    comp_rows = (l8 // DH_B) == r8
    row0 = lax.broadcasted_iota(jnp.int32, (PAGE_SIZE, DV_B), 0) == 0
    nt = lambda x, y: lax.dot_general(x, y, (((1,), (1,)), ((), ())), preferred_element_type=F32)
    units = [(s, h) for s in range(spb) for h in range(H_B)]
    cols = [slice(h * DV_B, (h + 1) * DV_B) for h in range(H_B)]
    head_rows = [pl.ds(h, PAGE_SIZE, stride=H_B) for h in range(H_B)]
    pages = lambda refs, s: refs[s * n_pages:(s + 1) * n_pages]
    ss = []
    for s, h in units:
        qrows = jnp.where(comp_rows, q_ref[s, :, cols[h]].astype(F32), 0.0).astype(BF16)
        ks = [r[head_rows[h], :].astype(BF16) for r in pages(k_all, s)]
        ks.append(jnp.where(row0, kn_ref[s, h:h + 1, :], 0.0).astype(BF16))
        sc = []
        for p in range(0, len(ks) - 1, 2):
            s2 = nt(qrows, jnp.concatenate([ks[p], ks[p + 1]], axis=0))
            sc += [s2[:, :PAGE_SIZE] + bias_ref[h, p], s2[:, PAGE_SIZE:] + bias_ref[h, p + 1]]
        if len(ks) % 2:
            sc.append(nt(qrows, ks[-1]) + bias_ref[h, len(ks) - 1])
        ss.append(sc)
    ps, ls = [], []
    for sc in ss:
        m = functools.reduce(jnp.maximum, [jnp.max(x, axis=-1, keepdims=True) for x in sc])
        p = [jnp.exp(x - m) for x in sc]
        ls.append(functools.reduce(jnp.add, [jnp.sum(x, axis=-1, keepdims=True) for x in p]))
        ps.append([x.astype(BF16) for x in p])
    for (s, h), p_u, l_u in zip(units, ps, ls):
        vs = [r[head_rows[h], :].astype(BF16) for r in pages(v_all, s)]
        vs.append(jnp.where(row0, vn_ref[s, h:h + 1, :], 0.0).astype(BF16))
        parts = [jnp.dot(jnp.concatenate([p_u[p], p_u[p + 1]], axis=1),
                         jnp.concatenate([vs[p], vs[p + 1]], axis=0), preferred_element_type=F32)
                 for p in range(0, len(vs) - 1, 2)]
        if len(vs) % 2:
            parts.append(jnp.dot(p_u[-1], vs[-1], preferred_element_type=F32))
        acc = functools.reduce(jnp.add, parts)
        outn = acc / l_u
        oh = outn[0:1] - lam * outn[1:2]
        on = oh * lax.rsqrt(jnp.mean(oh * oh, axis=-1, keepdims=True) + EPS) * gain
        o_ref[s, :, cols[h]] = on * (1.0 - lam_init)


def diff_attention_decode(qb, k32, v32, cache_k, cache_v, pt_flat, bias_dec, lam_rows, gain_row, layer,
                          lam_init, n_pages):
    B = qb.shape[0]
    spb = 2 if B % 2 == 0 else 1
    rowspec = pl.BlockSpec((spb, 1, W_BQK), lambda g, pt: (g, 0, 0))
    headspec = pl.BlockSpec((spb, H_B, DV_B), lambda g, pt: (g, 0, 0))
    const = lambda s: pl.BlockSpec(s, lambda g, pt: tuple(0 for _ in s))
    page_spec = lambda s, p: pl.BlockSpec(
        (None, None, PAGE_SIZE * H_B, DV_B),
        lambda g, pt, s=s, p=p: (layer, pt[(g * spb + s) * n_pages + p], 0, 0))
    page_specs = [page_spec(s, p) for s in range(spb) for p in range(n_pages)]
    grid_spec = pltpu.PrefetchScalarGridSpec(
        num_scalar_prefetch=1,
        grid=(B // spb,),
        in_specs=[rowspec, headspec, headspec, const(bias_dec.shape), const((4, DH_B)), const((1, DV_B))]
                 + page_specs * 2,
        out_specs=pl.BlockSpec((spb, 1, W_B), lambda g, pt: (g, 0, 0)),
    )
    return pl.pallas_call(
        functools.partial(_decode_kernel, n_pages=n_pages, lam_init=lam_init, spb=spb),
        grid_spec=grid_spec,
        out_shape=jax.ShapeDtypeStruct((B, 1, W_B), F32),
        compiler_params=_cp("arbitrary"),
        name="diff_attention_decode",
    )(pt_flat, qb, k32, v32, bias_dec, lam_rows, gain_row,
      *([cache_k] * (spb * n_pages)), *([cache_v] * (spb * n_pages)))


def _rwkv_prep_kernel(*refs, carry_shift, gate_v):
    it = iter(refs)
    z_ref, prev_ref, mu_ref, w0_ref, a0_ref, kk_ref, ka_ref, wlr_ref, ones_ref = (next(it) for _ in range(9))
    if gate_v:
        v0_ref, wvd_ref, wvu_ref, vf_ref = (next(it) for _ in range(4))
    r_ref, lw_ref, k_ref, v_ref, a_ref, b_ref, gg_ref = (next(it) for _ in range(7))
    z = z_ref[...]
    if carry_shift:
        carry_scr = next(it)

        @pl.when(pl.program_id(1) == 0)
        def _():
            carry_scr[...] = prev_ref[...]

        first = lax.broadcasted_iota(jnp.int32, z.shape, 0) == 0
        zprev = jnp.where(first, carry_scr[...], pltpu.roll(z, 1, 0))
        carry_scr[...] = z_ref[z.shape[0] - 1:z.shape[0], :]
    else:
        zprev = prev_ref[...]
    zs = z + (zprev - z) * mu_ref[...]
    rc, kc, vc = zs[:, :W_C], zs[:, W_C:2 * W_C], zs[:, 2 * W_C:3 * W_C]
    lr = zs[:, 3 * W_C:]
    head = lr[:, :R_DECAY + R_A]
    col = lax.broadcasted_iota(jnp.int32, head.shape, 1)
    act = jnp.concatenate([jnp.where(col < R_DECAY, jnp.tanh(head), head),
                           jax.nn.sigmoid(lr[:, R_DECAY + R_A:])], axis=1)
    up = jnp.dot(act.astype(BF16), wlr_ref[...], preferred_element_type=F32)
    y = -(w0_ref[...] + up[:, :W_C])
    softplus = jnp.maximum(y, 0.0) + jnp.log(1.0 + jnp.exp(-jnp.abs(y)))
    lw_ref[...] = -jnp.exp(-softplus - 0.5)
    if gate_v:
        down = jnp.dot(vc.astype(BF16), wvd_ref[...], preferred_element_type=F32)
        vgate = jax.nn.sigmoid(v0_ref[...] + jnp.dot(down.astype(BF16), wvu_ref[...], preferred_element_type=F32))
        vc = vc + (vf_ref[...].astype(F32) - vc) * vgate
    a = jax.nn.sigmoid(a0_ref[...] + up[:, W_C:2 * W_C])
    gg_ref[...] = up[:, 2 * W_C:].astype(gg_ref.dtype)
    kk = kc * kk_ref[...]
    norm = jnp.sqrt(_seg_sum_wide(kk * kk, ones_ref[...]))
    kk = kk / jnp.maximum(norm, 1e-12)
    r_ref[...] = rc.astype(r_ref.dtype)
    k_ref[...] = (kc * (1.0 + (a - 1.0) * ka_ref[...])).astype(k_ref.dtype)
    v_ref[...] = vc.astype(v_ref.dtype)
    a_ref[...] = (-kk).astype(a_ref.dtype)
    b_ref[...] = (kk * a).astype(b_ref.dtype)


def rwkv_prep(z3, prev, mu_row, w0, a0, k_k, k_a, w_lr, vgate, tm, carry_shift):
    G, Lg, _ = z3.shape
    blk = lambda w, idx: pl.BlockSpec((None, tm, w), lambda g, l: (g, l, idx))
    row = lambda w: pl.BlockSpec((1, w), lambda g, l: (0, 0))
    prev_spec = (pl.BlockSpec((None, 1, RWKV_PAD), lambda g, l: (g, 0, 0)) if carry_shift
                 else blk(RWKV_PAD, 0))
    in_specs = [blk(RWKV_PAD, Z_RWKV // RWKV_PAD), prev_spec, row(RWKV_PAD), row(W_C), row(W_C), row(W_C),
                row(W_C), pl.BlockSpec(w_lr.shape, lambda g, l: (0, 0)),
                pl.BlockSpec((128, 128), lambda g, l: (0, 0))]
    args = [z3, prev, mu_row, w0, a0, k_k, k_a, w_lr, _block_ones(DH_C)]
    if vgate is not None:
        v0, wvd, wvu, vfirst = vgate
        in_specs += [row(W_C), pl.BlockSpec(wvd.shape, lambda g, l: (0, 0)),
                     pl.BlockSpec(wvu.shape, lambda g, l: (0, 0)), blk(W_C, 0)]
        args += [v0, wvd, wvu, vfirst]
    return pl.pallas_call(
        functools.partial(_rwkv_prep_kernel, carry_shift=carry_shift, gate_v=vgate is not None),
        grid=(G, Lg // tm),
        in_specs=in_specs,
        out_specs=[blk(W_C, 0)] * 7,
        out_shape=[jax.ShapeDtypeStruct((G, Lg, W_C), F32 if i == 1 else BF16) for i in range(7)],
        scratch_shapes=[pltpu.VMEM((1, RWKV_PAD), F32)] if carry_shift else [],
        compiler_params=_cp("parallel", "arbitrary"),
        name="rwkv_prep",
    )(*args)


def _rwkv_chunk_kernel(r_ref, lw_ref, k_ref, v_ref, a_ref, b_ref, tri_ref, msl_ref, mli_ref, lvl_ref,
                       y_ref, so_ref, s_scr, *, n_chunks, nb):
    C = RW_CHUNK

    @pl.when(pl.program_id(1) == 0)
    def _():
        s_scr[...] = jnp.zeros_like(s_scr)

    head0 = lax.broadcasted_iota(jnp.int32, (C, 128), 1) < DH_C
    tri3, msl, mli = tri_ref[...], msl_ref[...], mli_ref[...]
    ri = lax.broadcasted_iota(jnp.int32, (128, 128), 0)
    ci = lax.broadcasted_iota(jnp.int32, (128, 128), 1)
    eye = (ri == ci).astype(F32)
    n_levels = lvl_ref.shape[0]
    chains = [(bi, p) for bi in range(nb) for p in range(H_C // 2)]

    def stack(x):
        return jnp.concatenate([jnp.where(head0, x, 0.0), jnp.where(head0, 0.0, x)], axis=0)

    def nt(x, y):
        return lax.dot_general(x, y, (((1,), (1,)), ((), ())), preferred_element_type=F32)

    def dot(x, y):
        return jnp.dot(x, y, preferred_element_type=F32)

    def chunk(ci_, carry):
        rows = pl.ds(pl.multiple_of(ci_ * C, C), C)
        ld = lambda ref, bi, p: ref[bi, rows, p * 128:(p + 1) * 128].astype(F32)
        lams = []
        for bi, p in chains:
            lw = ld(lw_ref, bi, p)
            hi = lw.astype(BF16)
            mid = (lw - hi.astype(F32)).astype(BF16)
            lo = (lw - hi.astype(F32) - mid.astype(F32)).astype(BF16)
            lams.append(dot(tri3, jnp.concatenate([hi, mid, lo], axis=0)))
        AR, BK, Vs, KB, dec = [], [], [], [], []
        for (bi, p), lam in zip(chains, lams):
            r, lw, k = ld(r_ref, bi, p), ld(lw_ref, bi, p), ld(k_ref, bi, p)
            v, a, b = ld(v_ref, bi, p), ld(a_ref, bi, p), ld(b_ref, bi, p)
            lam_c = lam[C - 1:C, :]
            e_neg = jnp.exp(-lam)
            e_tail = jnp.exp(lam_c - lam)
            AR.append(jnp.concatenate([stack(a * jnp.exp(lam - lw)), stack(r * jnp.exp(lam))],
                                      axis=0).astype(BF16))
            BK.append(jnp.concatenate([stack(b * e_neg), stack(k * e_neg)], axis=0).astype(BF16))
            KB.append(jnp.concatenate([stack(k * e_tail), stack(b * e_tail)], axis=0).astype(BF16))
            Vs.append(stack(v))
            dec.append(jnp.exp(lam_c))
        G = [nt(x, y) for x, y in zip(AR, BK)]
        I0 = [nt(x, s_scr[bi, p].astype(BF16)) for x, (bi, p) in zip(AR, chains)]
        rhs = [i0[:128] + dot((g[:128, 128:] * msl).astype(BF16), v.astype(BF16))
               for g, i0, v in zip(G, I0, Vs)]
        Nh, Nl, D = [], [], []
        for g in G:
            h, l = _split2(g[:128, :128] * msl)
            Nh.append(h)
            Nl.append(l)
            D.append(eye + (h * lvl_ref[0]).astype(F32) + (l * lvl_ref[0]).astype(F32))
        for lv in range(1, n_levels):
            m = lvl_ref[lv]
            Db = [d.astype(BF16) for d in D]
            X = [dot(h * m, db) for h, db in zip(Nh, Db)]
            D = [d + dot(db, x.astype(BF16)) for d, db, x in zip(D, Db, X)]
        Us = [dot(d.astype(BF16), x.astype(BF16)) for d, x in zip(D, rhs)]
        for (bi, p), g, i0, u, v in zip(chains, G, I0, Us, Vs):
            ys = i0[128:] + dot(jnp.concatenate([g[128:, :128] * mli, g[128:, 128:] * mli], axis=1).astype(BF16),
                                jnp.concatenate([u, v], axis=0).astype(BF16))
            y_ref[bi, rows, p * 128:(p + 1) * 128] = ys[:C] + ys[C:]
        for (bi, p), u, v, kb, d in zip(chains, Us, Vs, KB, dec):
            vu = jnp.concatenate([v, u], axis=0)
            s_scr[bi, p] = s_scr[bi, p] * d + dot(vu.T.astype(BF16), kb)
        return carry

    lax.fori_loop(0, n_chunks, chunk, 0)

    @pl.when(pl.program_id(1) == pl.num_programs(1) - 1)
    def _():
        so_ref[...] = s_scr[...]


def _rwkv_chunk_tables():
    C = RW_CHUNK
    t = np.arange(C)
    tri = (t[:, None] >= t[None, :]).astype(np.float32)
    tri3 = np.concatenate([tri, tri, tri], axis=1)
    i = np.arange(2 * C)
    same = (i[:, None] // C) == (i[None, :] // C)
    msl = (same & (i[:, None] > i[None, :])).astype(np.float32)
    mli = (same & (i[:, None] >= i[None, :])).astype(np.float32)
    levels = []
    n = 1
    while n < C:
        levels.append(same & ((i[:, None] // (2 * n)) == (i[None, :] // (2 * n)))
                      & ((i[:, None] // n) % 2 == 1) & ((i[None, :] // n) % 2 == 0))
        n *= 2
    return jnp.asarray(tri3, BF16), msl, mli, jnp.asarray(np.stack(levels).astype(np.float32), BF16)


def rwkv_chunk_scan(r, lw, k, v, a, b, tb, nb):
    B, L, _ = r.shape
    tri3, msl, mli, lvl = _rwkv_chunk_tables()
    blk = pl.BlockSpec((nb, tb, W_C), lambda bi, l: (bi, l, 0))
    c2 = lambda s: pl.BlockSpec(s, lambda bi, l: tuple(0 for _ in s))
    return pl.pallas_call(
        functools.partial(_rwkv_chunk_kernel, n_chunks=tb // RW_CHUNK, nb=nb),
        grid=(B // nb, L // tb),
        in_specs=[blk] * 6 + [c2(tri3.shape), c2(msl.shape), c2(mli.shape), c2(lvl.shape)],
        out_specs=[blk, pl.BlockSpec((nb, H_C // 2, 128, 128), lambda bi, l: (bi, 0, 0, 0))],
        out_shape=[jax.ShapeDtypeStruct((B, L, W_C), F32),
                   jax.ShapeDtypeStruct((B, H_C // 2, 128, 128), F32)],
        scratch_shapes=[pltpu.VMEM((nb, H_C // 2, 128, 128), F32)],
        compiler_params=_cp("parallel", "arbitrary"),
        name="rwkv_chunk_scan",
    )(r, lw, k, v, a, b, tri3, msl, mli, lvl)


def _rwkv_step_kernel(s_ref, w_ref, a_ref, b_ref, k_ref, r_ref, v_ref, stack_ref, y_ref, so_ref, vt_scr, y_scr):
    del stack_ref
    tr = lambda ref: ref[...].astype(F32).T
    wT, aT, bT, kT, rT = jnp.exp(tr(w_ref)), tr(a_ref), tr(b_ref), tr(k_ref), tr(r_ref)
    vt_scr[...] = tr(v_ref)
    for hh in range(2):
        ch = slice(hh * DH_C, (hh + 1) * DH_C)
        w, a, b, k, r = wT[ch], aT[ch], bT[ch], kT[ch], rT[ch]

        def value_row(i, carry):
            row = pl.ds(hh * DH_C + i, 1)
            S = s_ref[hh, i]
            sa = jnp.sum(S * a, axis=0, keepdims=True)
            Sn = S * w + sa * b + vt_scr[row, :] * k
            so_ref[hh, i] = Sn
            y_scr[row, :] = jnp.sum(Sn * r, axis=0, keepdims=True)
            return carry

        lax.fori_loop(0, DH_C, value_row, 0)
    y_ref[...] = y_scr[...].T


def rwkv_step(state_t, layer, lw, a, b, k, r, v, new_state_t):
    B = lw.shape[0]
    vec = pl.BlockSpec((B, 2 * DH_C), lambda p: (0, p))
    sspec = pl.BlockSpec((None, 2, DH_C, DH_C, B), lambda p: (layer, p, 0, 0, 0))
    return pl.pallas_call(
        _rwkv_step_kernel,
        grid=(H_C // 2,),
        in_specs=[sspec] + [vec] * 6 + [pl.BlockSpec(memory_space=pl.ANY)],
        out_specs=[vec, sspec],
        out_shape=[jax.ShapeDtypeStruct((B, W_C), F32),
                   jax.ShapeDtypeStruct(new_state_t.shape, F32)],
        scratch_shapes=[pltpu.VMEM((2 * DH_C, B), F32), pltpu.VMEM((2 * DH_C, B), F32)],
        input_output_aliases={7: 1},
        compiler_params=_cp("parallel"),
        name="rwkv_step",
    )(state_t, lw, a, b, k, r, v, new_state_t)


def _merge_kernel(x_ref, ya_ref, yb_ref, y_ref, r_ref, k_ref, v_ref, gg_ref, za_ref, zb_ref, zc_ref, gt_ref,
                  lng_ref, lnb_ref, rk_ref, ones_ref, wa_ref, wb_ref, wc_ref, wo_ref, o_ref):
    ones2 = ones_ref[...]
    y = y_ref[...]
    mu = _seg_sum_wide(y, ones2) * (1.0 / DH_C)
    d = y - mu
    var = _seg_sum_wide(d * d, ones2) * (1.0 / DH_C)
    ycn = d * lax.rsqrt(var + LNX_EPS) * lng_ref[...] + lnb_ref[...]
    f32 = lambda ref: ref[...].astype(F32)
    bonus = _seg_sum_wide(f32(r_ref) * f32(k_ref) * rk_ref[...], ones2) * f32(v_ref)
    yc = (ycn + bonus) * f32(gg_ref)
    proj = lambda t, w: jnp.dot(t.astype(BF16), w[...], preferred_element_type=F32)
    merged = (jax.nn.sigmoid(za_ref[...]) * proj(ya_ref[...], wa_ref)
              + jax.nn.sigmoid(zb_ref[...]) * proj(yb_ref[...], wb_ref)
              + jax.nn.sigmoid(zc_ref[...]) * proj(yc, wc_ref))
    o_ref[...] = x_ref[...] + gt_ref[...] * proj(merged, wo_ref)


def merge_out(x2, ya, yb, y, r, k, v, gg, z2, gt, lng, lnb, rk, wa, wb, wc, wo, tm):
    T, D = x2.shape
    G, R, _ = gt.shape
    tiles_per_group = (T // tm) // G
    tok = lambda w: pl.BlockSpec((tm, w), lambda m: (m, 0))
    gate = lambda idx: pl.BlockSpec((tm, D), lambda m: (m, Z_GATE // D + idx))
    row = lambda w: pl.BlockSpec((1, w), lambda m: (0, 0))
    full = lambda a: pl.BlockSpec(a.shape, lambda m: (0, 0))
    return pl.pallas_call(
        _merge_kernel,
        grid=(T // tm,),
        in_specs=[tok(D)] + [tok(W_C)] * 7 + [gate(0), gate(1), gate(2),
                  pl.BlockSpec((None, R, D), lambda m: (m // tiles_per_group, 0, 0)),
                  row(W_C), row(W_C), row(W_C), pl.BlockSpec((128, 128), lambda m: (0, 0)),
                  full(wa), full(wb), full(wc), full(wo)],
        out_specs=tok(D),
        out_shape=jax.ShapeDtypeStruct((T, D), F32),
        compiler_params=_cp("parallel"),
        name="merge_out",
    )(x2, ya, yb, y, r, k, v, gg, z2, z2, z2, gt, lng, lnb, rk, _block_ones(DH_C), wa, wb, wc, wo)


def _ffn_kernel(x_ref, g_ref, sc_ref, sh_ref, gt_ref, w1_ref, w2_ref, o_ref, h_scr, acc_scr):
    f = pl.program_id(1)

    @pl.when(f == 0)
    def _():
        x = x_ref[...]
        y = x * lax.rsqrt(jnp.mean(x * x, axis=-1, keepdims=True) + EPS) * g_ref[...]
        h_scr[...] = (y * (1.0 + sc_ref[...]) + sh_ref[...]).astype(BF16)
        acc_scr[...] = jnp.zeros_like(acc_scr)

    u = jnp.maximum(jnp.dot(h_scr[...], w1_ref[...], preferred_element_type=F32), 0.0)
    acc_scr[...] += jnp.dot((u * u).astype(BF16), w2_ref[...], preferred_element_type=F32)

    @pl.when(f == pl.num_programs(1) - 1)
    def _():
        o_ref[...] = x_ref[...] + gt_ref[...] * acc_scr[...]


def ffn(x2, gain_row, sc, sh, gt, w1, w2, tm, tf):
    T, D = x2.shape
    F = w1.shape[1]
    G, R, _ = sc.shape
    tiles_per_group = (T // tm) // G
    mod_spec = pl.BlockSpec((None, R, D), lambda m, f: (m // tiles_per_group, 0, 0))
    return pl.pallas_call(
        _ffn_kernel,
        grid=(T // tm, F // tf),
        in_specs=[pl.BlockSpec((tm, D), lambda m, f: (m, 0)),
                  pl.BlockSpec((1, D), lambda m, f: (0, 0)),
                  mod_spec, mod_spec, mod_spec,
                  pl.BlockSpec((D, tf), lambda m, f: (0, f)),
                  pl.BlockSpec((tf, D), lambda m, f: (f, 0))],
        out_specs=pl.BlockSpec((tm, D), lambda m, f: (m, 0)),
        out_shape=jax.ShapeDtypeStruct((T, D), F32),
        scratch_shapes=[pltpu.VMEM((tm, D), BF16), pltpu.VMEM((tm, D), F32)],
        compiler_params=_cp("parallel", "arbitrary"),
        name="ffn",
    )(x2, gain_row, sc, sh, gt, w1, w2)


def prep_w_in(w_in):
    depth, D, _ = w_in.shape
    wt = jnp.transpose(w_in, (0, 2, 1))
    o_diff, o_rwkv, o_gate = RET_COLS, RET_COLS + DIFF_COLS, RET_COLS + DIFF_COLS + RWKV_COLS
    return jnp.concatenate([
        wt[:, :RET_COLS], wt[:, o_rwkv:o_gate], jnp.zeros((depth, RWKV_PAD - RWKV_COLS, D), F32),
        wt[:, o_gate:], wt[:, o_diff:o_rwkv]], axis=1).astype(BF16)


def _prep_layer(p, l, w_in_p):
    place = lambda w, slot: jnp.pad(w, ((0, 0), (slot * W_C, (2 - slot) * W_C)))
    w_lr = jnp.concatenate([place(p['w_decay_up'][l], 0), place(p['w_a_up'][l], 1), place(p['w_g_up'][l], 2),
                            jnp.zeros((LR_COLS - R_DECAY - R_A - R_G, 3 * W_C), F32)], axis=0)
    row = lambda t: t.reshape(1, -1)
    lp = dict(
        layer=l, w_ada=p['w_ada'], b_ada=row(p['b_ada'][l]),
        norm1=row(p['norm1'][l]), norm2=row(p['norm2'][l]), w_in=w_in_p,
        gq=row(jnp.tile(p['qk_norm_q'][l], 2 * H_B)), gk=row(jnp.tile(p['qk_norm_k'][l], 2 * H_B)),
        lam_rows=jnp.stack([p['lambda_q1'][l], p['lambda_k1'][l], p['lambda_q2'][l], p['lambda_k2'][l]]),
        subln=row(p['subln_diff'][l]),
        mu=row(jnp.pad(p['mu_shift'][l], (0, RWKV_PAD - RWKV_COLS))),
        w0=row(p['w0'][l]), a0=row(p['a0'][l]), k_k=row(p['k_k'][l]), k_a=row(p['k_a'][l]),
        w_lr=w_lr.astype(BF16), r_k=row(p['r_k'][l]), lnx_g=row(p['lnx_g'][l]), lnx_b=row(p['lnx_b'][l]),
        w_up_a=p['w_up_a'][l].astype(BF16), w_up_b=p['w_up_b'][l].astype(BF16),
        w_up_c=p['w_up_c'][l].astype(BF16), w_out=p['w_out'][l].astype(BF16),
        w_ff1=p['w_ff1'][l].astype(BF16), w_ff2=p['w_ff2'][l].astype(BF16),
        lam_init=0.8 - 0.6 * math.exp(-0.3 * l),
    )
    if l > 0:
        lp['v0'] = row(p['v0'][l - 1])
        lp['w_v_down'] = jnp.pad(p['w_v_down'][l - 1], ((0, 0), (0, 128 - R_V))).astype(BF16)
        lp['w_v_up'] = jnp.pad(p['w_v_up'][l - 1], ((0, 128 - R_V), (0, 0))).astype(BF16)
    return lp


def _modulation(c, lp, per_token):
    B, D = c.shape
    start = 0 if not per_token else lp['mod'].shape[0] - B
    mod = lp['mod'][start:start + B]
    parts = [mod[:, i * D:(i + 1) * D] for i in range(6)]
    shape = (1, B, D) if per_token else (B, 1, D)
    return [t.reshape(shape) for t in parts]


def _trunk_prompt(x, c, layers, bias_p):
    B, L, D = x.shape
    T = B * L
    tiles = _prompt_tiles(L)
    x2 = x.reshape(T, D)
    cos_t, sin_t = _rope_tables(jnp.arange(L))
    ret_out, rwkv_out, shift_out = [], [], []
    k_st = jnp.zeros((len(layers), T * H_B, DV_B), F32)
    v_st = jnp.zeros((len(layers), T * H_B, DV_B), F32)
    v_first = None
    for l, lp in enumerate(layers):
        sh1, sc1, gt1, sh2, sc2, gt2 = _modulation(c, lp, per_token=False)
        z2 = in_proj(x2, lp['norm1'], sc1, sh1, lp['w_in'], l, tiles['in_proj_rows'],
                     tiles['in_proj_cols'])
        ZC = z2.shape[1]
        z3 = z2.reshape(B, L, ZC)
        ya, s_ret = retention_prompt(z3, cos_t, sin_t, tiles['retention'])
        qb, k_st, kb, v_st, vb = qk_norm(z2, lp['gq'], lp['gk'], tiles['qk_norm'],
                                         DH_B ** -0.5 * math.log2(math.e), l, k_st, v_st)
        r3 = lambda t: t.reshape(B, L, W_B)
        yb = diff_attention_prompt(r3(qb), r3(kb), r3(vb), bias_p, lp['lam_rows'], lp['subln'],
                                   lp['lam_init'], tiles['attention'])
        vgate = None if l == 0 else (lp['v0'], lp['w_v_down'], lp['w_v_up'], v_first)
        shift0 = jnp.zeros((B, 1, RWKV_PAD), F32)
        r, lw, k, v, a, b, gg = rwkv_prep(z3, shift0, lp['mu'], lp['w0'], lp['a0'], lp['k_k'], lp['k_a'],
                                          lp['w_lr'], vgate, tiles['rwkv_prep'], carry_shift=True)
        if l == 0:
            v_first = v
        y, s_pair = rwkv_chunk_scan(r, lw, k, v, a, b, tiles['rwkv_scan'], B)
        shift_out.append(z3[:, L - 1:, Z_RWKV:Z_RWKV + RWKV_COLS])
        f2 = lambda t: t.reshape(T, -1)
        x2 = merge_out(x2, f2(ya), f2(yb), f2(y), f2(r), f2(k), f2(v), f2(gg), z2, gt1, lp['lnx_g'],
                       lp['lnx_b'], lp['r_k'], lp['w_up_a'], lp['w_up_b'], lp['w_up_c'], lp['w_out'],
                       tiles['merge'])
        x2 = ffn(x2, lp['norm2'], sc2, sh2, gt2, lp['w_ff1'], lp['w_ff2'], tiles['ffn_rows'],
                 tiles['ffn_hidden'])
        ret_out.append(s_ret)
        sp = s_pair.reshape(B, H_C // 2, 2, DH_C, 2, DH_C)
        rwkv_out.append(jnp.stack([sp[:, :, 0, :, 0, :], sp[:, :, 1, :, 1, :]], axis=2)
                        .reshape(B, H_C, DH_C, DH_C))
    kv_shape = (len(layers), B, L, H_B, DV_B)
    return (x2.reshape(B, L, D), k_st.reshape(kv_shape), v_st.reshape(kv_shape), jnp.stack(ret_out),
            jnp.stack(rwkv_out), jnp.stack(shift_out))


def _trunk_decode(x, c, layers, bias_d, state_ret, state_rwkv, state_shift, cache_k, cache_v, page_table):
    B, _, D = x.shape
    n_pages = page_table.shape[1]
    past = n_pages * PAGE_SIZE
    x2 = x.reshape(B, D)
    pt_flat = page_table.reshape(-1)
    ck = cache_k.reshape(cache_k.shape[0], cache_k.shape[1], PAGE_SIZE * H_B, 2 * DH_B)
    cv = cache_v.reshape(cache_v.shape[0], cache_v.shape[1], PAGE_SIZE * H_B, DV_B)
    state_t = jnp.transpose(state_rwkv, (0, 2, 3, 4, 1))
    shift_out = []
    ret_st = jnp.zeros(state_ret.shape, F32)
    rwkv_st = jnp.zeros(state_t.shape, F32)
    k_st = jnp.zeros((len(layers), B * H_B, DV_B), F32)
    v_st = jnp.zeros((len(layers), B * H_B, DV_B), F32)
    v_first = None
    for l, lp in enumerate(layers):
        sh1, sc1, gt1, sh2, sc2, gt2 = _modulation(c, lp, per_token=True)
        z2 = in_proj(x2, lp['norm1'], sc1, sh1, lp['w_in'], l, B, 512)
        ZC = z2.shape[1]
        ya, ret_st = retention_step(z2, state_ret, l, past, ret_st)
        qb, k_st, kb, v_st, vb = qk_norm(z2, lp['gq'], lp['gk'], B, DH_B ** -0.5, l, k_st, v_st)
        h3 = lambda t: t[l].reshape(B, H_B, DV_B)
        yb = diff_attention_decode(qb.reshape(B, 1, W_BQK), h3(k_st), h3(v_st), ck, cv, pt_flat, bias_d,
                                   lp['lam_rows'], lp['subln'], l, lp['lam_init'], n_pages).reshape(B, W_B)
        vgate = None if l == 0 else (lp['v0'], lp['w_v_down'], lp['w_v_up'], v_first)
        prev = jnp.pad(state_shift[l].reshape(1, B, RWKV_COLS), ((0, 0), (0, 0), (0, RWKV_PAD - RWKV_COLS)))
        r, lw, k, v, a, b, gg = rwkv_prep(z2.reshape(1, B, ZC), prev, lp['mu'], lp['w0'], lp['a0'], lp['k_k'],
                                          lp['k_a'], lp['w_lr'], vgate, B, carry_shift=False)
        if l == 0:
            v_first = v
        f2 = lambda t: t.reshape(B, -1)
        y_dec, rwkv_st = rwkv_step(state_t, l, f2(lw), f2(a), f2(b), f2(k), f2(r), f2(v), rwkv_st)
        shift_out.append(z2[:, Z_RWKV:Z_RWKV + RWKV_COLS].reshape(B, 1, RWKV_COLS))
        x2 = merge_out(x2, ya, yb, y_dec, f2(r), f2(k), f2(v), f2(gg), z2, gt1, lp['lnx_g'], lp['lnx_b'],
                       lp['r_k'], lp['w_up_a'], lp['w_up_b'], lp['w_up_c'], lp['w_out'], B)
        x2 = ffn(x2, lp['norm2'], sc2, sh2, gt2, lp['w_ff1'], lp['w_ff2'], B, 1024)
    rwkv_state = jnp.transpose(rwkv_st, (0, 4, 1, 2, 3))
    kv_shape = (len(layers), B, 1, H_B, DV_B)
    return (x2.reshape(B, 1, D), k_st.reshape(kv_shape), v_st.reshape(kv_shape), ret_st,
            rwkv_state, jnp.stack(shift_out))


def _decode_buckets(n_pages):
    past = n_pages * PAGE_SIZE
    key = np.arange((n_pages + 1) * PAGE_SIZE)
    bk = np.where(key <= past, _bucket_np(np.maximum(past - key, 0)), -1)
    return np.broadcast_to(bk.reshape(n_pages + 1, 1, PAGE_SIZE), (n_pages + 1, 8, PAGE_SIZE)).astype(np.int32)


def kernel(x_prompt, x_sample, c_prompt, c_sample, cache_k_diff, cache_v_diff, page_table, state_ret, state_rwkv, state_shift, rel_bias, w_ada, b_ada, norm1, norm2, w_in, qk_norm_q, qk_norm_k, lambda_q1, lambda_k1, lambda_q2, lambda_k2, subln_diff, mu_shift, w0, w_decay_up, a0, w_a_up, w_g_up, v0, w_v_down, w_v_up, k_k, k_a, r_k, lnx_g, lnx_b, w_up_a, w_up_b, w_up_c, w_out, w_ff1, w_ff2):
    p = dict(w_ada=w_ada, b_ada=b_ada, norm1=norm1, norm2=norm2, w_in=w_in,
             qk_norm_q=qk_norm_q, qk_norm_k=qk_norm_k, lambda_q1=lambda_q1, lambda_k1=lambda_k1,
             lambda_q2=lambda_q2, lambda_k2=lambda_k2, subln_diff=subln_diff, mu_shift=mu_shift,
             w0=w0, w_decay_up=w_decay_up, a0=a0, w_a_up=w_a_up, w_g_up=w_g_up, v0=v0,
             w_v_down=w_v_down, w_v_up=w_v_up, k_k=k_k, k_a=k_a, r_k=r_k, lnx_g=lnx_g, lnx_b=lnx_b,
             w_up_a=w_up_a, w_up_b=w_up_b, w_up_c=w_up_c, w_out=w_out, w_ff1=w_ff1, w_ff2=w_ff2)
    depth = w_in.shape[0]
    w_in_p = prep_w_in(w_in)
    layers = [_prep_layer(p, l, w_in_p) for l in range(depth)]
    bp = c_prompt.shape[0]
    c_all = jnp.concatenate([jnp.pad(c_prompt, ((0, -bp % 16), (0, 0))), c_sample], axis=0)
    for lp in layers:
        lp['mod'] = ada_mod(c_all, lp['w_ada'], lp['b_ada'], lp['layer'])
    L = x_prompt.shape[1]
    n_pages = page_table.shape[1]
    bias_p = bias_tiles(rel_bias, _prompt_buckets(_prompt_tiles(L)['attention']), True)
    bd = bias_tiles(rel_bias, _decode_buckets(n_pages).reshape((n_pages + 1) * 8, PAGE_SIZE), False)
    bias_d = bd.reshape(H_B, n_pages + 1, 8, PAGE_SIZE)

    y_p, k_p, v_p, ret_p, rwkv_p, shift_p = _trunk_prompt(x_prompt, c_prompt, layers, bias_p)
    y_s, k_s, v_s, ret_s, rwkv_s, shift_s = _trunk_decode(
        x_sample, c_sample, layers, bias_d, state_ret, state_rwkv, state_shift,
        cache_k_diff, cache_v_diff, page_table)
    return (y_p, y_s, k_p, v_p, k_s, v_s, ret_p, ret_s, rwkv_p, rwkv_s, shift_p, shift_s)
```

```python
import functools
import math

import numpy as np
import jax
import jax.numpy as jnp
from jax import lax
from jax.experimental import pallas as pl
from jax.experimental.pallas import tpu as pltpu

F32 = jnp.float32
BF16 = jnp.bfloat16

H_A, DK_A, DV_A = 4, 128, 128
RET_CHUNK = 128
ROPE_BASE = 10000.0
H_B, DH_B, DV_B = 4, 64, 128
N_BUCKETS, MAX_DISTANCE = 32, 128
H_C, DH_C = 8, 64
R_DECAY, R_A, R_V, R_G = 64, 64, 32, 160
LNX_EPS = 64e-5
EPS = 1e-6
PAGE_SIZE = 128
W_AQ, W_A = H_A * DK_A, H_A * DV_A
W_BQK, W_B = H_B * 2 * DH_B, H_B * DV_B
W_C = H_C * DH_C
RET_COLS = 2 * W_AQ + 2 * W_A
DIFF_COLS = 2 * W_BQK + W_B
RWKV_COLS = 3 * W_C + R_DECAY + R_A + R_G
RWKV_PAD = 2048
LR_COLS = RWKV_PAD - 3 * W_C

Z_RET, Z_RWKV = 0, RET_COLS
Z_GATE = Z_RWKV + RWKV_PAD
NEG = -1e30
RW_CHUNK = 64
V7X_VMEM_BYTES = 64 * 1024 * 1024
VMEM_LIMIT = V7X_VMEM_BYTES * 7 // 8


def _cp(*sem):
    return pltpu.CompilerParams(dimension_semantics=sem, vmem_limit_bytes=VMEM_LIMIT)


def _prompt_tiles(L):
    cap = lambda n: min(n, L)
    return dict(in_proj_rows=cap(1024), in_proj_cols=2176, retention=cap(512), qk_norm=cap(1024),
                attention=cap(512), rwkv_prep=cap(512), rwkv_scan=cap(256), merge=cap(512),
                ffn_rows=cap(1024), ffn_hidden=1024)


def _silu(x):
    return x * jax.nn.sigmoid(x)


def _split2(x):
    hi = x.astype(BF16)
    lo = (x - hi.astype(F32)).astype(BF16)
    return hi, lo


def _seg_sum(x, ones):
    return jnp.dot(x.astype(BF16), ones, preferred_element_type=F32)


def _seg_sum_wide(x, ones2):
    return jnp.concatenate(
        [_seg_sum(x[:, c * 128:(c + 1) * 128], ones2) for c in range(x.shape[1] // 128)], axis=1)


def _mod_kernel(c_ref, w_ref, b_ref, o_ref):
    s = _silu(c_ref[...])
    o_ref[...] = jnp.dot(s.astype(BF16), w_ref[...].astype(BF16), preferred_element_type=F32) + b_ref[...]


def ada_mod(c_pad, w_stack, b_row, layer):
    R, D = c_pad.shape
    N = w_stack.shape[2]
    tn = 1536
    return pl.pallas_call(
        _mod_kernel,
        grid=(N // tn,),
        in_specs=[pl.BlockSpec((R, D), lambda n: (0, 0)),
                  pl.BlockSpec((None, D, tn), lambda n: (layer, 0, n)),
                  pl.BlockSpec((1, tn), lambda n: (0, n))],
        out_specs=pl.BlockSpec((R, tn), lambda n: (0, n)),
        out_shape=jax.ShapeDtypeStruct((R, N), F32),
        compiler_params=_cp("parallel"),
        name="ada_mod",
    )(c_pad, w_stack, b_row)


def _inproj_kernel(x_ref, g_ref, sc_ref, sh_ref, w_ref, z_ref, h_scr):
    @pl.when(pl.program_id(1) == 0)
    def _():
        x = x_ref[...]
        y = x * lax.rsqrt(jnp.mean(x * x, axis=-1, keepdims=True) + EPS) * g_ref[...]
        h_scr[...] = (y * (1.0 + sc_ref[...]) + sh_ref[...]).astype(BF16)

    z_ref[...] = lax.dot_general(h_scr[...], w_ref[...], (((1,), (1,)), ((), ())),
                                 preferred_element_type=F32)


def in_proj(x2, gain_row, sc, sh, w_bf, layer, tm, tn):
    T, D = x2.shape
    N = w_bf.shape[1]
    G, R, _ = sc.shape
    tiles_per_group = (T // tm) // G
    mod_spec = pl.BlockSpec((None, R, D), lambda m, n: (m // tiles_per_group, 0, 0))
    return pl.pallas_call(
        _inproj_kernel,
        grid=(T // tm, N // tn),
        in_specs=[pl.BlockSpec((tm, D), lambda m, n: (m, 0)),
                  pl.BlockSpec((1, D), lambda m, n: (0, 0)),
                  mod_spec, mod_spec,
                  pl.BlockSpec((None, tn, D), lambda m, n: (layer, n, 0))],
        out_specs=pl.BlockSpec((tm, tn), lambda m, n: (m, n)),
        out_shape=jax.ShapeDtypeStruct((T, N), F32),
        scratch_shapes=[pltpu.VMEM((tm, D), BF16)],
        compiler_params=_cp("parallel", "arbitrary"),
        name="in_proj",
    )(x2, gain_row, sc, sh, w_bf)


def _rope(x, cos, sin_signed):
    return x * cos + pltpu.roll(x, DK_A // 2, 1) * sin_signed


def _ret_kernel(q_ref, k_ref, v_ref, g_ref, cos_ref, sin_ref, inner_ref, cross_ref, tail_ref, cd_ref,
                ya_ref, so_ref, s_scr, *, n_sub):
    lt = pl.program_id(1)

    @pl.when(lt == 0)
    def _():
        s_scr[...] = jnp.zeros_like(s_scr)

    C = RET_CHUNK
    heads = range(H_A)
    cols = [slice(h * DK_A, (h + 1) * DK_A) for h in heads]
    dot = lambda x, y: jnp.dot(x, y, preferred_element_type=F32)
    for c in range(n_sub):
        rows = slice(c * C, (c + 1) * C)
        cos = cos_ref[rows, :]
        sin = sin_ref[rows, :]
        qb, kb, vb, kt = [], [], [], []
        for h in heads:
            q = _rope(q_ref[rows, cols[h]], cos, sin)
            k = _rope(k_ref[rows, cols[h]], cos, sin) * (DK_A ** -0.5)
            qb.append(q.astype(BF16))
            kb.append(k.astype(BF16))
            vb.append(v_ref[rows, cols[h]].astype(BF16))
            kt.append((k * tail_ref[h]).T.astype(BF16))
        S = [s_scr[h] for h in heads]
        sc = [lax.dot_general(qb[h], kb[h], (((1,), (1,)), ((), ())), preferred_element_type=F32) * inner_ref[h]
              for h in heads]
        qs = [dot(qb[h], S[h].astype(BF16)) * cross_ref[h] for h in heads]
        kv = [dot(kt[h], vb[h]) for h in heads]
        o = [dot(sc[h].astype(BF16), vb[h]) + qs[h] for h in heads]
        for h in heads:
            s_scr[h] = S[h] * cd_ref[h] + kv[h]
            on = o[h] * lax.rsqrt(jnp.mean(o[h] * o[h], axis=-1, keepdims=True) + EPS)
            ya_ref[rows, cols[h]] = (on * _silu(g_ref[rows, cols[h]])).astype(ya_ref.dtype)

    @pl.when(lt == pl.num_programs(1) - 1)
    def _():
        so_ref[...] = s_scr[...]


def _ret_tables(L):
    C = math.gcd(L, RET_CHUNK)
    log_g = np.log1p(-np.exp2(-5.0 - np.arange(H_A, dtype=np.float32))).astype(np.float32)
    i = np.arange(C, dtype=np.float32)
    dist = i[:, None] - i[None, :]
    causal = dist >= 0
    inner = np.where(causal[None], np.exp(np.where(causal, dist, 0.0)[None] * log_g[:, None, None]), 0.0)
    cross = np.exp((i[None, :] + 1.0) * log_g[:, None])
    tail = np.exp((C - 1.0 - i)[None, :] * log_g[:, None])
    chunk = np.exp(C * log_g)
    bc = lambda t: np.broadcast_to(t[:, :, None], (H_A, C, 128)).astype(np.float32)
    cd = np.broadcast_to(chunk[:, None, None], (H_A, 1, 128)).astype(np.float32)
    return inner.astype(np.float32), bc(cross), bc(tail), cd


def _rope_tables(pos):
    half = DK_A // 2
    inv = ROPE_BASE ** (-jnp.arange(half, dtype=F32) / half)
    ang = pos.astype(F32)[:, None] * inv[None, :]
    cos, sin = jnp.cos(ang), jnp.sin(ang)
    return jnp.concatenate([cos, cos], axis=-1), jnp.concatenate([-sin, sin], axis=-1)


def retention_prompt(z3, cos_t, sin_t, tb):
    B, L, ZC = z3.shape
    inner, cross, tail, cd = _ret_tables(L)
    zspec = lambda idx: pl.BlockSpec((None, tb, W_AQ), lambda b, l: (b, l, idx))
    full3 = lambda s: pl.BlockSpec(s, lambda b, l: (0, 0, 0))
    return pl.pallas_call(
        functools.partial(_ret_kernel, n_sub=tb // RET_CHUNK),
        grid=(B, L // tb),
        in_specs=[zspec(0), zspec(1), zspec(2), zspec(3),
                  pl.BlockSpec((tb, 128), lambda b, l: (l, 0)),
                  pl.BlockSpec((tb, 128), lambda b, l: (l, 0)),
                  full3(inner.shape), full3(cross.shape), full3(tail.shape), full3(cd.shape)],
        out_specs=[pl.BlockSpec((None, tb, W_A), lambda b, l: (b, l, 0)),
                   pl.BlockSpec((None, H_A, DK_A, DV_A), lambda b, l: (b, 0, 0, 0))],
        out_shape=[jax.ShapeDtypeStruct((B, L, W_A), BF16),
                   jax.ShapeDtypeStruct((B, H_A, DK_A, DV_A), F32)],
        scratch_shapes=[pltpu.VMEM((H_A, DK_A, DV_A), F32)],
        compiler_params=_cp("parallel", "arbitrary"),
        name="retention_prompt",
    )(z3, z3, z3, z3, cos_t, sin_t, inner, cross, tail, cd)


def _ret_step_kernel(q_ref, k_ref, v_ref, g_ref, cos_ref, sin_ref, gam_ref, s_ref, stack_ref, ya_ref, so_ref,
                     *, bb):
    del stack_ref
    cos, sin = cos_ref[...], sin_ref[...]
    row = lax.broadcasted_iota(jnp.int32, (bb, 128), 0)
    for h in range(H_A):
        cols = slice(h * DK_A, (h + 1) * DK_A)
        gam = gam_ref[h]
        q = _rope(q_ref[:, cols], cos, sin)
        k = _rope(k_ref[:, cols], cos, sin) * (DK_A ** -0.5)
        v = v_ref[:, cols]
        qb = q.astype(BF16)
        qk = jnp.sum(qb.astype(F32) * k.astype(BF16).astype(F32), axis=-1, keepdims=True)
        qs = jnp.zeros((bb, 128), F32)
        vb = v.astype(BF16)
        for b in range(0, bb, 2):
            S0, S1 = s_ref[b, h], s_ref[b + 1, h]
            first, second = row == b, row == b + 1
            q_s = jnp.dot(qb, jnp.concatenate([S0, S1], axis=1).astype(BF16), preferred_element_type=F32)
            qs = jnp.where(first, q_s[:, :DV_A], jnp.where(second, q_s[:, DV_A:], qs))
            k_pair = jnp.where(jnp.logical_or(first, second), k, 0.0).T.astype(BF16)
            v_pair = jnp.concatenate([jnp.where(first, vb, jnp.zeros_like(vb)),
                                      jnp.where(second, vb, jnp.zeros_like(vb))], axis=1)
            kv = jnp.dot(k_pair, v_pair, preferred_element_type=F32)
            so_ref[b, h] = S0 * gam + kv[:, :DV_A]
            so_ref[b + 1, h] = S1 * gam + kv[:, DV_A:]
        o = qk * v.astype(BF16).astype(F32) + qs * gam
        on = o * lax.rsqrt(jnp.mean(o * o, axis=-1, keepdims=True) + EPS)
        ya_ref[:, cols] = on * _silu(g_ref[:, cols])


def retention_step(z2, state, layer, pos, new_state):
    B = z2.shape[0]
    bb = 8
    cos_t, sin_t = _rope_tables(jnp.full((1,), pos))
    log_g = np.log1p(-np.exp2(-5.0 - np.arange(H_A, dtype=np.float32))).astype(np.float32)
    gam = np.broadcast_to(np.exp(log_g)[:, None, None], (H_A, 1, 128)).astype(np.float32)
    zspec = lambda idx: pl.BlockSpec((bb, W_AQ), lambda i: (i, idx))
    return pl.pallas_call(
        functools.partial(_ret_step_kernel, bb=bb),
        grid=(B // bb,),
        in_specs=[zspec(0), zspec(1), zspec(2), zspec(3),
                  pl.BlockSpec((1, 128), lambda i: (0, 0)),
                  pl.BlockSpec((1, 128), lambda i: (0, 0)),
                  pl.BlockSpec((H_A, 1, 128), lambda i: (0, 0, 0)),
                  pl.BlockSpec((None, bb, H_A, DK_A, DV_A), lambda i: (layer, i, 0, 0, 0)),
                  pl.BlockSpec(memory_space=pl.ANY)],
        out_specs=[pl.BlockSpec((bb, W_A), lambda i: (i, 0)),
                   pl.BlockSpec((None, bb, H_A, DK_A, DV_A), lambda i: (layer, i, 0, 0, 0))],
        out_shape=[jax.ShapeDtypeStruct((B, W_A), F32),
                   jax.ShapeDtypeStruct(new_state.shape, F32)],
        input_output_aliases={8: 1},
        compiler_params=_cp("parallel"),
        name="retention_step",
    )(z2, z2, z2, z2, cos_t, sin_t, gam, state, new_state)


def _qknorm_kernel(q_ref, k_ref, v_ref, gq_ref, gk_ref, ones_ref, k_stack_ref, v_stack_ref,
                   qb_ref, k32_ref, kb_ref, v32_ref, vb_ref, *, q_scale):
    del k_stack_ref, v_stack_ref
    ones2 = ones_ref[...]
    q, k, v = q_ref[...], k_ref[...], v_ref[...]
    tm = q.shape[0]
    qn = q * lax.rsqrt(_seg_sum_wide(q * q, ones2) * (1.0 / DH_B) + EPS) * gq_ref[...] * q_scale
    kn = k * lax.rsqrt(_seg_sum_wide(k * k, ones2) * (1.0 / DH_B) + EPS) * gk_ref[...]
    qb_ref[...] = qn.astype(BF16)
    kb_ref[...] = kn.astype(BF16)
    vb_ref[...] = v.astype(BF16)
    for h in range(H_B):
        head_rows = pl.ds(h, tm, stride=H_B)
        k32_ref[head_rows, :] = kn[:, h * DV_B:(h + 1) * DV_B]
        v32_ref[head_rows, :] = v[:, h * DV_B:(h + 1) * DV_B]


def _block_ones(group):
    i = np.arange(128)
    return jnp.asarray((i[:, None] // group == i[None, :] // group).astype(np.float32), BF16)


def qk_norm(z2, gq_row, gk_row, tm, q_scale, layer, k_stack, v_stack):
    T = z2.shape[0]
    base = (Z_GATE + 3 * 1024) // W_BQK
    zspec = lambda idx: pl.BlockSpec((tm, W_BQK), lambda m: (m, base + idx))
    row = pl.BlockSpec((1, W_BQK), lambda m: (0, 0))
    out = pl.BlockSpec((tm, W_BQK), lambda m: (m, 0))
    out_rows = pl.BlockSpec((None, tm * H_B, DV_B), lambda m: (layer, m, 0))
    stack = pl.BlockSpec(memory_space=pl.ANY)
    return pl.pallas_call(
        functools.partial(_qknorm_kernel, q_scale=q_scale),
        grid=(T // tm,),
        in_specs=[zspec(0), zspec(1), zspec(2), row, row, pl.BlockSpec((128, 128), lambda m: (0, 0)),
                  stack, stack],
        out_specs=[out, out_rows, out, out_rows, out],
        out_shape=[jax.ShapeDtypeStruct((T, W_BQK), BF16), jax.ShapeDtypeStruct(k_stack.shape, F32),
                   jax.ShapeDtypeStruct((T, W_BQK), BF16), jax.ShapeDtypeStruct(v_stack.shape, F32),
                   jax.ShapeDtypeStruct((T, W_B), BF16)],
        input_output_aliases={6: 1, 7: 3},
        compiler_params=_cp("parallel"),
        name="qk_norm",
    )(z2, z2, z2, gq_row, gk_row, _block_ones(DH_B), k_stack, v_stack)


def _bucket_np(n):
    max_exact = N_BUCKETS // 2
    nf = np.maximum(n, 1).astype(np.float32)
    large = max_exact + (np.log(nf / np.float32(max_exact)) / np.float32(math.log(MAX_DISTANCE / max_exact))
                         * np.float32(N_BUCKETS - max_exact)).astype(np.int32)
    large = np.minimum(large, N_BUCKETS - 1)
    return np.where(n < max_exact, n, large).astype(np.int32)


def _bias_kernel(tab_ref, bkt_ref, o_ref, *, log2_far_shift):
    h = pl.program_id(0)
    bk = bkt_ref[...]
    acc = jnp.zeros(bk.shape, F32)
    for b in range(N_BUCKETS):
        acc = jnp.where(bk == b, tab_ref[b, h], acc)
    if log2_far_shift:
        acc = (acc - tab_ref[N_BUCKETS - 1, h]) * math.log2(math.e)
    o_ref[...] = jnp.where(bk < 0, NEG, acc)


def bias_tiles(rel_bias, buckets, log2_far_shift):
    R, C = buckets.shape
    tr = min(R, 512)
    return pl.pallas_call(
        functools.partial(_bias_kernel, log2_far_shift=log2_far_shift),
        grid=(H_B, R // tr),
        in_specs=[pl.BlockSpec(memory_space=pltpu.SMEM),
                  pl.BlockSpec((tr, C), lambda h, r: (r, 0))],
        out_specs=pl.BlockSpec((None, tr, C), lambda h, r: (h, r, 0)),
        out_shape=jax.ShapeDtypeStruct((H_B, R, C), F32),
        compiler_params=_cp("parallel", "parallel"),
        name="bias_tiles",
    )(rel_bias, jnp.asarray(buckets))


def _prompt_buckets(t):
    r = np.arange(t)[:, None]
    c = np.arange(t)[None, :]
    diag = np.where(c <= r, _bucket_np(np.maximum(r - c, 0)), -1)
    off1 = _bucket_np(t + r - c)
    assert t >= MAX_DISTANCE
    return np.concatenate([diag, off1], axis=0).astype(np.int32)


def _lambda_full(lam_ref, lam_init):
    lv = lam_ref[...]
    s1 = jnp.sum(lv[0:1] * lv[1:2], axis=-1, keepdims=True)
    s2 = jnp.sum(lv[2:3] * lv[3:4], axis=-1, keepdims=True)
    return jnp.exp(s1) - jnp.exp(s2) + lam_init


def _flash_kernel(q_ref, k_ref, v_ref, bias_ref, lam_ref, gain_ref, o_ref, qq_scr, m_scr, acc_scr, s_scr,
                  *, lam_init):
    i = pl.program_id(2)
    t = q_ref.shape[0]

    q = q_ref[...]
    first = lax.broadcasted_iota(jnp.int32, q.shape, 1) < DH_B
    qq_scr[:t] = jnp.where(first, q, jnp.zeros_like(q))
    qq_scr[t:] = jnp.where(first, jnp.zeros_like(q), q)
    m_scr[...] = jnp.full(m_scr.shape, NEG, F32)
    acc_scr[...] = jnp.zeros_like(acc_scr)

    def raw_scores(comp, j):
        krows = pl.ds(pl.multiple_of(j * t, t), t)
        return lax.dot_general(qq_scr[comp * t:(comp + 1) * t], k_ref[krows, :], (((1,), (1,)), ((), ())),
                               preferred_element_type=F32)

    def accumulate(comp, s, v1):
        rows = slice(comp * t, (comp + 1) * t)
        m_old = m_scr[rows]
        m_new = jnp.maximum(m_old, jnp.max(s, axis=-1, keepdims=True))
        alpha = jnp.exp2(m_old - m_new)
        p = jnp.concatenate([jnp.exp2(s[:, c * 128:(c + 1) * 128] - m_new) for c in range(t // 128)],
                            axis=1).astype(BF16)
        acc_scr[rows] = (jnp.concatenate([alpha, alpha], axis=1) * acc_scr[rows]
                         + jnp.dot(p, v1, preferred_element_type=F32))
        m_scr[rows] = m_new

    def kv_block(j, tile, has_next):
        krows = pl.ds(pl.multiple_of(j * t, t), t)
        v1 = jnp.concatenate([v_ref[krows, :], jnp.ones((t, DV_B), BF16)], axis=1)
        with_bias = (lambda s: s) if tile is None else (lambda s: s + bias_ref[tile * t:(tile + 1) * t, :])
        s_first = s_scr[...]
        s_second = raw_scores(1, j)
        accumulate(0, with_bias(s_first), v1)
        if has_next:
            s_scr[...] = raw_scores(0, j + 1)
        accumulate(1, with_bias(s_second), v1)

    s_scr[...] = raw_scores(0, 0)

    def far_pair(jj, carry):
        kv_block(2 * jj, None, True)
        kv_block(2 * jj + 1, None, True)
        return carry

    n_far = jnp.maximum(i - 1, 0)
    lax.fori_loop(0, n_far // 2, far_pair, 0)

    @pl.when(jnp.logical_and(i >= 1, n_far % 2 == 1))
    def _():
        kv_block(i - 2, None, True)
        kv_block(i - 1, 1, True)
        kv_block(i, 0, False)

    @pl.when(jnp.logical_and(i >= 1, n_far % 2 == 0))
    def _():
        kv_block(i - 1, 1, True)
        kv_block(i, 0, False)

    @pl.when(i == 0)
    def _():
        kv_block(i, 0, False)

    lam = _lambda_full(lam_ref, lam_init)
    acc = acc_scr[...]
    ob = acc[:t, :DV_B] / acc[:t, DV_B:] - lam * (acc[t:, :DV_B] / acc[t:, DV_B:])
    on = ob * lax.rsqrt(jnp.mean(ob * ob, axis=-1, keepdims=True) + EPS) * gain_ref[...]
    o_ref[...] = (on * (1.0 - lam_init)).astype(o_ref.dtype)


def diff_attention_prompt(qb, kb, vb, bias, lam_rows, gain_row, lam_init, t):
    B, L, _ = qb.shape
    n = L // t
    kv_spec = pl.BlockSpec((None, L, DV_B), lambda b, h, i: (b, 0, h))
    return pl.pallas_call(
        functools.partial(_flash_kernel, lam_init=lam_init),
        grid=(B, H_B, n),
        in_specs=[pl.BlockSpec((None, t, DV_B), lambda b, h, i: (b, i, h)),
                  kv_spec, kv_spec,
                  pl.BlockSpec((None, 2 * t, t), lambda b, h, i: (h, 0, 0)),
                  pl.BlockSpec((4, DH_B), lambda b, h, i: (0, 0)),
                  pl.BlockSpec((1, DV_B), lambda b, h, i: (0, 0))],
        out_specs=pl.BlockSpec((None, t, DV_B), lambda b, h, i: (b, i, h)),
        out_shape=jax.ShapeDtypeStruct((B, L, W_B), BF16),
        scratch_shapes=[pltpu.VMEM((2 * t, DV_B), BF16), pltpu.VMEM((2 * t, 128), F32),
                        pltpu.VMEM((2 * t, 2 * DV_B), F32), pltpu.VMEM((t, t), F32)],
        compiler_params=_cp("parallel", "parallel", "arbitrary"),
        name="diff_attention_prompt",
    )(qb, kb, vb, bias, lam_rows, gain_row)


def _decode_kernel(pt_ref, q_ref, kn_ref, vn_ref, bias_ref, lam_ref, gain_ref, *rest, n_pages, lam_init, spb):
    n = spb * n_pages
    k_all, v_all, o_ref = rest[:n], rest[n:2 * n], rest[2 * n]
    lam = _lambda_full(lam_ref, lam_init)
    gain = gain_ref[...]
    r8 = lax.broadcasted_iota(jnp.int32, (8, 2 * DH_B), 0)
    l8 = lax.broadcasted_iota(jnp.int32, (8, 2 * DH_B), 1)
    comp_rows = (l8 // DH_B) == r8
    row0 = lax.broadcasted_iota(jnp.int32, (PAGE_SIZE, DV_B), 0) == 0
    nt = lambda x, y: lax.dot_general(x, y, (((1,), (1,)), ((), ())), preferred_element_type=F32)
    units = [(s, h) for s in range(spb) for h in range(H_B)]
    cols = [slice(h * DV_B, (h + 1) * DV_B) for h in range(H_B)]
    head_rows = [pl.ds(h, PAGE_SIZE, stride=H_B) for h in range(H_B)]
    pages = lambda refs, s: refs[s * n_pages:(s + 1) * n_pages]
    ss = []
    for s, h in units:
        qrows = jnp.where(comp_rows, q_ref[s, :, cols[h]].astype(F32), 0.0).astype(BF16)
        ks = [r[head_rows[h], :].astype(BF16) for r in pages(k_all, s)]
        ks.append(jnp.where(row0, kn_ref[s, h:h + 1, :], 0.0).astype(BF16))
        sc = []
        for p in range(0, len(ks) - 1, 2):
            s2 = nt(qrows, jnp.concatenate([ks[p], ks[p + 1]], axis=0))
            sc += [s2[:, :PAGE_SIZE] + bias_ref[h, p], s2[:, PAGE_SIZE:] + bias_ref[h, p + 1]]
        if len(ks) % 2:
            sc.append(nt(qrows, ks[-1]) + bias_ref[h, len(ks) - 1])
        ss.append(sc)
    ps, ls = [], []
    for sc in ss:
        m = functools.reduce(jnp.maximum, [jnp.max(x, axis=-1, keepdims=True) for x in sc])
        p = [jnp.exp(x - m) for x in sc]
        ls.append(functools.reduce(jnp.add, [jnp.sum(x, axis=-1, keepdims=True) for x in p]))
        ps.append([x.astype(BF16) for x in p])
    for (s, h), p_u, l_u in zip(units, ps, ls):
        vs = [r[head_rows[h], :].astype(BF16) for r in pages(v_all, s)]
        vs.append(jnp.where(row0, vn_ref[s, h:h + 1, :], 0.0).astype(BF16))
        parts = [jnp.dot(jnp.concatenate([p_u[p], p_u[p + 1]], axis=1),
                         jnp.concatenate([vs[p], vs[p + 1]], axis=0), preferred_element_type=F32)
                 for p in range(0, len(vs) - 1, 2)]
        if len(vs) % 2:
            parts.append(jnp.dot(p_u[-1], vs[-1], preferred_element_type=F32))
        acc = functools.reduce(jnp.add, parts)
        outn = acc / l_u
        oh = outn[0:1] - lam * outn[1:2]
        on = oh * lax.rsqrt(jnp.mean(oh * oh, axis=-1, keepdims=True) + EPS) * gain
        o_ref[s, :, cols[h]] = on * (1.0 - lam_init)


def diff_attention_decode(qb, k32, v32, cache_k, cache_v, pt_flat, bias_dec, lam_rows, gain_row, layer,
                          lam_init, n_pages):
    B = qb.shape[0]
    spb = 2 if B % 2 == 0 else 1
    rowspec = pl.BlockSpec((spb, 1, W_BQK), lambda g, pt: (g, 0, 0))
    headspec = pl.BlockSpec((spb, H_B, DV_B), lambda g, pt: (g, 0, 0))
    const = lambda s: pl.BlockSpec(s, lambda g, pt: tuple(0 for _ in s))
    page_spec = lambda s, p: pl.BlockSpec(
        (None, None, PAGE_SIZE * H_B, DV_B),
        lambda g, pt, s=s, p=p: (layer, pt[(g * spb + s) * n_pages + p], 0, 0))
    page_specs = [page_spec(s, p) for s in range(spb) for p in range(n_pages)]
    grid_spec = pltpu.PrefetchScalarGridSpec(
        num_scalar_prefetch=1,
        grid=(B // spb,),
        in_specs=[rowspec, headspec, headspec, const(bias_dec.shape), const((4, DH_B)), const((1, DV_B))]
                 + page_specs * 2,
        out_specs=pl.BlockSpec((spb, 1, W_B), lambda g, pt: (g, 0, 0)),
    )
    return pl.pallas_call(
        functools.partial(_decode_kernel, n_pages=n_pages, lam_init=lam_init, spb=spb),
        grid_spec=grid_spec,
        out_shape=jax.ShapeDtypeStruct((B, 1, W_B), F32),
        compiler_params=_cp("arbitrary"),
        name="diff_attention_decode",
    )(pt_flat, qb, k32, v32, bias_dec, lam_rows, gain_row,
      *([cache_k] * (spb * n_pages)), *([cache_v] * (spb * n_pages)))


def _rwkv_prep_kernel(*refs, carry_shift, gate_v):
    it = iter(refs)
    z_ref, prev_ref, mu_ref, w0_ref, a0_ref, kk_ref, ka_ref, wlr_ref, ones_ref = (next(it) for _ in range(9))
    if gate_v:
        v0_ref, wvd_ref, wvu_ref, vf_ref = (next(it) for _ in range(4))
    r_ref, lw_ref, k_ref, v_ref, a_ref, b_ref, gg_ref = (next(it) for _ in range(7))
    z = z_ref[...]
    if carry_shift:
        carry_scr = next(it)

        @pl.when(pl.program_id(1) == 0)
        def _():
            carry_scr[...] = prev_ref[...]

        first = lax.broadcasted_iota(jnp.int32, z.shape, 0) == 0
        zprev = jnp.where(first, carry_scr[...], pltpu.roll(z, 1, 0))
        carry_scr[...] = z_ref[z.shape[0] - 1:z.shape[0], :]
    else:
        zprev = prev_ref[...]
    zs = z + (zprev - z) * mu_ref[...]
    rc, kc, vc = zs[:, :W_C], zs[:, W_C:2 * W_C], zs[:, 2 * W_C:3 * W_C]
    lr = zs[:, 3 * W_C:]
    head = lr[:, :R_DECAY + R_A]
    col = lax.broadcasted_iota(jnp.int32, head.shape, 1)
    act = jnp.concatenate([jnp.where(col < R_DECAY, jnp.tanh(head), head),
                           jax.nn.sigmoid(lr[:, R_DECAY + R_A:])], axis=1)
    up = jnp.dot(act.astype(BF16), wlr_ref[...], preferred_element_type=F32)
    y = -(w0_ref[...] + up[:, :W_C])
    softplus = jnp.maximum(y, 0.0) + jnp.log(1.0 + jnp.exp(-jnp.abs(y)))
    lw_ref[...] = -jnp.exp(-softplus - 0.5)
    if gate_v:
        down = jnp.dot(vc.astype(BF16), wvd_ref[...], preferred_element_type=F32)
        vgate = jax.nn.sigmoid(v0_ref[...] + jnp.dot(down.astype(BF16), wvu_ref[...], preferred_element_type=F32))
        vc = vc + (vf_ref[...].astype(F32) - vc) * vgate
    a = jax.nn.sigmoid(a0_ref[...] + up[:, W_C:2 * W_C])
    gg_ref[...] = up[:, 2 * W_C:].astype(gg_ref.dtype)
    kk = kc * kk_ref[...]
    norm = jnp.sqrt(_seg_sum_wide(kk * kk, ones_ref[...]))
    kk = kk / jnp.maximum(norm, 1e-12)
    r_ref[...] = rc.astype(r_ref.dtype)
    k_ref[...] = (kc * (1.0 + (a - 1.0) * ka_ref[...])).astype(k_ref.dtype)
    v_ref[...] = vc.astype(v_ref.dtype)
    a_ref[...] = (-kk).astype(a_ref.dtype)
    b_ref[...] = (kk * a).astype(b_ref.dtype)


def rwkv_prep(z3, prev, mu_row, w0, a0, k_k, k_a, w_lr, vgate, tm, carry_shift):
    G, Lg, _ = z3.shape
    blk = lambda w, idx: pl.BlockSpec((None, tm, w), lambda g, l: (g, l, idx))
    row = lambda w: pl.BlockSpec((1, w), lambda g, l: (0, 0))
    prev_spec = (pl.BlockSpec((None, 1, RWKV_PAD), lambda g, l: (g, 0, 0)) if carry_shift
                 else blk(RWKV_PAD, 0))
    in_specs = [blk(RWKV_PAD, Z_RWKV // RWKV_PAD), prev_spec, row(RWKV_PAD), row(W_C), row(W_C), row(W_C),
                row(W_C), pl.BlockSpec(w_lr.shape, lambda g, l: (0, 0)),
                pl.BlockSpec((128, 128), lambda g, l: (0, 0))]
    args = [z3, prev, mu_row, w0, a0, k_k, k_a, w_lr, _block_ones(DH_C)]
    if vgate is not None:
        v0, wvd, wvu, vfirst = vgate
        in_specs += [row(W_C), pl.BlockSpec(wvd.shape, lambda g, l: (0, 0)),
                     pl.BlockSpec(wvu.shape, lambda g, l: (0, 0)), blk(W_C, 0)]
        args += [v0, wvd, wvu, vfirst]
    return pl.pallas_call(
        functools.partial(_rwkv_prep_kernel, carry_shift=carry_shift, gate_v=vgate is not None),
        grid=(G, Lg // tm),
        in_specs=in_specs,
        out_specs=[blk(W_C, 0)] * 7,
        out_shape=[jax.ShapeDtypeStruct((G, Lg, W_C), F32 if i == 1 else BF16) for i in range(7)],
        scratch_shapes=[pltpu.VMEM((1, RWKV_PAD), F32)] if carry_shift else [],
        compiler_params=_cp("parallel", "arbitrary"),
        name="rwkv_prep",
    )(*args)


def _rwkv_chunk_kernel(r_ref, lw_ref, k_ref, v_ref, a_ref, b_ref, tri_ref, msl_ref, mli_ref, lvl_ref,
                       y_ref, so_ref, s_scr, *, n_chunks, nb):
    C = RW_CHUNK

    @pl.when(pl.program_id(1) == 0)
    def _():
        s_scr[...] = jnp.zeros_like(s_scr)

    head0 = lax.broadcasted_iota(jnp.int32, (C, 128), 1) < DH_C
    tri3, msl, mli = tri_ref[...], msl_ref[...], mli_ref[...]
    ri = lax.broadcasted_iota(jnp.int32, (128, 128), 0)
    ci = lax.broadcasted_iota(jnp.int32, (128, 128), 1)
    eye = (ri == ci).astype(F32)
    n_levels = lvl_ref.shape[0]
    chains = [(bi, p) for bi in range(nb) for p in range(H_C // 2)]

    def stack(x):
        return jnp.concatenate([jnp.where(head0, x, 0.0), jnp.where(head0, 0.0, x)], axis=0)

    def nt(x, y):
        return lax.dot_general(x, y, (((1,), (1,)), ((), ())), preferred_element_type=F32)

    def dot(x, y):
        return jnp.dot(x, y, preferred_element_type=F32)

    def chunk(ci_, carry):
        rows = pl.ds(pl.multiple_of(ci_ * C, C), C)
        ld = lambda ref, bi, p: ref[bi, rows, p * 128:(p + 1) * 128].astype(F32)
        lams = []
        for bi, p in chains:
            lw = ld(lw_ref, bi, p)
            hi = lw.astype(BF16)
            mid = (lw - hi.astype(F32)).astype(BF16)
            lo = (lw - hi.astype(F32) - mid.astype(F32)).astype(BF16)
            lams.append(dot(tri3, jnp.concatenate([hi, mid, lo], axis=0)))
        AR, BK, Vs, KB, dec = [], [], [], [], []
        for (bi, p), lam in zip(chains, lams):
            r, lw, k = ld(r_ref, bi, p), ld(lw_ref, bi, p), ld(k_ref, bi, p)
            v, a, b = ld(v_ref, bi, p), ld(a_ref, bi, p), ld(b_ref, bi, p)
            lam_c = lam[C - 1:C, :]
            e_neg = jnp.exp(-lam)
            e_tail = jnp.exp(lam_c - lam)
            AR.append(jnp.concatenate([stack(a * jnp.exp(lam - lw)), stack(r * jnp.exp(lam))],
                                      axis=0).astype(BF16))
            BK.append(jnp.concatenate([stack(b * e_neg), stack(k * e_neg)], axis=0).astype(BF16))
            KB.append(jnp.concatenate([stack(k * e_tail), stack(b * e_tail)], axis=0).astype(BF16))
            Vs.append(stack(v))
            dec.append(jnp.exp(lam_c))
        G = [nt(x, y) for x, y in zip(AR, BK)]
        I0 = [nt(x, s_scr[bi, p].astype(BF16)) for x, (bi, p) in zip(AR, chains)]
        rhs = [i0[:128] + dot((g[:128, 128:] * msl).astype(BF16), v.astype(BF16))
               for g, i0, v in zip(G, I0, Vs)]
        Nh, Nl, D = [], [], []
        for g in G:
            h, l = _split2(g[:128, :128] * msl)
            Nh.append(h)
            Nl.append(l)
            D.append(eye + (h * lvl_ref[0]).astype(F32) + (l * lvl_ref[0]).astype(F32))
        for lv in range(1, n_levels):
            m = lvl_ref[lv]
            Db = [d.astype(BF16) for d in D]
            X = [dot(h * m, db) for h, db in zip(Nh, Db)]
            D = [d + dot(db, x.astype(BF16)) for d, db, x in zip(D, Db, X)]
        Us = [dot(d.astype(BF16), x.astype(BF16)) for d, x in zip(D, rhs)]
        for (bi, p), g, i0, u, v in zip(chains, G, I0, Us, Vs):
            ys = i0[128:] + dot(jnp.concatenate([g[128:, :128] * mli, g[128:, 128:] * mli], axis=1).astype(BF16),
                                jnp.concatenate([u, v], axis=0).astype(BF16))
            y_ref[bi, rows, p * 128:(p + 1) * 128] = ys[:C] + ys[C:]
        for (bi, p), u, v, kb, d in zip(chains, Us, Vs, KB, dec):
            vu = jnp.concatenate([v, u], axis=0)
            s_scr[bi, p] = s_scr[bi, p] * d + dot(vu.T.astype(BF16), kb)
        return carry

    lax.fori_loop(0, n_chunks, chunk, 0)

    @pl.when(pl.program_id(1) == pl.num_programs(1) - 1)
    def _():
        so_ref[...] = s_scr[...]


def _rwkv_chunk_tables():
    C = RW_CHUNK
    t = np.arange(C)
    tri = (t[:, None] >= t[None, :]).astype(np.float32)
    tri3 = np.concatenate([tri, tri, tri], axis=1)
    i = np.arange(2 * C)
    same = (i[:, None] // C) == (i[None, :] // C)
    msl = (same & (i[:, None] > i[None, :])).astype(np.float32)
    mli = (same & (i[:, None] >= i[None, :])).astype(np.float32)
    levels = []
    n = 1
    while n < C:
        levels.append(same & ((i[:, None] // (2 * n)) == (i[None, :] // (2 * n)))
                      & ((i[:, None] // n) % 2 == 1) & ((i[None, :] // n) % 2 == 0))
        n *= 2
    return jnp.asarray(tri3, BF16), msl, mli, jnp.asarray(np.stack(levels).astype(np.float32), BF16)


def rwkv_chunk_scan(r, lw, k, v, a, b, tb, nb):
    B, L, _ = r.shape
    tri3, msl, mli, lvl = _rwkv_chunk_tables()
    blk = pl.BlockSpec((nb, tb, W_C), lambda bi, l: (bi, l, 0))
    c2 = lambda s: pl.BlockSpec(s, lambda bi, l: tuple(0 for _ in s))
    return pl.pallas_call(
        functools.partial(_rwkv_chunk_kernel, n_chunks=tb // RW_CHUNK, nb=nb),
        grid=(B // nb, L // tb),
        in_specs=[blk] * 6 + [c2(tri3.shape), c2(msl.shape), c2(mli.shape), c2(lvl.shape)],
        out_specs=[blk, pl.BlockSpec((nb, H_C // 2, 128, 128), lambda bi, l: (bi, 0, 0, 0))],
        out_shape=[jax.ShapeDtypeStruct((B, L, W_C), F32),
                   jax.ShapeDtypeStruct((B, H_C // 2, 128, 128), F32)],
        scratch_shapes=[pltpu.VMEM((nb, H_C // 2, 128, 128), F32)],
        compiler_params=_cp("parallel", "arbitrary"),
        name="rwkv_chunk_scan",
    )(r, lw, k, v, a, b, tri3, msl, mli, lvl)


def _rwkv_step_kernel(s_ref, w_ref, a_ref, b_ref, k_ref, r_ref, v_ref, stack_ref, y_ref, so_ref, vt_scr, y_scr):
    del stack_ref
    tr = lambda ref: ref[...].astype(F32).T
    wT, aT, bT, kT, rT = jnp.exp(tr(w_ref)), tr(a_ref), tr(b_ref), tr(k_ref), tr(r_ref)
    vt_scr[...] = tr(v_ref)
    for hh in range(2):
        ch = slice(hh * DH_C, (hh + 1) * DH_C)
        w, a, b, k, r = wT[ch], aT[ch], bT[ch], kT[ch], rT[ch]

        def value_row(i, carry):
            row = pl.ds(hh * DH_C + i, 1)
            S = s_ref[hh, i]
            sa = jnp.sum(S * a, axis=0, keepdims=True)
            Sn = S * w + sa * b + vt_scr[row, :] * k
            so_ref[hh, i] = Sn
            y_scr[row, :] = jnp.sum(Sn * r, axis=0, keepdims=True)
            return carry

        lax.fori_loop(0, DH_C, value_row, 0)
    y_ref[...] = y_scr[...].T


def rwkv_step(state_t, layer, lw, a, b, k, r, v, new_state_t):
    B = lw.shape[0]
    vec = pl.BlockSpec((B, 2 * DH_C), lambda p: (0, p))
    sspec = pl.BlockSpec((None, 2, DH_C, DH_C, B), lambda p: (layer, p, 0, 0, 0))
    return pl.pallas_call(
        _rwkv_step_kernel,
        grid=(H_C // 2,),
        in_specs=[sspec] + [vec] * 6 + [pl.BlockSpec(memory_space=pl.ANY)],
        out_specs=[vec, sspec],
        out_shape=[jax.ShapeDtypeStruct((B, W_C), F32),
                   jax.ShapeDtypeStruct(new_state_t.shape, F32)],
        scratch_shapes=[pltpu.VMEM((2 * DH_C, B), F32), pltpu.VMEM((2 * DH_C, B), F32)],
        input_output_aliases={7: 1},
        compiler_params=_cp("parallel"),
        name="rwkv_step",
    )(state_t, lw, a, b, k, r, v, new_state_t)


def _merge_kernel(x_ref, ya_ref, yb_ref, y_ref, r_ref, k_ref, v_ref, gg_ref, za_ref, zb_ref, zc_ref, gt_ref,
                  lng_ref, lnb_ref, rk_ref, ones_ref, wa_ref, wb_ref, wc_ref, wo_ref, o_ref):
    ones2 = ones_ref[...]
    y = y_ref[...]
    mu = _seg_sum_wide(y, ones2) * (1.0 / DH_C)
    d = y - mu
    var = _seg_sum_wide(d * d, ones2) * (1.0 / DH_C)
    ycn = d * lax.rsqrt(var + LNX_EPS) * lng_ref[...] + lnb_ref[...]
    f32 = lambda ref: ref[...].astype(F32)
    bonus = _seg_sum_wide(f32(r_ref) * f32(k_ref) * rk_ref[...], ones2) * f32(v_ref)
    yc = (ycn + bonus) * f32(gg_ref)
    proj = lambda t, w: jnp.dot(t.astype(BF16), w[...], preferred_element_type=F32)
    merged = (jax.nn.sigmoid(za_ref[...]) * proj(ya_ref[...], wa_ref)
              + jax.nn.sigmoid(zb_ref[...]) * proj(yb_ref[...], wb_ref)
              + jax.nn.sigmoid(zc_ref[...]) * proj(yc, wc_ref))
    o_ref[...] = x_ref[...] + gt_ref[...] * proj(merged, wo_ref)


def merge_out(x2, ya, yb, y, r, k, v, gg, z2, gt, lng, lnb, rk, wa, wb, wc, wo, tm):
    T, D = x2.shape
    G, R, _ = gt.shape
    tiles_per_group = (T // tm) // G
    tok = lambda w: pl.BlockSpec((tm, w), lambda m: (m, 0))
    gate = lambda idx: pl.BlockSpec((tm, D), lambda m: (m, Z_GATE // D + idx))
    row = lambda w: pl.BlockSpec((1, w), lambda m: (0, 0))
    full = lambda a: pl.BlockSpec(a.shape, lambda m: (0, 0))
    return pl.pallas_call(
        _merge_kernel,
        grid=(T // tm,),
        in_specs=[tok(D)] + [tok(W_C)] * 7 + [gate(0), gate(1), gate(2),
                  pl.BlockSpec((None, R, D), lambda m: (m // tiles_per_group, 0, 0)),
                  row(W_C), row(W_C), row(W_C), pl.BlockSpec((128, 128), lambda m: (0, 0)),
                  full(wa), full(wb), full(wc), full(wo)],
        out_specs=tok(D),
        out_shape=jax.ShapeDtypeStruct((T, D), F32),
        compiler_params=_cp("parallel"),
        name="merge_out",
    )(x2, ya, yb, y, r, k, v, gg, z2, z2, z2, gt, lng, lnb, rk, _block_ones(DH_C), wa, wb, wc, wo)


def _ffn_kernel(x_ref, g_ref, sc_ref, sh_ref, gt_ref, w1_ref, w2_ref, o_ref, h_scr, acc_scr):
    f = pl.program_id(1)

    @pl.when(f == 0)
    def _():
        x = x_ref[...]
        y = x * lax.rsqrt(jnp.mean(x * x, axis=-1, keepdims=True) + EPS) * g_ref[...]
        h_scr[...] = (y * (1.0 + sc_ref[...]) + sh_ref[...]).astype(BF16)
        acc_scr[...] = jnp.zeros_like(acc_scr)

    u = jnp.maximum(jnp.dot(h_scr[...], w1_ref[...], preferred_element_type=F32), 0.0)
    acc_scr[...] += jnp.dot((u * u).astype(BF16), w2_ref[...], preferred_element_type=F32)

    @pl.when(f == pl.num_programs(1) - 1)
    def _():
        o_ref[...] = x_ref[...] + gt_ref[...] * acc_scr[...]


def ffn(x2, gain_row, sc, sh, gt, w1, w2, tm, tf):
    T, D = x2.shape
    F = w1.shape[1]
    G, R, _ = sc.shape
    tiles_per_group = (T // tm) // G
    mod_spec = pl.BlockSpec((None, R, D), lambda m, f: (m // tiles_per_group, 0, 0))
    return pl.pallas_call(
        _ffn_kernel,
        grid=(T // tm, F // tf),
        in_specs=[pl.BlockSpec((tm, D), lambda m, f: (m, 0)),
                  pl.BlockSpec((1, D), lambda m, f: (0, 0)),
                  mod_spec, mod_spec, mod_spec,
                  pl.BlockSpec((D, tf), lambda m, f: (0, f)),
                  pl.BlockSpec((tf, D), lambda m, f: (f, 0))],
        out_specs=pl.BlockSpec((tm, D), lambda m, f: (m, 0)),
        out_shape=jax.ShapeDtypeStruct((T, D), F32),
        scratch_shapes=[pltpu.VMEM((tm, D), BF16), pltpu.VMEM((tm, D), F32)],
        compiler_params=_cp("parallel", "arbitrary"),
        name="ffn",
    )(x2, gain_row, sc, sh, gt, w1, w2)


def prep_w_in(w_in):
    depth, D, _ = w_in.shape
    wt = jnp.transpose(w_in, (0, 2, 1))
    o_diff, o_rwkv, o_gate = RET_COLS, RET_COLS + DIFF_COLS, RET_COLS + DIFF_COLS + RWKV_COLS
    return jnp.concatenate([
        wt[:, :RET_COLS], wt[:, o_rwkv:o_gate], jnp.zeros((depth, RWKV_PAD - RWKV_COLS, D), F32),
        wt[:, o_gate:], wt[:, o_diff:o_rwkv]], axis=1).astype(BF16)


def _prep_layer(p, l, w_in_p):
    place = lambda w, slot: jnp.pad(w, ((0, 0), (slot * W_C, (2 - slot) * W_C)))
    w_lr = jnp.concatenate([place(p['w_decay_up'][l], 0), place(p['w_a_up'][l], 1), place(p['w_g_up'][l], 2),
                            jnp.zeros((LR_COLS - R_DECAY - R_A - R_G, 3 * W_C), F32)], axis=0)
    row = lambda t: t.reshape(1, -1)
    lp = dict(
        layer=l, w_ada=p['w_ada'], b_ada=row(p['b_ada'][l]),
        norm1=row(p['norm1'][l]), norm2=row(p['norm2'][l]), w_in=w_in_p,
        gq=row(jnp.tile(p['qk_norm_q'][l], 2 * H_B)), gk=row(jnp.tile(p['qk_norm_k'][l], 2 * H_B)),
        lam_rows=jnp.stack([p['lambda_q1'][l], p['lambda_k1'][l], p['lambda_q2'][l], p['lambda_k2'][l]]),
        subln=row(p['subln_diff'][l]),
        mu=row(jnp.pad(p['mu_shift'][l], (0, RWKV_PAD - RWKV_COLS))),
        w0=row(p['w0'][l]), a0=row(p['a0'][l]), k_k=row(p['k_k'][l]), k_a=row(p['k_a'][l]),
        w_lr=w_lr.astype(BF16), r_k=row(p['r_k'][l]), lnx_g=row(p['lnx_g'][l]), lnx_b=row(p['lnx_b'][l]),
        w_up_a=p['w_up_a'][l].astype(BF16), w_up_b=p['w_up_b'][l].astype(BF16),
        w_up_c=p['w_up_c'][l].astype(BF16), w_out=p['w_out'][l].astype(BF16),
        w_ff1=p['w_ff1'][l].astype(BF16), w_ff2=p['w_ff2'][l].astype(BF16),
        lam_init=0.8 - 0.6 * math.exp(-0.3 * l),
    )
    if l > 0:
        lp['v0'] = row(p['v0'][l - 1])
        lp['w_v_down'] = jnp.pad(p['w_v_down'][l - 1], ((0, 0), (0, 128 - R_V))).astype(BF16)
        lp['w_v_up'] = jnp.pad(p['w_v_up'][l - 1], ((0, 128 - R_V), (0, 0))).astype(BF16)
    return lp


def _modulation(c, lp, per_token):
    B, D = c.shape
    start = 0 if not per_token else lp['mod'].shape[0] - B
    mod = lp['mod'][start:start + B]
    parts = [mod[:, i * D:(i + 1) * D] for i in range(6)]
    shape = (1, B, D) if per_token else (B, 1, D)
    return [t.reshape(shape) for t in parts]


def _trunk_prompt(x, c, layers, bias_p):
    B, L, D = x.shape
    T = B * L
    tiles = _prompt_tiles(L)
    x2 = x.reshape(T, D)
    cos_t, sin_t = _rope_tables(jnp.arange(L))
    ret_out, rwkv_out, shift_out = [], [], []
    k_st = jnp.zeros((len(layers), T * H_B, DV_B), F32)
    v_st = jnp.zeros((len(layers), T * H_B, DV_B), F32)
    v_first = None
    for l, lp in enumerate(layers):
        sh1, sc1, gt1, sh2, sc2, gt2 = _modulation(c, lp, per_token=False)
        z2 = in_proj(x2, lp['norm1'], sc1, sh1, lp['w_in'], l, tiles['in_proj_rows'],
                     tiles['in_proj_cols'])
        ZC = z2.shape[1]
        z3 = z2.reshape(B, L, ZC)
        ya, s_ret = retention_prompt(z3, cos_t, sin_t, tiles['retention'])
        qb, k_st, kb, v_st, vb = qk_norm(z2, lp['gq'], lp['gk'], tiles['qk_norm'],
                                         DH_B ** -0.5 * math.log2(math.e), l, k_st, v_st)
        r3 = lambda t: t.reshape(B, L, W_B)
        yb = diff_attention_prompt(r3(qb), r3(kb), r3(vb), bias_p, lp['lam_rows'], lp['subln'],
                                   lp['lam_init'], tiles['attention'])
        vgate = None if l == 0 else (lp['v0'], lp['w_v_down'], lp['w_v_up'], v_first)
        shift0 = jnp.zeros((B, 1, RWKV_PAD), F32)
        r, lw, k, v, a, b, gg = rwkv_prep(z3, shift0, lp['mu'], lp['w0'], lp['a0'], lp['k_k'], lp['k_a'],
                                          lp['w_lr'], vgate, tiles['rwkv_prep'], carry_shift=True)
        if l == 0:
            v_first = v
        y, s_pair = rwkv_chunk_scan(r, lw, k, v, a, b, tiles['rwkv_scan'], B)
        shift_out.append(z3[:, L - 1:, Z_RWKV:Z_RWKV + RWKV_COLS])
        f2 = lambda t: t.reshape(T, -1)
        x2 = merge_out(x2, f2(ya), f2(yb), f2(y), f2(r), f2(k), f2(v), f2(gg), z2, gt1, lp['lnx_g'],
                       lp['lnx_b'], lp['r_k'], lp['w_up_a'], lp['w_up_b'], lp['w_up_c'], lp['w_out'],
                       tiles['merge'])
        x2 = ffn(x2, lp['norm2'], sc2, sh2, gt2, lp['w_ff1'], lp['w_ff2'], tiles['ffn_rows'],
                 tiles['ffn_hidden'])
        ret_out.append(s_ret)
        sp = s_pair.reshape(B, H_C // 2, 2, DH_C, 2, DH_C)
        rwkv_out.append(jnp.stack([sp[:, :, 0, :, 0, :], sp[:, :, 1, :, 1, :]], axis=2)
                        .reshape(B, H_C, DH_C, DH_C))
    kv_shape = (len(layers), B, L, H_B, DV_B)
    return (x2.reshape(B, L, D), k_st.reshape(kv_shape), v_st.reshape(kv_shape), jnp.stack(ret_out),
            jnp.stack(rwkv_out), jnp.stack(shift_out))


def _trunk_decode(x, c, layers, bias_d, state_ret, state_rwkv, state_shift, cache_k, cache_v, page_table):
    B, _, D = x.shape
    n_pages = page_table.shape[1]
    past = n_pages * PAGE_SIZE
    x2 = x.reshape(B, D)
    pt_flat = page_table.reshape(-1)
    ck = cache_k.reshape(cache_k.shape[0], cache_k.shape[1], PAGE_SIZE * H_B, 2 * DH_B)
    cv = cache_v.reshape(cache_v.shape[0], cache_v.shape[1], PAGE_SIZE * H_B, DV_B)
    state_t = jnp.transpose(state_rwkv, (0, 2, 3, 4, 1))
    shift_out = []
    ret_st = jnp.zeros(state_ret.shape, F32)
    rwkv_st = jnp.zeros(state_t.shape, F32)
    k_st = jnp.zeros((len(layers), B * H_B, DV_B), F32)
    v_st = jnp.zeros((len(layers), B * H_B, DV_B), F32)
    v_first = None
    for l, lp in enumerate(layers):
        sh1, sc1, gt1, sh2, sc2, gt2 = _modulation(c, lp, per_token=True)
        z2 = in_proj(x2, lp['norm1'], sc1, sh1, lp['w_in'], l, B, _prompt_tiles(B)['in_proj_cols'])
        ZC = z2.shape[1]
        ya, ret_st = retention_step(z2, state_ret, l, past, ret_st)
        qb, k_st, kb, v_st, vb = qk_norm(z2, lp['gq'], lp['gk'], B, DH_B ** -0.5, l, k_st, v_st)
        h3 = lambda t: t[l].reshape(B, H_B, DV_B)
        yb = diff_attention_decode(qb.reshape(B, 1, W_BQK), h3(k_st), h3(v_st), ck, cv, pt_flat, bias_d,
                                   lp['lam_rows'], lp['subln'], l, lp['lam_init'], n_pages).reshape(B, W_B)
        vgate = None if l == 0 else (lp['v0'], lp['w_v_down'], lp['w_v_up'], v_first)
        prev = jnp.pad(state_shift[l].reshape(1, B, RWKV_COLS), ((0, 0), (0, 0), (0, RWKV_PAD - RWKV_COLS)))
        r, lw, k, v, a, b, gg = rwkv_prep(z2.reshape(1, B, ZC), prev, lp['mu'], lp['w0'], lp['a0'], lp['k_k'],
                                          lp['k_a'], lp['w_lr'], vgate, B, carry_shift=False)
        if l == 0:
            v_first = v
        f2 = lambda t: t.reshape(B, -1)
        y_dec, rwkv_st = rwkv_step(state_t, l, f2(lw), f2(a), f2(b), f2(k), f2(r), f2(v), rwkv_st)
        shift_out.append(z2[:, Z_RWKV:Z_RWKV + RWKV_COLS].reshape(B, 1, RWKV_COLS))
        x2 = merge_out(x2, ya, yb, y_dec, f2(r), f2(k), f2(v), f2(gg), z2, gt1, lp['lnx_g'], lp['lnx_b'],
                       lp['r_k'], lp['w_up_a'], lp['w_up_b'], lp['w_up_c'], lp['w_out'], B)
        x2 = ffn(x2, lp['norm2'], sc2, sh2, gt2, lp['w_ff1'], lp['w_ff2'], B, 1024)
    rwkv_state = jnp.transpose(rwkv_st, (0, 4, 1, 2, 3))
    kv_shape = (len(layers), B, 1, H_B, DV_B)
    return (x2.reshape(B, 1, D), k_st.reshape(kv_shape), v_st.reshape(kv_shape), ret_st,
            rwkv_state, jnp.stack(shift_out))


def _decode_buckets(n_pages):
    past = n_pages * PAGE_SIZE
    key = np.arange((n_pages + 1) * PAGE_SIZE)
    bk = np.where(key <= past, _bucket_np(np.maximum(past - key, 0)), -1)
    return np.broadcast_to(bk.reshape(n_pages + 1, 1, PAGE_SIZE), (n_pages + 1, 8, PAGE_SIZE)).astype(np.int32)


def kernel(x_prompt, x_sample, c_prompt, c_sample, cache_k_diff, cache_v_diff, page_table, state_ret, state_rwkv, state_shift, rel_bias, w_ada, b_ada, norm1, norm2, w_in, qk_norm_q, qk_norm_k, lambda_q1, lambda_k1, lambda_q2, lambda_k2, subln_diff, mu_shift, w0, w_decay_up, a0, w_a_up, w_g_up, v0, w_v_down, w_v_up, k_k, k_a, r_k, lnx_g, lnx_b, w_up_a, w_up_b, w_up_c, w_out, w_ff1, w_ff2):
    p = dict(w_ada=w_ada, b_ada=b_ada, norm1=norm1, norm2=norm2, w_in=w_in,
             qk_norm_q=qk_norm_q, qk_norm_k=qk_norm_k, lambda_q1=lambda_q1, lambda_k1=lambda_k1,
             lambda_q2=lambda_q2, lambda_k2=lambda_k2, subln_diff=subln_diff, mu_shift=mu_shift,
             w0=w0, w_decay_up=w_decay_up, a0=a0, w_a_up=w_a_up, w_g_up=w_g_up, v0=v0,
             w_v_down=w_v_down, w_v_up=w_v_up, k_k=k_k, k_a=k_a, r_k=r_k, lnx_g=lnx_g, lnx_b=lnx_b,
             w_up_a=w_up_a, w_up_b=w_up_b, w_up_c=w_up_c, w_out=w_out, w_ff1=w_ff1, w_ff2=w_ff2)
    depth = w_in.shape[0]
    w_in_p = prep_w_in(w_in)
    layers = [_prep_layer(p, l, w_in_p) for l in range(depth)]
    bp = c_prompt.shape[0]
    c_all = jnp.concatenate([jnp.pad(c_prompt, ((0, -bp % 16), (0, 0))), c_sample], axis=0)
    for lp in layers:
        lp['mod'] = ada_mod(c_all, lp['w_ada'], lp['b_ada'], lp['layer'])
    L = x_prompt.shape[1]
    n_pages = page_table.shape[1]
    bias_p = bias_tiles(rel_bias, _prompt_buckets(_prompt_tiles(L)['attention']), True)
    bd = bias_tiles(rel_bias, _decode_buckets(n_pages).reshape((n_pages + 1) * 8, PAGE_SIZE), False)
    bias_d = bd.reshape(H_B, n_pages + 1, 8, PAGE_SIZE)

    y_p, k_p, v_p, ret_p, rwkv_p, shift_p = _trunk_prompt(x_prompt, c_prompt, layers, bias_p)
    y_s, k_s, v_s, ret_s, rwkv_s, shift_s = _trunk_decode(
        x_sample, c_sample, layers, bias_d, state_ret, state_rwkv, state_shift,
        cache_k_diff, cache_v_diff, page_table)
    return (y_p, y_s, k_p, v_p, k_s, v_s, ret_p, ret_s, rwkv_p, rwkv_s, shift_p, shift_s)
```

```python
import functools
import math

import numpy as np
import jax
import jax.numpy as jnp
from jax import lax
from jax.experimental import pallas as pl
from jax.experimental.pallas import tpu as pltpu

F32 = jnp.float32
BF16 = jnp.bfloat16

H_A, DK_A, DV_A = 4, 128, 128
RET_CHUNK = 128
ROPE_BASE = 10000.0
H_B, DH_B, DV_B = 4, 64, 128
N_BUCKETS, MAX_DISTANCE = 32, 128
H_C, DH_C = 8, 64
R_DECAY, R_A, R_V, R_G = 64, 64, 32, 160
LNX_EPS = 64e-5
EPS = 1e-6
PAGE_SIZE = 128
W_AQ, W_A = H_A * DK_A, H_A * DV_A
W_BQK, W_B = H_B * 2 * DH_B, H_B * DV_B
W_C = H_C * DH_C
RET_COLS = 2 * W_AQ + 2 * W_A
DIFF_COLS = 2 * W_BQK + W_B
RWKV_COLS = 3 * W_C + R_DECAY + R_A + R_G
RWKV_PAD = 2048
LR_COLS = RWKV_PAD - 3 * W_C

Z_RET, Z_RWKV = 0, RET_COLS
Z_GATE = Z_RWKV + RWKV_PAD
NEG = -1e30
RW_CHUNK = 64
V7X_VMEM_BYTES = 64 * 1024 * 1024
VMEM_LIMIT = V7X_VMEM_BYTES * 7 // 8


def _cp(*sem):
    return pltpu.CompilerParams(dimension_semantics=sem, vmem_limit_bytes=VMEM_LIMIT)


def _prompt_tiles(L):
    cap = lambda n: min(n, L)
    return dict(in_proj_rows=cap(1024), in_proj_cols=2176, retention=cap(512), qk_norm=cap(1024),
                attention=cap(512), rwkv_prep=cap(512), rwkv_scan=cap(256), merge=cap(512),
                ffn_rows=cap(1024), ffn_hidden=1024)


def _silu(x):
    return x * jax.nn.sigmoid(x)


def _split2(x):
    hi = x.astype(BF16)
    lo = (x - hi.astype(F32)).astype(BF16)
    return hi, lo


def _seg_sum(x, ones):
    return jnp.dot(x.astype(BF16), ones, preferred_element_type=F32)


def _seg_sum_wide(x, ones2):
    return jnp.concatenate(
        [_seg_sum(x[:, c * 128:(c + 1) * 128], ones2) for c in range(x.shape[1] // 128)], axis=1)


def _mod_kernel(c_ref, w_ref, b_ref, o_ref):
    s = _silu(c_ref[...])
    o_ref[...] = jnp.dot(s.astype(BF16), w_ref[...].astype(BF16), preferred_element_type=F32) + b_ref[...]


def ada_mod(c_pad, w_stack, b_row, layer):
    R, D = c_pad.shape
    N = w_stack.shape[2]
    tn = 1536
    return pl.pallas_call(
        _mod_kernel,
        grid=(N // tn,),
        in_specs=[pl.BlockSpec((R, D), lambda n: (0, 0)),
                  pl.BlockSpec((None, D, tn), lambda n: (layer, 0, n)),
                  pl.BlockSpec((1, tn), lambda n: (0, n))],
        out_specs=pl.BlockSpec((R, tn), lambda n: (0, n)),
        out_shape=jax.ShapeDtypeStruct((R, N), F32),
        compiler_params=_cp("parallel"),
        name="ada_mod",
    )(c_pad, w_stack, b_row)


def _inproj_kernel(x_ref, g_ref, sc_ref, sh_ref, w_ref, z_ref, h_scr):
    @pl.when(pl.program_id(1) == 0)
    def _():
        x = x_ref[...]
        y = x * lax.rsqrt(jnp.mean(x * x, axis=-1, keepdims=True) + EPS) * g_ref[...]
        h_scr[...] = (y * (1.0 + sc_ref[...]) + sh_ref[...]).astype(BF16)

    z_ref[...] = lax.dot_general(h_scr[...], w_ref[...], (((1,), (1,)), ((), ())),
                                 preferred_element_type=F32)


def in_proj(x2, gain_row, sc, sh, w_bf, layer, tm, tn):
    T, D = x2.shape
    N = w_bf.shape[1]
    G, R, _ = sc.shape
    tiles_per_group = (T // tm) // G
    mod_spec = pl.BlockSpec((None, R, D), lambda m, n: (m // tiles_per_group, 0, 0))
    return pl.pallas_call(
        _inproj_kernel,
        grid=(T // tm, N // tn),
        in_specs=[pl.BlockSpec((tm, D), lambda m, n: (m, 0)),
                  pl.BlockSpec((1, D), lambda m, n: (0, 0)),
                  mod_spec, mod_spec,
                  pl.BlockSpec((None, tn, D), lambda m, n: (layer, n, 0))],
        out_specs=pl.BlockSpec((tm, tn), lambda m, n: (m, n)),
        out_shape=jax.ShapeDtypeStruct((T, N), F32),
        scratch_shapes=[pltpu.VMEM((tm, D), BF16)],
        compiler_params=_cp("parallel", "arbitrary"),
        name="in_proj",
    )(x2, gain_row, sc, sh, w_bf)


def _rope(x, cos, sin_signed):
    return x * cos + pltpu.roll(x, DK_A // 2, 1) * sin_signed


def _ret_kernel(q_ref, k_ref, v_ref, g_ref, cos_ref, sin_ref, inner_ref, cross_ref, tail_ref, cd_ref,
                ya_ref, so_ref, s_scr, *, n_sub):
    lt = pl.program_id(1)

    @pl.when(lt == 0)
    def _():
        s_scr[...] = jnp.zeros_like(s_scr)

    C = RET_CHUNK
    heads = range(H_A)
    cols = [slice(h * DK_A, (h + 1) * DK_A) for h in heads]
    dot = lambda x, y: jnp.dot(x, y, preferred_element_type=F32)
    for c in range(n_sub):
        rows = slice(c * C, (c + 1) * C)
        cos = cos_ref[rows, :]
        sin = sin_ref[rows, :]
        qb, kb, vb, kt = [], [], [], []
        for h in heads:
            q = _rope(q_ref[rows, cols[h]], cos, sin)
            k = _rope(k_ref[rows, cols[h]], cos, sin) * (DK_A ** -0.5)
            qb.append(q.astype(BF16))
            kb.append(k.astype(BF16))
            vb.append(v_ref[rows, cols[h]].astype(BF16))
            kt.append((k * tail_ref[h]).T.astype(BF16))
        S = [s_scr[h] for h in heads]
        sc = [lax.dot_general(qb[h], kb[h], (((1,), (1,)), ((), ())), preferred_element_type=F32) * inner_ref[h]
              for h in heads]
        qs = [dot(qb[h], S[h].astype(BF16)) * cross_ref[h] for h in heads]
        kv = [dot(kt[h], vb[h]) for h in heads]
        o = [dot(sc[h].astype(BF16), vb[h]) + qs[h] for h in heads]
        for h in heads:
            s_scr[h] = S[h] * cd_ref[h] + kv[h]
            on = o[h] * lax.rsqrt(jnp.mean(o[h] * o[h], axis=-1, keepdims=True) + EPS)
            ya_ref[rows, cols[h]] = (on * _silu(g_ref[rows, cols[h]])).astype(ya_ref.dtype)

    @pl.when(lt == pl.num_programs(1) - 1)
    def _():
        so_ref[...] = s_scr[...]


def _ret_tables(L):
    C = math.gcd(L, RET_CHUNK)
    log_g = np.log1p(-np.exp2(-5.0 - np.arange(H_A, dtype=np.float32))).astype(np.float32)
    i = np.arange(C, dtype=np.float32)
    dist = i[:, None] - i[None, :]
    causal = dist >= 0
    inner = np.where(causal[None], np.exp(np.where(causal, dist, 0.0)[None] * log_g[:, None, None]), 0.0)
    cross = np.exp((i[None, :] + 1.0) * log_g[:, None])
    tail = np.exp((C - 1.0 - i)[None, :] * log_g[:, None])
    chunk = np.exp(C * log_g)
    bc = lambda t: np.broadcast_to(t[:, :, None], (H_A, C, 128)).astype(np.float32)
    cd = np.broadcast_to(chunk[:, None, None], (H_A, 1, 128)).astype(np.float32)
    return inner.astype(np.float32), bc(cross), bc(tail), cd


def _rope_tables(pos):
    half = DK_A // 2
    inv = ROPE_BASE ** (-jnp.arange(half, dtype=F32) / half)
    ang = pos.astype(F32)[:, None] * inv[None, :]
    cos, sin = jnp.cos(ang), jnp.sin(ang)
    return jnp.concatenate([cos, cos], axis=-1), jnp.concatenate([-sin, sin], axis=-1)


def retention_prompt(z3, cos_t, sin_t, tb):
    B, L, ZC = z3.shape
    inner, cross, tail, cd = _ret_tables(L)
    zspec = lambda idx: pl.BlockSpec((None, tb, W_AQ), lambda b, l: (b, l, idx))
    full3 = lambda s: pl.BlockSpec(s, lambda b, l: (0, 0, 0))
    return pl.pallas_call(
        functools.partial(_ret_kernel, n_sub=tb // RET_CHUNK),
        grid=(B, L // tb),
        in_specs=[zspec(0), zspec(1), zspec(2), zspec(3),
                  pl.BlockSpec((tb, 128), lambda b, l: (l, 0)),
                  pl.BlockSpec((tb, 128), lambda b, l: (l, 0)),
                  full3(inner.shape), full3(cross.shape), full3(tail.shape), full3(cd.shape)],
        out_specs=[pl.BlockSpec((None, tb, W_A), lambda b, l: (b, l, 0)),
                   pl.BlockSpec((None, H_A, DK_A, DV_A), lambda b, l: (b, 0, 0, 0))],
        out_shape=[jax.ShapeDtypeStruct((B, L, W_A), BF16),
                   jax.ShapeDtypeStruct((B, H_A, DK_A, DV_A), F32)],
        scratch_shapes=[pltpu.VMEM((H_A, DK_A, DV_A), F32)],
        compiler_params=_cp("parallel", "arbitrary"),
        name="retention_prompt",
    )(z3, z3, z3, z3, cos_t, sin_t, inner, cross, tail, cd)


def _ret_step_kernel(q_ref, k_ref, v_ref, g_ref, cos_ref, sin_ref, gam_ref, s_ref, stack_ref, ya_ref, so_ref,
                     *, bb):
    del stack_ref
    cos, sin = cos_ref[...], sin_ref[...]
    row = lax.broadcasted_iota(jnp.int32, (bb, 128), 0)
    for h in range(H_A):
        cols = slice(h * DK_A, (h + 1) * DK_A)
        gam = gam_ref[h]
        q = _rope(q_ref[:, cols], cos, sin)
        k = _rope(k_ref[:, cols], cos, sin) * (DK_A ** -0.5)
        v = v_ref[:, cols]
        qb = q.astype(BF16)
        qk = jnp.sum(qb.astype(F32) * k.astype(BF16).astype(F32), axis=-1, keepdims=True)
        qs = jnp.zeros((bb, 128), F32)
        vb = v.astype(BF16)
        for b in range(0, bb, 2):
            S0, S1 = s_ref[b, h], s_ref[b + 1, h]
            first, second = row == b, row == b + 1
            q_s = jnp.dot(qb, jnp.concatenate([S0, S1], axis=1).astype(BF16), preferred_element_type=F32)
            qs = jnp.where(first, q_s[:, :DV_A], jnp.where(second, q_s[:, DV_A:], qs))
            k_pair = jnp.where(jnp.logical_or(first, second), k, 0.0).T.astype(BF16)
            v_pair = jnp.concatenate([jnp.where(first, vb, jnp.zeros_like(vb)),
                                      jnp.where(second, vb, jnp.zeros_like(vb))], axis=1)
            kv = jnp.dot(k_pair, v_pair, preferred_element_type=F32)
            so_ref[b, h] = S0 * gam + kv[:, :DV_A]
            so_ref[b + 1, h] = S1 * gam + kv[:, DV_A:]
        o = qk * v.astype(BF16).astype(F32) + qs * gam
        on = o * lax.rsqrt(jnp.mean(o * o, axis=-1, keepdims=True) + EPS)
        ya_ref[:, cols] = on * _silu(g_ref[:, cols])


def retention_step(z2, state, layer, pos, new_state):
    B = z2.shape[0]
    bb = 8
    cos_t, sin_t = _rope_tables(jnp.full((1,), pos))
    log_g = np.log1p(-np.exp2(-5.0 - np.arange(H_A, dtype=np.float32))).astype(np.float32)
    gam = np.broadcast_to(np.exp(log_g)[:, None, None], (H_A, 1, 128)).astype(np.float32)
    zspec = lambda idx: pl.BlockSpec((bb, W_AQ), lambda i: (i, idx))
    return pl.pallas_call(
        functools.partial(_ret_step_kernel, bb=bb),
        grid=(B // bb,),
        in_specs=[zspec(0), zspec(1), zspec(2), zspec(3),
                  pl.BlockSpec((1, 128), lambda i: (0, 0)),
                  pl.BlockSpec((1, 128), lambda i: (0, 0)),
                  pl.BlockSpec((H_A, 1, 128), lambda i: (0, 0, 0)),
                  pl.BlockSpec((None, bb, H_A, DK_A, DV_A), lambda i: (layer, i, 0, 0, 0)),
                  pl.BlockSpec(memory_space=pl.ANY)],
        out_specs=[pl.BlockSpec((bb, W_A), lambda i: (i, 0)),
                   pl.BlockSpec((None, bb, H_A, DK_A, DV_A), lambda i: (layer, i, 0, 0, 0))],
        out_shape=[jax.ShapeDtypeStruct((B, W_A), F32),
                   jax.ShapeDtypeStruct(new_state.shape, F32)],
        input_output_aliases={8: 1},
        compiler_params=_cp("parallel"),
        name="retention_step",
    )(z2, z2, z2, z2, cos_t, sin_t, gam, state, new_state)


def _qknorm_kernel(q_ref, k_ref, v_ref, gq_ref, gk_ref, ones_ref, k_stack_ref, v_stack_ref,
                   qb_ref, k32_ref, kb_ref, v32_ref, vb_ref, *, q_scale):
    del k_stack_ref, v_stack_ref
    ones2 = ones_ref[...]
    q, k, v = q_ref[...], k_ref[...], v_ref[...]
    tm = q.shape[0]
    qn = q * lax.rsqrt(_seg_sum_wide(q * q, ones2) * (1.0 / DH_B) + EPS) * gq_ref[...] * q_scale
    kn = k * lax.rsqrt(_seg_sum_wide(k * k, ones2) * (1.0 / DH_B) + EPS) * gk_ref[...]
    qb_ref[...] = qn.astype(BF16)
    kb_ref[...] = kn.astype(BF16)
    vb_ref[...] = v.astype(BF16)
    for h in range(H_B):
        head_rows = pl.ds(h, tm, stride=H_B)
        k32_ref[head_rows, :] = kn[:, h * DV_B:(h + 1) * DV_B]
        v32_ref[head_rows, :] = v[:, h * DV_B:(h + 1) * DV_B]


def _block_ones(group):
    i = np.arange(128)
    return jnp.asarray((i[:, None] // group == i[None, :] // group).astype(np.float32), BF16)


def qk_norm(z2, gq_row, gk_row, tm, q_scale, layer, k_stack, v_stack):
    T = z2.shape[0]
    base = (Z_GATE + 3 * 1024) // W_BQK
    zspec = lambda idx: pl.BlockSpec((tm, W_BQK), lambda m: (m, base + idx))
    row = pl.BlockSpec((1, W_BQK), lambda m: (0, 0))
    out = pl.BlockSpec((tm, W_BQK), lambda m: (m, 0))
    out_rows = pl.BlockSpec((None, tm * H_B, DV_B), lambda m: (layer, m, 0))
    stack = pl.BlockSpec(memory_space=pl.ANY)
    return pl.pallas_call(
        functools.partial(_qknorm_kernel, q_scale=q_scale),
        grid=(T // tm,),
        in_specs=[zspec(0), zspec(1), zspec(2), row, row, pl.BlockSpec((128, 128), lambda m: (0, 0)),
                  stack, stack],
        out_specs=[out, out_rows, out, out_rows, out],
        out_shape=[jax.ShapeDtypeStruct((T, W_BQK), BF16), jax.ShapeDtypeStruct(k_stack.shape, F32),
                   jax.ShapeDtypeStruct((T, W_BQK), BF16), jax.ShapeDtypeStruct(v_stack.shape, F32),
                   jax.ShapeDtypeStruct((T, W_B), BF16)],
        input_output_aliases={6: 1, 7: 3},
        compiler_params=_cp("parallel"),
        name="qk_norm",
    )(z2, z2, z2, gq_row, gk_row, _block_ones(DH_B), k_stack, v_stack)


def _bucket_np(n):
    max_exact = N_BUCKETS // 2
    nf = np.maximum(n, 1).astype(np.float32)
    large = max_exact + (np.log(nf / np.float32(max_exact)) / np.float32(math.log(MAX_DISTANCE / max_exact))
                         * np.float32(N_BUCKETS - max_exact)).astype(np.int32)
    large = np.minimum(large, N_BUCKETS - 1)
    return np.where(n < max_exact, n, large).astype(np.int32)


def _bias_kernel(tab_ref, bkt_ref, o_ref, *, log2_far_shift):
    h = pl.program_id(0)
    bk = bkt_ref[...]
    acc = jnp.zeros(bk.shape, F32)
    for b in range(N_BUCKETS):
        acc = jnp.where(bk == b, tab_ref[b, h], acc)
    if log2_far_shift:
        acc = (acc - tab_ref[N_BUCKETS - 1, h]) * math.log2(math.e)
    o_ref[...] = jnp.where(bk < 0, NEG, acc)


def bias_tiles(rel_bias, buckets, log2_far_shift):
    R, C = buckets.shape
    tr = min(R, 512)
    return pl.pallas_call(
        functools.partial(_bias_kernel, log2_far_shift=log2_far_shift),
        grid=(H_B, R // tr),
        in_specs=[pl.BlockSpec(memory_space=pltpu.SMEM),
                  pl.BlockSpec((tr, C), lambda h, r: (r, 0))],
        out_specs=pl.BlockSpec((None, tr, C), lambda h, r: (h, r, 0)),
        out_shape=jax.ShapeDtypeStruct((H_B, R, C), F32),
        compiler_params=_cp("parallel", "parallel"),
        name="bias_tiles",
    )(rel_bias, jnp.asarray(buckets))


def _prompt_buckets(t):
    r = np.arange(t)[:, None]
    c = np.arange(t)[None, :]
    diag = np.where(c <= r, _bucket_np(np.maximum(r - c, 0)), -1)
    off1 = _bucket_np(t + r - c)
    assert t >= MAX_DISTANCE
    return np.concatenate([diag, off1], axis=0).astype(np.int32)


def _lambda_full(lam_ref, lam_init):
    lv = lam_ref[...]
    s1 = jnp.sum(lv[0:1] * lv[1:2], axis=-1, keepdims=True)
    s2 = jnp.sum(lv[2:3] * lv[3:4], axis=-1, keepdims=True)
    return jnp.exp(s1) - jnp.exp(s2) + lam_init


def _flash_kernel(q_ref, k_ref, v_ref, bias_ref, lam_ref, gain_ref, o_ref, *scratch, lam_init, heads):
    for hh in range(heads):
        cols = pl.ds(hh * DV_B, DV_B)
        _flash_head(q_ref.at[:, cols], k_ref.at[:, cols], v_ref.at[:, cols], bias_ref.at[hh], lam_ref, gain_ref,
                    o_ref.at[:, cols], *scratch, lam_init=lam_init)


def _flash_head(q_ref, k_ref, v_ref, bias_ref, lam_ref, gain_ref, o_ref, qq_scr, m_scr, acc_scr, s_scr,
                *, lam_init):
    i = pl.program_id(2)
    t = q_ref.shape[0]

    q = q_ref[...]
    first = lax.broadcasted_iota(jnp.int32, q.shape, 1) < DH_B
    qq_scr[:t] = jnp.where(first, q, jnp.zeros_like(q))
    qq_scr[t:] = jnp.where(first, jnp.zeros_like(q), q)
    m_scr[...] = jnp.full(m_scr.shape, NEG, F32)
    acc_scr[...] = jnp.zeros_like(acc_scr)

    def raw_scores(comp, j):
        krows = pl.ds(pl.multiple_of(j * t, t), t)
        return lax.dot_general(qq_scr[comp * t:(comp + 1) * t], k_ref[krows, :], (((1,), (1,)), ((), ())),
                               preferred_element_type=F32)

    def accumulate(comp, s, v1):
        rows = slice(comp * t, (comp + 1) * t)
        m_old = m_scr[rows]
        m_new = jnp.maximum(m_old, jnp.max(s, axis=-1, keepdims=True))
        alpha = jnp.exp2(m_old - m_new)
        p = jnp.concatenate([jnp.exp2(s[:, c * 128:(c + 1) * 128] - m_new) for c in range(t // 128)],
                            axis=1).astype(BF16)
        acc_scr[rows] = (jnp.concatenate([alpha, alpha], axis=1) * acc_scr[rows]
                         + jnp.dot(p, v1, preferred_element_type=F32))
        m_scr[rows] = m_new

    def kv_block(j, tile, has_next):
        krows = pl.ds(pl.multiple_of(j * t, t), t)
        v1 = jnp.concatenate([v_ref[krows, :], jnp.ones((t, DV_B), BF16)], axis=1)
        with_bias = (lambda s: s) if tile is None else (lambda s: s + bias_ref[tile * t:(tile + 1) * t, :])
        s_first = s_scr[...]
        s_second = raw_scores(1, j)
        accumulate(0, with_bias(s_first), v1)
        if has_next:
            s_scr[...] = raw_scores(0, j + 1)
        accumulate(1, with_bias(s_second), v1)

    s_scr[...] = raw_scores(0, 0)

    def far_pair(jj, carry):
        kv_block(2 * jj, None, True)
        kv_block(2 * jj + 1, None, True)
        return carry

    n_far = jnp.maximum(i - 1, 0)
    lax.fori_loop(0, n_far // 2, far_pair, 0)

    @pl.when(jnp.logical_and(i >= 1, n_far % 2 == 1))
    def _():
        kv_block(i - 2, None, True)
        kv_block(i - 1, 1, True)
        kv_block(i, 0, False)

    @pl.when(jnp.logical_and(i >= 1, n_far % 2 == 0))
    def _():
        kv_block(i - 1, 1, True)
        kv_block(i, 0, False)

    @pl.when(i == 0)
    def _():
        kv_block(i, 0, False)

    lam = _lambda_full(lam_ref, lam_init)
    acc = acc_scr[...]
    ob = acc[:t, :DV_B] / acc[:t, DV_B:] - lam * (acc[t:, :DV_B] / acc[t:, DV_B:])
    on = ob * lax.rsqrt(jnp.mean(ob * ob, axis=-1, keepdims=True) + EPS) * gain_ref[...]
    o_ref[...] = (on * (1.0 - lam_init)).astype(o_ref.dtype)


def diff_attention_prompt(qb, kb, vb, bias, lam_rows, gain_row, lam_init, t):
    B, L, _ = qb.shape
    n = L // t
    heads = 2
    kv_spec = pl.BlockSpec((None, L, heads * DV_B), lambda b, h, i: (b, 0, h))
    return pl.pallas_call(
        functools.partial(_flash_kernel, lam_init=lam_init, heads=heads),
        grid=(B, H_B // heads, n),
        in_specs=[pl.BlockSpec((None, t, heads * DV_B), lambda b, h, i: (b, i, h)),
                  kv_spec, kv_spec,
                  pl.BlockSpec((heads, 2 * t, t), lambda b, h, i: (h, 0, 0)),
                  pl.BlockSpec((4, DH_B), lambda b, h, i: (0, 0)),
                  pl.BlockSpec((1, DV_B), lambda b, h, i: (0, 0))],
        out_specs=pl.BlockSpec((None, t, heads * DV_B), lambda b, h, i: (b, i, h)),
        out_shape=jax.ShapeDtypeStruct((B, L, W_B), BF16),
        scratch_shapes=[pltpu.VMEM((2 * t, DV_B), BF16), pltpu.VMEM((2 * t, 128), F32),
                        pltpu.VMEM((2 * t, 2 * DV_B), F32), pltpu.VMEM((t, t), F32)],
        compiler_params=_cp("parallel", "parallel", "arbitrary"),
        name="diff_attention_prompt",
    )(qb, kb, vb, bias, lam_rows, gain_row)


def _decode_kernel(pt_ref, q_ref, kn_ref, vn_ref, bias_ref, lam_ref, gain_ref, *rest, n_pages, lam_init, spb):
    n = spb * n_pages
    k_all, v_all, o_ref = rest[:n], rest[n:2 * n], rest[2 * n]
    lam = _lambda_full(lam_ref, lam_init)
    gain = gain_ref[...]
    r8 = lax.broadcasted_iota(jnp.int32, (8, 2 * DH_B), 0)
    l8 = lax.broadcasted_iota(jnp.int32, (8, 2 * DH_B), 1)
    comp_rows = (l8 // DH_B) == r8
    row0 = lax.broadcasted_iota(jnp.int32, (PAGE_SIZE, DV_B), 0) == 0
    nt = lambda x, y: lax.dot_general(x, y, (((1,), (1,)), ((), ())), preferred_element_type=F32)
    units = [(s, h) for s in range(spb) for h in range(H_B)]
    cols = [slice(h * DV_B, (h + 1) * DV_B) for h in range(H_B)]
    head_rows = [pl.ds(h, PAGE_SIZE, stride=H_B) for h in range(H_B)]
    pages = lambda refs, s: refs[s * n_pages:(s + 1) * n_pages]
    ss = []
    for s, h in units:
        qrows = jnp.where(comp_rows, q_ref[s, :, cols[h]].astype(F32), 0.0).astype(BF16)
        ks = [r[head_rows[h], :].astype(BF16) for r in pages(k_all, s)]
        ks.append(jnp.where(row0, kn_ref[s, h:h + 1, :], 0.0).astype(BF16))
        sc = []
        for p in range(0, len(ks) - 1, 2):
            s2 = nt(qrows, jnp.concatenate([ks[p], ks[p + 1]], axis=0))
            sc += [s2[:, :PAGE_SIZE] + bias_ref[h, p], s2[:, PAGE_SIZE:] + bias_ref[h, p + 1]]
        if len(ks) % 2:
            sc.append(nt(qrows, ks[-1]) + bias_ref[h, len(ks) - 1])
        ss.append(sc)
    ps, ls = [], []
    for sc in ss:
        m = functools.reduce(jnp.maximum, [jnp.max(x, axis=-1, keepdims=True) for x in sc])
        p = [jnp.exp(x - m) for x in sc]
        ls.append(functools.reduce(jnp.add, [jnp.sum(x, axis=-1, keepdims=True) for x in p]))
        ps.append([x.astype(BF16) for x in p])
    for (s, h), p_u, l_u in zip(units, ps, ls):
        vs = [r[head_rows[h], :].astype(BF16) for r in pages(v_all, s)]
        vs.append(jnp.where(row0, vn_ref[s, h:h + 1, :], 0.0).astype(BF16))
        parts = [jnp.dot(jnp.concatenate([p_u[p], p_u[p + 1]], axis=1),
                         jnp.concatenate([vs[p], vs[p + 1]], axis=0), preferred_element_type=F32)
                 for p in range(0, len(vs) - 1, 2)]
        if len(vs) % 2:
            parts.append(jnp.dot(p_u[-1], vs[-1], preferred_element_type=F32))
        acc = functools.reduce(jnp.add, parts)
        outn = acc / l_u
        oh = outn[0:1] - lam * outn[1:2]
        on = oh * lax.rsqrt(jnp.mean(oh * oh, axis=-1, keepdims=True) + EPS) * gain
        o_ref[s, :, cols[h]] = on * (1.0 - lam_init)


def diff_attention_decode(qb, k32, v32, cache_k, cache_v, pt_flat, bias_dec, lam_rows, gain_row, layer,
                          lam_init, n_pages):
    B = qb.shape[0]
    spb = 2 if B % 2 == 0 else 1
    rowspec = pl.BlockSpec((spb, 1, W_BQK), lambda g, pt: (g, 0, 0))
    headspec = pl.BlockSpec((spb, H_B, DV_B), lambda g, pt: (g, 0, 0))
    const = lambda s: pl.BlockSpec(s, lambda g, pt: tuple(0 for _ in s))
    page_spec = lambda s, p: pl.BlockSpec(
        (None, None, PAGE_SIZE * H_B, DV_B),
        lambda g, pt, s=s, p=p: (layer, pt[(g * spb + s) * n_pages + p], 0, 0))
    page_specs = [page_spec(s, p) for s in range(spb) for p in range(n_pages)]
    grid_spec = pltpu.PrefetchScalarGridSpec(
        num_scalar_prefetch=1,
        grid=(B // spb,),
        in_specs=[rowspec, headspec, headspec, const(bias_dec.shape), const((4, DH_B)), const((1, DV_B))]
                 + page_specs * 2,
        out_specs=pl.BlockSpec((spb, 1, W_B), lambda g, pt: (g, 0, 0)),
    )
    return pl.pallas_call(
        functools.partial(_decode_kernel, n_pages=n_pages, lam_init=lam_init, spb=spb),
        grid_spec=grid_spec,
        out_shape=jax.ShapeDtypeStruct((B, 1, W_B), F32),
        compiler_params=_cp("arbitrary"),
        name="diff_attention_decode",
    )(pt_flat, qb, k32, v32, bias_dec, lam_rows, gain_row,
      *([cache_k] * (spb * n_pages)), *([cache_v] * (spb * n_pages)))


def _rwkv_prep_kernel(*refs, carry_shift, gate_v):
    it = iter(refs)
    z_ref, prev_ref, mu_ref, w0_ref, a0_ref, kk_ref, ka_ref, wlr_ref, ones_ref = (next(it) for _ in range(9))
    if gate_v:
        v0_ref, wvd_ref, wvu_ref, vf_ref = (next(it) for _ in range(4))
    r_ref, lw_ref, k_ref, v_ref, a_ref, b_ref, gg_ref = (next(it) for _ in range(7))
    z = z_ref[...]
    if carry_shift:
        carry_scr = next(it)

        @pl.when(pl.program_id(1) == 0)
        def _():
            carry_scr[...] = prev_ref[...]

        first = lax.broadcasted_iota(jnp.int32, z.shape, 0) == 0
        zprev = jnp.where(first, carry_scr[...], pltpu.roll(z, 1, 0))
        carry_scr[...] = z_ref[z.shape[0] - 1:z.shape[0], :]
    else:
        zprev = prev_ref[...]
    zs = z + (zprev - z) * mu_ref[...]
    rc, kc, vc = zs[:, :W_C], zs[:, W_C:2 * W_C], zs[:, 2 * W_C:3 * W_C]
    lr = zs[:, 3 * W_C:]
    head = lr[:, :R_DECAY + R_A]
    col = lax.broadcasted_iota(jnp.int32, head.shape, 1)
    act = jnp.concatenate([jnp.where(col < R_DECAY, jnp.tanh(head), head),
                           jax.nn.sigmoid(lr[:, R_DECAY + R_A:])], axis=1)
    up = jnp.dot(act.astype(BF16), wlr_ref[...], preferred_element_type=F32)
    y = -(w0_ref[...] + up[:, :W_C])
    softplus = jnp.maximum(y, 0.0) + jnp.log(1.0 + jnp.exp(-jnp.abs(y)))
    lw_ref[...] = -jnp.exp(-softplus - 0.5)
    if gate_v:
        down = jnp.dot(vc.astype(BF16), wvd_ref[...], preferred_element_type=F32)
        vgate = jax.nn.sigmoid(v0_ref[...] + jnp.dot(down.astype(BF16), wvu_ref[...], preferred_element_type=F32))
        vc = vc + (vf_ref[...].astype(F32) - vc) * vgate
    a = jax.nn.sigmoid(a0_ref[...] + up[:, W_C:2 * W_C])
    gg_ref[...] = up[:, 2 * W_C:].astype(gg_ref.dtype)
    kk = kc * kk_ref[...]
    norm = jnp.sqrt(_seg_sum_wide(kk * kk, ones_ref[...]))
    kk = kk / jnp.maximum(norm, 1e-12)
    r_ref[...] = rc.astype(r_ref.dtype)
    k_ref[...] = (kc * (1.0 + (a - 1.0) * ka_ref[...])).astype(k_ref.dtype)
    v_ref[...] = vc.astype(v_ref.dtype)
    a_ref[...] = (-kk).astype(a_ref.dtype)
    b_ref[...] = (kk * a).astype(b_ref.dtype)


def rwkv_prep(z3, prev, mu_row, w0, a0, k_k, k_a, w_lr, vgate, tm, carry_shift):
    G, Lg, _ = z3.shape
    blk = lambda w, idx: pl.BlockSpec((None, tm, w), lambda g, l: (g, l, idx))
    row = lambda w: pl.BlockSpec((1, w), lambda g, l: (0, 0))
    prev_spec = (pl.BlockSpec((None, 1, RWKV_PAD), lambda g, l: (g, 0, 0)) if carry_shift
                 else blk(RWKV_PAD, 0))
    in_specs = [blk(RWKV_PAD, Z_RWKV // RWKV_PAD), prev_spec, row(RWKV_PAD), row(W_C), row(W_C), row(W_C),
                row(W_C), pl.BlockSpec(w_lr.shape, lambda g, l: (0, 0)),
                pl.BlockSpec((128, 128), lambda g, l: (0, 0))]
    args = [z3, prev, mu_row, w0, a0, k_k, k_a, w_lr, _block_ones(DH_C)]
    if vgate is not None:
        v0, wvd, wvu, vfirst = vgate
        in_specs += [row(W_C), pl.BlockSpec(wvd.shape, lambda g, l: (0, 0)),
                     pl.BlockSpec(wvu.shape, lambda g, l: (0, 0)), blk(W_C, 0)]
        args += [v0, wvd, wvu, vfirst]
    return pl.pallas_call(
        functools.partial(_rwkv_prep_kernel, carry_shift=carry_shift, gate_v=vgate is not None),
        grid=(G, Lg // tm),
        in_specs=in_specs,
        out_specs=[blk(W_C, 0)] * 7,
        out_shape=[jax.ShapeDtypeStruct((G, Lg, W_C), F32 if i == 1 else BF16) for i in range(7)],
        scratch_shapes=[pltpu.VMEM((1, RWKV_PAD), F32)] if carry_shift else [],
        compiler_params=_cp("parallel", "arbitrary"),
        name="rwkv_prep",
    )(*args)


def _rwkv_chunk_kernel(r_ref, lw_ref, k_ref, v_ref, a_ref, b_ref, tri_ref, msl_ref, mli_ref, lvl_ref,
                       y_ref, so_ref, s_scr, *, n_chunks, nb):
    C = RW_CHUNK

    @pl.when(pl.program_id(1) == 0)
    def _():
        s_scr[...] = jnp.zeros_like(s_scr)

    head0 = lax.broadcasted_iota(jnp.int32, (C, 128), 1) < DH_C
    tri3, msl, mli = tri_ref[...], msl_ref[...], mli_ref[...]
    ri = lax.broadcasted_iota(jnp.int32, (128, 128), 0)
    ci = lax.broadcasted_iota(jnp.int32, (128, 128), 1)
    eye = (ri == ci).astype(F32)
    n_levels = lvl_ref.shape[0]
    chains = [(bi, p) for bi in range(nb) for p in range(H_C // 2)]

    def stack(x):
        return jnp.concatenate([jnp.where(head0, x, 0.0), jnp.where(head0, 0.0, x)], axis=0)

    def nt(x, y):
        return lax.dot_general(x, y, (((1,), (1,)), ((), ())), preferred_element_type=F32)

    def dot(x, y):
        return jnp.dot(x, y, preferred_element_type=F32)

    def chunk(ci_, carry):
        rows = pl.ds(pl.multiple_of(ci_ * C, C), C)
        ld = lambda ref, bi, p: ref[bi, rows, p * 128:(p + 1) * 128].astype(F32)
        lams = []
        for bi, p in chains:
            lw = ld(lw_ref, bi, p)
            hi = lw.astype(BF16)
            mid = (lw - hi.astype(F32)).astype(BF16)
            lo = (lw - hi.astype(F32) - mid.astype(F32)).astype(BF16)
            lams.append(dot(tri3, jnp.concatenate([hi, mid, lo], axis=0)))
        AR, BK, Vs, KB, dec = [], [], [], [], []
        for (bi, p), lam in zip(chains, lams):
            r, lw, k = ld(r_ref, bi, p), ld(lw_ref, bi, p), ld(k_ref, bi, p)
            v, a, b = ld(v_ref, bi, p), ld(a_ref, bi, p), ld(b_ref, bi, p)
            lam_c = lam[C - 1:C, :]
            e_neg = jnp.exp(-lam)
            e_tail = jnp.exp(lam_c - lam)
            AR.append(jnp.concatenate([stack(a * jnp.exp(lam - lw)), stack(r * jnp.exp(lam))],
                                      axis=0).astype(BF16))
            BK.append(jnp.concatenate([stack(b * e_neg), stack(k * e_neg)], axis=0).astype(BF16))
            KB.append(jnp.concatenate([stack(k * e_tail), stack(b * e_tail)], axis=0).astype(BF16))
            Vs.append(stack(v))
            dec.append(jnp.exp(lam_c))
        G = [nt(x, y) for x, y in zip(AR, BK)]
        I0 = [nt(x, s_scr[bi, p].astype(BF16)) for x, (bi, p) in zip(AR, chains)]
        rhs = [i0[:128] + dot((g[:128, 128:] * msl).astype(BF16), v.astype(BF16))
               for g, i0, v in zip(G, I0, Vs)]
        Nh, Nl, D = [], [], []
        for g in G:
            h, l = _split2(g[:128, :128] * msl)
            Nh.append(h)
            Nl.append(l)
            D.append(eye + (h * lvl_ref[0]).astype(F32) + (l * lvl_ref[0]).astype(F32))
        for lv in range(1, n_levels):
            m = lvl_ref[lv]
            Db = [d.astype(BF16) for d in D]
            X = [dot(h * m, db) for h, db in zip(Nh, Db)]
            D = [d + dot(db, x.astype(BF16)) for d, db, x in zip(D, Db, X)]
        Us = [dot(d.astype(BF16), x.astype(BF16)) for d, x in zip(D, rhs)]
        for (bi, p), g, i0, u, v in zip(chains, G, I0, Us, Vs):
            ys = i0[128:] + dot(jnp.concatenate([g[128:, :128] * mli, g[128:, 128:] * mli], axis=1).astype(BF16),
                                jnp.concatenate([u, v], axis=0).astype(BF16))
            y_ref[bi, rows, p * 128:(p + 1) * 128] = ys[:C] + ys[C:]
        for (bi, p), u, v, kb, d in zip(chains, Us, Vs, KB, dec):
            vu = jnp.concatenate([v, u], axis=0)
            s_scr[bi, p] = s_scr[bi, p] * d + dot(vu.T.astype(BF16), kb)
        return carry

    lax.fori_loop(0, n_chunks, chunk, 0)

    @pl.when(pl.program_id(1) == pl.num_programs(1) - 1)
    def _():
        so_ref[...] = s_scr[...]


def _rwkv_chunk_tables():
    C = RW_CHUNK
    t = np.arange(C)
    tri = (t[:, None] >= t[None, :]).astype(np.float32)
    tri3 = np.concatenate([tri, tri, tri], axis=1)
    i = np.arange(2 * C)
    same = (i[:, None] // C) == (i[None, :] // C)
    msl = (same & (i[:, None] > i[None, :])).astype(np.float32)
    mli = (same & (i[:, None] >= i[None, :])).astype(np.float32)
    levels = []
    n = 1
    while n < C:
        levels.append(same & ((i[:, None] // (2 * n)) == (i[None, :] // (2 * n)))
                      & ((i[:, None] // n) % 2 == 1) & ((i[None, :] // n) % 2 == 0))
        n *= 2
    return jnp.asarray(tri3, BF16), msl, mli, jnp.asarray(np.stack(levels).astype(np.float32), BF16)


def rwkv_chunk_scan(r, lw, k, v, a, b, tb, nb):
    B, L, _ = r.shape
    tri3, msl, mli, lvl = _rwkv_chunk_tables()
    blk = pl.BlockSpec((nb, tb, W_C), lambda bi, l: (bi, l, 0))
    c2 = lambda s: pl.BlockSpec(s, lambda bi, l: tuple(0 for _ in s))
    return pl.pallas_call(
        functools.partial(_rwkv_chunk_kernel, n_chunks=tb // RW_CHUNK, nb=nb),
        grid=(B // nb, L // tb),
        in_specs=[blk] * 6 + [c2(tri3.shape), c2(msl.shape), c2(mli.shape), c2(lvl.shape)],
        out_specs=[blk, pl.BlockSpec((nb, H_C // 2, 128, 128), lambda bi, l: (bi, 0, 0, 0))],
        out_shape=[jax.ShapeDtypeStruct((B, L, W_C), F32),
                   jax.ShapeDtypeStruct((B, H_C // 2, 128, 128), F32)],
        scratch_shapes=[pltpu.VMEM((nb, H_C // 2, 128, 128), F32)],
        compiler_params=_cp("parallel", "arbitrary"),
        name="rwkv_chunk_scan",
    )(r, lw, k, v, a, b, tri3, msl, mli, lvl)


def _rwkv_step_kernel(s_ref, w_ref, a_ref, b_ref, k_ref, r_ref, v_ref, stack_ref, y_ref, so_ref, vt_scr, y_scr):
    del stack_ref
    tr = lambda ref: ref[...].astype(F32).T
    wT, aT, bT, kT, rT = jnp.exp(tr(w_ref)), tr(a_ref), tr(b_ref), tr(k_ref), tr(r_ref)
    vt_scr[...] = tr(v_ref)
    for hh in range(2):
        ch = slice(hh * DH_C, (hh + 1) * DH_C)
        w, a, b, k, r = wT[ch], aT[ch], bT[ch], kT[ch], rT[ch]

        def value_row(i, carry):
            row = pl.ds(hh * DH_C + i, 1)
            S = s_ref[hh, i]
            sa = jnp.sum(S * a, axis=0, keepdims=True)
            Sn = S * w + sa * b + vt_scr[row, :] * k
            so_ref[hh, i] = Sn
            y_scr[row, :] = jnp.sum(Sn * r, axis=0, keepdims=True)
            return carry

        lax.fori_loop(0, DH_C, value_row, 0)
    y_ref[...] = y_scr[...].T


def rwkv_step(state_t, layer, lw, a, b, k, r, v, new_state_t):
    B = lw.shape[0]
    vec = pl.BlockSpec((B, 2 * DH_C), lambda p: (0, p))
    sspec = pl.BlockSpec((None, 2, DH_C, DH_C, B), lambda p: (layer, p, 0, 0, 0))
    return pl.pallas_call(
        _rwkv_step_kernel,
        grid=(H_C // 2,),
        in_specs=[sspec] + [vec] * 6 + [pl.BlockSpec(memory_space=pl.ANY)],
        out_specs=[vec, sspec],
        out_shape=[jax.ShapeDtypeStruct((B, W_C), F32),
                   jax.ShapeDtypeStruct(new_state_t.shape, F32)],
        scratch_shapes=[pltpu.VMEM((2 * DH_C, B), F32), pltpu.VMEM((2 * DH_C, B), F32)],
        input_output_aliases={7: 1},
        compiler_params=_cp("parallel"),
        name="rwkv_step",
    )(state_t, lw, a, b, k, r, v, new_state_t)


def _merge_kernel(x_ref, ya_ref, yb_ref, y_ref, r_ref, k_ref, v_ref, gg_ref, za_ref, zb_ref, zc_ref, gt_ref,
                  lng_ref, lnb_ref, rk_ref, ones_ref, wa_ref, wb_ref, wc_ref, wo_ref, o_ref):
    ones2 = ones_ref[...]
    y = y_ref[...]
    mu = _seg_sum_wide(y, ones2) * (1.0 / DH_C)
    d = y - mu
    var = _seg_sum_wide(d * d, ones2) * (1.0 / DH_C)
    ycn = d * lax.rsqrt(var + LNX_EPS) * lng_ref[...] + lnb_ref[...]
    f32 = lambda ref: ref[...].astype(F32)
    bonus = _seg_sum_wide(f32(r_ref) * f32(k_ref) * rk_ref[...], ones2) * f32(v_ref)
    yc = (ycn + bonus) * f32(gg_ref)
    proj = lambda t, w: jnp.dot(t.astype(BF16), w[...], preferred_element_type=F32)
    merged = (jax.nn.sigmoid(za_ref[...]) * proj(ya_ref[...], wa_ref)
              + jax.nn.sigmoid(zb_ref[...]) * proj(yb_ref[...], wb_ref)
              + jax.nn.sigmoid(zc_ref[...]) * proj(yc, wc_ref))
    o_ref[...] = x_ref[...] + gt_ref[...] * proj(merged, wo_ref)


def merge_out(x2, ya, yb, y, r, k, v, gg, z2, gt, lng, lnb, rk, wa, wb, wc, wo, tm):
    T, D = x2.shape
    G, R, _ = gt.shape
    tiles_per_group = (T // tm) // G
    tok = lambda w: pl.BlockSpec((tm, w), lambda m: (m, 0))
    gate = lambda idx: pl.BlockSpec((tm, D), lambda m: (m, Z_GATE // D + idx))
    row = lambda w: pl.BlockSpec((1, w), lambda m: (0, 0))
    full = lambda a: pl.BlockSpec(a.shape, lambda m: (0, 0))
    return pl.pallas_call(
        _merge_kernel,
        grid=(T // tm,),
        in_specs=[tok(D)] + [tok(W_C)] * 7 + [gate(0), gate(1), gate(2),
                  pl.BlockSpec((None, R, D), lambda m: (m // tiles_per_group, 0, 0)),
                  row(W_C), row(W_C), row(W_C), pl.BlockSpec((128, 128), lambda m: (0, 0)),
                  full(wa), full(wb), full(wc), full(wo)],
        out_specs=tok(D),
        out_shape=jax.ShapeDtypeStruct((T, D), F32),
        compiler_params=_cp("parallel"),
        name="merge_out",
    )(x2, ya, yb, y, r, k, v, gg, z2, z2, z2, gt, lng, lnb, rk, _block_ones(DH_C), wa, wb, wc, wo)


def _ffn_kernel(x_ref, g_ref, sc_ref, sh_ref, gt_ref, w1_ref, w2_ref, o_ref, h_scr, acc_scr):
    f = pl.program_id(1)

    @pl.when(f == 0)
    def _():
        x = x_ref[...]
        y = x * lax.rsqrt(jnp.mean(x * x, axis=-1, keepdims=True) + EPS) * g_ref[...]
        h_scr[...] = (y * (1.0 + sc_ref[...]) + sh_ref[...]).astype(BF16)
        acc_scr[...] = jnp.zeros_like(acc_scr)

    u = jnp.maximum(jnp.dot(h_scr[...], w1_ref[...], preferred_element_type=F32), 0.0)
    acc_scr[...] += jnp.dot((u * u).astype(BF16), w2_ref[...], preferred_element_type=F32)

    @pl.when(f == pl.num_programs(1) - 1)
    def _():
        o_ref[...] = x_ref[...] + gt_ref[...] * acc_scr[...]


def ffn(x2, gain_row, sc, sh, gt, w1, w2, tm, tf):
    T, D = x2.shape
    F = w1.shape[1]
    G, R, _ = sc.shape
    tiles_per_group = (T // tm) // G
    mod_spec = pl.BlockSpec((None, R, D), lambda m, f: (m // tiles_per_group, 0, 0))
    return pl.pallas_call(
        _ffn_kernel,
        grid=(T // tm, F // tf),
        in_specs=[pl.BlockSpec((tm, D), lambda m, f: (m, 0)),
                  pl.BlockSpec((1, D), lambda m, f: (0, 0)),
                  mod_spec, mod_spec, mod_spec,
                  pl.BlockSpec((D, tf), lambda m, f: (0, f)),
                  pl.BlockSpec((tf, D), lambda m, f: (f, 0))],
        out_specs=pl.BlockSpec((tm, D), lambda m, f: (m, 0)),
        out_shape=jax.ShapeDtypeStruct((T, D), F32),
        scratch_shapes=[pltpu.VMEM((tm, D), BF16), pltpu.VMEM((tm, D), F32)],
        compiler_params=_cp("parallel", "arbitrary"),
        name="ffn",
    )(x2, gain_row, sc, sh, gt, w1, w2)


def prep_w_in(w_in):
    depth, D, _ = w_in.shape
    wt = jnp.transpose(w_in, (0, 2, 1))
    o_diff, o_rwkv, o_gate = RET_COLS, RET_COLS + DIFF_COLS, RET_COLS + DIFF_COLS + RWKV_COLS
    return jnp.concatenate([
        wt[:, :RET_COLS], wt[:, o_rwkv:o_gate], jnp.zeros((depth, RWKV_PAD - RWKV_COLS, D), F32),
        wt[:, o_gate:], wt[:, o_diff:o_rwkv]], axis=1).astype(BF16)


def _prep_layer(p, l, w_in_p):
    place = lambda w, slot: jnp.pad(w, ((0, 0), (slot * W_C, (2 - slot) * W_C)))
    w_lr = jnp.concatenate([place(p['w_decay_up'][l], 0), place(p['w_a_up'][l], 1), place(p['w_g_up'][l], 2),
                            jnp.zeros((LR_COLS - R_DECAY - R_A - R_G, 3 * W_C), F32)], axis=0)
    row = lambda t: t.reshape(1, -1)
    lp = dict(
        layer=l, w_ada=p['w_ada'], b_ada=row(p['b_ada'][l]),
        norm1=row(p['norm1'][l]), norm2=row(p['norm2'][l]), w_in=w_in_p,
        gq=row(jnp.tile(p['qk_norm_q'][l], 2 * H_B)), gk=row(jnp.tile(p['qk_norm_k'][l], 2 * H_B)),
        lam_rows=jnp.stack([p['lambda_q1'][l], p['lambda_k1'][l], p['lambda_q2'][l], p['lambda_k2'][l]]),
        subln=row(p['subln_diff'][l]),
        mu=row(jnp.pad(p['mu_shift'][l], (0, RWKV_PAD - RWKV_COLS))),
        w0=row(p['w0'][l]), a0=row(p['a0'][l]), k_k=row(p['k_k'][l]), k_a=row(p['k_a'][l]),
        w_lr=w_lr.astype(BF16), r_k=row(p['r_k'][l]), lnx_g=row(p['lnx_g'][l]), lnx_b=row(p['lnx_b'][l]),
        w_up_a=p['w_up_a'][l].astype(BF16), w_up_b=p['w_up_b'][l].astype(BF16),
        w_up_c=p['w_up_c'][l].astype(BF16), w_out=p['w_out'][l].astype(BF16),
        w_ff1=p['w_ff1'][l].astype(BF16), w_ff2=p['w_ff2'][l].astype(BF16),
        lam_init=0.8 - 0.6 * math.exp(-0.3 * l),
    )
    if l > 0:
        lp['v0'] = row(p['v0'][l - 1])
        lp['w_v_down'] = jnp.pad(p['w_v_down'][l - 1], ((0, 0), (0, 128 - R_V))).astype(BF16)
        lp['w_v_up'] = jnp.pad(p['w_v_up'][l - 1], ((0, 128 - R_V), (0, 0))).astype(BF16)
    return lp


def _modulation(c, lp, per_token):
    B, D = c.shape
    start = 0 if not per_token else lp['mod'].shape[0] - B
    mod = lp['mod'][start:start + B]
    parts = [mod[:, i * D:(i + 1) * D] for i in range(6)]
    shape = (1, B, D) if per_token else (B, 1, D)
    return [t.reshape(shape) for t in parts]


def _trunk_prompt(x, c, layers, bias_p):
    B, L, D = x.shape
    T = B * L
    tiles = _prompt_tiles(L)
    x2 = x.reshape(T, D)
    cos_t, sin_t = _rope_tables(jnp.arange(L))
    ret_out, rwkv_out, shift_out = [], [], []
    k_st = jnp.zeros((len(layers), T * H_B, DV_B), F32)
    v_st = jnp.zeros((len(layers), T * H_B, DV_B), F32)
    v_first = None
    for l, lp in enumerate(layers):
        sh1, sc1, gt1, sh2, sc2, gt2 = _modulation(c, lp, per_token=False)
        z2 = in_proj(x2, lp['norm1'], sc1, sh1, lp['w_in'], l, tiles['in_proj_rows'],
                     tiles['in_proj_cols'])
        ZC = z2.shape[1]
        z3 = z2.reshape(B, L, ZC)
        ya, s_ret = retention_prompt(z3, cos_t, sin_t, tiles['retention'])
        qb, k_st, kb, v_st, vb = qk_norm(z2, lp['gq'], lp['gk'], tiles['qk_norm'],
                                         DH_B ** -0.5 * math.log2(math.e), l, k_st, v_st)
        r3 = lambda t: t.reshape(B, L, W_B)
        yb = diff_attention_prompt(r3(qb), r3(kb), r3(vb), bias_p, lp['lam_rows'], lp['subln'],
                                   lp['lam_init'], tiles['attention'])
        vgate = None if l == 0 else (lp['v0'], lp['w_v_down'], lp['w_v_up'], v_first)
        shift0 = jnp.zeros((B, 1, RWKV_PAD), F32)
        r, lw, k, v, a, b, gg = rwkv_prep(z3, shift0, lp['mu'], lp['w0'], lp['a0'], lp['k_k'], lp['k_a'],
                                          lp['w_lr'], vgate, tiles['rwkv_prep'], carry_shift=True)
        if l == 0:
            v_first = v
        y, s_pair = rwkv_chunk_scan(r, lw, k, v, a, b, tiles['rwkv_scan'], B)
        shift_out.append(z3[:, L - 1:, Z_RWKV:Z_RWKV + RWKV_COLS])
        f2 = lambda t: t.reshape(T, -1)
        x2 = merge_out(x2, f2(ya), f2(yb), f2(y), f2(r), f2(k), f2(v), f2(gg), z2, gt1, lp['lnx_g'],
                       lp['lnx_b'], lp['r_k'], lp['w_up_a'], lp['w_up_b'], lp['w_up_c'], lp['w_out'],
                       tiles['merge'])
        x2 = ffn(x2, lp['norm2'], sc2, sh2, gt2, lp['w_ff1'], lp['w_ff2'], tiles['ffn_rows'],
                 tiles['ffn_hidden'])
        ret_out.append(s_ret)
        sp = s_pair.reshape(B, H_C // 2, 2, DH_C, 2, DH_C)
        rwkv_out.append(jnp.stack([sp[:, :, 0, :, 0, :], sp[:, :, 1, :, 1, :]], axis=2)
                        .reshape(B, H_C, DH_C, DH_C))
    kv_shape = (len(layers), B, L, H_B, DV_B)
    return (x2.reshape(B, L, D), k_st.reshape(kv_shape), v_st.reshape(kv_shape), jnp.stack(ret_out),
            jnp.stack(rwkv_out), jnp.stack(shift_out))


def _trunk_decode(x, c, layers, bias_d, state_ret, state_rwkv, state_shift, cache_k, cache_v, page_table):
    B, _, D = x.shape
    n_pages = page_table.shape[1]
    past = n_pages * PAGE_SIZE
    x2 = x.reshape(B, D)
    pt_flat = page_table.reshape(-1)
    ck = cache_k.reshape(cache_k.shape[0], cache_k.shape[1], PAGE_SIZE * H_B, 2 * DH_B)
    cv = cache_v.reshape(cache_v.shape[0], cache_v.shape[1], PAGE_SIZE * H_B, DV_B)
    state_t = jnp.transpose(state_rwkv, (0, 2, 3, 4, 1))
    shift_out = []
    ret_st = jnp.zeros(state_ret.shape, F32)
    rwkv_st = jnp.zeros(state_t.shape, F32)
    k_st = jnp.zeros((len(layers), B * H_B, DV_B), F32)
    v_st = jnp.zeros((len(layers), B * H_B, DV_B), F32)
    v_first = None
    for l, lp in enumerate(layers):
        sh1, sc1, gt1, sh2, sc2, gt2 = _modulation(c, lp, per_token=True)
        z2 = in_proj(x2, lp['norm1'], sc1, sh1, lp['w_in'], l, B, _prompt_tiles(B)['in_proj_cols'])
        ZC = z2.shape[1]
        ya, ret_st = retention_step(z2, state_ret, l, past, ret_st)
        qb, k_st, kb, v_st, vb = qk_norm(z2, lp['gq'], lp['gk'], B, DH_B ** -0.5, l, k_st, v_st)
        h3 = lambda t: t[l].reshape(B, H_B, DV_B)
        yb = diff_attention_decode(qb.reshape(B, 1, W_BQK), h3(k_st), h3(v_st), ck, cv, pt_flat, bias_d,
                                   lp['lam_rows'], lp['subln'], l, lp['lam_init'], n_pages).reshape(B, W_B)
        vgate = None if l == 0 else (lp['v0'], lp['w_v_down'], lp['w_v_up'], v_first)
        prev = jnp.pad(state_shift[l].reshape(1, B, RWKV_COLS), ((0, 0), (0, 0), (0, RWKV_PAD - RWKV_COLS)))
        r, lw, k, v, a, b, gg = rwkv_prep(z2.reshape(1, B, ZC), prev, lp['mu'], lp['w0'], lp['a0'], lp['k_k'],
                                          lp['k_a'], lp['w_lr'], vgate, B, carry_shift=False)
        if l == 0:
            v_first = v
        f2 = lambda t: t.reshape(B, -1)
        y_dec, rwkv_st = rwkv_step(state_t, l, f2(lw), f2(a), f2(b), f2(k), f2(r), f2(v), rwkv_st)
        shift_out.append(z2[:, Z_RWKV:Z_RWKV + RWKV_COLS].reshape(B, 1, RWKV_COLS))
        x2 = merge_out(x2, ya, yb, y_dec, f2(r), f2(k), f2(v), f2(gg), z2, gt1, lp['lnx_g'], lp['lnx_b'],
                       lp['r_k'], lp['w_up_a'], lp['w_up_b'], lp['w_up_c'], lp['w_out'], B)
        x2 = ffn(x2, lp['norm2'], sc2, sh2, gt2, lp['w_ff1'], lp['w_ff2'], B, 1024)
    rwkv_state = jnp.transpose(rwkv_st, (0, 4, 1, 2, 3))
    kv_shape = (len(layers), B, 1, H_B, DV_B)
    return (x2.reshape(B, 1, D), k_st.reshape(kv_shape), v_st.reshape(kv_shape), ret_st,
            rwkv_state, jnp.stack(shift_out))


def _decode_buckets(n_pages):
    past = n_pages * PAGE_SIZE
    key = np.arange((n_pages + 1) * PAGE_SIZE)
    bk = np.where(key <= past, _bucket_np(np.maximum(past - key, 0)), -1)
    return np.broadcast_to(bk.reshape(n_pages + 1, 1, PAGE_SIZE), (n_pages + 1, 8, PAGE_SIZE)).astype(np.int32)


def kernel(x_prompt, x_sample, c_prompt, c_sample, cache_k_diff, cache_v_diff, page_table, state_ret, state_rwkv, state_shift, rel_bias, w_ada, b_ada, norm1, norm2, w_in, qk_norm_q, qk_norm_k, lambda_q1, lambda_k1, lambda_q2, lambda_k2, subln_diff, mu_shift, w0, w_decay_up, a0, w_a_up, w_g_up, v0, w_v_down, w_v_up, k_k, k_a, r_k, lnx_g, lnx_b, w_up_a, w_up_b, w_up_c, w_out, w_ff1, w_ff2):
    p = dict(w_ada=w_ada, b_ada=b_ada, norm1=norm1, norm2=norm2, w_in=w_in,
             qk_norm_q=qk_norm_q, qk_norm_k=qk_norm_k, lambda_q1=lambda_q1, lambda_k1=lambda_k1,
             lambda_q2=lambda_q2, lambda_k2=lambda_k2, subln_diff=subln_diff, mu_shift=mu_shift,
             w0=w0, w_decay_up=w_decay_up, a0=a0, w_a_up=w_a_up, w_g_up=w_g_up, v0=v0,
             w_v_down=w_v_down, w_v_up=w_v_up, k_k=k_k, k_a=k_a, r_k=r_k, lnx_g=lnx_g, lnx_b=lnx_b,
             w_up_a=w_up_a, w_up_b=w_up_b, w_up_c=w_up_c, w_out=w_out, w_ff1=w_ff1, w_ff2=w_ff2)
    depth = w_in.shape[0]
    w_in_p = prep_w_in(w_in)
    layers = [_prep_layer(p, l, w_in_p) for l in range(depth)]
    bp = c_prompt.shape[0]
    c_all = jnp.concatenate([jnp.pad(c_prompt, ((0, -bp % 16), (0, 0))), c_sample], axis=0)
    for lp in layers:
        lp['mod'] = ada_mod(c_all, lp['w_ada'], lp['b_ada'], lp['layer'])
    L = x_prompt.shape[1]
    n_pages = page_table.shape[1]
    bias_p = bias_tiles(rel_bias, _prompt_buckets(_prompt_tiles(L)['attention']), True)
    bd = bias_tiles(rel_bias, _decode_buckets(n_pages).reshape((n_pages + 1) * 8, PAGE_SIZE), False)
    bias_d = bd.reshape(H_B, n_pages + 1, 8, PAGE_SIZE)

    y_p, k_p, v_p, ret_p, rwkv_p, shift_p = _trunk_prompt(x_prompt, c_prompt, layers, bias_p)
    y_s, k_s, v_s, ret_s, rwkv_s, shift_s = _trunk_decode(
        x_sample, c_sample, layers, bias_d, state_ret, state_rwkv, state_shift,
        cache_k_diff, cache_v_diff, page_table)
    return (y_p, y_s, k_p, v_p, k_s, v_s, ret_p, ret_s, rwkv_p, rwkv_s, shift_p, shift_s)
```

```python
import functools
import math

import numpy as np
import jax
import jax.numpy as jnp
from jax import lax
from jax.experimental import pallas as pl
from jax.experimental.pallas import tpu as pltpu

F32 = jnp.float32
BF16 = jnp.bfloat16

H_A, DK_A, DV_A = 4, 128, 128
RET_CHUNK = 128
ROPE_BASE = 10000.0
H_B, DH_B, DV_B = 4, 64, 128
N_BUCKETS, MAX_DISTANCE = 32, 128
H_C, DH_C = 8, 64
R_DECAY, R_A, R_V, R_G = 64, 64, 32, 160
LNX_EPS = 64e-5
EPS = 1e-6
PAGE_SIZE = 128
W_AQ, W_A = H_A * DK_A, H_A * DV_A
W_BQK, W_B = H_B * 2 * DH_B, H_B * DV_B
W_C = H_C * DH_C
RET_COLS = 2 * W_AQ + 2 * W_A
DIFF_COLS = 2 * W_BQK + W_B
RWKV_COLS = 3 * W_C + R_DECAY + R_A + R_G
RWKV_PAD = 2048
LR_COLS = RWKV_PAD - 3 * W_C

Z_RET, Z_RWKV = 0, RET_COLS
Z_GATE = Z_RWKV + RWKV_PAD
NEG = -1e30
RW_CHUNK = 64
V7X_VMEM_BYTES = 64 * 1024 * 1024
VMEM_LIMIT = V7X_VMEM_BYTES * 7 // 8


def _cp(*sem):
    return pltpu.CompilerParams(dimension_semantics=sem, vmem_limit_bytes=VMEM_LIMIT)


def _prompt_tiles(L):
    cap = lambda n: min(n, L)
    return dict(in_proj_rows=cap(1024), in_proj_cols=2176, retention=cap(512), qk_norm=cap(1024),
                attention=cap(512), rwkv_prep=cap(512), rwkv_scan=cap(256), merge=cap(512),
                ffn_rows=cap(1024), ffn_hidden=1024)


def _silu(x):
    return x * jax.nn.sigmoid(x)


def _split2(x):
    hi = x.astype(BF16)
    lo = (x - hi.astype(F32)).astype(BF16)
    return hi, lo


def _seg_sum(x, ones):
    return jnp.dot(x.astype(BF16), ones, preferred_element_type=F32)


def _seg_sum_wide(x, ones2):
    return jnp.concatenate(
        [_seg_sum(x[:, c * 128:(c + 1) * 128], ones2) for c in range(x.shape[1] // 128)], axis=1)


def _mod_kernel(c_ref, w_ref, b_ref, o_ref):
    s = _silu(c_ref[...])
    o_ref[...] = jnp.dot(s.astype(BF16), w_ref[...].astype(BF16), preferred_element_type=F32) + b_ref[...]


def ada_mod(c_pad, w_stack, b_row, layer):
    R, D = c_pad.shape
    N = w_stack.shape[2]
    tn = 1536
    return pl.pallas_call(
        _mod_kernel,
        grid=(N // tn,),
        in_specs=[pl.BlockSpec((R, D), lambda n: (0, 0)),
                  pl.BlockSpec((None, D, tn), lambda n: (layer, 0, n)),
                  pl.BlockSpec((1, tn), lambda n: (0, n))],
        out_specs=pl.BlockSpec((R, tn), lambda n: (0, n)),
        out_shape=jax.ShapeDtypeStruct((R, N), F32),
        compiler_params=_cp("parallel"),
        name="ada_mod",
    )(c_pad, w_stack, b_row)


def _inproj_kernel(x_ref, g_ref, sc_ref, sh_ref, w_ref, z_ref, h_scr):
    @pl.when(pl.program_id(1) == 0)
    def _():
        x = x_ref[...]
        y = x * lax.rsqrt(jnp.mean(x * x, axis=-1, keepdims=True) + EPS) * g_ref[...]
        h_scr[...] = (y * (1.0 + sc_ref[...]) + sh_ref[...]).astype(BF16)

    z_ref[...] = lax.dot_general(h_scr[...], w_ref[...], (((1,), (1,)), ((), ())),
                                 preferred_element_type=F32)


def in_proj(x2, gain_row, sc, sh, w_bf, layer, tm, tn):
    T, D = x2.shape
    N = w_bf.shape[1]
    G, R, _ = sc.shape
    tiles_per_group = (T // tm) // G
    mod_spec = pl.BlockSpec((None, R, D), lambda m, n: (m // tiles_per_group, 0, 0))
    return pl.pallas_call(
        _inproj_kernel,
        grid=(T // tm, N // tn),
        in_specs=[pl.BlockSpec((tm, D), lambda m, n: (m, 0)),
                  pl.BlockSpec((1, D), lambda m, n: (0, 0)),
                  mod_spec, mod_spec,
                  pl.BlockSpec((None, tn, D), lambda m, n: (layer, n, 0))],
        out_specs=pl.BlockSpec((tm, tn), lambda m, n: (m, n)),
        out_shape=jax.ShapeDtypeStruct((T, N), F32),
        scratch_shapes=[pltpu.VMEM((tm, D), BF16)],
        compiler_params=_cp("parallel", "arbitrary"),
        name="in_proj",
    )(x2, gain_row, sc, sh, w_bf)


def _rope(x, cos, sin_signed):
    return x * cos + pltpu.roll(x, DK_A // 2, 1) * sin_signed


def _ret_kernel(q_ref, k_ref, v_ref, g_ref, cos_ref, sin_ref, inner_ref, cross_ref, tail_ref, cd_ref,
                ya_ref, so_ref, s_scr, *, n_sub):
    lt = pl.program_id(1)

    @pl.when(lt == 0)
    def _():
        s_scr[...] = jnp.zeros_like(s_scr)

    C = RET_CHUNK
    heads = range(H_A)
    cols = [slice(h * DK_A, (h + 1) * DK_A) for h in heads]
    dot = lambda x, y: jnp.dot(x, y, preferred_element_type=F32)
    for c in range(n_sub):
        rows = slice(c * C, (c + 1) * C)
        cos = cos_ref[rows, :]
        sin = sin_ref[rows, :]
        qb, kb, vb, kt = [], [], [], []
        for h in heads:
            q = _rope(q_ref[rows, cols[h]], cos, sin)
            k = _rope(k_ref[rows, cols[h]], cos, sin) * (DK_A ** -0.5)
            qb.append(q.astype(BF16))
            kb.append(k.astype(BF16))
            vb.append(v_ref[rows, cols[h]].astype(BF16))
            kt.append((k * tail_ref[h]).T.astype(BF16))
        S = [s_scr[h] for h in heads]
        sc = [lax.dot_general(qb[h], kb[h], (((1,), (1,)), ((), ())), preferred_element_type=F32) * inner_ref[h]
              for h in heads]
        qs = [dot(qb[h], S[h].astype(BF16)) * cross_ref[h] for h in heads]
        kv = [dot(kt[h], vb[h]) for h in heads]
        o = [dot(sc[h].astype(BF16), vb[h]) + qs[h] for h in heads]
        for h in heads:
            s_scr[h] = S[h] * cd_ref[h] + kv[h]
            on = o[h] * lax.rsqrt(jnp.mean(o[h] * o[h], axis=-1, keepdims=True) + EPS)
            ya_ref[rows, cols[h]] = (on * _silu(g_ref[rows, cols[h]])).astype(ya_ref.dtype)

    @pl.when(lt == pl.num_programs(1) - 1)
    def _():
        so_ref[...] = s_scr[...]


def _ret_tables(L):
    C = math.gcd(L, RET_CHUNK)
    log_g = np.log1p(-np.exp2(-5.0 - np.arange(H_A, dtype=np.float32))).astype(np.float32)
    i = np.arange(C, dtype=np.float32)
    dist = i[:, None] - i[None, :]
    causal = dist >= 0
    inner = np.where(causal[None], np.exp(np.where(causal, dist, 0.0)[None] * log_g[:, None, None]), 0.0)
    cross = np.exp((i[None, :] + 1.0) * log_g[:, None])
    tail = np.exp((C - 1.0 - i)[None, :] * log_g[:, None])
    chunk = np.exp(C * log_g)
    bc = lambda t: np.broadcast_to(t[:, :, None], (H_A, C, 128)).astype(np.float32)
    cd = np.broadcast_to(chunk[:, None, None], (H_A, 1, 128)).astype(np.float32)
    return inner.astype(np.float32), bc(cross), bc(tail), cd


def _rope_tables(pos):
    half = DK_A // 2
    inv = ROPE_BASE ** (-jnp.arange(half, dtype=F32) / half)
    ang = pos.astype(F32)[:, None] * inv[None, :]
    cos, sin = jnp.cos(ang), jnp.sin(ang)
    return jnp.concatenate([cos, cos], axis=-1), jnp.concatenate([-sin, sin], axis=-1)


def retention_prompt(z3, cos_t, sin_t, tb):
    B, L, ZC = z3.shape
    inner, cross, tail, cd = _ret_tables(L)
    zspec = lambda idx: pl.BlockSpec((None, tb, W_AQ), lambda b, l: (b, l, idx))
    full3 = lambda s: pl.BlockSpec(s, lambda b, l: (0, 0, 0))
    return pl.pallas_call(
        functools.partial(_ret_kernel, n_sub=tb // RET_CHUNK),
        grid=(B, L // tb),
        in_specs=[zspec(0), zspec(1), zspec(2), zspec(3),
                  pl.BlockSpec((tb, 128), lambda b, l: (l, 0)),
                  pl.BlockSpec((tb, 128), lambda b, l: (l, 0)),
                  full3(inner.shape), full3(cross.shape), full3(tail.shape), full3(cd.shape)],
        out_specs=[pl.BlockSpec((None, tb, W_A), lambda b, l: (b, l, 0)),
                   pl.BlockSpec((None, H_A, DK_A, DV_A), lambda b, l: (b, 0, 0, 0))],
        out_shape=[jax.ShapeDtypeStruct((B, L, W_A), BF16),
                   jax.ShapeDtypeStruct((B, H_A, DK_A, DV_A), F32)],
        scratch_shapes=[pltpu.VMEM((H_A, DK_A, DV_A), F32)],
        compiler_params=_cp("parallel", "arbitrary"),
        name="retention_prompt",
    )(z3, z3, z3, z3, cos_t, sin_t, inner, cross, tail, cd)


def _ret_step_kernel(q_ref, k_ref, v_ref, g_ref, cos_ref, sin_ref, gam_ref, s_ref, stack_ref, ya_ref, so_ref,
                     *, bb):
    del stack_ref
    cos, sin = cos_ref[...], sin_ref[...]
    row = lax.broadcasted_iota(jnp.int32, (bb, 128), 0)
    for h in range(H_A):
        cols = slice(h * DK_A, (h + 1) * DK_A)
        gam = gam_ref[h]
        q = _rope(q_ref[:, cols], cos, sin)
        k = _rope(k_ref[:, cols], cos, sin) * (DK_A ** -0.5)
        v = v_ref[:, cols]
        qb = q.astype(BF16)
        qk = jnp.sum(qb.astype(F32) * k.astype(BF16).astype(F32), axis=-1, keepdims=True)
        qs = jnp.zeros((bb, 128), F32)
        vb = v.astype(BF16)
        for b in range(0, bb, 2):
            S0, S1 = s_ref[b, h], s_ref[b + 1, h]
            first, second = row == b, row == b + 1
            q_s = jnp.dot(qb, jnp.concatenate([S0, S1], axis=1).astype(BF16), preferred_element_type=F32)
            qs = jnp.where(first, q_s[:, :DV_A], jnp.where(second, q_s[:, DV_A:], qs))
            k_pair = jnp.where(jnp.logical_or(first, second), k, 0.0).T.astype(BF16)
            v_pair = jnp.concatenate([jnp.where(first, vb, jnp.zeros_like(vb)),
                                      jnp.where(second, vb, jnp.zeros_like(vb))], axis=1)
            kv = jnp.dot(k_pair, v_pair, preferred_element_type=F32)
            so_ref[b, h] = S0 * gam + kv[:, :DV_A]
            so_ref[b + 1, h] = S1 * gam + kv[:, DV_A:]
        o = qk * v.astype(BF16).astype(F32) + qs * gam
        on = o * lax.rsqrt(jnp.mean(o * o, axis=-1, keepdims=True) + EPS)
        ya_ref[:, cols] = on * _silu(g_ref[:, cols])


def retention_step(z2, state, layer, pos, new_state):
    B = z2.shape[0]
    bb = 8
    cos_t, sin_t = _rope_tables(jnp.full((1,), pos))
    log_g = np.log1p(-np.exp2(-5.0 - np.arange(H_A, dtype=np.float32))).astype(np.float32)
    gam = np.broadcast_to(np.exp(log_g)[:, None, None], (H_A, 1, 128)).astype(np.float32)
    zspec = lambda idx: pl.BlockSpec((bb, W_AQ), lambda i: (i, idx))
    return pl.pallas_call(
        functools.partial(_ret_step_kernel, bb=bb),
        grid=(B // bb,),
        in_specs=[zspec(0), zspec(1), zspec(2), zspec(3),
                  pl.BlockSpec((1, 128), lambda i: (0, 0)),
                  pl.BlockSpec((1, 128), lambda i: (0, 0)),
                  pl.BlockSpec((H_A, 1, 128), lambda i: (0, 0, 0)),
                  pl.BlockSpec((None, bb, H_A, DK_A, DV_A), lambda i: (layer, i, 0, 0, 0)),
                  pl.BlockSpec(memory_space=pl.ANY)],
        out_specs=[pl.BlockSpec((bb, W_A), lambda i: (i, 0)),
                   pl.BlockSpec((None, bb, H_A, DK_A, DV_A), lambda i: (layer, i, 0, 0, 0))],
        out_shape=[jax.ShapeDtypeStruct((B, W_A), F32),
                   jax.ShapeDtypeStruct(new_state.shape, F32)],
        input_output_aliases={8: 1},
        compiler_params=_cp("parallel"),
        name="retention_step",
    )(z2, z2, z2, z2, cos_t, sin_t, gam, state, new_state)


def _qknorm_kernel(q_ref, k_ref, v_ref, gq_ref, gk_ref, ones_ref, k_stack_ref, v_stack_ref,
                   qb_ref, k32_ref, kb_ref, v32_ref, vb_ref, *, q_scale):
    del k_stack_ref, v_stack_ref
    ones2 = ones_ref[...]
    q, k, v = q_ref[...], k_ref[...], v_ref[...]
    tm = q.shape[0]
    qn = q * lax.rsqrt(_seg_sum_wide(q * q, ones2) * (1.0 / DH_B) + EPS) * gq_ref[...] * q_scale
    kn = k * lax.rsqrt(_seg_sum_wide(k * k, ones2) * (1.0 / DH_B) + EPS) * gk_ref[...]
    qb_ref[...] = qn.astype(BF16)
    kb_ref[...] = kn.astype(BF16)
    vb_ref[...] = v.astype(BF16)
    for h in range(H_B):
        head_rows = pl.ds(h, tm, stride=H_B)
        k32_ref[head_rows, :] = kn[:, h * DV_B:(h + 1) * DV_B]
        v32_ref[head_rows, :] = v[:, h * DV_B:(h + 1) * DV_B]


def _block_ones(group):
    i = np.arange(128)
    return jnp.asarray((i[:, None] // group == i[None, :] // group).astype(np.float32), BF16)


def qk_norm(z2, gq_row, gk_row, tm, q_scale, layer, k_stack, v_stack):
    T = z2.shape[0]
    base = (Z_GATE + 3 * 1024) // W_BQK
    zspec = lambda idx: pl.BlockSpec((tm, W_BQK), lambda m: (m, base + idx))
    row = pl.BlockSpec((1, W_BQK), lambda m: (0, 0))
    out = pl.BlockSpec((tm, W_BQK), lambda m: (m, 0))
    out_rows = pl.BlockSpec((None, tm * H_B, DV_B), lambda m: (layer, m, 0))
    stack = pl.BlockSpec(memory_space=pl.ANY)
    return pl.pallas_call(
        functools.partial(_qknorm_kernel, q_scale=q_scale),
        grid=(T // tm,),
        in_specs=[zspec(0), zspec(1), zspec(2), row, row, pl.BlockSpec((128, 128), lambda m: (0, 0)),
                  stack, stack],
        out_specs=[out, out_rows, out, out_rows, out],
        out_shape=[jax.ShapeDtypeStruct((T, W_BQK), BF16), jax.ShapeDtypeStruct(k_stack.shape, F32),
                   jax.ShapeDtypeStruct((T, W_BQK), BF16), jax.ShapeDtypeStruct(v_stack.shape, F32),
                   jax.ShapeDtypeStruct((T, W_B), BF16)],
        input_output_aliases={6: 1, 7: 3},
        compiler_params=_cp("parallel"),
        name="qk_norm",
    )(z2, z2, z2, gq_row, gk_row, _block_ones(DH_B), k_stack, v_stack)


def _bucket_np(n):
    max_exact = N_BUCKETS // 2
    nf = np.maximum(n, 1).astype(np.float32)
    large = max_exact + (np.log(nf / np.float32(max_exact)) / np.float32(math.log(MAX_DISTANCE / max_exact))
                         * np.float32(N_BUCKETS - max_exact)).astype(np.int32)
    large = np.minimum(large, N_BUCKETS - 1)
    return np.where(n < max_exact, n, large).astype(np.int32)


def _bias_kernel(tab_ref, bkt_ref, o_ref, *, log2_far_shift):
    h = pl.program_id(0)
    bk = bkt_ref[...]
    acc = jnp.zeros(bk.shape, F32)
    for b in range(N_BUCKETS):
        acc = jnp.where(bk == b, tab_ref[b, h], acc)
    if log2_far_shift:
        acc = (acc - tab_ref[N_BUCKETS - 1, h]) * math.log2(math.e)
    o_ref[...] = jnp.where(bk < 0, NEG, acc)


def bias_tiles(rel_bias, buckets, log2_far_shift):
    R, C = buckets.shape
    tr = min(R, 512)
    return pl.pallas_call(
        functools.partial(_bias_kernel, log2_far_shift=log2_far_shift),
        grid=(H_B, R // tr),
        in_specs=[pl.BlockSpec(memory_space=pltpu.SMEM),
                  pl.BlockSpec((tr, C), lambda h, r: (r, 0))],
        out_specs=pl.BlockSpec((None, tr, C), lambda h, r: (h, r, 0)),
        out_shape=jax.ShapeDtypeStruct((H_B, R, C), F32),
        compiler_params=_cp("parallel", "parallel"),
        name="bias_tiles",
    )(rel_bias, jnp.asarray(buckets))


def _prompt_buckets(t):
    r = np.arange(t)[:, None]
    c = np.arange(t)[None, :]
    diag = np.where(c <= r, _bucket_np(np.maximum(r - c, 0)), -1)
    off1 = _bucket_np(t + r - c)
    assert t >= MAX_DISTANCE
    return np.concatenate([diag, off1], axis=0).astype(np.int32)


def _lambda_full(lam_ref, lam_init):
    lv = lam_ref[...]
    s1 = jnp.sum(lv[0:1] * lv[1:2], axis=-1, keepdims=True)
    s2 = jnp.sum(lv[2:3] * lv[3:4], axis=-1, keepdims=True)
    return jnp.exp(s1) - jnp.exp(s2) + lam_init


def _flash_kernel(q_ref, k_ref, v_ref, bias_ref, lam_ref, gain_ref, o_ref, *scratch, lam_init, heads):
    for hh in range(heads):
        cols = pl.ds(hh * DV_B, DV_B)
        _flash_head(q_ref.at[:, cols], k_ref.at[:, cols], v_ref.at[:, cols], bias_ref.at[hh], lam_ref, gain_ref,
                    o_ref.at[:, cols], *scratch, lam_init=lam_init)


def _flash_head(q_ref, k_ref, v_ref, bias_ref, lam_ref, gain_ref, o_ref, qq_scr, m_scr, acc_scr, s_scr,
                *, lam_init):
    i = pl.program_id(2)
    t = q_ref.shape[0]

    q = q_ref[...]
    first = lax.broadcasted_iota(jnp.int32, q.shape, 1) < DH_B
    qq_scr[:t] = jnp.where(first, q, jnp.zeros_like(q))
    qq_scr[t:] = jnp.where(first, jnp.zeros_like(q), q)
    m_scr[...] = jnp.full(m_scr.shape, NEG, F32)
    acc_scr[...] = jnp.zeros_like(acc_scr)

    def raw_scores(comp, j):
        krows = pl.ds(pl.multiple_of(j * t, t), t)
        return lax.dot_general(qq_scr[comp * t:(comp + 1) * t], k_ref[krows, :], (((1,), (1,)), ((), ())),
                               preferred_element_type=F32)

    def accumulate(comp, s, v1):
        rows = slice(comp * t, (comp + 1) * t)
        m_old = m_scr[rows]
        m_new = jnp.maximum(m_old, jnp.max(s, axis=-1, keepdims=True))
        alpha = jnp.exp2(m_old - m_new)
        p = jnp.concatenate([jnp.exp2(s[:, c * 128:(c + 1) * 128] - m_new) for c in range(t // 128)],
                            axis=1).astype(BF16)
        acc_scr[rows] = (jnp.concatenate([alpha, alpha], axis=1) * acc_scr[rows]
                         + jnp.dot(p, v1, preferred_element_type=F32))
        m_scr[rows] = m_new

    def kv_block(j, tile, has_next):
        krows = pl.ds(pl.multiple_of(j * t, t), t)
        v1 = jnp.concatenate([v_ref[krows, :], jnp.ones((t, DV_B), BF16)], axis=1)
        with_bias = (lambda s: s) if tile is None else (lambda s: s + bias_ref[tile * t:(tile + 1) * t, :])
        s_first = s_scr[...]
        s_second = raw_scores(1, j)
        accumulate(0, with_bias(s_first), v1)
        if has_next:
            s_scr[...] = raw_scores(0, j + 1)
        accumulate(1, with_bias(s_second), v1)

    s_scr[...] = raw_scores(0, 0)

    def far_pair(jj, carry):
        kv_block(2 * jj, None, True)
        kv_block(2 * jj + 1, None, True)
        return carry

    n_far = jnp.maximum(i - 1, 0)
    lax.fori_loop(0, n_far // 2, far_pair, 0)

    @pl.when(jnp.logical_and(i >= 1, n_far % 2 == 1))
    def _():
        kv_block(i - 2, None, True)
        kv_block(i - 1, 1, True)
        kv_block(i, 0, False)

    @pl.when(jnp.logical_and(i >= 1, n_far % 2 == 0))
    def _():
        kv_block(i - 1, 1, True)
        kv_block(i, 0, False)

    @pl.when(i == 0)
    def _():
        kv_block(i, 0, False)

    lam = _lambda_full(lam_ref, lam_init)
    acc = acc_scr[...]
    ob = acc[:t, :DV_B] / acc[:t, DV_B:] - lam * (acc[t:, :DV_B] / acc[t:, DV_B:])
    on = ob * lax.rsqrt(jnp.mean(ob * ob, axis=-1, keepdims=True) + EPS) * gain_ref[...]
    o_ref[...] = (on * (1.0 - lam_init)).astype(o_ref.dtype)


def diff_attention_prompt(qb, kb, vb, bias, lam_rows, gain_row, lam_init, t):
    B, L, _ = qb.shape
    n = L // t
    heads = H_B
    kv_spec = pl.BlockSpec((None, L, heads * DV_B), lambda b, h, i: (b, 0, h))
    return pl.pallas_call(
        functools.partial(_flash_kernel, lam_init=lam_init, heads=heads),
        grid=(B, H_B // heads, n),
        in_specs=[pl.BlockSpec((None, t, heads * DV_B), lambda b, h, i: (b, i, h)),
                  kv_spec, kv_spec,
                  pl.BlockSpec((heads, 2 * t, t), lambda b, h, i: (h, 0, 0)),
                  pl.BlockSpec((4, DH_B), lambda b, h, i: (0, 0)),
                  pl.BlockSpec((1, DV_B), lambda b, h, i: (0, 0))],
        out_specs=pl.BlockSpec((None, t, heads * DV_B), lambda b, h, i: (b, i, h)),
        out_shape=jax.ShapeDtypeStruct((B, L, W_B), BF16),
        scratch_shapes=[pltpu.VMEM((2 * t, DV_B), BF16), pltpu.VMEM((2 * t, 128), F32),
                        pltpu.VMEM((2 * t, 2 * DV_B), F32), pltpu.VMEM((t, t), F32)],
        compiler_params=_cp("parallel", "parallel", "arbitrary"),
        name="diff_attention_prompt",
    )(qb, kb, vb, bias, lam_rows, gain_row)


def _decode_kernel(pt_ref, q_ref, kn_ref, vn_ref, bias_ref, lam_ref, gain_ref, *rest, n_pages, lam_init, spb):
    n = spb * n_pages
    k_all, v_all, o_ref = rest[:n], rest[n:2 * n], rest[2 * n]
    lam = _lambda_full(lam_ref, lam_init)
    gain = gain_ref[...]
    r8 = lax.broadcasted_iota(jnp.int32, (8, 2 * DH_B), 0)
    l8 = lax.broadcasted_iota(jnp.int32, (8, 2 * DH_B), 1)
    comp_rows = (l8 // DH_B) == r8
    row0 = lax.broadcasted_iota(jnp.int32, (PAGE_SIZE, DV_B), 0) == 0
    nt = lambda x, y: lax.dot_general(x, y, (((1,), (1,)), ((), ())), preferred_element_type=F32)
    units = [(s, h) for s in range(spb) for h in range(H_B)]
    cols = [slice(h * DV_B, (h + 1) * DV_B) for h in range(H_B)]
    head_rows = [pl.ds(h, PAGE_SIZE, stride=H_B) for h in range(H_B)]
    pages = lambda refs, s: refs[s * n_pages:(s + 1) * n_pages]
    ss = []
    for s, h in units:
        qrows = jnp.where(comp_rows, q_ref[s, :, cols[h]].astype(F32), 0.0).astype(BF16)
        ks = [r[head_rows[h], :].astype(BF16) for r in pages(k_all, s)]
        ks.append(jnp.where(row0, kn_ref[s, h:h + 1, :], 0.0).astype(BF16))
        sc = []
        for p in range(0, len(ks) - 1, 2):
            s2 = nt(qrows, jnp.concatenate([ks[p], ks[p + 1]], axis=0))
            sc += [s2[:, :PAGE_SIZE] + bias_ref[h, p], s2[:, PAGE_SIZE:] + bias_ref[h, p + 1]]
        if len(ks) % 2:
            sc.append(nt(qrows, ks[-1]) + bias_ref[h, len(ks) - 1])
        ss.append(sc)
    ps, ls = [], []
    for sc in ss:
        m = functools.reduce(jnp.maximum, [jnp.max(x, axis=-1, keepdims=True) for x in sc])
        p = [jnp.exp(x - m) for x in sc]
        ls.append(functools.reduce(jnp.add, [jnp.sum(x, axis=-1, keepdims=True) for x in p]))
        ps.append([x.astype(BF16) for x in p])
    for (s, h), p_u, l_u in zip(units, ps, ls):
        vs = [r[head_rows[h], :].astype(BF16) for r in pages(v_all, s)]
        vs.append(jnp.where(row0, vn_ref[s, h:h + 1, :], 0.0).astype(BF16))
        parts = [jnp.dot(jnp.concatenate([p_u[p], p_u[p + 1]], axis=1),
                         jnp.concatenate([vs[p], vs[p + 1]], axis=0), preferred_element_type=F32)
                 for p in range(0, len(vs) - 1, 2)]
        if len(vs) % 2:
            parts.append(jnp.dot(p_u[-1], vs[-1], preferred_element_type=F32))
        acc = functools.reduce(jnp.add, parts)
        outn = acc / l_u
        oh = outn[0:1] - lam * outn[1:2]
        on = oh * lax.rsqrt(jnp.mean(oh * oh, axis=-1, keepdims=True) + EPS) * gain
        o_ref[s, :, cols[h]] = on * (1.0 - lam_init)


def diff_attention_decode(qb, k32, v32, cache_k, cache_v, pt_flat, bias_dec, lam_rows, gain_row, layer,
                          lam_init, n_pages):
    B = qb.shape[0]
    spb = 2 if B % 2 == 0 else 1
    rowspec = pl.BlockSpec((spb, 1, W_BQK), lambda g, pt: (g, 0, 0))
    headspec = pl.BlockSpec((spb, H_B, DV_B), lambda g, pt: (g, 0, 0))
    const = lambda s: pl.BlockSpec(s, lambda g, pt: tuple(0 for _ in s))
    page_spec = lambda s, p: pl.BlockSpec(
        (None, None, PAGE_SIZE * H_B, DV_B),
        lambda g, pt, s=s, p=p: (layer, pt[(g * spb + s) * n_pages + p], 0, 0))
    page_specs = [page_spec(s, p) for s in range(spb) for p in range(n_pages)]
    grid_spec = pltpu.PrefetchScalarGridSpec(
        num_scalar_prefetch=1,
        grid=(B // spb,),
        in_specs=[rowspec, headspec, headspec, const(bias_dec.shape), const((4, DH_B)), const((1, DV_B))]
                 + page_specs * 2,
        out_specs=pl.BlockSpec((spb, 1, W_B), lambda g, pt: (g, 0, 0)),
    )
    return pl.pallas_call(
        functools.partial(_decode_kernel, n_pages=n_pages, lam_init=lam_init, spb=spb),
        grid_spec=grid_spec,
        out_shape=jax.ShapeDtypeStruct((B, 1, W_B), F32),
        compiler_params=_cp("arbitrary"),
        name="diff_attention_decode",
    )(pt_flat, qb, k32, v32, bias_dec, lam_rows, gain_row,
      *([cache_k] * (spb * n_pages)), *([cache_v] * (spb * n_pages)))


def _rwkv_prep_kernel(*refs, carry_shift, gate_v):
    it = iter(refs)
    z_ref, prev_ref, mu_ref, w0_ref, a0_ref, kk_ref, ka_ref, wlr_ref, ones_ref = (next(it) for _ in range(9))
    if gate_v:
        v0_ref, wvd_ref, wvu_ref, vf_ref = (next(it) for _ in range(4))
    r_ref, lw_ref, k_ref, v_ref, a_ref, b_ref, gg_ref = (next(it) for _ in range(7))
    z = z_ref[...]
    if carry_shift:
        carry_scr = next(it)

        @pl.when(pl.program_id(1) == 0)
        def _():
            carry_scr[...] = prev_ref[...]

        first = lax.broadcasted_iota(jnp.int32, z.shape, 0) == 0
        zprev = jnp.where(first, carry_scr[...], pltpu.roll(z, 1, 0))
        carry_scr[...] = z_ref[z.shape[0] - 1:z.shape[0], :]
    else:
        zprev = prev_ref[...]
    zs = z + (zprev - z) * mu_ref[...]
    rc, kc, vc = zs[:, :W_C], zs[:, W_C:2 * W_C], zs[:, 2 * W_C:3 * W_C]
    lr = zs[:, 3 * W_C:]
    head = lr[:, :R_DECAY + R_A]
    col = lax.broadcasted_iota(jnp.int32, head.shape, 1)
    act = jnp.concatenate([jnp.where(col < R_DECAY, jnp.tanh(head), head),
                           jax.nn.sigmoid(lr[:, R_DECAY + R_A:])], axis=1)
    up = jnp.dot(act.astype(BF16), wlr_ref[...], preferred_element_type=F32)
    y = -(w0_ref[...] + up[:, :W_C])
    softplus = jnp.maximum(y, 0.0) + jnp.log(1.0 + jnp.exp(-jnp.abs(y)))
    lw_ref[...] = -jnp.exp(-softplus - 0.5)
    if gate_v:
        down = jnp.dot(vc.astype(BF16), wvd_ref[...], preferred_element_type=F32)
        vgate = jax.nn.sigmoid(v0_ref[...] + jnp.dot(down.astype(BF16), wvu_ref[...], preferred_element_type=F32))
        vc = vc + (vf_ref[...].astype(F32) - vc) * vgate
    a = jax.nn.sigmoid(a0_ref[...] + up[:, W_C:2 * W_C])
    gg_ref[...] = up[:, 2 * W_C:].astype(gg_ref.dtype)
    kk = kc * kk_ref[...]
    norm = jnp.sqrt(_seg_sum_wide(kk * kk, ones_ref[...]))
    kk = kk / jnp.maximum(norm, 1e-12)
    r_ref[...] = rc.astype(r_ref.dtype)
    k_ref[...] = (kc * (1.0 + (a - 1.0) * ka_ref[...])).astype(k_ref.dtype)
    v_ref[...] = vc.astype(v_ref.dtype)
    a_ref[...] = (-kk).astype(a_ref.dtype)
    b_ref[...] = (kk * a).astype(b_ref.dtype)


def rwkv_prep(z3, prev, mu_row, w0, a0, k_k, k_a, w_lr, vgate, tm, carry_shift):
    G, Lg, _ = z3.shape
    blk = lambda w, idx: pl.BlockSpec((None, tm, w), lambda g, l: (g, l, idx))
    row = lambda w: pl.BlockSpec((1, w), lambda g, l: (0, 0))
    prev_spec = (pl.BlockSpec((None, 1, RWKV_PAD), lambda g, l: (g, 0, 0)) if carry_shift
                 else blk(RWKV_PAD, 0))
    in_specs = [blk(RWKV_PAD, Z_RWKV // RWKV_PAD), prev_spec, row(RWKV_PAD), row(W_C), row(W_C), row(W_C),
                row(W_C), pl.BlockSpec(w_lr.shape, lambda g, l: (0, 0)),
                pl.BlockSpec((128, 128), lambda g, l: (0, 0))]
    args = [z3, prev, mu_row, w0, a0, k_k, k_a, w_lr, _block_ones(DH_C)]
    if vgate is not None:
        v0, wvd, wvu, vfirst = vgate
        in_specs += [row(W_C), pl.BlockSpec(wvd.shape, lambda g, l: (0, 0)),
                     pl.BlockSpec(wvu.shape, lambda g, l: (0, 0)), blk(W_C, 0)]
        args += [v0, wvd, wvu, vfirst]
    return pl.pallas_call(
        functools.partial(_rwkv_prep_kernel, carry_shift=carry_shift, gate_v=vgate is not None),
        grid=(G, Lg // tm),
        in_specs=in_specs,
        out_specs=[blk(W_C, 0)] * 7,
        out_shape=[jax.ShapeDtypeStruct((G, Lg, W_C), F32 if i == 1 else BF16) for i in range(7)],
        scratch_shapes=[pltpu.VMEM((1, RWKV_PAD), F32)] if carry_shift else [],
        compiler_params=_cp("parallel", "arbitrary"),
        name="rwkv_prep",
    )(*args)


def _rwkv_chunk_kernel(r_ref, lw_ref, k_ref, v_ref, a_ref, b_ref, tri_ref, msl_ref, mli_ref, lvl_ref,
                       y_ref, so_ref, s_scr, *, n_chunks, nb):
    C = RW_CHUNK

    @pl.when(pl.program_id(1) == 0)
    def _():
        s_scr[...] = jnp.zeros_like(s_scr)

    head0 = lax.broadcasted_iota(jnp.int32, (C, 128), 1) < DH_C
    tri3, msl, mli = tri_ref[...], msl_ref[...], mli_ref[...]
    ri = lax.broadcasted_iota(jnp.int32, (128, 128), 0)
    ci = lax.broadcasted_iota(jnp.int32, (128, 128), 1)
    eye = (ri == ci).astype(F32)
    n_levels = lvl_ref.shape[0]
    chains = [(bi, p) for bi in range(nb) for p in range(H_C // 2)]

    def stack(x):
        return jnp.concatenate([jnp.where(head0, x, 0.0), jnp.where(head0, 0.0, x)], axis=0)

    def nt(x, y):
        return lax.dot_general(x, y, (((1,), (1,)), ((), ())), preferred_element_type=F32)

    def dot(x, y):
        return jnp.dot(x, y, preferred_element_type=F32)

    def chunk(ci_, carry):
        rows = pl.ds(pl.multiple_of(ci_ * C, C), C)
        ld = lambda ref, bi, p: ref[bi, rows, p * 128:(p + 1) * 128].astype(F32)
        lams = []
        for bi, p in chains:
            lw = ld(lw_ref, bi, p)
            hi = lw.astype(BF16)
            mid = (lw - hi.astype(F32)).astype(BF16)
            lo = (lw - hi.astype(F32) - mid.astype(F32)).astype(BF16)
            lams.append(dot(tri3, jnp.concatenate([hi, mid, lo], axis=0)))
        AR, BK, Vs, KB, dec = [], [], [], [], []
        for (bi, p), lam in zip(chains, lams):
            r, lw, k = ld(r_ref, bi, p), ld(lw_ref, bi, p), ld(k_ref, bi, p)
            v, a, b = ld(v_ref, bi, p), ld(a_ref, bi, p), ld(b_ref, bi, p)
            lam_c = lam[C - 1:C, :]
            e_neg = jnp.exp(-lam)
            e_tail = jnp.exp(lam_c - lam)
            AR.append(jnp.concatenate([stack(a * jnp.exp(lam - lw)), stack(r * jnp.exp(lam))],
                                      axis=0).astype(BF16))
            BK.append(jnp.concatenate([stack(b * e_neg), stack(k * e_neg)], axis=0).astype(BF16))
            KB.append(jnp.concatenate([stack(k * e_tail), stack(b * e_tail)], axis=0).astype(BF16))
            Vs.append(stack(v))
            dec.append(jnp.exp(lam_c))
        G = [nt(x, y) for x, y in zip(AR, BK)]
        I0 = [nt(x, s_scr[bi, p].astype(BF16)) for x, (bi, p) in zip(AR, chains)]
        rhs = [i0[:128] + dot((g[:128, 128:] * msl).astype(BF16), v.astype(BF16))
               for g, i0, v in zip(G, I0, Vs)]
        Nh, Nl, D = [], [], []
        for g in G:
            h, l = _split2(g[:128, :128] * msl)
            Nh.append(h)
            Nl.append(l)
            D.append(eye + (h * lvl_ref[0]).astype(F32) + (l * lvl_ref[0]).astype(F32))
        for lv in range(1, n_levels):
            m = lvl_ref[lv]
            Db = [d.astype(BF16) for d in D]
            X = [dot(h * m, db) for h, db in zip(Nh, Db)]
            D = [d + dot(db, x.astype(BF16)) for d, db, x in zip(D, Db, X)]
        Us = [dot(d.astype(BF16), x.astype(BF16)) for d, x in zip(D, rhs)]
        for (bi, p), g, i0, u, v in zip(chains, G, I0, Us, Vs):
            ys = i0[128:] + dot(jnp.concatenate([g[128:, :128] * mli, g[128:, 128:] * mli], axis=1).astype(BF16),
                                jnp.concatenate([u, v], axis=0).astype(BF16))
            y_ref[bi, rows, p * 128:(p + 1) * 128] = ys[:C] + ys[C:]
        for (bi, p), u, v, kb, d in zip(chains, Us, Vs, KB, dec):
            vu = jnp.concatenate([v, u], axis=0)
            s_scr[bi, p] = s_scr[bi, p] * d + dot(vu.T.astype(BF16), kb)
        return carry

    lax.fori_loop(0, n_chunks, chunk, 0)

    @pl.when(pl.program_id(1) == pl.num_programs(1) - 1)
    def _():
        so_ref[...] = s_scr[...]


def _rwkv_chunk_tables():
    C = RW_CHUNK
    t = np.arange(C)
    tri = (t[:, None] >= t[None, :]).astype(np.float32)
    tri3 = np.concatenate([tri, tri, tri], axis=1)
    i = np.arange(2 * C)
    same = (i[:, None] // C) == (i[None, :] // C)
    msl = (same & (i[:, None] > i[None, :])).astype(np.float32)
    mli = (same & (i[:, None] >= i[None, :])).astype(np.float32)
    levels = []
    n = 1
    while n < C:
        levels.append(same & ((i[:, None] // (2 * n)) == (i[None, :] // (2 * n)))
                      & ((i[:, None] // n) % 2 == 1) & ((i[None, :] // n) % 2 == 0))
        n *= 2
    return jnp.asarray(tri3, BF16), msl, mli, jnp.asarray(np.stack(levels).astype(np.float32), BF16)


def rwkv_chunk_scan(r, lw, k, v, a, b, tb, nb):
    B, L, _ = r.shape
    tri3, msl, mli, lvl = _rwkv_chunk_tables()
    blk = pl.BlockSpec((nb, tb, W_C), lambda bi, l: (bi, l, 0))
    c2 = lambda s: pl.BlockSpec(s, lambda bi, l: tuple(0 for _ in s))
    return pl.pallas_call(
        functools.partial(_rwkv_chunk_kernel, n_chunks=tb // RW_CHUNK, nb=nb),
        grid=(B // nb, L // tb),
        in_specs=[blk] * 6 + [c2(tri3.shape), c2(msl.shape), c2(mli.shape), c2(lvl.shape)],
        out_specs=[blk, pl.BlockSpec((nb, H_C // 2, 128, 128), lambda bi, l: (bi, 0, 0, 0))],
        out_shape=[jax.ShapeDtypeStruct((B, L, W_C), F32),
                   jax.ShapeDtypeStruct((B, H_C // 2, 128, 128), F32)],
        scratch_shapes=[pltpu.VMEM((nb, H_C // 2, 128, 128), F32)],
        compiler_params=_cp("parallel", "arbitrary"),
        name="rwkv_chunk_scan",
    )(r, lw, k, v, a, b, tri3, msl, mli, lvl)


def _rwkv_step_kernel(s_ref, w_ref, a_ref, b_ref, k_ref, r_ref, v_ref, stack_ref, y_ref, so_ref, vt_scr, y_scr):
    del stack_ref
    tr = lambda ref: ref[...].astype(F32).T
    wT, aT, bT, kT, rT = jnp.exp(tr(w_ref)), tr(a_ref), tr(b_ref), tr(k_ref), tr(r_ref)
    vt_scr[...] = tr(v_ref)
    for hh in range(2):
        ch = slice(hh * DH_C, (hh + 1) * DH_C)
        w, a, b, k, r = wT[ch], aT[ch], bT[ch], kT[ch], rT[ch]

        def value_row(i, carry):
            row = pl.ds(hh * DH_C + i, 1)
            S = s_ref[hh, i]
            sa = jnp.sum(S * a, axis=0, keepdims=True)
            Sn = S * w + sa * b + vt_scr[row, :] * k
            so_ref[hh, i] = Sn
            y_scr[row, :] = jnp.sum(Sn * r, axis=0, keepdims=True)
            return carry

        lax.fori_loop(0, DH_C, value_row, 0)
    y_ref[...] = y_scr[...].T


def rwkv_step(state_t, layer, lw, a, b, k, r, v, new_state_t):
    B = lw.shape[0]
    vec = pl.BlockSpec((B, 2 * DH_C), lambda p: (0, p))
    sspec = pl.BlockSpec((None, 2, DH_C, DH_C, B), lambda p: (layer, p, 0, 0, 0))
    return pl.pallas_call(
        _rwkv_step_kernel,
        grid=(H_C // 2,),
        in_specs=[sspec] + [vec] * 6 + [pl.BlockSpec(memory_space=pl.ANY)],
        out_specs=[vec, sspec],
        out_shape=[jax.ShapeDtypeStruct((B, W_C), F32),
                   jax.ShapeDtypeStruct(new_state_t.shape, F32)],
        scratch_shapes=[pltpu.VMEM((2 * DH_C, B), F32), pltpu.VMEM((2 * DH_C, B), F32)],
        input_output_aliases={7: 1},
        compiler_params=_cp("parallel"),
        name="rwkv_step",
    )(state_t, lw, a, b, k, r, v, new_state_t)


def _merge_kernel(x_ref, ya_ref, yb_ref, y_ref, r_ref, k_ref, v_ref, gg_ref, za_ref, zb_ref, zc_ref, gt_ref,
                  lng_ref, lnb_ref, rk_ref, ones_ref, wa_ref, wb_ref, wc_ref, wo_ref, o_ref):
    ones2 = ones_ref[...]
    y = y_ref[...]
    mu = _seg_sum_wide(y, ones2) * (1.0 / DH_C)
    d = y - mu
    var = _seg_sum_wide(d * d, ones2) * (1.0 / DH_C)
    ycn = d * lax.rsqrt(var + LNX_EPS) * lng_ref[...] + lnb_ref[...]
    f32 = lambda ref: ref[...].astype(F32)
    bonus = _seg_sum_wide(f32(r_ref) * f32(k_ref) * rk_ref[...], ones2) * f32(v_ref)
    yc = (ycn + bonus) * f32(gg_ref)
    proj = lambda t, w: jnp.dot(t.astype(BF16), w[...], preferred_element_type=F32)
    merged = (jax.nn.sigmoid(za_ref[...]) * proj(ya_ref[...], wa_ref)
              + jax.nn.sigmoid(zb_ref[...]) * proj(yb_ref[...], wb_ref)
              + jax.nn.sigmoid(zc_ref[...]) * proj(yc, wc_ref))
    o_ref[...] = x_ref[...] + gt_ref[...] * proj(merged, wo_ref)


def merge_out(x2, ya, yb, y, r, k, v, gg, z2, gt, lng, lnb, rk, wa, wb, wc, wo, tm):
    T, D = x2.shape
    G, R, _ = gt.shape
    tiles_per_group = (T // tm) // G
    tok = lambda w: pl.BlockSpec((tm, w), lambda m: (m, 0))
    gate = lambda idx: pl.BlockSpec((tm, D), lambda m: (m, Z_GATE // D + idx))
    row = lambda w: pl.BlockSpec((1, w), lambda m: (0, 0))
    full = lambda a: pl.BlockSpec(a.shape, lambda m: (0, 0))
    return pl.pallas_call(
        _merge_kernel,
        grid=(T // tm,),
        in_specs=[tok(D)] + [tok(W_C)] * 7 + [gate(0), gate(1), gate(2),
                  pl.BlockSpec((None, R, D), lambda m: (m // tiles_per_group, 0, 0)),
                  row(W_C), row(W_C), row(W_C), pl.BlockSpec((128, 128), lambda m: (0, 0)),
                  full(wa), full(wb), full(wc), full(wo)],
        out_specs=tok(D),
        out_shape=jax.ShapeDtypeStruct((T, D), F32),
        compiler_params=_cp("parallel"),
        name="merge_out",
    )(x2, ya, yb, y, r, k, v, gg, z2, z2, z2, gt, lng, lnb, rk, _block_ones(DH_C), wa, wb, wc, wo)


def _ffn_kernel(x_ref, g_ref, sc_ref, sh_ref, gt_ref, w1_ref, w2_ref, o_ref, h_scr, acc_scr):
    f = pl.program_id(1)

    @pl.when(f == 0)
    def _():
        x = x_ref[...]
        y = x * lax.rsqrt(jnp.mean(x * x, axis=-1, keepdims=True) + EPS) * g_ref[...]
        h_scr[...] = (y * (1.0 + sc_ref[...]) + sh_ref[...]).astype(BF16)
        acc_scr[...] = jnp.zeros_like(acc_scr)

    u = jnp.maximum(jnp.dot(h_scr[...], w1_ref[...], preferred_element_type=F32), 0.0)
    acc_scr[...] += jnp.dot((u * u).astype(BF16), w2_ref[...], preferred_element_type=F32)

    @pl.when(f == pl.num_programs(1) - 1)
    def _():
        o_ref[...] = x_ref[...] + gt_ref[...] * acc_scr[...]


def ffn(x2, gain_row, sc, sh, gt, w1, w2, tm, tf):
    T, D = x2.shape
    F = w1.shape[1]
    G, R, _ = sc.shape
    tiles_per_group = (T // tm) // G
    mod_spec = pl.BlockSpec((None, R, D), lambda m, f: (m // tiles_per_group, 0, 0))
    return pl.pallas_call(
        _ffn_kernel,
        grid=(T // tm, F // tf),
        in_specs=[pl.BlockSpec((tm, D), lambda m, f: (m, 0)),
                  pl.BlockSpec((1, D), lambda m, f: (0, 0)),
                  mod_spec, mod_spec, mod_spec,
                  pl.BlockSpec((D, tf), lambda m, f: (0, f)),
                  pl.BlockSpec((tf, D), lambda m, f: (f, 0))],
        out_specs=pl.BlockSpec((tm, D), lambda m, f: (m, 0)),
        out_shape=jax.ShapeDtypeStruct((T, D), F32),
        scratch_shapes=[pltpu.VMEM((tm, D), BF16), pltpu.VMEM((tm, D), F32)],
        compiler_params=_cp("parallel", "arbitrary"),
        name="ffn",
    )(x2, gain_row, sc, sh, gt, w1, w2)


def prep_w_in(w_in):
    depth, D, _ = w_in.shape
    wt = jnp.transpose(w_in, (0, 2, 1))
    o_diff, o_rwkv, o_gate = RET_COLS, RET_COLS + DIFF_COLS, RET_COLS + DIFF_COLS + RWKV_COLS
    return jnp.concatenate([
        wt[:, :RET_COLS], wt[:, o_rwkv:o_gate], jnp.zeros((depth, RWKV_PAD - RWKV_COLS, D), F32),
        wt[:, o_gate:], wt[:, o_diff:o_rwkv]], axis=1).astype(BF16)


def _prep_layer(p, l, w_in_p):
    place = lambda w, slot: jnp.pad(w, ((0, 0), (slot * W_C, (2 - slot) * W_C)))
    w_lr = jnp.concatenate([place(p['w_decay_up'][l], 0), place(p['w_a_up'][l], 1), place(p['w_g_up'][l], 2),
                            jnp.zeros((LR_COLS - R_DECAY - R_A - R_G, 3 * W_C), F32)], axis=0)
    row = lambda t: t.reshape(1, -1)
    lp = dict(
        layer=l, w_ada=p['w_ada'], b_ada=row(p['b_ada'][l]),
        norm1=row(p['norm1'][l]), norm2=row(p['norm2'][l]), w_in=w_in_p,
        gq=row(jnp.tile(p['qk_norm_q'][l], 2 * H_B)), gk=row(jnp.tile(p['qk_norm_k'][l], 2 * H_B)),
        lam_rows=jnp.stack([p['lambda_q1'][l], p['lambda_k1'][l], p['lambda_q2'][l], p['lambda_k2'][l]]),
        subln=row(p['subln_diff'][l]),
        mu=row(jnp.pad(p['mu_shift'][l], (0, RWKV_PAD - RWKV_COLS))),
        w0=row(p['w0'][l]), a0=row(p['a0'][l]), k_k=row(p['k_k'][l]), k_a=row(p['k_a'][l]),
        w_lr=w_lr.astype(BF16), r_k=row(p['r_k'][l]), lnx_g=row(p['lnx_g'][l]), lnx_b=row(p['lnx_b'][l]),
        w_up_a=p['w_up_a'][l].astype(BF16), w_up_b=p['w_up_b'][l].astype(BF16),
        w_up_c=p['w_up_c'][l].astype(BF16), w_out=p['w_out'][l].astype(BF16),
        w_ff1=p['w_ff1'][l].astype(BF16), w_ff2=p['w_ff2'][l].astype(BF16),
        lam_init=0.8 - 0.6 * math.exp(-0.3 * l),
    )
    if l > 0:
        lp['v0'] = row(p['v0'][l - 1])
        lp['w_v_down'] = jnp.pad(p['w_v_down'][l - 1], ((0, 0), (0, 128 - R_V))).astype(BF16)
        lp['w_v_up'] = jnp.pad(p['w_v_up'][l - 1], ((0, 128 - R_V), (0, 0))).astype(BF16)
    return lp


def _modulation(c, lp, per_token):
    B, D = c.shape
    start = 0 if not per_token else lp['mod'].shape[0] - B
    mod = lp['mod'][start:start + B]
    parts = [mod[:, i * D:(i + 1) * D] for i in range(6)]
    shape = (1, B, D) if per_token else (B, 1, D)
    return [t.reshape(shape) for t in parts]


def _trunk_prompt(x, c, layers, bias_p):
    B, L, D = x.shape
    T = B * L
    tiles = _prompt_tiles(L)
    x2 = x.reshape(T, D)
    cos_t, sin_t = _rope_tables(jnp.arange(L))
    ret_out, rwkv_out, shift_out = [], [], []
    k_st = jnp.zeros((len(layers), T * H_B, DV_B), F32)
    v_st = jnp.zeros((len(layers), T * H_B, DV_B), F32)
    v_first = None
    for l, lp in enumerate(layers):
        sh1, sc1, gt1, sh2, sc2, gt2 = _modulation(c, lp, per_token=False)
        z2 = in_proj(x2, lp['norm1'], sc1, sh1, lp['w_in'], l, tiles['in_proj_rows'],
                     tiles['in_proj_cols'])
        ZC = z2.shape[1]
        z3 = z2.reshape(B, L, ZC)
        ya, s_ret = retention_prompt(z3, cos_t, sin_t, tiles['retention'])
        qb, k_st, kb, v_st, vb = qk_norm(z2, lp['gq'], lp['gk'], tiles['qk_norm'],
                                         DH_B ** -0.5 * math.log2(math.e), l, k_st, v_st)
        r3 = lambda t: t.reshape(B, L, W_B)
        yb = diff_attention_prompt(r3(qb), r3(kb), r3(vb), bias_p, lp['lam_rows'], lp['subln'],
                                   lp['lam_init'], tiles['attention'])
        vgate = None if l == 0 else (lp['v0'], lp['w_v_down'], lp['w_v_up'], v_first)
        shift0 = jnp.zeros((B, 1, RWKV_PAD), F32)
        r, lw, k, v, a, b, gg = rwkv_prep(z3, shift0, lp['mu'], lp['w0'], lp['a0'], lp['k_k'], lp['k_a'],
                                          lp['w_lr'], vgate, tiles['rwkv_prep'], carry_shift=True)
        if l == 0:
            v_first = v
        y, s_pair = rwkv_chunk_scan(r, lw, k, v, a, b, tiles['rwkv_scan'], B)
        shift_out.append(z3[:, L - 1:, Z_RWKV:Z_RWKV + RWKV_COLS])
        f2 = lambda t: t.reshape(T, -1)
        x2 = merge_out(x2, f2(ya), f2(yb), f2(y), f2(r), f2(k), f2(v), f2(gg), z2, gt1, lp['lnx_g'],
                       lp['lnx_b'], lp['r_k'], lp['w_up_a'], lp['w_up_b'], lp['w_up_c'], lp['w_out'],
                       tiles['merge'])
        x2 = ffn(x2, lp['norm2'], sc2, sh2, gt2, lp['w_ff1'], lp['w_ff2'], tiles['ffn_rows'],
                 tiles['ffn_hidden'])
        ret_out.append(s_ret)
        sp = s_pair.reshape(B, H_C // 2, 2, DH_C, 2, DH_C)
        rwkv_out.append(jnp.stack([sp[:, :, 0, :, 0, :], sp[:, :, 1, :, 1, :]], axis=2)
                        .reshape(B, H_C, DH_C, DH_C))
    kv_shape = (len(layers), B, L, H_B, DV_B)
    return (x2.reshape(B, L, D), k_st.reshape(kv_shape), v_st.reshape(kv_shape), jnp.stack(ret_out),
            jnp.stack(rwkv_out), jnp.stack(shift_out))


def _trunk_decode(x, c, layers, bias_d, state_ret, state_rwkv, state_shift, cache_k, cache_v, page_table):
    B, _, D = x.shape
    n_pages = page_table.shape[1]
    past = n_pages * PAGE_SIZE
    x2 = x.reshape(B, D)
    pt_flat = page_table.reshape(-1)
    ck = cache_k.reshape(cache_k.shape[0], cache_k.shape[1], PAGE_SIZE * H_B, 2 * DH_B)
    cv = cache_v.reshape(cache_v.shape[0], cache_v.shape[1], PAGE_SIZE * H_B, DV_B)
    state_t = jnp.transpose(state_rwkv, (0, 2, 3, 4, 1))
    shift_out = []
    ret_st = jnp.zeros(state_ret.shape, F32)
    rwkv_st = jnp.zeros(state_t.shape, F32)
    k_st = jnp.zeros((len(layers), B * H_B, DV_B), F32)
    v_st = jnp.zeros((len(layers), B * H_B, DV_B), F32)
    v_first = None
    for l, lp in enumerate(layers):
        sh1, sc1, gt1, sh2, sc2, gt2 = _modulation(c, lp, per_token=True)
        z2 = in_proj(x2, lp['norm1'], sc1, sh1, lp['w_in'], l, B, _prompt_tiles(B)['in_proj_cols'])
        ZC = z2.shape[1]
        ya, ret_st = retention_step(z2, state_ret, l, past, ret_st)
        qb, k_st, kb, v_st, vb = qk_norm(z2, lp['gq'], lp['gk'], B, DH_B ** -0.5, l, k_st, v_st)
        h3 = lambda t: t[l].reshape(B, H_B, DV_B)
        yb = diff_attention_decode(qb.reshape(B, 1, W_BQK), h3(k_st), h3(v_st), ck, cv, pt_flat, bias_d,
                                   lp['lam_rows'], lp['subln'], l, lp['lam_init'], n_pages).reshape(B, W_B)
        vgate = None if l == 0 else (lp['v0'], lp['w_v_down'], lp['w_v_up'], v_first)
        prev = jnp.pad(state_shift[l].reshape(1, B, RWKV_COLS), ((0, 0), (0, 0), (0, RWKV_PAD - RWKV_COLS)))
        r, lw, k, v, a, b, gg = rwkv_prep(z2.reshape(1, B, ZC), prev, lp['mu'], lp['w0'], lp['a0'], lp['k_k'],
                                          lp['k_a'], lp['w_lr'], vgate, B, carry_shift=False)
        if l == 0:
            v_first = v
        f2 = lambda t: t.reshape(B, -1)
        y_dec, rwkv_st = rwkv_step(state_t, l, f2(lw), f2(a), f2(b), f2(k), f2(r), f2(v), rwkv_st)
        shift_out.append(z2[:, Z_RWKV:Z_RWKV + RWKV_COLS].reshape(B, 1, RWKV_COLS))
        x2 = merge_out(x2, ya, yb, y_dec, f2(r), f2(k), f2(v), f2(gg), z2, gt1, lp['lnx_g'], lp['lnx_b'],
                       lp['r_k'], lp['w_up_a'], lp['w_up_b'], lp['w_up_c'], lp['w_out'], B)
        x2 = ffn(x2, lp['norm2'], sc2, sh2, gt2, lp['w_ff1'], lp['w_ff2'], B, 1024)
    rwkv_state = jnp.transpose(rwkv_st, (0, 4, 1, 2, 3))
    kv_shape = (len(layers), B, 1, H_B, DV_B)
    return (x2.reshape(B, 1, D), k_st.reshape(kv_shape), v_st.reshape(kv_shape), ret_st,
            rwkv_state, jnp.stack(shift_out))


def _decode_buckets(n_pages):
    past = n_pages * PAGE_SIZE
    key = np.arange((n_pages + 1) * PAGE_SIZE)
    bk = np.where(key <= past, _bucket_np(np.maximum(past - key, 0)), -1)
    return np.broadcast_to(bk.reshape(n_pages + 1, 1, PAGE_SIZE), (n_pages + 1, 8, PAGE_SIZE)).astype(np.int32)


def kernel(x_prompt, x_sample, c_prompt, c_sample, cache_k_diff, cache_v_diff, page_table, state_ret, state_rwkv, state_shift, rel_bias, w_ada, b_ada, norm1, norm2, w_in, qk_norm_q, qk_norm_k, lambda_q1, lambda_k1, lambda_q2, lambda_k2, subln_diff, mu_shift, w0, w_decay_up, a0, w_a_up, w_g_up, v0, w_v_down, w_v_up, k_k, k_a, r_k, lnx_g, lnx_b, w_up_a, w_up_b, w_up_c, w_out, w_ff1, w_ff2):
    p = dict(w_ada=w_ada, b_ada=b_ada, norm1=norm1, norm2=norm2, w_in=w_in,
             qk_norm_q=qk_norm_q, qk_norm_k=qk_norm_k, lambda_q1=lambda_q1, lambda_k1=lambda_k1,
             lambda_q2=lambda_q2, lambda_k2=lambda_k2, subln_diff=subln_diff, mu_shift=mu_shift,
             w0=w0, w_decay_up=w_decay_up, a0=a0, w_a_up=w_a_up, w_g_up=w_g_up, v0=v0,
             w_v_down=w_v_down, w_v_up=w_v_up, k_k=k_k, k_a=k_a, r_k=r_k, lnx_g=lnx_g, lnx_b=lnx_b,
             w_up_a=w_up_a, w_up_b=w_up_b, w_up_c=w_up_c, w_out=w_out, w_ff1=w_ff1, w_ff2=w_ff2)
    depth = w_in.shape[0]
    w_in_p = prep_w_in(w_in)
    layers = [_prep_layer(p, l, w_in_p) for l in range(depth)]
    bp = c_prompt.shape[0]
    c_all = jnp.concatenate([jnp.pad(c_prompt, ((0, -bp % 16), (0, 0))), c_sample], axis=0)
    for lp in layers:
        lp['mod'] = ada_mod(c_all, lp['w_ada'], lp['b_ada'], lp['layer'])
    L = x_prompt.shape[1]
    n_pages = page_table.shape[1]
    bias_p = bias_tiles(rel_bias, _prompt_buckets(_prompt_tiles(L)['attention']), True)
    bd = bias_tiles(rel_bias, _decode_buckets(n_pages).reshape((n_pages + 1) * 8, PAGE_SIZE), False)
    bias_d = bd.reshape(H_B, n_pages + 1, 8, PAGE_SIZE)

    y_p, k_p, v_p, ret_p, rwkv_p, shift_p = _trunk_prompt(x_prompt, c_prompt, layers, bias_p)
    y_s, k_s, v_s, ret_s, rwkv_s, shift_s = _trunk_decode(
        x_sample, c_sample, layers, bias_d, state_ret, state_rwkv, state_shift,
        cache_k_diff, cache_v_diff, page_table)
    return (y_p, y_s, k_p, v_p, k_s, v_s, ret_p, ret_s, rwkv_p, rwkv_s, shift_p, shift_s)
```
